```python
import math
import jax, jax.numpy as jnp
from jax import lax
import numpy as np

D_MODEL = 1024
BATCH = 4
SEQ = 4096
DEPTH = 1

CHUNK = 64
Q_BLOCK = 128
ATT_HEADS = 4
ATT_DH = 64
ATT_DV = 2 * ATT_DH
ATT_QK = ATT_HEADS * 2 * ATT_DH
ATT_WIDTH = ATT_HEADS * ATT_DV
HG_HEADS = 4
HG_DK = 128
HG_DV = 128
HG_KW = HG_HEADS * HG_DK
HG_WIDTH = HG_HEADS * HG_DV
D_FF = 2816
CONV_W = 3
NUM_BUCKETS = 32
MAX_DISTANCE = 128
ALPHA = (2 * DEPTH) ** 0.25
BETA = (8 * DEPTH) ** -0.25
EPS = 1e-5

OFF_AQ = 0
OFF_AK = OFF_AQ + ATT_QK
OFF_AV = OFF_AK + ATT_QK
OFF_HQ = OFF_AV + ATT_WIDTH
OFF_HF = OFF_HQ + HG_KW
OFF_HI = OFF_HF + HG_KW
OFF_HG = OFF_HI + HG_WIDTH
OFF_GATE = OFF_HG + HG_WIDTH
IN_COLS = OFF_GATE + 2 * D_MODEL

kernel_name = 'hybrid_diffattn_hgrn2_convffn_block'


def rms_norm(x, w):
    xf = x.astype(jnp.float32)
    y = xf * lax.rsqrt(jnp.mean(xf * xf, axis=-1, keepdims=True) + EPS)
    return (y * w.astype(jnp.float32)).astype(x.dtype)


def layer_norm(x, g, b):
    xf = x.astype(jnp.float32)
    mu = jnp.mean(xf, axis=-1, keepdims=True)
    var = jnp.mean(jnp.square(xf - mu), axis=-1, keepdims=True)
    y = (xf - mu) * lax.rsqrt(var + EPS) * g.astype(jnp.float32) + b.astype(jnp.float32)
    return y.astype(x.dtype)


def rel_bucket(rel):
    nb = NUM_BUCKETS // 2
    max_exact = nb // 2
    ret = jnp.where(rel > 0, nb, 0)
    n = jnp.abs(rel)
    nf = jnp.maximum(n, 1).astype(jnp.float32)
    large = max_exact + (jnp.log(nf / max_exact) / math.log(MAX_DISTANCE / max_exact)
                         * (nb - max_exact)).astype(jnp.int32)
    large = jnp.minimum(large, nb - 1)
    return ret + jnp.where(n < max_exact, n, large)


def diff_attention(q, k, v, rel_bias, lam):
    B, S = q.shape[0], q.shape[1]
    nqb = S // Q_BLOCK
    q1 = q[..., 0, :].transpose(0, 2, 1, 3)
    q2 = q[..., 1, :].transpose(0, 2, 1, 3)
    k1 = k[..., 0, :].transpose(0, 2, 1, 3)
    k2 = k[..., 1, :].transpose(0, 2, 1, 3)
    vv = v.transpose(0, 2, 1, 3)
    qb1 = q1.reshape(B, ATT_HEADS, nqb, Q_BLOCK, ATT_DH).transpose(2, 0, 1, 3, 4)
    qb2 = q2.reshape(B, ATT_HEADS, nqb, Q_BLOCK, ATT_DH).transpose(2, 0, 1, 3, 4)
    kpos = jnp.arange(S)
    scale = ATT_DH ** -0.5

    def block(args):
        qa, qb, bi = args
        qpos = bi * Q_BLOCK + jnp.arange(Q_BLOCK)
        rel = kpos[None, :] - qpos[:, None]
        bias = rel_bias[rel_bucket(rel)].astype(jnp.float32).transpose(2, 0, 1)
        mask = (kpos[None, :] // CHUNK) <= (qpos[:, None] // CHUNK)

        def probs(qq, kk):
            s = jnp.einsum('bhqd,bhkd->bhqk', qq, kk).astype(jnp.float32) * scale + bias
            return jax.nn.softmax(jnp.where(mask, s, -jnp.inf), axis=-1)

        p = probs(qa, k1) - lam * probs(qb, k2)
        return jnp.einsum('bhqk,bhkd->bhqd', p.astype(vv.dtype), vv)

    o = lax.map(block, (qb1, qb2, jnp.arange(nqb)))
    return o.transpose(1, 0, 3, 2, 4).reshape(B, S, ATT_HEADS, ATT_DV)


def hgrn2(q, f_pre, i, lb):
    B, S = q.shape[0], q.shape[1]
    nc = S // CHUNK
    sig = jax.nn.sigmoid(f_pre.astype(jnp.float32))
    logf = jnp.log(lb + (1.0 - lb) * sig)
    kk = (1.0 - lb) * (1.0 - sig)

    def chunks(t):
        return t.astype(jnp.float32).reshape(B, nc, CHUNK, HG_HEADS, -1).transpose(1, 0, 3, 2, 4)

    qc, kc, ic, lc = chunks(q), chunks(kk), chunks(i), chunks(logf)
    bc = jnp.cumsum(lc, axis=-2)
    causal = jnp.tril(jnp.ones((CHUNK, CHUNK), dtype=bool))

    def step(state, inp):
        q_, k_, i_, b_ = inp
        inter = jnp.einsum('bhtk,bhkv->bhtv', q_ * jnp.exp(b_), state)
        diff = b_[:, :, :, None, :] - b_[:, :, None, :, :]
        decay = jnp.where(causal[:, :, None], jnp.exp(jnp.minimum(diff, 0.0)), 0.0)
        a = jnp.einsum('bhtk,bhtsk,bhsk->bhts', q_, decay, k_)
        o = inter + jnp.einsum('bhts,bhsv->bhtv', a, i_)
        b_last = b_[:, :, -1, :]
        new_state = jnp.exp(b_last)[..., None] * state + jnp.einsum(
            'bhsk,bhsv->bhkv', k_ * jnp.exp(b_last[:, :, None, :] - b_), i_)
        return new_state, o

    s0 = jnp.zeros((B, HG_HEADS, HG_DK, HG_DV), jnp.float32)
    _, o = lax.scan(step, s0, (qc, kc, ic, bc))
    return o.transpose(1, 0, 3, 2, 4).reshape(B, S, HG_HEADS, HG_DV)


def causal_dwconv(u, w, b):
    out = lax.conv_general_dilated(
        u, w[:, None, :].astype(u.dtype), window_strides=(1,), padding=[(CONV_W - 1, 0)],
        dimension_numbers=('NWC', 'WIO', 'NWC'), feature_group_count=u.shape[-1])
    return out + b


def setup_inputs(seed: int = 0) -> dict:
    key = jax.random.key(seed)
    ks = jax.random.split(key, 24)
    n = lambda k, s: jax.random.normal(k, s, jnp.float32)
    D = D_MODEL
    return {
        'x': n(ks[0], (BATCH, SEQ, D)),
        'w_in': n(ks[1], (DEPTH, D, IN_COLS)) * D ** -0.5,
        'b_gate': n(ks[2], (DEPTH, 2 * D)) * 0.02,
        'lambda_q1': n(ks[3], (DEPTH, ATT_DH)) * 0.1,
        'lambda_k1': n(ks[4], (DEPTH, ATT_DH)) * 0.1,
        'lambda_q2': n(ks[5], (DEPTH, ATT_DH)) * 0.1,
        'lambda_k2': n(ks[6], (DEPTH, ATT_DH)) * 0.1,
        'attn_subln_w': 1.0 + 0.02 * n(ks[7], (DEPTH, ATT_DV)),
        'rel_bias': n(ks[8], (NUM_BUCKETS, ATT_HEADS)) * 0.2,
        'hgrn_lb_logits': n(ks[9], (DEPTH + 1, HG_KW)),
        'hgrn_norm_w': 1.0 + 0.02 * n(ks[10], (DEPTH, HG_DV)),
        'w_branch_attn': n(ks[11], (DEPTH, ATT_WIDTH, D)) * ATT_WIDTH ** -0.5,
        'w_branch_hgrn': n(ks[12], (DEPTH, HG_WIDTH, D)) * HG_WIDTH ** -0.5,
        'w_out': n(ks[13], (DEPTH, D, D)) * D ** -0.5 * BETA,
        'ln1_g': 1.0 + 0.02 * n(ks[14], (DEPTH, D)),
        'ln1_b': 0.02 * n(ks[15], (DEPTH, D)),
        'w_ffn_gate': n(ks[16], (DEPTH, D, D_FF)) * D ** -0.5,
        'w_ffn_up': n(ks[17], (DEPTH, D, D_FF)) * D ** -0.5,
        'ffn_conv_w': n(ks[18], (DEPTH, CONV_W, D_FF)) * CONV_W ** -0.5,
        'ffn_conv_b': 0.02 * n(ks[19], (DEPTH, D_FF)),
        'w_ffn_down': n(ks[20], (DEPTH, D_FF, D)) * D_FF ** -0.5 * BETA,
        'ln2_g': 1.0 + 0.02 * n(ks[21], (DEPTH, D)),
        'ln2_b': 0.02 * n(ks[22], (DEPTH, D)),
    }


def reference(x, w_in, b_gate, lambda_q1, lambda_k1, lambda_q2, lambda_k2, attn_subln_w,
              rel_bias, hgrn_lb_logits, hgrn_norm_w, w_branch_attn, w_branch_hgrn, w_out,
              ln1_g, ln1_b, w_ffn_gate, w_ffn_up, ffn_conv_w, ffn_conv_b, w_ffn_down,
              ln2_g, ln2_b):
    B, S, D = x.shape
    lb_all = jnp.cumsum(jax.nn.softmax(hgrn_lb_logits.astype(jnp.float32), axis=0), axis=0)
    h = x
    for l in range(DEPTH):
        lam_init = 0.8 - 0.6 * math.exp(-0.3 * l)
        proj = h @ w_in[l]
        aq = proj[..., OFF_AQ:OFF_AK].reshape(B, S, ATT_HEADS, 2, ATT_DH)
        ak = proj[..., OFF_AK:OFF_AV].reshape(B, S, ATT_HEADS, 2, ATT_DH)
        av = proj[..., OFF_AV:OFF_HQ].reshape(B, S, ATT_HEADS, ATT_DV)
        hq = proj[..., OFF_HQ:OFF_HF].reshape(B, S, HG_HEADS, HG_DK)
        hf = proj[..., OFF_HF:OFF_HI].reshape(B, S, HG_HEADS, HG_DK)
        hi = proj[..., OFF_HI:OFF_HG].reshape(B, S, HG_HEADS, HG_DV)
        hog = proj[..., OFF_HG:OFF_GATE]
        gates = jax.nn.sigmoid(proj[..., OFF_GATE:] + b_gate[l])

        lam = (jnp.exp(jnp.sum(lambda_q1[l].astype(jnp.float32) * lambda_k1[l].astype(jnp.float32)))
               - jnp.exp(jnp.sum(lambda_q2[l].astype(jnp.float32) * lambda_k2[l].astype(jnp.float32)))
               + lam_init)
        ya = diff_attention(aq, ak, av, rel_bias, lam)
        ya = (rms_norm(ya, attn_subln_w[l]) * (1.0 - lam_init)).reshape(B, S, ATT_WIDTH)

        yh = hgrn2(hq, hf, hi, lb_all[l].reshape(HG_HEADS, HG_DK))
        yh = rms_norm(yh, hgrn_norm_w[l]).astype(h.dtype).reshape(B, S, HG_WIDTH) * jax.nn.silu(hog)

        merged = gates[..., :D] * (ya @ w_branch_attn[l]) + gates[..., D:] * (yh @ w_branch_hgrn[l])
        h = layer_norm(ALPHA * h + merged @ w_out[l], ln1_g[l], ln1_b[l])

        g = causal_dwconv(h @ w_ffn_gate[l], ffn_conv_w[l], ffn_conv_b[l])
        ff = (jax.nn.gelu(g) * (h @ w_ffn_up[l])) @ w_ffn_down[l]
        h = layer_norm(ALPHA * h + ff, ln2_g[l], ln2_b[l])
    return h
```

```python
import functools
import math

import numpy as np
import jax
import jax.numpy as jnp
from jax import lax
from jax.experimental import pallas as pl
from jax.experimental.pallas import tpu as pltpu

F32 = jnp.float32
BF16 = jnp.bfloat16

CHUNK = 64
ATT_HEADS = 4
ATT_DH = 64
HG_HEADS = 4
HEAD_W = 128
NUM_BUCKETS = 32
MAX_DISTANCE = 128
EPS = 1e-5
NEG = -1e30

ATT_BQ = 128
ATT_BK = 256
HG_C = 64
HG_LEVELS = (32, 16, 8)
HG_DIAG = 8
FFN_FC = 256

VMEM_LIMIT = 56 * 1024 * 1024


def _cparams(n_axes):
    return pltpu.CompilerParams(
        dimension_semantics=("arbitrary",) * n_axes,
        vmem_limit_bytes=VMEM_LIMIT)


def _const_spec(shape):
    nd = len(shape)
    return pl.BlockSpec(shape, lambda *_: (0,) * nd, pipeline_mode=pl.Buffered(1))


def _inproj_kernel(x_ref, w_ref, wvt_ref, bg_ref, lb_ref,
                   q_ref, k_ref, vt_ref, hq_ref, kk_ref, hi_ref, sog_ref,
                   logf_ref, gates_ref, *, tm, offs):
    xb = x_ref[...].astype(BF16)

    def mm(off, width=512):
        return jnp.dot(xb, w_ref[:, off:off + width], preferred_element_type=F32)

    q_ref[...] = (mm(offs["aq"]) * (ATT_DH ** -0.5)).astype(BF16)
    k_ref[...] = mm(offs["ak"]).astype(BF16)
    for t in range(tm // ATT_BK):
        vt = lax.dot_general(wvt_ref[...], xb[t * ATT_BK:(t + 1) * ATT_BK],
                             (((1,), (1,)), ((), ())), preferred_element_type=F32)
        vt_ref[t] = vt.astype(BF16)
    hq_ref[...] = mm(offs["hq"]).astype(BF16)
    sig = jax.nn.sigmoid(mm(offs["hf"]))
    lb = lb_ref[...]
    logf_ref[...] = jnp.log(lb + (1.0 - lb) * sig)
    kk_ref[...] = ((1.0 - lb) * (1.0 - sig)).astype(BF16)
    hi_ref[...] = mm(offs["hi"]).astype(BF16)
    hog = mm(offs["hg"])
    sog_ref[...] = (hog * jax.nn.sigmoid(hog)).astype(BF16)
    for j in range(gates_ref.shape[-1] // 512):
        g = mm(offs["gate"] + j * 512) + bg_ref[:, j * 512:(j + 1) * 512]
        gates_ref[:, j * 512:(j + 1) * 512] = jax.nn.sigmoid(g).astype(BF16)


def _inproj(x, w_bf, wvt_bf, b_gate, lb, offs, tm=512):
    B, S, D = x.shape
    ncols = w_bf.shape[1]
    gate_w = ncols - offs["gate"]
    row = lambda w: pl.BlockSpec((None, tm, w), lambda b, s: (b, s, 0))
    out_shape = (
        jax.ShapeDtypeStruct((B, S, 512), BF16),
        jax.ShapeDtypeStruct((B, S, 512), BF16),
        jax.ShapeDtypeStruct((B, S // ATT_BK, 512, ATT_BK), BF16),
        jax.ShapeDtypeStruct((B, S, 512), BF16),
        jax.ShapeDtypeStruct((B, S, 512), BF16),
        jax.ShapeDtypeStruct((B, S, 512), BF16),
        jax.ShapeDtypeStruct((B, S, 512), BF16),
        jax.ShapeDtypeStruct((B, S, 512), F32),
        jax.ShapeDtypeStruct((B, S, gate_w), BF16),
    )
    out_specs = (
        row(512), row(512),
        pl.BlockSpec((None, tm // ATT_BK, 512, ATT_BK), lambda b, s: (b, s, 0, 0)),
        row(512), row(512), row(512), row(512), row(512), row(gate_w),
    )
    return pl.pallas_call(
        functools.partial(_inproj_kernel, tm=tm, offs=offs),
        grid=(B, S // tm),
        in_specs=[row(D), _const_spec(w_bf.shape), _const_spec(wvt_bf.shape),
                  _const_spec(b_gate.shape), _const_spec(lb.shape)],
        out_specs=out_specs,
        out_shape=out_shape,
        compiler_params=_cparams(2),
        name="inproj",
    )(x, w_bf, wvt_bf, b_gate, lb)


def _rel_bucket_np(rel):
    nb = NUM_BUCKETS // 2
    max_exact = nb // 2
    ret = np.where(rel > 0, nb, 0)
    n = np.abs(rel)
    nf = np.maximum(n, 1).astype(np.float32)
    large = max_exact + (np.log(nf / np.float32(max_exact))
                         / np.float32(math.log(MAX_DISTANCE / max_exact))
                         * np.float32(nb - max_exact)).astype(np.int32)
    large = np.minimum(large, nb - 1)
    return ret + np.where(n < max_exact, n, large)


def _attn_bias_index():
    base = 4 * ATT_BK
    i = np.arange(ATT_BQ)[None, :]
    j = np.arange(ATT_BK)[:, None]
    buckets, masks = [], []
    for par in range(2):
        qpos = base + ATT_BQ * par + i
        for which in range(2):
            kpos = base - which * ATT_BK + j
            buckets.append(_rel_bucket_np(kpos - qpos))
            masks.append((kpos // CHUNK) <= (qpos // CHUNK))
    buckets.append(np.zeros((ATT_BK, ATT_BQ), np.int64))
    masks.append(np.zeros((ATT_BK, ATT_BQ), bool))
    return np.stack(buckets).astype(np.int32), np.stack(masks)


def _attn_kernel(lam_ref, q_ref, k_ref, vt_ref, bias_ref, w_ref, o_ref, *, out_scale):
    S = q_ref.shape[0]
    lam = lam_ref[0]
    lane = lax.broadcasted_iota(jnp.int32, (ATT_BQ, HEAD_W), 1)
    first_map = lane < ATT_DH

    def qblock(qi, carry):
        q = q_ref[pl.ds(pl.multiple_of(qi * ATT_BQ, ATT_BQ), ATT_BQ), :]
        zero = jnp.zeros_like(q)
        qs = jnp.concatenate([jnp.where(first_map, q, zero),
                              jnp.where(first_map, zero, q)], axis=0)
        kb_last = qi // 2
        par = qi % 2

        def scores(kb):
            kblk = k_ref[pl.ds(pl.multiple_of(kb * ATT_BK, ATT_BK), ATT_BK), :]
            return lax.dot_general(kblk, qs, (((1,), (1,)), ((), ())),
                                   preferred_element_type=F32)

        def step(kb, bias, state):
            m, l, acc = state
            s = scores(kb)
            if bias is not None:
                s = s + bias
            m_new = jnp.maximum(m, jnp.max(s, axis=0, keepdims=True))
            alpha = jnp.exp(m - m_new)
            p = jnp.exp(s - m_new)
            l = alpha * l + jnp.sum(p, axis=0, keepdims=True)
            acc = alpha * acc + jnp.dot(vt_ref[kb], p.astype(BF16),
                                        preferred_element_type=F32)
            return m_new, l, acc

        s = scores(kb_last) + bias_ref[par * 2]
        m = jnp.max(s, axis=0, keepdims=True)
        p = jnp.exp(s - m)
        l = jnp.sum(p, axis=0, keepdims=True)
        acc = jnp.dot(vt_ref[kb_last], p.astype(BF16), preferred_element_type=F32)
        has_prev = kb_last >= 1
        bidx = jnp.where(has_prev, par * 2 + 1, 4)
        state = step(jnp.maximum(kb_last - 1, 0), bias_ref[bidx], (m, l, acc))
        state = lax.fori_loop(0, jnp.maximum(kb_last - 1, 0),
                              lambda kb, st: step(kb, None, st), state)
        m, l, acc = state
        on = acc * (1.0 / l)
        o = on[:, :ATT_BQ] - lam * on[:, ATT_BQ:]
        ms = jnp.mean(o * o, axis=0, keepdims=True)
        y = o * lax.rsqrt(ms + EPS) * w_ref[...] * out_scale
        o_ref[pl.ds(pl.multiple_of(qi * ATT_BQ, ATT_BQ), ATT_BQ), :] = y.T.astype(BF16)
        return carry

    lax.fori_loop(0, S // ATT_BQ, qblock, 0)


def _attention(q, k, vt, bias_tiles, subln_w, lam, out_scale):
    B, S, _ = q.shape
    nkb = S // ATT_BK
    head = pl.BlockSpec((None, S, HEAD_W), lambda b, h: (b, 0, h))
    return pl.pallas_call(
        functools.partial(_attn_kernel, out_scale=out_scale),
        grid=(B, ATT_HEADS),
        in_specs=[
            pl.BlockSpec(memory_space=pltpu.SMEM),
            head, head,
            pl.BlockSpec((None, nkb, HEAD_W, ATT_BK), lambda b, h: (b, 0, h, 0)),
            pl.BlockSpec((None, 5, ATT_BK, 2 * ATT_BQ), lambda b, h: (h, 0, 0, 0)),
            pl.BlockSpec((HEAD_W, 1), lambda b, h: (0, 0)),
        ],
        out_specs=head,
        out_shape=jax.ShapeDtypeStruct((B, S, ATT_HEADS * HEAD_W), BF16),
        compiler_params=_cparams(2),
        name="diffattn",
    )(lam, q, k, vt, bias_tiles, subln_w)


def _hgrn_kernel(hq_ref, kk_ref, hi_ref, sog_ref, logf_ref, w_ref, y_ref,
                 state_ref, kbuf_ref, bbuf_ref, *, tc):
    C = HG_C

    @pl.when(pl.program_id(1) == 0)
    def _():
        state_ref[...] = jnp.zeros_like(state_ref)

    kbuf_ref[0:HG_DIAG, :] = jnp.zeros((HG_DIAG, HEAD_W), F32)
    bbuf_ref[0:HG_DIAG, :] = jnp.zeros((HG_DIAG, HEAD_W), F32)

    row = lax.broadcasted_iota(jnp.int32, (C, HEAD_W), 0)
    r2 = lax.broadcasted_iota(jnp.int32, (C, C), 0)
    c2 = lax.broadcasted_iota(jnp.int32, (C, C), 1)
    tri = (c2 <= r2).astype(BF16)
    same_blk = (r2 // HG_DIAG) == (c2 // HG_DIAG)

    def chunk(c, carry):
        r0 = pl.multiple_of(c * C, C)
        for h in range(HG_HEADS):
            cols = slice(h * HEAD_W, (h + 1) * HEAD_W)
            lf = logf_ref[pl.ds(r0, C), cols]
            p0 = lf.astype(BF16)
            e1 = lf - p0.astype(F32)
            p1 = e1.astype(BF16)
            p2 = (e1 - p1.astype(F32)).astype(BF16)
            b = (jnp.dot(tri, p0, preferred_element_type=F32)
                 + jnp.dot(tri, p1, preferred_element_type=F32)
                 + jnp.dot(tri, p2, preferred_element_type=F32))
            q = hq_ref[pl.ds(r0, C), cols].astype(F32)
            kk = kk_ref[pl.ds(r0, C), cols].astype(F32)
            iv = hi_ref[pl.ds(r0, C), cols]
            st = state_ref[h]
            b_last = b[C - 1:C, :]

            inter = lax.dot_general((q * jnp.exp(b)).astype(BF16), st.astype(BF16),
                                    (((1,), (1,)), ((), ())), preferred_element_type=F32)

            a = jnp.zeros((C, C), F32)
            for hs in HG_LEVELS:
                parts = []
                for g in range(C // (2 * hs)):
                    r = g * 2 * hs + hs - 1
                    parts.append(jnp.broadcast_to(b[r:r + 1, :], (2 * hs, HEAD_W)))
                bref = parts[0] if len(parts) == 1 else jnp.concatenate(parts, axis=0)
                is_q = ((row // hs) % 2) == 1
                e = jnp.exp(jnp.where(is_q, b - bref, bref - b))
                ql = jnp.where(is_q, q * e, 0.0).astype(BF16)
                kl = jnp.where(is_q, 0.0, kk * e).astype(BF16)
                pl_ = lax.dot_general(ql, kl, (((1,), (1,)), ((), ())),
                                      preferred_element_type=F32)
                a = a + jnp.where((r2 // (2 * hs)) == (c2 // (2 * hs)), pl_, 0.0)
            kbuf_ref[HG_DIAG:, :] = kk
            bbuf_ref[HG_DIAG:, :] = b
            for d in range(HG_DIAG):
                ks = kbuf_ref[HG_DIAG - d:HG_DIAG - d + C, :]
                bs = bbuf_ref[HG_DIAG - d:HG_DIAG - d + C, :]
                term = q * ks * jnp.exp(jnp.minimum(b - bs, 0.0))
                dsum = jnp.sum(term, axis=-1, keepdims=True)
                a = a + jnp.where(((r2 - c2) == d) & same_blk, dsum, 0.0)

            o = inter + jnp.dot(a.astype(BF16), iv, preferred_element_type=F32)

            k_end = (kk * jnp.exp(b_last - b)).astype(BF16)
            upd = lax.dot_general(iv, k_end, (((0,), (0,)), ((), ())),
                                  preferred_element_type=F32)
            state_ref[h] = st * jnp.exp(b_last) + upd

            ms = jnp.mean(o * o, axis=-1, keepdims=True)
            y = o * lax.rsqrt(ms + EPS) * w_ref[...]
            y = y * sog_ref[pl.ds(r0, C), cols].astype(F32)
            y_ref[pl.ds(r0, C), cols] = y.astype(BF16)
        return carry

    lax.fori_loop(0, tc // C, chunk, 0)


def _hgrn(hq, kk, hi, sog, logf, norm_w, tc=512):
    B, S, W = hq.shape
    blk = pl.BlockSpec((None, tc, W), lambda b, s: (b, s, 0))
    return pl.pallas_call(
        functools.partial(_hgrn_kernel, tc=tc),
        grid=(B, S // tc),
        in_specs=[blk, blk, blk, blk, blk, pl.BlockSpec((1, HEAD_W), lambda b, s: (0, 0))],
        out_specs=blk,
        out_shape=jax.ShapeDtypeStruct((B, S, W), BF16),
        scratch_shapes=[pltpu.VMEM((HG_HEADS, HEAD_W, HEAD_W), F32),
                        pltpu.VMEM((HG_C + HG_DIAG, HEAD_W), F32),
                        pltpu.VMEM((HG_C + HG_DIAG, HEAD_W), F32)],
        compiler_params=_cparams(2),
        name="hgrn2",
    )(hq, kk, hi, sog, logf, norm_w)


def _layer_norm(z, g, b):
    mu = jnp.mean(z, axis=-1, keepdims=True)
    zc = z - mu
    var = jnp.mean(zc * zc, axis=-1, keepdims=True)
    return zc * lax.rsqrt(var + EPS) * g + b


def _merge_kernel(ya_ref, yh_ref, gates_ref, x_ref, wa_ref, wh_ref, wo_ref,
                  g_ref, b_ref, o_ref, *, alpha):
    D = x_ref.shape[-1]
    pa = jnp.dot(ya_ref[...], wa_ref[...], preferred_element_type=F32)
    ph = jnp.dot(yh_ref[...], wh_ref[...], preferred_element_type=F32)
    merged = (gates_ref[:, :D].astype(F32) * pa + gates_ref[:, D:].astype(F32) * ph)
    mix = jnp.dot(merged.astype(BF16), wo_ref[...], preferred_element_type=F32)
    o_ref[...] = _layer_norm(alpha * x_ref[...] + mix, g_ref[...], b_ref[...])


def _merge(ya, yh, gates, x, wa, wh, wo, g, b, alpha, tm=512):
    N, D = x.shape
    row = lambda w: pl.BlockSpec((tm, w), lambda i: (i, 0))
    return pl.pallas_call(
        functools.partial(_merge_kernel, alpha=alpha),
        grid=(N // tm,),
        in_specs=[row(ya.shape[1]), row(yh.shape[1]), row(gates.shape[1]), row(D),
                  _const_spec(wa.shape), _const_spec(wh.shape), _const_spec(wo.shape),
                  _const_spec(g.shape), _const_spec(b.shape)],
        out_specs=row(D),
        out_shape=jax.ShapeDtypeStruct((N, D), F32),
        compiler_params=_cparams(1),
        name="merge_ln",
    )(ya, yh, gates, x, wa, wh, wo, g, b)


def _ffn_kernel(h_ref, wg_ref, wu_ref, cw_ref, cb_ref, wd_ref, g_ref, b_ref, o_ref,
                tail_ref, gbuf_ref, *, alpha, tm):
    nfc = wg_ref.shape[0]
    PAD = 8

    @pl.when(pl.program_id(1) == 0)
    def _():
        tail_ref[...] = jnp.zeros_like(tail_ref)

    h = h_ref[...]
    hb = h.astype(BF16)
    acc = jnp.zeros(h.shape, F32)
    for fc in range(nfc):
        gte = jnp.dot(hb, wg_ref[fc], preferred_element_type=F32)
        up = jnp.dot(hb, wu_ref[fc], preferred_element_type=F32)
        gbuf_ref[0:PAD, :] = tail_ref[fc]
        gbuf_ref[PAD:, :] = gte
        tail_ref[fc] = gte[tm - PAD:, :]
        cw = cw_ref[fc]
        conv = (gte * cw[2:3, :]
                + gbuf_ref[PAD - 1:PAD - 1 + tm, :] * cw[1:2, :]
                + gbuf_ref[PAD - 2:PAD - 2 + tm, :] * cw[0:1, :]
                + cb_ref[fc])
        act = jax.nn.gelu(conv) * up
        acc = acc + jnp.dot(act.astype(BF16), wd_ref[fc], preferred_element_type=F32)
    o_ref[...] = _layer_norm(alpha * h + acc, g_ref[...], b_ref[...])


def _ffn(h, wg, wu, cw, cb, wd, g, b, alpha, tm=512):
    B, S, D = h.shape
    nfc, _, fcw = wg.shape
    row = pl.BlockSpec((None, tm, D), lambda bi, s: (bi, s, 0))
    return pl.pallas_call(
        functools.partial(_ffn_kernel, alpha=alpha, tm=tm),
        grid=(B, S // tm),
        in_specs=[row, _const_spec(wg.shape), _const_spec(wu.shape), _const_spec(cw.shape),
                  _const_spec(cb.shape), _const_spec(wd.shape),
                  _const_spec(g.shape), _const_spec(b.shape)],
        out_specs=row,
        out_shape=jax.ShapeDtypeStruct((B, S, D), F32),
        scratch_shapes=[pltpu.VMEM((nfc, 8, fcw), F32),
                        pltpu.VMEM((tm + 8, fcw), F32)],
        compiler_params=_cparams(2),
        name="ffn_ln",
    )(h, wg, wu, cw, cb, wd, g, b)


def kernel(x, w_in, b_gate, lambda_q1, lambda_k1, lambda_q2, lambda_k2, attn_subln_w,
           rel_bias, hgrn_lb_logits, hgrn_norm_w, w_branch_attn, w_branch_hgrn, w_out,
           ln1_g, ln1_b, w_ffn_gate, w_ffn_up, ffn_conv_w, ffn_conv_b, w_ffn_down,
           ln2_g, ln2_b):
    B, S, D = x.shape
    depth = w_in.shape[0]
    d_ff = w_ffn_gate.shape[-1]
    att_qk = ATT_HEADS * 2 * ATT_DH
    att_w = ATT_HEADS * HEAD_W
    hg_w = HG_HEADS * HEAD_W
    offs = {"aq": 0, "ak": att_qk, "av": 2 * att_qk}
    offs["hq"] = offs["av"] + att_w
    offs["hf"] = offs["hq"] + hg_w
    offs["hi"] = offs["hf"] + hg_w
    offs["hg"] = offs["hi"] + hg_w
    offs["gate"] = offs["hg"] + hg_w
    alpha = (2 * depth) ** 0.25

    lb_all = jnp.cumsum(jax.nn.softmax(hgrn_lb_logits.astype(F32), axis=0), axis=0)
    bucket_idx, live = _attn_bias_index()

    h = x
    for l in range(depth):
        lam_init = 0.8 - 0.6 * math.exp(-0.3 * l)
        lam = (jnp.exp(jnp.sum(lambda_q1[l].astype(F32) * lambda_k1[l].astype(F32)))
               - jnp.exp(jnp.sum(lambda_q2[l].astype(F32) * lambda_k2[l].astype(F32)))
               + lam_init).reshape(1)

        far = rel_bias[NUM_BUCKETS // 2 - 1].astype(F32)
        tiles = rel_bias.astype(F32)[bucket_idx] - far
        tiles = jnp.where(live[..., None], tiles, NEG)
        tiles = jnp.transpose(tiles, (3, 0, 1, 2))
        tiles = jnp.concatenate([tiles, tiles], axis=-1)

        w_l = w_in[l]
        w_bf = w_l.astype(BF16)
        wvt_bf = w_l[:, offs["av"]:offs["hq"]].T.astype(BF16)
        (q, k, vt, hq, kk, hi, sog, logf, gates) = _inproj(
            h, w_bf, wvt_bf, b_gate[l][None, :], lb_all[l][None, :], offs)

        ya = _attention(q, k, vt, tiles, attn_subln_w[l][:, None].astype(F32),
                        lam, 1.0 - lam_init)
        yh = _hgrn(hq, kk, hi, sog, logf, hgrn_norm_w[l][None, :].astype(F32))

        h1 = _merge(ya.reshape(B * S, att_w), yh.reshape(B * S, hg_w),
                    gates.reshape(B * S, 2 * D), h.reshape(B * S, D),
                    w_branch_attn[l].astype(BF16), w_branch_hgrn[l].astype(BF16),
                    w_out[l].astype(BF16), ln1_g[l][None, :], ln1_b[l][None, :], alpha)

        nfc = d_ff // FFN_FC
        wg = w_ffn_gate[l].astype(BF16).reshape(D, nfc, FFN_FC).transpose(1, 0, 2)
        wu = w_ffn_up[l].astype(BF16).reshape(D, nfc, FFN_FC).transpose(1, 0, 2)
        wd = w_ffn_down[l].astype(BF16).reshape(nfc, FFN_FC, D)
        cw = ffn_conv_w[l].reshape(-1, nfc, FFN_FC).transpose(1, 0, 2)
        cb = ffn_conv_b[l].reshape(nfc, 1, FFN_FC)
        h = _ffn(h1.reshape(B, S, D), wg, wu, cw, cb, wd,
                 ln2_g[l][None, :], ln2_b[l][None, :], alpha)
    return h
```

```python
import functools
import math

import numpy as np
import jax
import jax.numpy as jnp
from jax import lax
from jax.experimental import pallas as pl
from jax.experimental.pallas import tpu as pltpu

F32 = jnp.float32
BF16 = jnp.bfloat16

CHUNK = 64
ATT_HEADS = 4
ATT_DH = 64
HG_HEADS = 4
HEAD_W = 128
NUM_BUCKETS = 32
MAX_DISTANCE = 128
EPS = 1e-5
NEG = -1e30

ATT_BQ = 128
ATT_BK = 256
ATT_G = 4
ATT_GK = ATT_G * ATT_BQ // ATT_BK
ATT_NEAR = (0, -1, -2)
HG_C = 64
HG_LEVELS = (32, 16, 8)
HG_DIAG = 8
FFN_FC = 256

VMEM_LIMIT = 56 * 1024 * 1024


def _cparams(n_axes):
    return pltpu.CompilerParams(
        dimension_semantics=("arbitrary",) * n_axes,
        vmem_limit_bytes=VMEM_LIMIT)


def _const_spec(shape):
    nd = len(shape)
    return pl.BlockSpec(shape, lambda *_: (0,) * nd, pipeline_mode=pl.Buffered(1))


def _inproj_kernel(x_ref, w_ref, wvt_ref, bg_ref, lb_ref,
                   q_ref, k_ref, vt_ref, hq_ref, kk_ref, hi_ref, sog_ref,
                   logf_ref, gates_ref, *, tm, offs):
    xb = x_ref[...].astype(BF16)

    def mm(off, width=512):
        return jnp.dot(xb, w_ref[:, off:off + width], preferred_element_type=F32)

    q_ref[...] = (mm(offs["aq"]) * (ATT_DH ** -0.5)).astype(BF16)
    k_ref[...] = mm(offs["ak"]).astype(BF16)
    for t in range(tm // ATT_BK):
        vt = lax.dot_general(wvt_ref[...], xb[t * ATT_BK:(t + 1) * ATT_BK],
                             (((1,), (1,)), ((), ())), preferred_element_type=F32)
        vt_ref[t] = vt.astype(BF16)
    hq_ref[...] = mm(offs["hq"]).astype(BF16)
    sig = jax.nn.sigmoid(mm(offs["hf"]))
    lb = lb_ref[...]
    logf_ref[...] = jnp.log(lb + (1.0 - lb) * sig)
    kk_ref[...] = ((1.0 - lb) * (1.0 - sig)).astype(BF16)
    hi_ref[...] = mm(offs["hi"]).astype(BF16)
    hog = mm(offs["hg"])
    sog_ref[...] = (hog * jax.nn.sigmoid(hog)).astype(BF16)
    for j in range(gates_ref.shape[-1] // 512):
        g = mm(offs["gate"] + j * 512) + bg_ref[:, j * 512:(j + 1) * 512]
        gates_ref[:, j * 512:(j + 1) * 512] = jax.nn.sigmoid(g).astype(BF16)


def _inproj(x, w_bf, wvt_bf, b_gate, lb, offs, tm=512):
    B, S, D = x.shape
    ncols = w_bf.shape[1]
    gate_w = ncols - offs["gate"]
    row = lambda w: pl.BlockSpec((None, tm, w), lambda b, s: (b, s, 0))
    out_shape = (
        jax.ShapeDtypeStruct((B, S, 512), BF16),
        jax.ShapeDtypeStruct((B, S, 512), BF16),
        jax.ShapeDtypeStruct((B, S // ATT_BK, 512, ATT_BK), BF16),
        jax.ShapeDtypeStruct((B, S, 512), BF16),
        jax.ShapeDtypeStruct((B, S, 512), BF16),
        jax.ShapeDtypeStruct((B, S, 512), BF16),
        jax.ShapeDtypeStruct((B, S, 512), BF16),
        jax.ShapeDtypeStruct((B, S, 512), F32),
        jax.ShapeDtypeStruct((B, S, gate_w), BF16),
    )
    out_specs = (
        row(512), row(512),
        pl.BlockSpec((None, tm // ATT_BK, 512, ATT_BK), lambda b, s: (b, s, 0, 0)),
        row(512), row(512), row(512), row(512), row(512), row(gate_w),
    )
    return pl.pallas_call(
        functools.partial(_inproj_kernel, tm=tm, offs=offs),
        grid=(B, S // tm),
        in_specs=[row(D), _const_spec(w_bf.shape), _const_spec(wvt_bf.shape),
                  _const_spec(b_gate.shape), _const_spec(lb.shape)],
        out_specs=out_specs,
        out_shape=out_shape,
        compiler_params=_cparams(2),
        name="inproj",
    )(x, w_bf, wvt_bf, b_gate, lb)


def _rel_bucket_np(rel):
    nb = NUM_BUCKETS // 2
    max_exact = nb // 2
    ret = np.where(rel > 0, nb, 0)
    n = np.abs(rel)
    nf = np.maximum(n, 1).astype(np.float32)
    large = max_exact + (np.log(nf / np.float32(max_exact))
                         / np.float32(math.log(MAX_DISTANCE / max_exact))
                         * np.float32(nb - max_exact)).astype(np.int32)
    large = np.minimum(large, nb - 1)
    return ret + np.where(n < max_exact, n, large)


def _attn_bucket_tiles():
    i = np.arange(ATT_BQ)[None, :]
    j = np.arange(ATT_BK)[:, None]
    tiles = []
    for d in ATT_NEAR:
        kpos = d * ATT_BQ + j
        live = (kpos // CHUNK) <= (i // CHUNK)
        tiles.append(np.where(live, _rel_bucket_np(kpos - i), -1))
    far = _rel_bucket_np(np.asarray((min(ATT_NEAR) - 1) * ATT_BQ + ATT_BK - 1))
    assert int(far) == NUM_BUCKETS // 2 - 1
    return np.stack(tiles).astype(np.int32)


def _attn_kernel(lam_ref, rb_ref, bkt_ref, q_ref, k_ref, vt_ref, w_ref, o_ref,
                 bias_ref, qs_ref, s0_ref, s1_ref, p0_ref, p1_ref, a0_ref, a1_ref,
                 m_ref, l_ref, acc_ref, *, out_scale):
    S = q_ref.shape[0]
    nkb = S // ATT_BK
    W2 = 2 * ATT_BQ
    GW = ATT_G * W2
    near = tuple(range(ATT_GK)) + (-1,)
    NN = len(near)
    ZERO_IDX, NEG_IDX = NN, NN + 1
    lam = lam_ref[0]
    h = pl.program_id(1)

    far = rb_ref[NUM_BUCKETS // 2 - 1, h]
    for n, kbrel in enumerate(near):
        for j in range(ATT_G):
            d = (kbrel * ATT_BK) // ATT_BQ - j
            cols = slice(j * W2, (j + 1) * W2)
            if d > max(ATT_NEAR):
                bias_ref[n, :, cols] = jnp.full((ATT_BK, W2), NEG, F32)
            elif d < min(ATT_NEAR):
                bias_ref[n, :, cols] = jnp.zeros((ATT_BK, W2), F32)
    for t, dt in enumerate(ATT_NEAR):
        bk = bkt_ref[t]
        tile = jnp.full(bk.shape, NEG, F32)
        for bucket in range(NUM_BUCKETS):
            tile = jnp.where(bk == bucket, rb_ref[bucket, h] - far, tile)
        tile2 = jnp.concatenate([tile, tile], axis=1)
        for n, kbrel in enumerate(near):
            for j in range(ATT_G):
                if (kbrel * ATT_BK) // ATT_BQ - j == dt:
                    bias_ref[n, :, j * W2:(j + 1) * W2] = tile2
    bias_ref[ZERO_IDX] = jnp.zeros((ATT_BK, GW), F32)
    bias_ref[NEG_IDX] = jnp.full((ATT_BK, GW), NEG, F32)

    lane = lax.broadcasted_iota(jnp.int32, (ATT_BQ, HEAD_W), 1)
    first_map = lane < ATT_DH

    def group(g, carry):
        q0 = pl.multiple_of(g * (ATT_G * ATT_BQ), ATT_G * ATT_BQ)
        for j in range(ATT_G):
            q = q_ref[pl.ds(q0 + j * ATT_BQ, ATT_BQ), :]
            zero = jnp.zeros_like(q)
            qs_ref[j * W2:j * W2 + ATT_BQ, :] = jnp.where(first_map, q, zero)
            qs_ref[j * W2 + ATT_BQ:(j + 1) * W2, :] = jnp.where(first_map, zero, q)
        kb0 = g * ATT_GK
        n_far = jnp.maximum(kb0 - 1, 0)
        n_steps = NN + n_far

        def step_kb(n):
            kb = n - NN
            for i, kbrel in enumerate(near):
                kb = jnp.where(n == i, kb0 + kbrel, kb)
            return jnp.clip(kb, 0, nkb - 1)

        def step_bias(n):
            idx = jnp.where(n < n_steps, ZERO_IDX, NEG_IDX)
            for i, kbrel in enumerate(near):
                here = i if kbrel >= 0 else jnp.where(g > 0, i, NEG_IDX)
                idx = jnp.where(n == i, here, idx)
            return idx

        def scores(n, s_ref):
            kb = step_kb(n)
            kblk = k_ref[pl.ds(pl.multiple_of(kb * ATT_BK, ATT_BK), ATT_BK), :]
            bidx = step_bias(n)
            for j in range(ATT_G):
                cols = slice(j * W2, (j + 1) * W2)
                s = lax.dot_general(kblk, qs_ref[cols, :], (((1,), (1,)), ((), ())),
                                    preferred_element_type=F32)
                s_ref[:, cols] = s + bias_ref[bidx, :, cols]

        def soft(s_ref, p_ref, a_ref):
            for j in range(ATT_G):
                cols = slice(j * W2, (j + 1) * W2)
                s = s_ref[:, cols]
                m_old = m_ref[:, cols]
                m_new = jnp.maximum(m_old, jnp.max(s, axis=0, keepdims=True))
                alpha = jnp.exp(m_old - m_new)
                p = jnp.exp(s - m_new)
                l_ref[:, cols] = alpha * l_ref[:, cols] + jnp.sum(p, axis=0, keepdims=True)
                m_ref[:, cols] = m_new
                a_ref[:, cols] = alpha
                p_ref[:, cols] = p.astype(BF16)

        def accum(n, p_ref, a_ref):
            vtb = vt_ref[step_kb(n)]
            for j in range(ATT_G):
                cols = slice(j * W2, (j + 1) * W2)
                acc_ref[:, cols] = a_ref[:, cols] * acc_ref[:, cols] + jnp.dot(
                    vtb, p_ref[:, cols], preferred_element_type=F32)

        m_ref[...] = jnp.full(m_ref.shape, NEG, F32)
        l_ref[...] = jnp.zeros(l_ref.shape, F32)
        acc_ref[...] = jnp.zeros(acc_ref.shape, F32)
        s1_ref[...] = jnp.full(s1_ref.shape, 2.0 * NEG, F32)
        p0_ref[...] = jnp.zeros(p0_ref.shape, BF16)
        a0_ref[...] = jnp.ones(a0_ref.shape, F32)

        def two_steps(i, c):
            n = 2 * i
            accum(n - 2, p0_ref, a0_ref)
            scores(n, s0_ref)
            soft(s1_ref, p1_ref, a1_ref)
            scores(n + 1, s1_ref)
            accum(n - 1, p1_ref, a1_ref)
            soft(s0_ref, p0_ref, a0_ref)
            return c

        lax.fori_loop(0, (n_steps + 3) // 2, two_steps, 0)

        for j in range(ATT_G):
            cols = slice(j * W2, (j + 1) * W2)
            on = acc_ref[:, cols] * (1.0 / l_ref[:, cols])
            o = on[:, :ATT_BQ] - lam * on[:, ATT_BQ:]
            ms = jnp.mean(o * o, axis=0, keepdims=True)
            y = o * lax.rsqrt(ms + EPS) * w_ref[...] * out_scale
            o_ref[pl.ds(q0 + j * ATT_BQ, ATT_BQ), :] = y.T.astype(BF16)
        return carry

    lax.fori_loop(0, S // (ATT_G * ATT_BQ), group, 0)


def _attention(q, k, vt, rel_bias, subln_w, lam, out_scale):
    B, S, _ = q.shape
    nkb = S // ATT_BK
    GW = ATT_G * 2 * ATT_BQ
    bkt = jnp.asarray(_attn_bucket_tiles())
    head = pl.BlockSpec((None, S, HEAD_W), lambda b, h: (b, 0, h))
    smem = pl.BlockSpec(memory_space=pltpu.SMEM)
    return pl.pallas_call(
        functools.partial(_attn_kernel, out_scale=out_scale),
        grid=(B, ATT_HEADS),
        in_specs=[
            smem, smem, _const_spec(bkt.shape),
            head, head,
            pl.BlockSpec((None, nkb, HEAD_W, ATT_BK), lambda b, h: (b, 0, h, 0)),
            pl.BlockSpec((HEAD_W, 1), lambda b, h: (0, 0)),
        ],
        out_specs=head,
        out_shape=jax.ShapeDtypeStruct((B, S, ATT_HEADS * HEAD_W), BF16),
        scratch_shapes=[
            pltpu.VMEM((ATT_GK + 3, ATT_BK, GW), F32),
            pltpu.VMEM((GW, HEAD_W), BF16),
            pltpu.VMEM((ATT_BK, GW), F32),
            pltpu.VMEM((ATT_BK, GW), F32),
            pltpu.VMEM((ATT_BK, GW), BF16),
            pltpu.VMEM((ATT_BK, GW), BF16),
            pltpu.VMEM((1, GW), F32),
            pltpu.VMEM((1, GW), F32),
            pltpu.VMEM((1, GW), F32),
            pltpu.VMEM((1, GW), F32),
            pltpu.VMEM((HEAD_W, GW), F32),
        ],
        compiler_params=_cparams(2),
        name="diffattn",
    )(lam, rel_bias, bkt, q, k, vt, subln_w)


def _hgrn_kernel(hq_ref, kk_ref, hi_ref, sog_ref, logf_ref, w_ref, y_ref,
                 state_ref, kbuf_ref, bbuf_ref, *, tc):
    C = HG_C

    @pl.when(pl.program_id(1) == 0)
    def _():
        state_ref[...] = jnp.zeros_like(state_ref)

    kbuf_ref[0:HG_DIAG, :] = jnp.zeros((HG_DIAG, HEAD_W), F32)
    bbuf_ref[0:HG_DIAG, :] = jnp.zeros((HG_DIAG, HEAD_W), F32)

    row = lax.broadcasted_iota(jnp.int32, (C, HEAD_W), 0)
    r2 = lax.broadcasted_iota(jnp.int32, (C, C), 0)
    c2 = lax.broadcasted_iota(jnp.int32, (C, C), 1)
    tri = (c2 <= r2).astype(BF16)
    same_blk = (r2 // HG_DIAG) == (c2 // HG_DIAG)

    def chunk(c, carry):
        r0 = pl.multiple_of(c * C, C)
        for h in range(HG_HEADS):
            cols = slice(h * HEAD_W, (h + 1) * HEAD_W)
            lf = logf_ref[pl.ds(r0, C), cols]
            p0 = lf.astype(BF16)
            e1 = lf - p0.astype(F32)
            p1 = e1.astype(BF16)
            p2 = (e1 - p1.astype(F32)).astype(BF16)
            b = (jnp.dot(tri, p0, preferred_element_type=F32)
                 + jnp.dot(tri, p1, preferred_element_type=F32)
                 + jnp.dot(tri, p2, preferred_element_type=F32))
            q = hq_ref[pl.ds(r0, C), cols].astype(F32)
            kk = kk_ref[pl.ds(r0, C), cols].astype(F32)
            iv = hi_ref[pl.ds(r0, C), cols]
            st = state_ref[h]
            b_last = b[C - 1:C, :]

            inter = lax.dot_general((q * jnp.exp(b)).astype(BF16), st.astype(BF16),
                                    (((1,), (1,)), ((), ())), preferred_element_type=F32)

            a = jnp.zeros((C, C), F32)
            for hs in HG_LEVELS:
                parts = []
                for g in range(C // (2 * hs)):
                    r = g * 2 * hs + hs - 1
                    parts.append(jnp.broadcast_to(b[r:r + 1, :], (2 * hs, HEAD_W)))
                bref = parts[0] if len(parts) == 1 else jnp.concatenate(parts, axis=0)
                is_q = ((row // hs) % 2) == 1
                e = jnp.exp(jnp.where(is_q, b - bref, bref - b))
                ql = jnp.where(is_q, q * e, 0.0).astype(BF16)
                kl = jnp.where(is_q, 0.0, kk * e).astype(BF16)
                pl_ = lax.dot_general(ql, kl, (((1,), (1,)), ((), ())),
                                      preferred_element_type=F32)
                a = a + jnp.where((r2 // (2 * hs)) == (c2 // (2 * hs)), pl_, 0.0)
            kbuf_ref[HG_DIAG:, :] = kk
            bbuf_ref[HG_DIAG:, :] = b
            for d in range(HG_DIAG):
                ks = kbuf_ref[HG_DIAG - d:HG_DIAG - d + C, :]
                bs = bbuf_ref[HG_DIAG - d:HG_DIAG - d + C, :]
                term = q * ks * jnp.exp(jnp.minimum(b - bs, 0.0))
                dsum = jnp.sum(term, axis=-1, keepdims=True)
                a = a + jnp.where(((r2 - c2) == d) & same_blk, dsum, 0.0)

            o = inter + jnp.dot(a.astype(BF16), iv, preferred_element_type=F32)

            k_end = (kk * jnp.exp(b_last - b)).astype(BF16)
            upd = lax.dot_general(iv, k_end, (((0,), (0,)), ((), ())),
                                  preferred_element_type=F32)
            state_ref[h] = st * jnp.exp(b_last) + upd

            ms = jnp.mean(o * o, axis=-1, keepdims=True)
            y = o * lax.rsqrt(ms + EPS) * w_ref[...]
            y = y * sog_ref[pl.ds(r0, C), cols].astype(F32)
            y_ref[pl.ds(r0, C), cols] = y.astype(BF16)
        return carry

    lax.fori_loop(0, tc // C, chunk, 0)


def _hgrn(hq, kk, hi, sog, logf, norm_w, tc=512):
    B, S, W = hq.shape
    blk = pl.BlockSpec((None, tc, W), lambda b, s: (b, s, 0))
    return pl.pallas_call(
        functools.partial(_hgrn_kernel, tc=tc),
        grid=(B, S // tc),
        in_specs=[blk, blk, blk, blk, blk, pl.BlockSpec((1, HEAD_W), lambda b, s: (0, 0))],
        out_specs=blk,
        out_shape=jax.ShapeDtypeStruct((B, S, W), BF16),
        scratch_shapes=[pltpu.VMEM((HG_HEADS, HEAD_W, HEAD_W), F32),
                        pltpu.VMEM((HG_C + HG_DIAG, HEAD_W), F32),
                        pltpu.VMEM((HG_C + HG_DIAG, HEAD_W), F32)],
        compiler_params=_cparams(2),
        name="hgrn2",
    )(hq, kk, hi, sog, logf, norm_w)


def _layer_norm(z, g, b):
    mu = jnp.mean(z, axis=-1, keepdims=True)
    zc = z - mu
    var = jnp.mean(zc * zc, axis=-1, keepdims=True)
    return zc * lax.rsqrt(var + EPS) * g + b


def _merge_kernel(ya_ref, yh_ref, gates_ref, x_ref, wa_ref, wh_ref, wo_ref,
                  g_ref, b_ref, o_ref, *, alpha):
    D = x_ref.shape[-1]
    pa = jnp.dot(ya_ref[...], wa_ref[...], preferred_element_type=F32)
    ph = jnp.dot(yh_ref[...], wh_ref[...], preferred_element_type=F32)
    merged = (gates_ref[:, :D].astype(F32) * pa + gates_ref[:, D:].astype(F32) * ph)
    mix = jnp.dot(merged.astype(BF16), wo_ref[...], preferred_element_type=F32)
    o_ref[...] = _layer_norm(alpha * x_ref[...] + mix, g_ref[...], b_ref[...])


def _merge(ya, yh, gates, x, wa, wh, wo, g, b, alpha, tm=512):
    N, D = x.shape
    row = lambda w: pl.BlockSpec((tm, w), lambda i: (i, 0))
    return pl.pallas_call(
        functools.partial(_merge_kernel, alpha=alpha),
        grid=(N // tm,),
        in_specs=[row(ya.shape[1]), row(yh.shape[1]), row(gates.shape[1]), row(D),
                  _const_spec(wa.shape), _const_spec(wh.shape), _const_spec(wo.shape),
                  _const_spec(g.shape), _const_spec(b.shape)],
        out_specs=row(D),
        out_shape=jax.ShapeDtypeStruct((N, D), F32),
        compiler_params=_cparams(1),
        name="merge_ln",
    )(ya, yh, gates, x, wa, wh, wo, g, b)


def _ffn_kernel(h_ref, wg_ref, wu_ref, cw_ref, cb_ref, wd_ref, g_ref, b_ref, o_ref,
                tail_ref, gbuf_ref, *, alpha, tm):
    d_ff = wg_ref.shape[1]
    PAD = 8

    @pl.when(pl.program_id(1) == 0)
    def _():
        tail_ref[...] = jnp.zeros_like(tail_ref)

    h = h_ref[...]
    hb = h.astype(BF16)
    acc = jnp.zeros(h.shape, F32)
    for fc in range(d_ff // FFN_FC):
        cols = slice(fc * FFN_FC, (fc + 1) * FFN_FC)
        gte = jnp.dot(hb, wg_ref[:, cols], preferred_element_type=F32)
        up = jnp.dot(hb, wu_ref[:, cols], preferred_element_type=F32)
        gbuf_ref[0:PAD, :] = tail_ref[:, cols]
        gbuf_ref[PAD:, :] = gte
        tail_ref[:, cols] = gte[tm - PAD:, :]
        conv = (gte * cw_ref[2:3, cols]
                + gbuf_ref[PAD - 1:PAD - 1 + tm, :] * cw_ref[1:2, cols]
                + gbuf_ref[PAD - 2:PAD - 2 + tm, :] * cw_ref[0:1, cols]
                + cb_ref[:, cols])
        act = jax.nn.gelu(conv) * up
        acc = acc + jnp.dot(act.astype(BF16), wd_ref[cols, :], preferred_element_type=F32)
    o_ref[...] = _layer_norm(alpha * h + acc, g_ref[...], b_ref[...])


def _ffn(h, wg, wu, cw, cb, wd, g, b, alpha, tm=512):
    B, S, D = h.shape
    d_ff = wg.shape[1]
    row = pl.BlockSpec((None, tm, D), lambda bi, s: (bi, s, 0))
    return pl.pallas_call(
        functools.partial(_ffn_kernel, alpha=alpha, tm=tm),
        grid=(B, S // tm),
        in_specs=[row, _const_spec(wg.shape), _const_spec(wu.shape), _const_spec(cw.shape),
                  _const_spec(cb.shape), _const_spec(wd.shape),
                  _const_spec(g.shape), _const_spec(b.shape)],
        out_specs=row,
        out_shape=jax.ShapeDtypeStruct((B, S, D), F32),
        scratch_shapes=[pltpu.VMEM((8, d_ff), F32),
                        pltpu.VMEM((tm + 8, FFN_FC), F32)],
        compiler_params=_cparams(2),
        name="ffn_ln",
    )(h, wg, wu, cw, cb, wd, g, b)


def kernel(x, w_in, b_gate, lambda_q1, lambda_k1, lambda_q2, lambda_k2, attn_subln_w,
           rel_bias, hgrn_lb_logits, hgrn_norm_w, w_branch_attn, w_branch_hgrn, w_out,
           ln1_g, ln1_b, w_ffn_gate, w_ffn_up, ffn_conv_w, ffn_conv_b, w_ffn_down,
           ln2_g, ln2_b):
    B, S, D = x.shape
    depth = w_in.shape[0]
    att_qk = ATT_HEADS * 2 * ATT_DH
    att_w = ATT_HEADS * HEAD_W
    hg_w = HG_HEADS * HEAD_W
    offs = {"aq": 0, "ak": att_qk, "av": 2 * att_qk}
    offs["hq"] = offs["av"] + att_w
    offs["hf"] = offs["hq"] + hg_w
    offs["hi"] = offs["hf"] + hg_w
    offs["hg"] = offs["hi"] + hg_w
    offs["gate"] = offs["hg"] + hg_w
    alpha = (2 * depth) ** 0.25

    lb_all = jnp.cumsum(jax.nn.softmax(hgrn_lb_logits.astype(F32), axis=0), axis=0)

    h = x
    for l in range(depth):
        lam_init = 0.8 - 0.6 * math.exp(-0.3 * l)
        lam = (jnp.exp(jnp.sum(lambda_q1[l].astype(F32) * lambda_k1[l].astype(F32)))
               - jnp.exp(jnp.sum(lambda_q2[l].astype(F32) * lambda_k2[l].astype(F32)))
               + lam_init).reshape(1)

        w_l = w_in[l]
        w_bf = w_l.astype(BF16)
        wvt_bf = w_l[:, offs["av"]:offs["hq"]].T.astype(BF16)
        (q, k, vt, hq, kk, hi, sog, logf, gates) = _inproj(
            h, w_bf, wvt_bf, b_gate[l][None, :], lb_all[l][None, :], offs)

        ya = _attention(q, k, vt, rel_bias.astype(F32),
                        attn_subln_w[l][:, None].astype(F32), lam, 1.0 - lam_init)
        yh = _hgrn(hq, kk, hi, sog, logf, hgrn_norm_w[l][None, :].astype(F32))

        h1 = _merge(ya.reshape(B * S, att_w), yh.reshape(B * S, hg_w),
                    gates.reshape(B * S, 2 * D), h.reshape(B * S, D),
                    w_branch_attn[l].astype(BF16), w_branch_hgrn[l].astype(BF16),
                    w_out[l].astype(BF16), ln1_g[l][None, :], ln1_b[l][None, :], alpha)

        h = _ffn(h1.reshape(B, S, D), w_ffn_gate[l].astype(BF16), w_ffn_up[l].astype(BF16),
                 ffn_conv_w[l], ffn_conv_b[l][None, :], w_ffn_down[l].astype(BF16),
                 ln2_g[l][None, :], ln2_b[l][None, :], alpha)
    return h
```

```python
import functools
import math

import numpy as np
import jax
import jax.numpy as jnp
from jax import lax
from jax.experimental import pallas as pl
from jax.experimental.pallas import tpu as pltpu

F32 = jnp.float32
BF16 = jnp.bfloat16

CHUNK = 64
ATT_HEADS = 4
ATT_DH = 64
HG_HEADS = 4
HEAD_W = 128
NUM_BUCKETS = 32
MAX_DISTANCE = 128
EPS = 1e-5
NEG = -1e30

ATT_BQ = 128
ATT_BK = 256
ATT_G = 4
ATT_GK = ATT_G * ATT_BQ // ATT_BK
ATT_NEAR = (0, -1, -2)
ATT_VR = HEAD_W + 16
LOG2E = math.log2(math.e)
HG_C = 64
HG_LEVELS = (32, 16, 8)
HG_DIAG = 8
FFN_FC = 256

VMEM_LIMIT = 56 * 1024 * 1024


def _cparams(n_axes):
    return pltpu.CompilerParams(
        dimension_semantics=("arbitrary",) * n_axes,
        vmem_limit_bytes=VMEM_LIMIT)


def _const_spec(shape):
    nd = len(shape)
    return pl.BlockSpec(shape, lambda *_: (0,) * nd, pipeline_mode=pl.Buffered(1))


def _inproj_kernel(x_ref, w_ref, wvt_ref, bg_ref, lb_ref,
                   q_ref, k_ref, vt_ref, hq_ref, kk_ref, hi_ref, sog_ref,
                   logf_ref, gates_ref, *, tm, offs):
    xb = x_ref[...].astype(BF16)

    def mm(off, width=512):
        return jnp.dot(xb, w_ref[:, off:off + width], preferred_element_type=F32)

    q_ref[...] = (mm(offs["aq"]) * (ATT_DH ** -0.5 * LOG2E)).astype(BF16)
    k_ref[...] = mm(offs["ak"]).astype(BF16)
    for t in range(tm // ATT_BK):
        vt = lax.dot_general(wvt_ref[...], xb[t * ATT_BK:(t + 1) * ATT_BK],
                             (((1,), (1,)), ((), ())), preferred_element_type=F32)
        for hd in range(ATT_HEADS):
            vt_ref[t, hd * ATT_VR:hd * ATT_VR + HEAD_W, :] = (
                vt[hd * HEAD_W:(hd + 1) * HEAD_W].astype(BF16))
            vt_ref[t, hd * ATT_VR + HEAD_W:(hd + 1) * ATT_VR, :] = jnp.ones(
                (ATT_VR - HEAD_W, ATT_BK), BF16)
    hq_ref[...] = mm(offs["hq"]).astype(BF16)
    sig = jax.nn.sigmoid(mm(offs["hf"]))
    lb = lb_ref[...]
    logf_ref[...] = jnp.log(lb + (1.0 - lb) * sig)
    kk_ref[...] = ((1.0 - lb) * (1.0 - sig)).astype(BF16)
    hi_ref[...] = mm(offs["hi"]).astype(BF16)
    hog = mm(offs["hg"])
    sog_ref[...] = (hog * jax.nn.sigmoid(hog)).astype(BF16)
    for j in range(gates_ref.shape[-1] // 512):
        g = mm(offs["gate"] + j * 512) + bg_ref[:, j * 512:(j + 1) * 512]
        gates_ref[:, j * 512:(j + 1) * 512] = jax.nn.sigmoid(g).astype(BF16)


def _inproj(x, w_bf, wvt_bf, b_gate, lb, offs, tm=512):
    B, S, D = x.shape
    ncols = w_bf.shape[1]
    gate_w = ncols - offs["gate"]
    row = lambda w: pl.BlockSpec((None, tm, w), lambda b, s: (b, s, 0))
    out_shape = (
        jax.ShapeDtypeStruct((B, S, 512), BF16),
        jax.ShapeDtypeStruct((B, S, 512), BF16),
        jax.ShapeDtypeStruct((B, S // ATT_BK, ATT_HEADS * ATT_VR, ATT_BK), BF16),
        jax.ShapeDtypeStruct((B, S, 512), BF16),
        jax.ShapeDtypeStruct((B, S, 512), BF16),
        jax.ShapeDtypeStruct((B, S, 512), BF16),
        jax.ShapeDtypeStruct((B, S, 512), BF16),
        jax.ShapeDtypeStruct((B, S, 512), F32),
        jax.ShapeDtypeStruct((B, S, gate_w), BF16),
    )
    out_specs = (
        row(512), row(512),
        pl.BlockSpec((None, tm // ATT_BK, ATT_HEADS * ATT_VR, ATT_BK),
                     lambda b, s: (b, s, 0, 0)),
        row(512), row(512), row(512), row(512), row(512), row(gate_w),
    )
    return pl.pallas_call(
        functools.partial(_inproj_kernel, tm=tm, offs=offs),
        grid=(B, S // tm),
        in_specs=[row(D), _const_spec(w_bf.shape), _const_spec(wvt_bf.shape),
                  _const_spec(b_gate.shape), _const_spec(lb.shape)],
        out_specs=out_specs,
        out_shape=out_shape,
        compiler_params=_cparams(2),
        name="inproj",
    )(x, w_bf, wvt_bf, b_gate, lb)


def _rel_bucket_np(rel):
    nb = NUM_BUCKETS // 2
    max_exact = nb // 2
    ret = np.where(rel > 0, nb, 0)
    n = np.abs(rel)
    nf = np.maximum(n, 1).astype(np.float32)
    large = max_exact + (np.log(nf / np.float32(max_exact))
                         / np.float32(math.log(MAX_DISTANCE / max_exact))
                         * np.float32(nb - max_exact)).astype(np.int32)
    large = np.minimum(large, nb - 1)
    return ret + np.where(n < max_exact, n, large)


def _attn_bucket_tiles():
    i = np.arange(ATT_BQ)[None, :]
    j = np.arange(ATT_BK)[:, None]
    tiles = []
    for d in ATT_NEAR:
        kpos = d * ATT_BQ + j
        live = (kpos // CHUNK) <= (i // CHUNK)
        tiles.append(np.where(live, _rel_bucket_np(kpos - i), -1))
    far = _rel_bucket_np(np.asarray((min(ATT_NEAR) - 1) * ATT_BQ + ATT_BK - 1))
    assert int(far) == NUM_BUCKETS // 2 - 1
    return np.stack(tiles).astype(np.int32)


def _attn_kernel(lam_ref, rb_ref, bkt_ref, q_ref, k_ref, vt_ref, w_ref, o_ref,
                 bias_ref, qs_ref, s0_ref, s1_ref, p0_ref, p1_ref, a0_ref, a1_ref,
                 m_ref, acc_ref, *, out_scale):
    S = q_ref.shape[0]
    nkb = S // ATT_BK
    W2 = 2 * ATT_BQ
    near = tuple(range(ATT_GK)) + (-1,)
    lam = lam_ref[0]
    h = pl.program_id(1)

    far = rb_ref[NUM_BUCKETS // 2 - 1, h]
    for n, kbrel in enumerate(near):
        for j in range(ATT_G):
            d = (kbrel * ATT_BK) // ATT_BQ - j
            cols = slice(j * W2, (j + 1) * W2)
            if d > max(ATT_NEAR):
                bias_ref[n, :, cols] = jnp.full((ATT_BK, W2), NEG, F32)
            elif d < min(ATT_NEAR):
                bias_ref[n, :, cols] = jnp.zeros((ATT_BK, W2), F32)
    for t, dt in enumerate(ATT_NEAR):
        bk = bkt_ref[t]
        tile = jnp.full(bk.shape, NEG, F32)
        for bucket in range(NUM_BUCKETS):
            tile = jnp.where(bk == bucket, (rb_ref[bucket, h] - far) * LOG2E, tile)
        tile2 = jnp.concatenate([tile, tile], axis=1)
        for n, kbrel in enumerate(near):
            for j in range(ATT_G):
                if (kbrel * ATT_BK) // ATT_BQ - j == dt:
                    bias_ref[n, :, j * W2:(j + 1) * W2] = tile2

    lane = lax.broadcasted_iota(jnp.int32, (ATT_BQ, HEAD_W), 1)
    first_map = lane < ATT_DH

    def scores(kb, s_ref, bias_idx):
        kblk = k_ref[pl.ds(pl.multiple_of(kb * ATT_BK, ATT_BK), ATT_BK), :]
        for j in range(ATT_G):
            cols = slice(j * W2, (j + 1) * W2)
            s = lax.dot_general(kblk, qs_ref[cols, :], (((1,), (1,)), ((), ())),
                                preferred_element_type=F32)
            if bias_idx is not None:
                s = s + bias_ref[bias_idx, :, cols]
            s_ref[:, cols] = s

    def soft(s_ref, p_ref, a_ref, first=False):
        for j in range(ATT_G):
            cols = slice(j * W2, (j + 1) * W2)
            s = s_ref[:, cols]
            m_new = jnp.max(s, axis=0, keepdims=True)
            if first:
                a_ref[:, cols] = jnp.ones_like(m_new)
            else:
                m_old = m_ref[:, cols]
                m_new = jnp.maximum(m_old, m_new)
                a_ref[:, cols] = jnp.exp2(m_old - m_new)
            m_ref[:, cols] = m_new
            p_ref[:, cols] = jnp.exp2(s - m_new).astype(BF16)

    def accum(kb, p_ref, a_ref):
        vtb = vt_ref[kb]
        for j in range(ATT_G):
            cols = slice(j * W2, (j + 1) * W2)
            acc_ref[:, cols] = a_ref[:, cols] * acc_ref[:, cols] + jnp.dot(
                vtb, p_ref[:, cols], preferred_element_type=F32)

    def group(g, carry):
        q0 = pl.multiple_of(g * (ATT_G * ATT_BQ), ATT_G * ATT_BQ)
        for j in range(ATT_G):
            q = q_ref[pl.ds(q0 + j * ATT_BQ, ATT_BQ), :]
            zero = jnp.zeros_like(q)
            qs_ref[j * W2:j * W2 + ATT_BQ, :] = jnp.where(first_map, q, zero)
            qs_ref[j * W2 + ATT_BQ:(j + 1) * W2, :] = jnp.where(first_map, zero, q)
        kb0 = g * ATT_GK
        acc_ref[...] = jnp.zeros(acc_ref.shape, F32)

        n_steps = jnp.where(g > 0, ATT_GK + 1 + (kb0 - 1), ATT_GK)

        def step_kb(n):
            kb = n - (ATT_GK + 1)
            for i, kbrel in enumerate(near):
                kb = jnp.where(n == i, kb0 + kbrel, kb)
            return jnp.clip(kb, 0, nkb - 1)

        scores(kb0, s0_ref, 0)
        scores(kb0 + 1, s1_ref, 1)
        soft(s0_ref, p0_ref, a0_ref, first=True)

        @pl.when(g > 0)
        def _():
            accum(kb0, p0_ref, a0_ref)
            scores(kb0 - 1, s0_ref, 2)
            soft(s1_ref, p1_ref, a1_ref)
            scores(0, s1_ref, None)
            accum(kb0 + 1, p1_ref, a1_ref)
            soft(s0_ref, p0_ref, a0_ref)

        def far_steps(i, c):
            n = 2 * i
            accum(step_kb(n - 2), p0_ref, a0_ref)
            scores(n - 3, s0_ref, None)
            soft(s1_ref, p1_ref, a1_ref)
            scores(n - 2, s1_ref, None)
            accum(n - 4, p1_ref, a1_ref)
            soft(s0_ref, p0_ref, a0_ref)
            return c

        lax.fori_loop(2, n_steps // 2, far_steps, 0)

        accum(step_kb(n_steps - 2), p0_ref, a0_ref)
        soft(s1_ref, p1_ref, a1_ref)
        accum(step_kb(n_steps - 1), p1_ref, a1_ref)

        for j in range(ATT_G):
            cols = slice(j * W2, (j + 1) * W2)
            on = acc_ref[:HEAD_W, cols] * (1.0 / acc_ref[HEAD_W:HEAD_W + 1, cols])
            o = on[:, :ATT_BQ] - lam * on[:, ATT_BQ:]
            ms = jnp.mean(o * o, axis=0, keepdims=True)
            y = o * lax.rsqrt(ms + EPS) * w_ref[...] * out_scale
            o_ref[pl.ds(q0 + j * ATT_BQ, ATT_BQ), :] = y.T.astype(BF16)
        return carry

    lax.fori_loop(0, S // (ATT_G * ATT_BQ), group, 0)


def _attention(q, k, vt, rel_bias, subln_w, lam, out_scale):
    B, S, _ = q.shape
    nkb = S // ATT_BK
    GW = ATT_G * 2 * ATT_BQ
    assert ATT_GK == 2 and ATT_BK == 2 * ATT_BQ
    bkt = jnp.asarray(_attn_bucket_tiles())
    head = pl.BlockSpec((None, S, HEAD_W), lambda b, h: (b, 0, h))
    smem = pl.BlockSpec(memory_space=pltpu.SMEM)
    return pl.pallas_call(
        functools.partial(_attn_kernel, out_scale=out_scale),
        grid=(B, ATT_HEADS),
        in_specs=[
            smem, smem, _const_spec(bkt.shape),
            head, head,
            pl.BlockSpec((None, nkb, ATT_VR, ATT_BK), lambda b, h: (b, 0, h, 0)),
            pl.BlockSpec((HEAD_W, 1), lambda b, h: (0, 0)),
        ],
        out_specs=head,
        out_shape=jax.ShapeDtypeStruct((B, S, ATT_HEADS * HEAD_W), BF16),
        scratch_shapes=[
            pltpu.VMEM((ATT_GK + 1, ATT_BK, GW), F32),
            pltpu.VMEM((GW, HEAD_W), BF16),
            pltpu.VMEM((ATT_BK, GW), F32),
            pltpu.VMEM((ATT_BK, GW), F32),
            pltpu.VMEM((ATT_BK, GW), BF16),
            pltpu.VMEM((ATT_BK, GW), BF16),
            pltpu.VMEM((1, GW), F32),
            pltpu.VMEM((1, GW), F32),
            pltpu.VMEM((1, GW), F32),
            pltpu.VMEM((ATT_VR, GW), F32),
        ],
        compiler_params=_cparams(2),
        name="diffattn",
    )(lam, rel_bias, bkt, q, k, vt, subln_w)


def _hgrn_kernel(hq_ref, kk_ref, hi_ref, sog_ref, logf_ref, w_ref, y_ref,
                 state_ref, kbuf_ref, bbuf_ref, *, tc):
    C = HG_C

    @pl.when(pl.program_id(1) == 0)
    def _():
        state_ref[...] = jnp.zeros_like(state_ref)

    kbuf_ref[0:HG_DIAG, :] = jnp.zeros((HG_DIAG, HEAD_W), F32)
    bbuf_ref[0:HG_DIAG, :] = jnp.zeros((HG_DIAG, HEAD_W), F32)

    row = lax.broadcasted_iota(jnp.int32, (C, HEAD_W), 0)
    r2 = lax.broadcasted_iota(jnp.int32, (C, C), 0)
    c2 = lax.broadcasted_iota(jnp.int32, (C, C), 1)
    tri = (c2 <= r2).astype(BF16)
    same_blk = (r2 // HG_DIAG) == (c2 // HG_DIAG)

    def chunk(c, carry):
        r0 = pl.multiple_of(c * C, C)
        for h in range(HG_HEADS):
            cols = slice(h * HEAD_W, (h + 1) * HEAD_W)
            lf = logf_ref[pl.ds(r0, C), cols]
            p0 = lf.astype(BF16)
            e1 = lf - p0.astype(F32)
            p1 = e1.astype(BF16)
            p2 = (e1 - p1.astype(F32)).astype(BF16)
            b = (jnp.dot(tri, p0, preferred_element_type=F32)
                 + jnp.dot(tri, p1, preferred_element_type=F32)
                 + jnp.dot(tri, p2, preferred_element_type=F32))
            q = hq_ref[pl.ds(r0, C), cols].astype(F32)
            kk = kk_ref[pl.ds(r0, C), cols].astype(F32)
            iv = hi_ref[pl.ds(r0, C), cols]
            st = state_ref[h]
            b_last = b[C - 1:C, :]

            inter = lax.dot_general((q * jnp.exp(b)).astype(BF16), st.astype(BF16),
                                    (((1,), (1,)), ((), ())), preferred_element_type=F32)

            a = jnp.zeros((C, C), F32)
            for hs in HG_LEVELS:
                parts = []
                for g in range(C // (2 * hs)):
                    r = g * 2 * hs + hs - 1
                    parts.append(jnp.broadcast_to(b[r:r + 1, :], (2 * hs, HEAD_W)))
                bref = parts[0] if len(parts) == 1 else jnp.concatenate(parts, axis=0)
                is_q = ((row // hs) % 2) == 1
                e = jnp.exp(jnp.where(is_q, b - bref, bref - b))
                ql = jnp.where(is_q, q * e, 0.0).astype(BF16)
                kl = jnp.where(is_q, 0.0, kk * e).astype(BF16)
                pl_ = lax.dot_general(ql, kl, (((1,), (1,)), ((), ())),
                                      preferred_element_type=F32)
                a = a + jnp.where((r2 // (2 * hs)) == (c2 // (2 * hs)), pl_, 0.0)
            kbuf_ref[HG_DIAG:, :] = kk
            bbuf_ref[HG_DIAG:, :] = b
            for d in range(HG_DIAG):
                ks = kbuf_ref[HG_DIAG - d:HG_DIAG - d + C, :]
                bs = bbuf_ref[HG_DIAG - d:HG_DIAG - d + C, :]
                term = q * ks * jnp.exp(jnp.minimum(b - bs, 0.0))
                dsum = jnp.sum(term, axis=-1, keepdims=True)
                a = a + jnp.where(((r2 - c2) == d) & same_blk, dsum, 0.0)

            o = inter + jnp.dot(a.astype(BF16), iv, preferred_element_type=F32)

            k_end = (kk * jnp.exp(b_last - b)).astype(BF16)
            upd = lax.dot_general(iv, k_end, (((0,), (0,)), ((), ())),
                                  preferred_element_type=F32)
            state_ref[h] = st * jnp.exp(b_last) + upd

            ms = jnp.mean(o * o, axis=-1, keepdims=True)
            y = o * lax.rsqrt(ms + EPS) * w_ref[...]
            y = y * sog_ref[pl.ds(r0, C), cols].astype(F32)
            y_ref[pl.ds(r0, C), cols] = y.astype(BF16)
        return carry

    lax.fori_loop(0, tc // C, chunk, 0)


def _hgrn(hq, kk, hi, sog, logf, norm_w, tc=512):
    B, S, W = hq.shape
    blk = pl.BlockSpec((None, tc, W), lambda b, s: (b, s, 0))
    return pl.pallas_call(
        functools.partial(_hgrn_kernel, tc=tc),
        grid=(B, S // tc),
        in_specs=[blk, blk, blk, blk, blk, pl.BlockSpec((1, HEAD_W), lambda b, s: (0, 0))],
        out_specs=blk,
        out_shape=jax.ShapeDtypeStruct((B, S, W), BF16),
        scratch_shapes=[pltpu.VMEM((HG_HEADS, HEAD_W, HEAD_W), F32),
                        pltpu.VMEM((HG_C + HG_DIAG, HEAD_W), F32),
                        pltpu.VMEM((HG_C + HG_DIAG, HEAD_W), F32)],
        compiler_params=_cparams(2),
        name="hgrn2",
    )(hq, kk, hi, sog, logf, norm_w)


def _layer_norm(z, g, b):
    mu = jnp.mean(z, axis=-1, keepdims=True)
    zc = z - mu
    var = jnp.mean(zc * zc, axis=-1, keepdims=True)
    return zc * lax.rsqrt(var + EPS) * g + b


def _merge_kernel(ya_ref, yh_ref, gates_ref, x_ref, wa_ref, wh_ref, wo_ref,
                  g_ref, b_ref, o_ref, *, alpha):
    D = x_ref.shape[-1]
    pa = jnp.dot(ya_ref[...], wa_ref[...], preferred_element_type=F32)
    ph = jnp.dot(yh_ref[...], wh_ref[...], preferred_element_type=F32)
    merged = (gates_ref[:, :D].astype(F32) * pa + gates_ref[:, D:].astype(F32) * ph)
    mix = jnp.dot(merged.astype(BF16), wo_ref[...], preferred_element_type=F32)
    o_ref[...] = _layer_norm(alpha * x_ref[...] + mix, g_ref[...], b_ref[...])


def _merge(ya, yh, gates, x, wa, wh, wo, g, b, alpha, tm=512):
    N, D = x.shape
    row = lambda w: pl.BlockSpec((tm, w), lambda i: (i, 0))
    return pl.pallas_call(
        functools.partial(_merge_kernel, alpha=alpha),
        grid=(N // tm,),
        in_specs=[row(ya.shape[1]), row(yh.shape[1]), row(gates.shape[1]), row(D),
                  _const_spec(wa.shape), _const_spec(wh.shape), _const_spec(wo.shape),
                  _const_spec(g.shape), _const_spec(b.shape)],
        out_specs=row(D),
        out_shape=jax.ShapeDtypeStruct((N, D), F32),
        compiler_params=_cparams(1),
        name="merge_ln",
    )(ya, yh, gates, x, wa, wh, wo, g, b)


def _ffn_kernel(h_ref, wg_ref, wu_ref, cw_ref, cb_ref, wd_ref, g_ref, b_ref, o_ref,
                tail_ref, gbuf_ref, *, alpha, tm):
    d_ff = wg_ref.shape[1]
    PAD = 8

    @pl.when(pl.program_id(1) == 0)
    def _():
        tail_ref[...] = jnp.zeros_like(tail_ref)

    h = h_ref[...]
    hb = h.astype(BF16)
    acc = jnp.zeros(h.shape, F32)
    for fc in range(d_ff // FFN_FC):
        cols = slice(fc * FFN_FC, (fc + 1) * FFN_FC)
        gte = jnp.dot(hb, wg_ref[:, cols], preferred_element_type=F32)
        up = jnp.dot(hb, wu_ref[:, cols], preferred_element_type=F32)
        gbuf_ref[0:PAD, :] = tail_ref[:, cols]
        gbuf_ref[PAD:, :] = gte
        tail_ref[:, cols] = gte[tm - PAD:, :]
        conv = (gte * cw_ref[2:3, cols]
                + gbuf_ref[PAD - 1:PAD - 1 + tm, :] * cw_ref[1:2, cols]
                + gbuf_ref[PAD - 2:PAD - 2 + tm, :] * cw_ref[0:1, cols]
                + cb_ref[:, cols])
        act = jax.nn.gelu(conv) * up
        acc = acc + jnp.dot(act.astype(BF16), wd_ref[cols, :], preferred_element_type=F32)
    o_ref[...] = _layer_norm(alpha * h + acc, g_ref[...], b_ref[...])


def _ffn(h, wg, wu, cw, cb, wd, g, b, alpha, tm=512):
    B, S, D = h.shape
    d_ff = wg.shape[1]
    row = pl.BlockSpec((None, tm, D), lambda bi, s: (bi, s, 0))
    return pl.pallas_call(
        functools.partial(_ffn_kernel, alpha=alpha, tm=tm),
        grid=(B, S // tm),
        in_specs=[row, _const_spec(wg.shape), _const_spec(wu.shape), _const_spec(cw.shape),
                  _const_spec(cb.shape), _const_spec(wd.shape),
                  _const_spec(g.shape), _const_spec(b.shape)],
        out_specs=row,
        out_shape=jax.ShapeDtypeStruct((B, S, D), F32),
        scratch_shapes=[pltpu.VMEM((8, d_ff), F32),
                        pltpu.VMEM((tm + 8, FFN_FC), F32)],
        compiler_params=_cparams(2),
        name="ffn_ln",
    )(h, wg, wu, cw, cb, wd, g, b)


def kernel(x, w_in, b_gate, lambda_q1, lambda_k1, lambda_q2, lambda_k2, attn_subln_w,
           rel_bias, hgrn_lb_logits, hgrn_norm_w, w_branch_attn, w_branch_hgrn, w_out,
           ln1_g, ln1_b, w_ffn_gate, w_ffn_up, ffn_conv_w, ffn_conv_b, w_ffn_down,
           ln2_g, ln2_b):
    B, S, D = x.shape
    depth = w_in.shape[0]
    att_qk = ATT_HEADS * 2 * ATT_DH
    att_w = ATT_HEADS * HEAD_W
    hg_w = HG_HEADS * HEAD_W
    offs = {"aq": 0, "ak": att_qk, "av": 2 * att_qk}
    offs["hq"] = offs["av"] + att_w
    offs["hf"] = offs["hq"] + hg_w
    offs["hi"] = offs["hf"] + hg_w
    offs["hg"] = offs["hi"] + hg_w
    offs["gate"] = offs["hg"] + hg_w
    alpha = (2 * depth) ** 0.25

    lb_all = jnp.cumsum(jax.nn.softmax(hgrn_lb_logits.astype(F32), axis=0), axis=0)

    h = x
    for l in range(depth):
        lam_init = 0.8 - 0.6 * math.exp(-0.3 * l)
        lam = (jnp.exp(jnp.sum(lambda_q1[l].astype(F32) * lambda_k1[l].astype(F32)))
               - jnp.exp(jnp.sum(lambda_q2[l].astype(F32) * lambda_k2[l].astype(F32)))
               + lam_init).reshape(1)

        w_l = w_in[l]
        w_bf = w_l.astype(BF16)
        wvt_bf = w_l[:, offs["av"]:offs["hq"]].T.astype(BF16)
        (q, k, vt, hq, kk, hi, sog, logf, gates) = _inproj(
            h, w_bf, wvt_bf, b_gate[l][None, :], lb_all[l][None, :], offs)

        ya = _attention(q, k, vt, rel_bias.astype(F32),
                        attn_subln_w[l][:, None].astype(F32), lam, 1.0 - lam_init)
        yh = _hgrn(hq, kk, hi, sog, logf, hgrn_norm_w[l][None, :].astype(F32))

        h1 = _merge(ya.reshape(B * S, att_w), yh.reshape(B * S, hg_w),
                    gates.reshape(B * S, 2 * D), h.reshape(B * S, D),
                    w_branch_attn[l].astype(BF16), w_branch_hgrn[l].astype(BF16),
                    w_out[l].astype(BF16), ln1_g[l][None, :], ln1_b[l][None, :], alpha)

        h = _ffn(h1.reshape(B, S, D), w_ffn_gate[l].astype(BF16), w_ffn_up[l].astype(BF16),
                 ffn_conv_w[l], ffn_conv_b[l][None, :], w_ffn_down[l].astype(BF16),
                 ln2_g[l][None, :], ln2_b[l][None, :], alpha)
    return h
```

```python
import functools
import math

import numpy as np
import jax
import jax.numpy as jnp
from jax import lax
from jax.experimental import pallas as pl
from jax.experimental.pallas import tpu as pltpu

F32 = jnp.float32
BF16 = jnp.bfloat16

CHUNK = 64
ATT_HEADS = 4
ATT_DH = 64
HG_HEADS = 4
HEAD_W = 128
NUM_BUCKETS = 32
MAX_DISTANCE = 128
EPS = 1e-5
NEG = -1e30

ATT_BQ = 128
ATT_BK = 256
ATT_G = 4
ATT_GK = ATT_G * ATT_BQ // ATT_BK
ATT_NEAR = (0, -1, -2)
ATT_VR = HEAD_W + 16
LOG2E = math.log2(math.e)
HG_C = 64
HG_LEVELS = (32, 16, 8)
HG_DIAG = 8
FFN_FC = 256
ROW_CHUNK = 256

VMEM_LIMIT = 56 * 1024 * 1024


def _cparams(n_axes):
    return pltpu.CompilerParams(
        dimension_semantics=("arbitrary",) * n_axes,
        vmem_limit_bytes=VMEM_LIMIT)


def _const_spec(shape):
    nd = len(shape)
    return pl.BlockSpec(shape, lambda *_: (0,) * nd, pipeline_mode=pl.Buffered(1))


def _inproj_kernel(x_ref, w_ref, wvt_ref, bg_ref, lb_ref,
                   q_ref, k_ref, vt_ref, hq_ref, kk_ref, hi_ref, sog_ref,
                   logf_ref, gates_ref, *, tm, offs):
    xb = x_ref[...].astype(BF16)

    def mm(off, width=512):
        return jnp.dot(xb, w_ref[:, off:off + width], preferred_element_type=F32)

    q_ref[...] = (mm(offs["aq"]) * (ATT_DH ** -0.5 * LOG2E)).astype(BF16)
    k_ref[...] = mm(offs["ak"]).astype(BF16)
    for t in range(tm // ATT_BK):
        vt = lax.dot_general(wvt_ref[...], xb[t * ATT_BK:(t + 1) * ATT_BK],
                             (((1,), (1,)), ((), ())), preferred_element_type=F32)
        for hd in range(ATT_HEADS):
            vt_ref[t, hd * ATT_VR:hd * ATT_VR + HEAD_W, :] = (
                vt[hd * HEAD_W:(hd + 1) * HEAD_W].astype(BF16))
            vt_ref[t, hd * ATT_VR + HEAD_W:(hd + 1) * ATT_VR, :] = jnp.ones(
                (ATT_VR - HEAD_W, ATT_BK), BF16)
    hq_ref[...] = mm(offs["hq"]).astype(BF16)
    sig = jax.nn.sigmoid(mm(offs["hf"]))
    lb = lb_ref[...]
    logf_ref[...] = jnp.log(lb + (1.0 - lb) * sig)
    kk_ref[...] = ((1.0 - lb) * (1.0 - sig)).astype(BF16)
    hi_ref[...] = mm(offs["hi"]).astype(BF16)
    hog = mm(offs["hg"])
    sog_ref[...] = (hog * jax.nn.sigmoid(hog)).astype(BF16)
    for j in range(gates_ref.shape[-1] // 512):
        g = mm(offs["gate"] + j * 512) + bg_ref[:, j * 512:(j + 1) * 512]
        gates_ref[:, j * 512:(j + 1) * 512] = jax.nn.sigmoid(g).astype(BF16)


def _inproj(x, w_bf, wvt_bf, b_gate, lb, offs, tm=512):
    B, S, D = x.shape
    ncols = w_bf.shape[1]
    gate_w = ncols - offs["gate"]
    row = lambda w: pl.BlockSpec((None, tm, w), lambda b, s: (b, s, 0))
    out_shape = (
        jax.ShapeDtypeStruct((B, S, 512), BF16),
        jax.ShapeDtypeStruct((B, S, 512), BF16),
        jax.ShapeDtypeStruct((B, S // ATT_BK, ATT_HEADS * ATT_VR, ATT_BK), BF16),
        jax.ShapeDtypeStruct((B, S, 512), BF16),
        jax.ShapeDtypeStruct((B, S, 512), BF16),
        jax.ShapeDtypeStruct((B, S, 512), BF16),
        jax.ShapeDtypeStruct((B, S, 512), BF16),
        jax.ShapeDtypeStruct((B, S, 512), F32),
        jax.ShapeDtypeStruct((B, S, gate_w), BF16),
    )
    out_specs = (
        row(512), row(512),
        pl.BlockSpec((None, tm // ATT_BK, ATT_HEADS * ATT_VR, ATT_BK),
                     lambda b, s: (b, s, 0, 0)),
        row(512), row(512), row(512), row(512), row(512), row(gate_w),
    )
    return pl.pallas_call(
        functools.partial(_inproj_kernel, tm=tm, offs=offs),
        grid=(B, S // tm),
        in_specs=[row(D), _const_spec(w_bf.shape), _const_spec(wvt_bf.shape),
                  _const_spec(b_gate.shape), _const_spec(lb.shape)],
        out_specs=out_specs,
        out_shape=out_shape,
        compiler_params=_cparams(2),
        name="inproj",
    )(x, w_bf, wvt_bf, b_gate, lb)


def _rel_bucket_np(rel):
    nb = NUM_BUCKETS // 2
    max_exact = nb // 2
    ret = np.where(rel > 0, nb, 0)
    n = np.abs(rel)
    nf = np.maximum(n, 1).astype(np.float32)
    large = max_exact + (np.log(nf / np.float32(max_exact))
                         / np.float32(math.log(MAX_DISTANCE / max_exact))
                         * np.float32(nb - max_exact)).astype(np.int32)
    large = np.minimum(large, nb - 1)
    return ret + np.where(n < max_exact, n, large)


def _attn_bucket_tiles():
    i = np.arange(ATT_BQ)[None, :]
    j = np.arange(ATT_BK)[:, None]
    tiles = []
    for d in ATT_NEAR:
        kpos = d * ATT_BQ + j
        live = (kpos // CHUNK) <= (i // CHUNK)
        tiles.append(np.where(live, _rel_bucket_np(kpos - i), -1))
    far = _rel_bucket_np(np.asarray((min(ATT_NEAR) - 1) * ATT_BQ + ATT_BK - 1))
    assert int(far) == NUM_BUCKETS // 2 - 1
    return np.stack(tiles).astype(np.int32)


def _attn_kernel(lam_ref, rb_ref, bkt_ref, q_ref, k_ref, vt_ref, w_ref, o_ref,
                 bias_ref, qs_ref, s0_ref, s1_ref, p0_ref, p1_ref, a0_ref, a1_ref,
                 m_ref, acc_ref, *, out_scale):
    S = q_ref.shape[0]
    nkb = S // ATT_BK
    W2 = 2 * ATT_BQ
    near = tuple(range(ATT_GK)) + (-1,)
    lam = lam_ref[0]
    h = pl.program_id(1)

    far = rb_ref[NUM_BUCKETS // 2 - 1, h]
    for n, kbrel in enumerate(near):
        for j in range(ATT_G):
            d = (kbrel * ATT_BK) // ATT_BQ - j
            cols = slice(j * W2, (j + 1) * W2)
            if d > max(ATT_NEAR):
                bias_ref[n, :, cols] = jnp.full((ATT_BK, W2), NEG, F32)
            elif d < min(ATT_NEAR):
                bias_ref[n, :, cols] = jnp.zeros((ATT_BK, W2), F32)
    for t, dt in enumerate(ATT_NEAR):
        bk = bkt_ref[t]
        tile = jnp.full(bk.shape, NEG, F32)
        for bucket in range(NUM_BUCKETS):
            tile = jnp.where(bk == bucket, (rb_ref[bucket, h] - far) * LOG2E, tile)
        tile2 = jnp.concatenate([tile, tile], axis=1)
        for n, kbrel in enumerate(near):
            for j in range(ATT_G):
                if (kbrel * ATT_BK) // ATT_BQ - j == dt:
                    bias_ref[n, :, j * W2:(j + 1) * W2] = tile2

    lane = lax.broadcasted_iota(jnp.int32, (ATT_BQ, HEAD_W), 1)
    first_map = lane < ATT_DH

    def scores(kb, s_ref, bias_idx):
        kblk = k_ref[pl.ds(pl.multiple_of(kb * ATT_BK, ATT_BK), ATT_BK), :]
        for j in range(ATT_G):
            cols = slice(j * W2, (j + 1) * W2)
            s = lax.dot_general(kblk, qs_ref[cols, :], (((1,), (1,)), ((), ())),
                                preferred_element_type=F32)
            if bias_idx is not None:
                s = s + bias_ref[bias_idx, :, cols]
            s_ref[:, cols] = s

    def soft(s_ref, p_ref, a_ref, first=False):
        for j in range(ATT_G):
            cols = slice(j * W2, (j + 1) * W2)
            s = s_ref[:, cols]
            m_new = jnp.max(s, axis=0, keepdims=True)
            if first:
                a_ref[:, cols] = jnp.ones_like(m_new)
            else:
                m_old = m_ref[:, cols]
                m_new = jnp.maximum(m_old, m_new)
                a_ref[:, cols] = jnp.exp2(m_old - m_new)
            m_ref[:, cols] = m_new
            p_ref[:, cols] = jnp.exp2(s - m_new).astype(BF16)

    def accum(kb, p_ref, a_ref):
        vtb = vt_ref[kb]
        for j in range(ATT_G):
            cols = slice(j * W2, (j + 1) * W2)
            acc_ref[:, cols] = a_ref[:, cols] * acc_ref[:, cols] + jnp.dot(
                vtb, p_ref[:, cols], preferred_element_type=F32)

    def group(g, carry):
        q0 = pl.multiple_of(g * (ATT_G * ATT_BQ), ATT_G * ATT_BQ)
        for j in range(ATT_G):
            q = q_ref[pl.ds(q0 + j * ATT_BQ, ATT_BQ), :]
            zero = jnp.zeros_like(q)
            qs_ref[j * W2:j * W2 + ATT_BQ, :] = jnp.where(first_map, q, zero)
            qs_ref[j * W2 + ATT_BQ:(j + 1) * W2, :] = jnp.where(first_map, zero, q)
        kb0 = g * ATT_GK
        acc_ref[...] = jnp.zeros(acc_ref.shape, F32)

        n_steps = jnp.where(g > 0, ATT_GK + 1 + (kb0 - 1), ATT_GK)

        def step_kb(n):
            kb = n - (ATT_GK + 1)
            for i, kbrel in enumerate(near):
                kb = jnp.where(n == i, kb0 + kbrel, kb)
            return jnp.clip(kb, 0, nkb - 1)

        scores(kb0, s0_ref, 0)
        scores(kb0 + 1, s1_ref, 1)
        soft(s0_ref, p0_ref, a0_ref, first=True)

        @pl.when(g > 0)
        def _():
            accum(kb0, p0_ref, a0_ref)
            scores(kb0 - 1, s0_ref, 2)
            soft(s1_ref, p1_ref, a1_ref)
            scores(0, s1_ref, None)
            accum(kb0 + 1, p1_ref, a1_ref)
            soft(s0_ref, p0_ref, a0_ref)

        def far_steps(i, c):
            n = 2 * i
            accum(step_kb(n - 2), p0_ref, a0_ref)
            scores(n - 3, s0_ref, None)
            soft(s1_ref, p1_ref, a1_ref)
            scores(n - 2, s1_ref, None)
            accum(n - 4, p1_ref, a1_ref)
            soft(s0_ref, p0_ref, a0_ref)
            return c

        lax.fori_loop(2, n_steps // 2, far_steps, 0)

        accum(step_kb(n_steps - 2), p0_ref, a0_ref)
        soft(s1_ref, p1_ref, a1_ref)
        accum(step_kb(n_steps - 1), p1_ref, a1_ref)

        for j in range(ATT_G):
            cols = slice(j * W2, (j + 1) * W2)
            on = acc_ref[:HEAD_W, cols] * (1.0 / acc_ref[HEAD_W:HEAD_W + 1, cols])
            o = on[:, :ATT_BQ] - lam * on[:, ATT_BQ:]
            ms = jnp.mean(o * o, axis=0, keepdims=True)
            y = o * lax.rsqrt(ms + EPS) * w_ref[...] * out_scale
            o_ref[pl.ds(q0 + j * ATT_BQ, ATT_BQ), :] = y.T.astype(BF16)
        return carry

    lax.fori_loop(0, S // (ATT_G * ATT_BQ), group, 0)


def _attention(q, k, vt, rel_bias, subln_w, lam, out_scale):
    B, S, _ = q.shape
    nkb = S // ATT_BK
    GW = ATT_G * 2 * ATT_BQ
    assert ATT_GK == 2 and ATT_BK == 2 * ATT_BQ
    bkt = jnp.asarray(_attn_bucket_tiles())
    head = pl.BlockSpec((None, S, HEAD_W), lambda b, h: (b, 0, h))
    smem = pl.BlockSpec(memory_space=pltpu.SMEM)
    return pl.pallas_call(
        functools.partial(_attn_kernel, out_scale=out_scale),
        grid=(B, ATT_HEADS),
        in_specs=[
            smem, smem, _const_spec(bkt.shape),
            head, head,
            pl.BlockSpec((None, nkb, ATT_VR, ATT_BK), lambda b, h: (b, 0, h, 0)),
            pl.BlockSpec((HEAD_W, 1), lambda b, h: (0, 0)),
        ],
        out_specs=head,
        out_shape=jax.ShapeDtypeStruct((B, S, ATT_HEADS * HEAD_W), BF16),
        scratch_shapes=[
            pltpu.VMEM((ATT_GK + 1, ATT_BK, GW), F32),
            pltpu.VMEM((GW, HEAD_W), BF16),
            pltpu.VMEM((ATT_BK, GW), F32),
            pltpu.VMEM((ATT_BK, GW), F32),
            pltpu.VMEM((ATT_BK, GW), BF16),
            pltpu.VMEM((ATT_BK, GW), BF16),
            pltpu.VMEM((1, GW), F32),
            pltpu.VMEM((1, GW), F32),
            pltpu.VMEM((1, GW), F32),
            pltpu.VMEM((ATT_VR, GW), F32),
        ],
        compiler_params=_cparams(2),
        name="diffattn",
    )(lam, rel_bias, bkt, q, k, vt, subln_w)


def _hgrn_kernel(hq_ref, kk_ref, hi_ref, sog_ref, logf_ref, w_ref, y_ref,
                 state_ref, kbuf_ref, bbuf_ref, *, tc):
    C = HG_C

    @pl.when(pl.program_id(1) == 0)
    def _():
        state_ref[...] = jnp.zeros_like(state_ref)

    kbuf_ref[0:HG_DIAG, :] = jnp.zeros((HG_DIAG, HEAD_W), F32)
    bbuf_ref[0:HG_DIAG, :] = jnp.zeros((HG_DIAG, HEAD_W), F32)

    row = lax.broadcasted_iota(jnp.int32, (C, HEAD_W), 0)
    r2 = lax.broadcasted_iota(jnp.int32, (C, C), 0)
    c2 = lax.broadcasted_iota(jnp.int32, (C, C), 1)
    tri = (c2 <= r2).astype(BF16)
    same_blk = (r2 // HG_DIAG) == (c2 // HG_DIAG)

    def chunk(c, carry):
        r0 = pl.multiple_of(c * C, C)
        for h in range(HG_HEADS):
            cols = slice(h * HEAD_W, (h + 1) * HEAD_W)
            lf = logf_ref[pl.ds(r0, C), cols]
            p0 = lf.astype(BF16)
            e1 = lf - p0.astype(F32)
            p1 = e1.astype(BF16)
            p2 = (e1 - p1.astype(F32)).astype(BF16)
            b = (jnp.dot(tri, p0, preferred_element_type=F32)
                 + jnp.dot(tri, p1, preferred_element_type=F32)
                 + jnp.dot(tri, p2, preferred_element_type=F32))
            q = hq_ref[pl.ds(r0, C), cols].astype(F32)
            kk = kk_ref[pl.ds(r0, C), cols].astype(F32)
            iv = hi_ref[pl.ds(r0, C), cols]
            st = state_ref[h]
            b_last = b[C - 1:C, :]

            inter = lax.dot_general((q * jnp.exp(b)).astype(BF16), st.astype(BF16),
                                    (((1,), (1,)), ((), ())), preferred_element_type=F32)

            a = jnp.zeros((C, C), F32)
            for hs in HG_LEVELS:
                parts = []
                for g in range(C // (2 * hs)):
                    r = g * 2 * hs + hs - 1
                    parts.append(jnp.broadcast_to(b[r:r + 1, :], (2 * hs, HEAD_W)))
                bref = parts[0] if len(parts) == 1 else jnp.concatenate(parts, axis=0)
                is_q = ((row // hs) % 2) == 1
                e = jnp.exp(jnp.where(is_q, b - bref, bref - b))
                ql = jnp.where(is_q, q * e, 0.0).astype(BF16)
                kl = jnp.where(is_q, 0.0, kk * e).astype(BF16)
                pl_ = lax.dot_general(ql, kl, (((1,), (1,)), ((), ())),
                                      preferred_element_type=F32)
                a = a + jnp.where((r2 // (2 * hs)) == (c2 // (2 * hs)), pl_, 0.0)
            kbuf_ref[HG_DIAG:, :] = kk
            bbuf_ref[HG_DIAG:, :] = b
            for d in range(HG_DIAG):
                ks = kbuf_ref[HG_DIAG - d:HG_DIAG - d + C, :]
                bs = bbuf_ref[HG_DIAG - d:HG_DIAG - d + C, :]
                term = q * ks * jnp.exp(jnp.minimum(b - bs, 0.0))
                dsum = jnp.sum(term, axis=-1, keepdims=True)
                a = a + jnp.where(((r2 - c2) == d) & same_blk, dsum, 0.0)

            o = inter + jnp.dot(a.astype(BF16), iv, preferred_element_type=F32)

            k_end = (kk * jnp.exp(b_last - b)).astype(BF16)
            upd = lax.dot_general(iv, k_end, (((0,), (0,)), ((), ())),
                                  preferred_element_type=F32)
            state_ref[h] = st * jnp.exp(b_last) + upd

            ms = jnp.mean(o * o, axis=-1, keepdims=True)
            y = o * lax.rsqrt(ms + EPS) * w_ref[...]
            y = y * sog_ref[pl.ds(r0, C), cols].astype(F32)
            y_ref[pl.ds(r0, C), cols] = y.astype(BF16)
        return carry

    lax.fori_loop(0, tc // C, chunk, 0)


def _hgrn(hq, kk, hi, sog, logf, norm_w, tc=512):
    B, S, W = hq.shape
    blk = pl.BlockSpec((None, tc, W), lambda b, s: (b, s, 0))
    return pl.pallas_call(
        functools.partial(_hgrn_kernel, tc=tc),
        grid=(B, S // tc),
        in_specs=[blk, blk, blk, blk, blk, pl.BlockSpec((1, HEAD_W), lambda b, s: (0, 0))],
        out_specs=blk,
        out_shape=jax.ShapeDtypeStruct((B, S, W), BF16),
        scratch_shapes=[pltpu.VMEM((HG_HEADS, HEAD_W, HEAD_W), F32),
                        pltpu.VMEM((HG_C + HG_DIAG, HEAD_W), F32),
                        pltpu.VMEM((HG_C + HG_DIAG, HEAD_W), F32)],
        compiler_params=_cparams(2),
        name="hgrn2",
    )(hq, kk, hi, sog, logf, norm_w)


def _layer_norm(z, g, b):
    mu = jnp.mean(z, axis=-1, keepdims=True)
    zc = z - mu
    var = jnp.mean(zc * zc, axis=-1, keepdims=True)
    return zc * lax.rsqrt(var + EPS) * g + b


def _merge_kernel(ya_ref, yh_ref, gates_ref, x_ref, wa_ref, wh_ref, wo_ref,
                  g_ref, b_ref, o_ref, *, alpha):
    D = x_ref.shape[-1]
    tm = x_ref.shape[0]

    def branches(r):
        rows = slice(r * ROW_CHUNK, (r + 1) * ROW_CHUNK)
        pa = jnp.dot(ya_ref[rows, :], wa_ref[...], preferred_element_type=F32)
        ph = jnp.dot(yh_ref[rows, :], wh_ref[...], preferred_element_type=F32)
        return pa, ph

    n_chunks = tm // ROW_CHUNK
    nxt = branches(0)
    for r in range(n_chunks):
        rows = slice(r * ROW_CHUNK, (r + 1) * ROW_CHUNK)
        pa, ph = nxt
        if r + 1 < n_chunks:
            nxt = branches(r + 1)
        merged = (gates_ref[rows, :D].astype(F32) * pa
                  + gates_ref[rows, D:].astype(F32) * ph)
        mix = jnp.dot(merged.astype(BF16), wo_ref[...], preferred_element_type=F32)
        o_ref[rows, :] = _layer_norm(alpha * x_ref[rows, :] + mix, g_ref[...], b_ref[...])


def _merge(ya, yh, gates, x, wa, wh, wo, g, b, alpha, tm=1024):
    N, D = x.shape
    row = lambda w: pl.BlockSpec((tm, w), lambda i: (i, 0))
    return pl.pallas_call(
        functools.partial(_merge_kernel, alpha=alpha),
        grid=(N // tm,),
        in_specs=[row(ya.shape[1]), row(yh.shape[1]), row(gates.shape[1]), row(D),
                  _const_spec(wa.shape), _const_spec(wh.shape), _const_spec(wo.shape),
                  _const_spec(g.shape), _const_spec(b.shape)],
        out_specs=row(D),
        out_shape=jax.ShapeDtypeStruct((N, D), F32),
        compiler_params=_cparams(1),
        name="merge_ln",
    )(ya, yh, gates, x, wa, wh, wo, g, b)


def _ffn_kernel(h_ref, wg_ref, wu_ref, cw_ref, cb_ref, wd_ref, g_ref, b_ref, o_ref,
                tail_ref, gbuf_ref, *, alpha, tm):
    d_ff = wg_ref.shape[1]
    PAD = 8

    @pl.when(pl.program_id(1) == 0)
    def _():
        tail_ref[...] = jnp.zeros_like(tail_ref)

    RC = ROW_CHUNK
    nfc = d_ff // FFN_FC
    steps = [(r, fc) for r in range(tm // RC) for fc in range(nfc)]
    hb = {}

    def gate_up(r, fc):
        if r not in hb:
            hb[r] = h_ref[r * RC:(r + 1) * RC, :].astype(BF16)
        cols = slice(fc * FFN_FC, (fc + 1) * FFN_FC)
        return (jnp.dot(hb[r], wg_ref[:, cols], preferred_element_type=F32),
                jnp.dot(hb[r], wu_ref[:, cols], preferred_element_type=F32))

    nxt = gate_up(*steps[0])
    acc = None
    for t, (r, fc) in enumerate(steps):
        rows = slice(r * RC, (r + 1) * RC)
        cols = slice(fc * FFN_FC, (fc + 1) * FFN_FC)
        gte, up = nxt
        if t + 1 < len(steps):
            nxt = gate_up(*steps[t + 1])
        gbuf = gbuf_ref.at[t % 2]
        gbuf[0:PAD, :] = tail_ref[:, cols]
        gbuf[PAD:, :] = gte
        tail_ref[:, cols] = gte[RC - PAD:, :]
        conv = (gte * cw_ref[2:3, cols]
                + gbuf[PAD - 1:PAD - 1 + RC, :] * cw_ref[1:2, cols]
                + gbuf[PAD - 2:PAD - 2 + RC, :] * cw_ref[0:1, cols]
                + cb_ref[:, cols])
        act = jax.nn.gelu(conv) * up
        down = jnp.dot(act.astype(BF16), wd_ref[cols, :], preferred_element_type=F32)
        acc = down if fc == 0 else acc + down
        if fc == nfc - 1:
            o_ref[rows, :] = _layer_norm(alpha * h_ref[rows, :] + acc, g_ref[...], b_ref[...])


def _ffn(h, wg, wu, cw, cb, wd, g, b, alpha, tm=1024):
    B, S, D = h.shape
    d_ff = wg.shape[1]
    row = pl.BlockSpec((None, tm, D), lambda bi, s: (bi, s, 0))
    return pl.pallas_call(
        functools.partial(_ffn_kernel, alpha=alpha, tm=tm),
        grid=(B, S // tm),
        in_specs=[row, _const_spec(wg.shape), _const_spec(wu.shape), _const_spec(cw.shape),
                  _const_spec(cb.shape), _const_spec(wd.shape),
                  _const_spec(g.shape), _const_spec(b.shape)],
        out_specs=row,
        out_shape=jax.ShapeDtypeStruct((B, S, D), F32),
        scratch_shapes=[pltpu.VMEM((8, d_ff), F32),
                        pltpu.VMEM((2, ROW_CHUNK + 8, FFN_FC), F32)],
        compiler_params=_cparams(2),
        name="ffn_ln",
    )(h, wg, wu, cw, cb, wd, g, b)


def kernel(x, w_in, b_gate, lambda_q1, lambda_k1, lambda_q2, lambda_k2, attn_subln_w,
           rel_bias, hgrn_lb_logits, hgrn_norm_w, w_branch_attn, w_branch_hgrn, w_out,
           ln1_g, ln1_b, w_ffn_gate, w_ffn_up, ffn_conv_w, ffn_conv_b, w_ffn_down,
           ln2_g, ln2_b):
    B, S, D = x.shape
    depth = w_in.shape[0]
    att_qk = ATT_HEADS * 2 * ATT_DH
    att_w = ATT_HEADS * HEAD_W
    hg_w = HG_HEADS * HEAD_W
    offs = {"aq": 0, "ak": att_qk, "av": 2 * att_qk}
    offs["hq"] = offs["av"] + att_w
    offs["hf"] = offs["hq"] + hg_w
    offs["hi"] = offs["hf"] + hg_w
    offs["hg"] = offs["hi"] + hg_w
    offs["gate"] = offs["hg"] + hg_w
    alpha = (2 * depth) ** 0.25

    lb_all = jnp.cumsum(jax.nn.softmax(hgrn_lb_logits.astype(F32), axis=0), axis=0)

    h = x
    for l in range(depth):
        lam_init = 0.8 - 0.6 * math.exp(-0.3 * l)
        lam = (jnp.exp(jnp.sum(lambda_q1[l].astype(F32) * lambda_k1[l].astype(F32)))
               - jnp.exp(jnp.sum(lambda_q2[l].astype(F32) * lambda_k2[l].astype(F32)))
               + lam_init).reshape(1)

        w_l = w_in[l]
        w_bf = w_l.astype(BF16)
        wvt_bf = w_l[:, offs["av"]:offs["hq"]].T.astype(BF16)
        (q, k, vt, hq, kk, hi, sog, logf, gates) = _inproj(
            h, w_bf, wvt_bf, b_gate[l][None, :], lb_all[l][None, :], offs)

        ya = _attention(q, k, vt, rel_bias.astype(F32),
                        attn_subln_w[l][:, None].astype(F32), lam, 1.0 - lam_init)
        yh = _hgrn(hq, kk, hi, sog, logf, hgrn_norm_w[l][None, :].astype(F32))

        h1 = _merge(ya.reshape(B * S, att_w), yh.reshape(B * S, hg_w),
                    gates.reshape(B * S, 2 * D), h.reshape(B * S, D),
                    w_branch_attn[l].astype(BF16), w_branch_hgrn[l].astype(BF16),
                    w_out[l].astype(BF16), ln1_g[l][None, :], ln1_b[l][None, :], alpha)

        h = _ffn(h1.reshape(B, S, D), w_ffn_gate[l].astype(BF16), w_ffn_up[l].astype(BF16),
                 ffn_conv_w[l], ffn_conv_b[l][None, :], w_ffn_down[l].astype(BF16),
                 ln2_g[l][None, :], ln2_b[l][None, :], alpha)
    return h
```

```python
import functools
import math

import numpy as np
import jax
import jax.numpy as jnp
from jax import lax
from jax.experimental import pallas as pl
from jax.experimental.pallas import tpu as pltpu

F32 = jnp.float32
BF16 = jnp.bfloat16

CHUNK = 64
ATT_HEADS = 4
ATT_DH = 64
HG_HEADS = 4
HEAD_W = 128
NUM_BUCKETS = 32
MAX_DISTANCE = 128
EPS = 1e-5
NEG = -1e30

ATT_BQ = 128
ATT_BK = 256
ATT_G = 4
ATT_GK = ATT_G * ATT_BQ // ATT_BK
ATT_NEAR = (0, -1, -2)
ATT_VR = HEAD_W + 16
LOG2E = math.log2(math.e)
HG_C = 128
HG_LEVELS = (64, 32, 16, 8, 4)
HG_DIAG = 4
FFN_FC = 256
ROW_CHUNK = 256

VMEM_LIMIT = 56 * 1024 * 1024


def _cparams(n_axes):
    return pltpu.CompilerParams(
        dimension_semantics=("arbitrary",) * n_axes,
        vmem_limit_bytes=VMEM_LIMIT)


def _const_spec(shape):
    nd = len(shape)
    return pl.BlockSpec(shape, lambda *_: (0,) * nd, pipeline_mode=pl.Buffered(1))


def _inproj_kernel(x_ref, w_ref, wvt_ref, bg_ref, lb_ref,
                   q_ref, k_ref, vt_ref, hq_ref, kk_ref, hi_ref, sog_ref,
                   logf_ref, gates_ref, *, tm, offs):
    xb = x_ref[...].astype(BF16)

    def mm(off, width=512):
        return jnp.dot(xb, w_ref[:, off:off + width], preferred_element_type=F32)

    q_ref[...] = (mm(offs["aq"]) * (ATT_DH ** -0.5 * LOG2E)).astype(BF16)
    k_ref[...] = mm(offs["ak"]).astype(BF16)
    for t in range(tm // ATT_BK):
        vt = lax.dot_general(wvt_ref[...], xb[t * ATT_BK:(t + 1) * ATT_BK],
                             (((1,), (1,)), ((), ())), preferred_element_type=F32)
        for hd in range(ATT_HEADS):
            vt_ref[t, hd * ATT_VR:hd * ATT_VR + HEAD_W, :] = (
                vt[hd * HEAD_W:(hd + 1) * HEAD_W].astype(BF16))
            vt_ref[t, hd * ATT_VR + HEAD_W:(hd + 1) * ATT_VR, :] = jnp.ones(
                (ATT_VR - HEAD_W, ATT_BK), BF16)
    hq_ref[...] = mm(offs["hq"]).astype(BF16)
    sig = jax.nn.sigmoid(mm(offs["hf"]))
    lb = lb_ref[...]
    logf_ref[...] = jnp.log2(lb + (1.0 - lb) * sig)
    kk_ref[...] = ((1.0 - lb) * (1.0 - sig)).astype(BF16)
    hi_ref[...] = mm(offs["hi"]).astype(BF16)
    hog = mm(offs["hg"])
    sog_ref[...] = (hog * jax.nn.sigmoid(hog)).astype(BF16)
    for j in range(gates_ref.shape[-1] // 512):
        g = mm(offs["gate"] + j * 512) + bg_ref[:, j * 512:(j + 1) * 512]
        gates_ref[:, j * 512:(j + 1) * 512] = jax.nn.sigmoid(g).astype(BF16)


def _inproj(x, w_bf, wvt_bf, b_gate, lb, offs, tm=512):
    B, S, D = x.shape
    ncols = w_bf.shape[1]
    gate_w = ncols - offs["gate"]
    row = lambda w: pl.BlockSpec((None, tm, w), lambda b, s: (b, s, 0))
    out_shape = (
        jax.ShapeDtypeStruct((B, S, 512), BF16),
        jax.ShapeDtypeStruct((B, S, 512), BF16),
        jax.ShapeDtypeStruct((B, S // ATT_BK, ATT_HEADS * ATT_VR, ATT_BK), BF16),
        jax.ShapeDtypeStruct((B, S, 512), BF16),
        jax.ShapeDtypeStruct((B, S, 512), BF16),
        jax.ShapeDtypeStruct((B, S, 512), BF16),
        jax.ShapeDtypeStruct((B, S, 512), BF16),
        jax.ShapeDtypeStruct((B, S, 512), F32),
        jax.ShapeDtypeStruct((B, S, gate_w), BF16),
    )
    out_specs = (
        row(512), row(512),
        pl.BlockSpec((None, tm // ATT_BK, ATT_HEADS * ATT_VR, ATT_BK),
                     lambda b, s: (b, s, 0, 0)),
        row(512), row(512), row(512), row(512), row(512), row(gate_w),
    )
    return pl.pallas_call(
        functools.partial(_inproj_kernel, tm=tm, offs=offs),
        grid=(B, S // tm),
        in_specs=[row(D), _const_spec(w_bf.shape), _const_spec(wvt_bf.shape),
                  _const_spec(b_gate.shape), _const_spec(lb.shape)],
        out_specs=out_specs,
        out_shape=out_shape,
        compiler_params=_cparams(2),
        name="inproj",
    )(x, w_bf, wvt_bf, b_gate, lb)


def _rel_bucket_np(rel):
    nb = NUM_BUCKETS // 2
    max_exact = nb // 2
    ret = np.where(rel > 0, nb, 0)
    n = np.abs(rel)
    nf = np.maximum(n, 1).astype(np.float32)
    large = max_exact + (np.log(nf / np.float32(max_exact))
                         / np.float32(math.log(MAX_DISTANCE / max_exact))
                         * np.float32(nb - max_exact)).astype(np.int32)
    large = np.minimum(large, nb - 1)
    return ret + np.where(n < max_exact, n, large)


def _attn_bucket_tiles():
    i = np.arange(ATT_BQ)[None, :]
    j = np.arange(ATT_BK)[:, None]
    tiles = []
    for d in ATT_NEAR:
        kpos = d * ATT_BQ + j
        live = (kpos // CHUNK) <= (i // CHUNK)
        tiles.append(np.where(live, _rel_bucket_np(kpos - i), -1))
    far = _rel_bucket_np(np.asarray((min(ATT_NEAR) - 1) * ATT_BQ + ATT_BK - 1))
    assert int(far) == NUM_BUCKETS // 2 - 1
    return np.stack(tiles).astype(np.int32)


def _attn_kernel(lam_ref, rb_ref, bkt_ref, q_ref, k_ref, vt_ref, w_ref, o_ref,
                 bias_ref, qs_ref, s0_ref, s1_ref, p0_ref, p1_ref, a0_ref, a1_ref,
                 m_ref, acc_ref, *, out_scale):
    S = q_ref.shape[0]
    nkb = S // ATT_BK
    W2 = 2 * ATT_BQ
    near = tuple(range(ATT_GK)) + (-1,)
    lam = lam_ref[0]
    h = pl.program_id(1)

    far = rb_ref[NUM_BUCKETS // 2 - 1, h]
    for n, kbrel in enumerate(near):
        for j in range(ATT_G):
            d = (kbrel * ATT_BK) // ATT_BQ - j
            cols = slice(j * W2, (j + 1) * W2)
            if d > max(ATT_NEAR):
                bias_ref[n, :, cols] = jnp.full((ATT_BK, W2), NEG, F32)
            elif d < min(ATT_NEAR):
                bias_ref[n, :, cols] = jnp.zeros((ATT_BK, W2), F32)
    for t, dt in enumerate(ATT_NEAR):
        bk = bkt_ref[t]
        tile = jnp.full(bk.shape, NEG, F32)
        for bucket in range(NUM_BUCKETS):
            tile = jnp.where(bk == bucket, (rb_ref[bucket, h] - far) * LOG2E, tile)
        tile2 = jnp.concatenate([tile, tile], axis=1)
        for n, kbrel in enumerate(near):
            for j in range(ATT_G):
                if (kbrel * ATT_BK) // ATT_BQ - j == dt:
                    bias_ref[n, :, j * W2:(j + 1) * W2] = tile2

    lane = lax.broadcasted_iota(jnp.int32, (ATT_BQ, HEAD_W), 1)
    first_map = lane < ATT_DH

    def scores(kb, s_ref, bias_idx):
        kblk = k_ref[pl.ds(pl.multiple_of(kb * ATT_BK, ATT_BK), ATT_BK), :]
        for j in range(ATT_G):
            cols = slice(j * W2, (j + 1) * W2)
            s = lax.dot_general(kblk, qs_ref[cols, :], (((1,), (1,)), ((), ())),
                                preferred_element_type=F32)
            if bias_idx is not None:
                s = s + bias_ref[bias_idx, :, cols]
            s_ref[:, cols] = s

    def soft(s_ref, p_ref, a_ref, first=False):
        for j in range(ATT_G):
            cols = slice(j * W2, (j + 1) * W2)
            s = s_ref[:, cols]
            m_new = jnp.max(s, axis=0, keepdims=True)
            if first:
                a_ref[:, cols] = jnp.ones_like(m_new)
            else:
                m_old = m_ref[:, cols]
                m_new = jnp.maximum(m_old, m_new)
                a_ref[:, cols] = jnp.exp2(m_old - m_new)
            m_ref[:, cols] = m_new
            p_ref[:, cols] = jnp.exp2(s - m_new).astype(BF16)

    def accum(kb, p_ref, a_ref):
        vtb = vt_ref[kb]
        for j in range(ATT_G):
            cols = slice(j * W2, (j + 1) * W2)
            acc_ref[:, cols] = a_ref[:, cols] * acc_ref[:, cols] + jnp.dot(
                vtb, p_ref[:, cols], preferred_element_type=F32)

    def group(g, carry):
        q0 = pl.multiple_of(g * (ATT_G * ATT_BQ), ATT_G * ATT_BQ)
        for j in range(ATT_G):
            q = q_ref[pl.ds(q0 + j * ATT_BQ, ATT_BQ), :]
            zero = jnp.zeros_like(q)
            qs_ref[j * W2:j * W2 + ATT_BQ, :] = jnp.where(first_map, q, zero)
            qs_ref[j * W2 + ATT_BQ:(j + 1) * W2, :] = jnp.where(first_map, zero, q)
        kb0 = g * ATT_GK
        acc_ref[...] = jnp.zeros(acc_ref.shape, F32)

        n_steps = jnp.where(g > 0, ATT_GK + 1 + (kb0 - 1), ATT_GK)

        def step_kb(n):
            kb = n - (ATT_GK + 1)
            for i, kbrel in enumerate(near):
                kb = jnp.where(n == i, kb0 + kbrel, kb)
            return jnp.clip(kb, 0, nkb - 1)

        scores(kb0, s0_ref, 0)
        scores(kb0 + 1, s1_ref, 1)
        soft(s0_ref, p0_ref, a0_ref, first=True)

        @pl.when(g > 0)
        def _():
            accum(kb0, p0_ref, a0_ref)
            scores(kb0 - 1, s0_ref, 2)
            soft(s1_ref, p1_ref, a1_ref)
            scores(0, s1_ref, None)
            accum(kb0 + 1, p1_ref, a1_ref)
            soft(s0_ref, p0_ref, a0_ref)

        def far_steps(i, c):
            n = 2 * i
            accum(step_kb(n - 2), p0_ref, a0_ref)
            scores(n - 3, s0_ref, None)
            soft(s1_ref, p1_ref, a1_ref)
            scores(n - 2, s1_ref, None)
            accum(n - 4, p1_ref, a1_ref)
            soft(s0_ref, p0_ref, a0_ref)
            return c

        lax.fori_loop(2, n_steps // 2, far_steps, 0)

        accum(step_kb(n_steps - 2), p0_ref, a0_ref)
        soft(s1_ref, p1_ref, a1_ref)
        accum(step_kb(n_steps - 1), p1_ref, a1_ref)

        for j in range(ATT_G):
            cols = slice(j * W2, (j + 1) * W2)
            on = acc_ref[:HEAD_W, cols] * (1.0 / acc_ref[HEAD_W:HEAD_W + 1, cols])
            o = on[:, :ATT_BQ] - lam * on[:, ATT_BQ:]
            ms = jnp.mean(o * o, axis=0, keepdims=True)
            y = o * lax.rsqrt(ms + EPS) * w_ref[...] * out_scale
            o_ref[pl.ds(q0 + j * ATT_BQ, ATT_BQ), :] = y.T.astype(BF16)
        return carry

    lax.fori_loop(0, S // (ATT_G * ATT_BQ), group, 0)


def _attention(q, k, vt, rel_bias, subln_w, lam, out_scale):
    B, S, _ = q.shape
    nkb = S // ATT_BK
    GW = ATT_G * 2 * ATT_BQ
    assert ATT_GK == 2 and ATT_BK == 2 * ATT_BQ
    bkt = jnp.asarray(_attn_bucket_tiles())
    head = pl.BlockSpec((None, S, HEAD_W), lambda b, h: (b, 0, h))
    smem = pl.BlockSpec(memory_space=pltpu.SMEM)
    return pl.pallas_call(
        functools.partial(_attn_kernel, out_scale=out_scale),
        grid=(B, ATT_HEADS),
        in_specs=[
            smem, smem, _const_spec(bkt.shape),
            head, head,
            pl.BlockSpec((None, nkb, ATT_VR, ATT_BK), lambda b, h: (b, 0, h, 0)),
            pl.BlockSpec((HEAD_W, 1), lambda b, h: (0, 0)),
        ],
        out_specs=head,
        out_shape=jax.ShapeDtypeStruct((B, S, ATT_HEADS * HEAD_W), BF16),
        scratch_shapes=[
            pltpu.VMEM((ATT_GK + 1, ATT_BK, GW), F32),
            pltpu.VMEM((GW, HEAD_W), BF16),
            pltpu.VMEM((ATT_BK, GW), F32),
            pltpu.VMEM((ATT_BK, GW), F32),
            pltpu.VMEM((ATT_BK, GW), BF16),
            pltpu.VMEM((ATT_BK, GW), BF16),
            pltpu.VMEM((1, GW), F32),
            pltpu.VMEM((1, GW), F32),
            pltpu.VMEM((1, GW), F32),
            pltpu.VMEM((ATT_VR, GW), F32),
        ],
        compiler_params=_cparams(2),
        name="diffattn",
    )(lam, rel_bias, bkt, q, k, vt, subln_w)


def _hgrn_kernel(hq_ref, kk_ref, hi_ref, sog_ref, lf_ref, w_ref, y_ref,
                 state_ref, kbuf_ref, bbuf_ref, *, tc):
    C, H, W = HG_C, HG_HEADS, HEAD_W
    PAD = 8

    @pl.when(pl.program_id(1) == 0)
    def _():
        state_ref[...] = jnp.zeros_like(state_ref)

    kbuf_ref[:, 0:PAD, :] = jnp.zeros((H, PAD, W), F32)
    bbuf_ref[:, 0:PAD, :] = jnp.zeros((H, PAD, W), F32)

    row = lax.broadcasted_iota(jnp.int32, (C, W), 0)
    r2 = lax.broadcasted_iota(jnp.int32, (C, C), 0)
    c2 = lax.broadcasted_iota(jnp.int32, (C, C), 1)
    tri = (c2 <= r2).astype(BF16)
    is_q = {hs: ((row // hs) % 2) == 1 for hs in HG_LEVELS}
    sign = {hs: jnp.where(is_q[hs], 1.0, -1.0).astype(F32) for hs in HG_LEVELS}
    keep = {hs: ((r2 // (2 * hs)) == (c2 // (2 * hs)))
            & (((r2 // hs) % 2) == 1) & (((c2 // hs) % 2) == 0) for hs in HG_LEVELS}
    same_blk = (r2 // HG_DIAG) == (c2 // HG_DIAG)
    on_diag = {d: ((r2 - c2) == d) & same_blk for d in range(HG_DIAG)}

    def level_ref(b, hs):
        if hs >= PAD:
            parts = [jnp.broadcast_to(b[g * 2 * hs + hs - 1:g * 2 * hs + hs, :], (2 * hs, W))
                     for g in range(C // (2 * hs))]
            return parts[0] if len(parts) == 1 else jnp.concatenate(parts, axis=0)
        b3 = b.reshape(C // (2 * hs), 2 * hs, W)
        return jnp.broadcast_to(b3[:, hs - 1:hs, :], b3.shape).reshape(C, W)

    def chunk(c, carry):
        r0 = pl.multiple_of(c * C, C)
        lf = lf_ref[pl.ds(r0, C), :]
        p0 = lf.astype(BF16)
        p1 = (lf - p0.astype(F32)).astype(BF16)
        b_all = (jnp.dot(tri, p0, preferred_element_type=F32)
                 + jnp.dot(tri, p1, preferred_element_type=F32))

        heads = range(H)
        cols = [slice(h * W, (h + 1) * W) for h in heads]
        b = [b_all[:, cols[h]] for h in heads]
        qf = [hq_ref[pl.ds(r0, C), cols[h]].astype(F32) for h in heads]
        kf = [kk_ref[pl.ds(r0, C), cols[h]].astype(F32) for h in heads]
        iv = [hi_ref[pl.ds(r0, C), cols[h]] for h in heads]
        st = [state_ref[h] for h in heads]
        b_last = [b[h][C - 1:C, :] for h in heads]

        inter = [lax.dot_general((qf[h] * jnp.exp2(b[h])).astype(BF16), st[h].astype(BF16),
                                 (((1,), (1,)), ((), ())), preferred_element_type=F32)
                 for h in heads]

        a = [None] * H
        for hs in HG_LEVELS:
            for h in heads:
                e = jnp.exp2((b[h] - level_ref(b[h], hs)) * sign[hs])
                y = (jnp.where(is_q[hs], qf[h], kf[h]) * e).astype(BF16)
                sc = lax.dot_general(y, y, (((1,), (1,)), ((), ())),
                                     preferred_element_type=F32)
                sc = jnp.where(keep[hs], sc, 0.0)
                a[h] = sc if a[h] is None else a[h] + sc
        for h in heads:
            kbuf_ref[h, PAD:, :] = kf[h]
            bbuf_ref[h, PAD:, :] = b[h]
        for d in range(HG_DIAG):
            for h in heads:
                if d == 0:
                    term = qf[h] * kf[h]
                else:
                    ks = kbuf_ref[h, PAD - d:PAD - d + C, :]
                    bs = bbuf_ref[h, PAD - d:PAD - d + C, :]
                    term = qf[h] * ks * jnp.exp2(b[h] - bs)
                dsum = jnp.sum(term, axis=-1, keepdims=True)
                a[h] = a[h] + jnp.where(on_diag[d], dsum, 0.0)

        o = [inter[h] + jnp.dot(a[h].astype(BF16), iv[h], preferred_element_type=F32)
             for h in heads]

        for h in heads:
            k_end = (kf[h] * jnp.exp2(b_last[h] - b[h])).astype(BF16)
            upd = lax.dot_general(iv[h], k_end, (((0,), (0,)), ((), ())),
                                  preferred_element_type=F32)
            state_ref[h] = st[h] * jnp.exp2(b_last[h]) + upd

        for h in heads:
            ms = jnp.mean(o[h] * o[h], axis=-1, keepdims=True)
            y = o[h] * lax.rsqrt(ms + EPS) * w_ref[...]
            y = y * sog_ref[pl.ds(r0, C), cols[h]].astype(F32)
            y_ref[pl.ds(r0, C), cols[h]] = y.astype(BF16)
        return carry

    lax.fori_loop(0, tc // C, chunk, 0)


def _hgrn(hq, kk, hi, sog, logf, norm_w, tc=512):
    B, S, W = hq.shape
    blk = pl.BlockSpec((None, tc, W), lambda b, s: (b, s, 0))
    return pl.pallas_call(
        functools.partial(_hgrn_kernel, tc=tc),
        grid=(B, S // tc),
        in_specs=[blk, blk, blk, blk, blk, pl.BlockSpec((1, HEAD_W), lambda b, s: (0, 0))],
        out_specs=blk,
        out_shape=jax.ShapeDtypeStruct((B, S, W), BF16),
        scratch_shapes=[pltpu.VMEM((HG_HEADS, HEAD_W, HEAD_W), F32),
                        pltpu.VMEM((HG_HEADS, HG_C + 8, HEAD_W), F32),
                        pltpu.VMEM((HG_HEADS, HG_C + 8, HEAD_W), F32)],
        compiler_params=_cparams(2),
        name="hgrn2",
    )(hq, kk, hi, sog, logf, norm_w)


def _layer_norm(z, g, b):
    mu = jnp.mean(z, axis=-1, keepdims=True)
    zc = z - mu
    var = jnp.mean(zc * zc, axis=-1, keepdims=True)
    return zc * lax.rsqrt(var + EPS) * g + b


def _merge_kernel(ya_ref, yh_ref, gates_ref, x_ref, wa_ref, wh_ref, wo_ref,
                  g_ref, b_ref, o_ref, *, alpha):
    D = x_ref.shape[-1]
    tm = x_ref.shape[0]

    def branches(r):
        rows = slice(r * ROW_CHUNK, (r + 1) * ROW_CHUNK)
        pa = jnp.dot(ya_ref[rows, :], wa_ref[...], preferred_element_type=F32)
        ph = jnp.dot(yh_ref[rows, :], wh_ref[...], preferred_element_type=F32)
        return pa, ph

    n_chunks = tm // ROW_CHUNK
    nxt = branches(0)
    for r in range(n_chunks):
        rows = slice(r * ROW_CHUNK, (r + 1) * ROW_CHUNK)
        pa, ph = nxt
        if r + 1 < n_chunks:
            nxt = branches(r + 1)
        merged = (gates_ref[rows, :D].astype(F32) * pa
                  + gates_ref[rows, D:].astype(F32) * ph)
        mix = jnp.dot(merged.astype(BF16), wo_ref[...], preferred_element_type=F32)
        o_ref[rows, :] = _layer_norm(alpha * x_ref[rows, :] + mix, g_ref[...], b_ref[...])


def _merge(ya, yh, gates, x, wa, wh, wo, g, b, alpha, tm=1024):
    N, D = x.shape
    row = lambda w: pl.BlockSpec((tm, w), lambda i: (i, 0))
    return pl.pallas_call(
        functools.partial(_merge_kernel, alpha=alpha),
        grid=(N // tm,),
        in_specs=[row(ya.shape[1]), row(yh.shape[1]), row(gates.shape[1]), row(D),
                  _const_spec(wa.shape), _const_spec(wh.shape), _const_spec(wo.shape),
                  _const_spec(g.shape), _const_spec(b.shape)],
        out_specs=row(D),
        out_shape=jax.ShapeDtypeStruct((N, D), F32),
        compiler_params=_cparams(1),
        name="merge_ln",
    )(ya, yh, gates, x, wa, wh, wo, g, b)


def _ffn_kernel(h_ref, wg_ref, wu_ref, cw_ref, cb_ref, wd_ref, g_ref, b_ref, o_ref,
                tail_ref, gbuf_ref, *, alpha, tm):
    d_ff = wg_ref.shape[1]
    PAD = 8

    @pl.when(pl.program_id(1) == 0)
    def _():
        tail_ref[...] = jnp.zeros_like(tail_ref)

    RC = ROW_CHUNK
    nfc = d_ff // FFN_FC
    steps = [(r, fc) for r in range(tm // RC) for fc in range(nfc)]
    hb = {}

    def gate_up(r, fc):
        if r not in hb:
            hb[r] = h_ref[r * RC:(r + 1) * RC, :].astype(BF16)
        cols = slice(fc * FFN_FC, (fc + 1) * FFN_FC)
        return (jnp.dot(hb[r], wg_ref[:, cols], preferred_element_type=F32),
                jnp.dot(hb[r], wu_ref[:, cols], preferred_element_type=F32))

    nxt = gate_up(*steps[0])
    acc = None
    for t, (r, fc) in enumerate(steps):
        rows = slice(r * RC, (r + 1) * RC)
        cols = slice(fc * FFN_FC, (fc + 1) * FFN_FC)
        gte, up = nxt
        if t + 1 < len(steps):
            nxt = gate_up(*steps[t + 1])
        gbuf = gbuf_ref.at[t % 2]
        gbuf[0:PAD, :] = tail_ref[:, cols]
        gbuf[PAD:, :] = gte
        tail_ref[:, cols] = gte[RC - PAD:, :]
        conv = (gte * cw_ref[2:3, cols]
                + gbuf[PAD - 1:PAD - 1 + RC, :] * cw_ref[1:2, cols]
                + gbuf[PAD - 2:PAD - 2 + RC, :] * cw_ref[0:1, cols]
                + cb_ref[:, cols])
        act = jax.nn.gelu(conv) * up
        down = jnp.dot(act.astype(BF16), wd_ref[cols, :], preferred_element_type=F32)
        acc = down if fc == 0 else acc + down
        if fc == nfc - 1:
            o_ref[rows, :] = _layer_norm(alpha * h_ref[rows, :] + acc, g_ref[...], b_ref[...])


def _ffn(h, wg, wu, cw, cb, wd, g, b, alpha, tm=1024):
    B, S, D = h.shape
    d_ff = wg.shape[1]
    row = pl.BlockSpec((None, tm, D), lambda bi, s: (bi, s, 0))
    return pl.pallas_call(
        functools.partial(_ffn_kernel, alpha=alpha, tm=tm),
        grid=(B, S // tm),
        in_specs=[row, _const_spec(wg.shape), _const_spec(wu.shape), _const_spec(cw.shape),
                  _const_spec(cb.shape), _const_spec(wd.shape),
                  _const_spec(g.shape), _const_spec(b.shape)],
        out_specs=row,
        out_shape=jax.ShapeDtypeStruct((B, S, D), F32),
        scratch_shapes=[pltpu.VMEM((8, d_ff), F32),
                        pltpu.VMEM((2, ROW_CHUNK + 8, FFN_FC), F32)],
        compiler_params=_cparams(2),
        name="ffn_ln",
    )(h, wg, wu, cw, cb, wd, g, b)


def kernel(x, w_in, b_gate, lambda_q1, lambda_k1, lambda_q2, lambda_k2, attn_subln_w,
           rel_bias, hgrn_lb_logits, hgrn_norm_w, w_branch_attn, w_branch_hgrn, w_out,
           ln1_g, ln1_b, w_ffn_gate, w_ffn_up, ffn_conv_w, ffn_conv_b, w_ffn_down,
           ln2_g, ln2_b):
    B, S, D = x.shape
    depth = w_in.shape[0]
    att_qk = ATT_HEADS * 2 * ATT_DH
    att_w = ATT_HEADS * HEAD_W
    hg_w = HG_HEADS * HEAD_W
    offs = {"aq": 0, "ak": att_qk, "av": 2 * att_qk}
    offs["hq"] = offs["av"] + att_w
    offs["hf"] = offs["hq"] + hg_w
    offs["hi"] = offs["hf"] + hg_w
    offs["hg"] = offs["hi"] + hg_w
    offs["gate"] = offs["hg"] + hg_w
    alpha = (2 * depth) ** 0.25

    lb_all = jnp.cumsum(jax.nn.softmax(hgrn_lb_logits.astype(F32), axis=0), axis=0)

    h = x
    for l in range(depth):
        lam_init = 0.8 - 0.6 * math.exp(-0.3 * l)
        lam = (jnp.exp(jnp.sum(lambda_q1[l].astype(F32) * lambda_k1[l].astype(F32)))
               - jnp.exp(jnp.sum(lambda_q2[l].astype(F32) * lambda_k2[l].astype(F32)))
               + lam_init).reshape(1)

        w_l = w_in[l]
        w_bf = w_l.astype(BF16)
        wvt_bf = w_l[:, offs["av"]:offs["hq"]].T.astype(BF16)
        (q, k, vt, hq, kk, hi, sog, logf, gates) = _inproj(
            h, w_bf, wvt_bf, b_gate[l][None, :], lb_all[l][None, :], offs)

        ya = _attention(q, k, vt, rel_bias.astype(F32),
                        attn_subln_w[l][:, None].astype(F32), lam, 1.0 - lam_init)
        yh = _hgrn(hq, kk, hi, sog, logf, hgrn_norm_w[l][None, :].astype(F32))

        h1 = _merge(ya.reshape(B * S, att_w), yh.reshape(B * S, hg_w),
                    gates.reshape(B * S, 2 * D), h.reshape(B * S, D),
                    w_branch_attn[l].astype(BF16), w_branch_hgrn[l].astype(BF16),
                    w_out[l].astype(BF16), ln1_g[l][None, :], ln1_b[l][None, :], alpha)

        h = _ffn(h1.reshape(B, S, D), w_ffn_gate[l].astype(BF16), w_ffn_up[l].astype(BF16),
                 ffn_conv_w[l], ffn_conv_b[l][None, :], w_ffn_down[l].astype(BF16),
                 ln2_g[l][None, :], ln2_b[l][None, :], alpha)
    return h
```

```python
import functools
import math

import numpy as np
import jax
import jax.numpy as jnp
from jax import lax
from jax.experimental import pallas as pl
from jax.experimental.pallas import tpu as pltpu

F32 = jnp.float32
BF16 = jnp.bfloat16

CHUNK = 64
ATT_HEADS = 4
ATT_DH = 64
HG_HEADS = 4
HEAD_W = 128
NUM_BUCKETS = 32
MAX_DISTANCE = 128
EPS = 1e-5
NEG = -1e30

ATT_BQ = 128
ATT_BK = 256
ATT_G = 4
ATT_GK = ATT_G * ATT_BQ // ATT_BK
ATT_NEAR = (0, -1, -2)
ATT_VR = HEAD_W + 16
LOG2E = math.log2(math.e)
HG_C = 128
HG_LEVELS = (64, 32, 16, 8, 4)
HG_DIAG = 4
FFN_FC = 256
ROW_CHUNK = 256

VMEM_LIMIT = 56 * 1024 * 1024


def _cparams(n_axes):
    return pltpu.CompilerParams(
        dimension_semantics=("arbitrary",) * n_axes,
        vmem_limit_bytes=VMEM_LIMIT)


def _const_spec(shape):
    nd = len(shape)
    return pl.BlockSpec(shape, lambda *_: (0,) * nd, pipeline_mode=pl.Buffered(1))


def _inproj_kernel(x_ref, w_ref, wvt_ref, bg_ref, lb_ref,
                   q_ref, k_ref, vt_ref, hq_ref, kk_ref, hi_ref, sog_ref,
                   logf_ref, gates_ref, *, tm, offs):
    xb = x_ref[...].astype(BF16)

    def mm(off, width=512):
        return jnp.dot(xb, w_ref[:, off:off + width], preferred_element_type=F32)

    q_ref[...] = (mm(offs["aq"]) * (ATT_DH ** -0.5 * LOG2E)).astype(BF16)
    k_ref[...] = mm(offs["ak"]).astype(BF16)
    for t in range(tm // ATT_BK):
        vt = lax.dot_general(wvt_ref[...], xb[t * ATT_BK:(t + 1) * ATT_BK],
                             (((1,), (1,)), ((), ())), preferred_element_type=F32)
        for hd in range(ATT_HEADS):
            vt_ref[t, hd * ATT_VR:hd * ATT_VR + HEAD_W, :] = (
                vt[hd * HEAD_W:(hd + 1) * HEAD_W].astype(BF16))
            vt_ref[t, hd * ATT_VR + HEAD_W:(hd + 1) * ATT_VR, :] = jnp.ones(
                (ATT_VR - HEAD_W, ATT_BK), BF16)
    hq_ref[...] = mm(offs["hq"]).astype(BF16)
    sig = jax.nn.sigmoid(mm(offs["hf"]))
    lb = lb_ref[...]
    logf_ref[...] = jnp.log2(lb + (1.0 - lb) * sig)
    kk_ref[...] = ((1.0 - lb) * (1.0 - sig)).astype(BF16)
    hi_ref[...] = mm(offs["hi"]).astype(BF16)
    hog = mm(offs["hg"])
    sog_ref[...] = (hog * jax.nn.sigmoid(hog)).astype(BF16)
    for j in range(gates_ref.shape[-1] // 512):
        g = mm(offs["gate"] + j * 512) + bg_ref[:, j * 512:(j + 1) * 512]
        gates_ref[:, j * 512:(j + 1) * 512] = jax.nn.sigmoid(g).astype(BF16)


def _inproj(x, w_bf, wvt_bf, b_gate, lb, offs, tm=512):
    B, S, D = x.shape
    ncols = w_bf.shape[1]
    gate_w = ncols - offs["gate"]
    row = lambda w: pl.BlockSpec((None, tm, w), lambda b, s: (b, s, 0))
    out_shape = (
        jax.ShapeDtypeStruct((B, S, 512), BF16),
        jax.ShapeDtypeStruct((B, S, 512), BF16),
        jax.ShapeDtypeStruct((B, S // ATT_BK, ATT_HEADS * ATT_VR, ATT_BK), BF16),
        jax.ShapeDtypeStruct((B, S, 512), BF16),
        jax.ShapeDtypeStruct((B, S, 512), BF16),
        jax.ShapeDtypeStruct((B, S, 512), BF16),
        jax.ShapeDtypeStruct((B, S, 512), BF16),
        jax.ShapeDtypeStruct((B, S, 512), F32),
        jax.ShapeDtypeStruct((B, S, gate_w), BF16),
    )
    out_specs = (
        row(512), row(512),
        pl.BlockSpec((None, tm // ATT_BK, ATT_HEADS * ATT_VR, ATT_BK),
                     lambda b, s: (b, s, 0, 0)),
        row(512), row(512), row(512), row(512), row(512), row(gate_w),
    )
    return pl.pallas_call(
        functools.partial(_inproj_kernel, tm=tm, offs=offs),
        grid=(B, S // tm),
        in_specs=[row(D), _const_spec(w_bf.shape), _const_spec(wvt_bf.shape),
                  _const_spec(b_gate.shape), _const_spec(lb.shape)],
        out_specs=out_specs,
        out_shape=out_shape,
        compiler_params=_cparams(2),
        name="inproj",
    )(x, w_bf, wvt_bf, b_gate, lb)


def _rel_bucket_np(rel):
    nb = NUM_BUCKETS // 2
    max_exact = nb // 2
    ret = np.where(rel > 0, nb, 0)
    n = np.abs(rel)
    nf = np.maximum(n, 1).astype(np.float32)
    large = max_exact + (np.log(nf / np.float32(max_exact))
                         / np.float32(math.log(MAX_DISTANCE / max_exact))
                         * np.float32(nb - max_exact)).astype(np.int32)
    large = np.minimum(large, nb - 1)
    return ret + np.where(n < max_exact, n, large)


def _attn_bucket_tiles():
    i = np.arange(ATT_BQ)[None, :]
    j = np.arange(ATT_BK)[:, None]
    tiles = []
    for d in ATT_NEAR:
        kpos = d * ATT_BQ + j
        live = (kpos // CHUNK) <= (i // CHUNK)
        tiles.append(np.where(live, _rel_bucket_np(kpos - i), -1))
    far = _rel_bucket_np(np.asarray((min(ATT_NEAR) - 1) * ATT_BQ + ATT_BK - 1))
    assert int(far) == NUM_BUCKETS // 2 - 1
    return np.stack(tiles).astype(np.int32)


def _attn_kernel(lam_ref, rb_ref, bkt_ref, q_ref, k_ref, vt_ref, w_ref, o_ref,
                 bias_ref, qs_ref, s0_ref, s1_ref, p0_ref, p1_ref, a0_ref, a1_ref,
                 m_ref, acc_ref, *, out_scale):
    S = q_ref.shape[0]
    nkb = S // ATT_BK
    W2 = 2 * ATT_BQ
    near = tuple(range(ATT_GK)) + (-1,)
    lam = lam_ref[0]
    h = pl.program_id(0)

    @pl.when(pl.program_id(1) == 0)
    def _():
        far = rb_ref[NUM_BUCKETS // 2 - 1, h]
        for n, kbrel in enumerate(near):
            for j in range(ATT_G):
                d = (kbrel * ATT_BK) // ATT_BQ - j
                cols = slice(j * W2, (j + 1) * W2)
                if d > max(ATT_NEAR):
                    bias_ref[n, :, cols] = jnp.full((ATT_BK, W2), NEG, F32)
                elif d < min(ATT_NEAR):
                    bias_ref[n, :, cols] = jnp.zeros((ATT_BK, W2), F32)
        for t, dt in enumerate(ATT_NEAR):
            bk = bkt_ref[t]
            tile = jnp.full(bk.shape, NEG, F32)
            for bucket in range(NUM_BUCKETS):
                tile = jnp.where(bk == bucket, (rb_ref[bucket, h] - far) * LOG2E, tile)
            tile2 = jnp.concatenate([tile, tile], axis=1)
            for n, kbrel in enumerate(near):
                for j in range(ATT_G):
                    if (kbrel * ATT_BK) // ATT_BQ - j == dt:
                        bias_ref[n, :, j * W2:(j + 1) * W2] = tile2

    lane = lax.broadcasted_iota(jnp.int32, (ATT_BQ, HEAD_W), 1)
    first_map = lane < ATT_DH

    s_bufs, p_bufs, a_bufs = (s0_ref, s1_ref), (p0_ref, p1_ref), (a0_ref, a1_ref)

    def scores(step, j, buf):
        kb, bias_idx = step
        cols = slice(j * W2, (j + 1) * W2)
        kblk = k_ref[pl.ds(pl.multiple_of(kb * ATT_BK, ATT_BK), ATT_BK), :]
        s = lax.dot_general(kblk, qs_ref[cols, :], (((1,), (1,)), ((), ())),
                            preferred_element_type=F32)
        if bias_idx is not None:
            s = s + bias_ref[bias_idx, :, cols]
        s_bufs[buf][:, cols] = s

    def soft(j, buf, first=False):
        cols = slice(j * W2, (j + 1) * W2)
        s = s_bufs[buf][:, cols]
        m_new = jnp.max(s, axis=0, keepdims=True)
        if first:
            a_bufs[buf][:, cols] = jnp.ones_like(m_new)
        else:
            m_old = m_ref[:, cols]
            m_new = jnp.maximum(m_old, m_new)
            a_bufs[buf][:, cols] = jnp.exp2(m_old - m_new)
        m_ref[:, cols] = m_new
        p_bufs[buf][:, cols] = jnp.exp2(s - m_new).astype(BF16)

    def accum(kb, j, buf):
        cols = slice(j * W2, (j + 1) * W2)
        acc_ref[:, cols] = a_bufs[buf][:, cols] * acc_ref[:, cols] + jnp.dot(
            vt_ref[kb], p_bufs[buf][:, cols], preferred_element_type=F32)

    AHEAD = 3

    def two_steps(steps, first=False, prev_kb=None, dead=()):
        for k in range(2 * ATT_G):
            ua = k + AHEAD
            if (ua // ATT_G, ua % ATT_G) not in dead:
                scores(steps[ua // ATT_G], ua % ATT_G, (ua // ATT_G) % 2)
            uc = k - 1
            if k >= 1 and (uc // ATT_G, uc % ATT_G) not in dead:
                accum(steps[uc // ATT_G][0], uc % ATT_G, uc // ATT_G)
            elif k == 0 and prev_kb is not None:
                accum(prev_kb, ATT_G - 1, 1)
            if (k // ATT_G, k % ATT_G) not in dead:
                soft(k % ATT_G, k // ATT_G, first=first and k < ATT_G)

    def group(g, carry):
        q0 = pl.multiple_of(g * (ATT_G * ATT_BQ), ATT_G * ATT_BQ)
        for j in range(ATT_G):
            q = q_ref[pl.ds(q0 + j * ATT_BQ, ATT_BQ), :]
            zero = jnp.zeros_like(q)
            qs_ref[j * W2:j * W2 + ATT_BQ, :] = jnp.where(first_map, q, zero)
            qs_ref[j * W2 + ATT_BQ:(j + 1) * W2, :] = jnp.where(first_map, zero, q)
        kb0 = g * ATT_GK
        acc_ref[...] = jnp.zeros(acc_ref.shape, F32)

        n_steps = jnp.where(g > 0, ATT_GK + 1 + (kb0 - 1), ATT_GK)
        inside = [(kb0 + i, i) for i in range(ATT_GK)]
        before = (jnp.maximum(kb0 - 1, 0), ATT_GK)

        def far(n):
            return (jnp.clip(n - (ATT_GK + 1), 0, nkb - 1), None)

        for j in range(AHEAD):
            scores(inside[0], j, 0)
        dead = {(i, j) for i in range(ATT_GK) for j in range(ATT_G)
                if (i * ATT_BK) // ATT_BQ - j > max(ATT_NEAR)}
        two_steps(inside + [before], first=True, dead=dead)

        @pl.when(g > 0)
        def _():
            two_steps([before, far(3), far(4)], prev_kb=inside[1][0])

        def far_steps(i, c):
            n = 2 * i
            two_steps([far(n), far(n + 1), far(n + 2)], prev_kb=far(n - 1)[0])
            return c

        lax.fori_loop(2, n_steps // 2, far_steps, 0)

        last_kb = jnp.where(g > 0, far(n_steps - 1)[0], inside[1][0])
        accum(last_kb, ATT_G - 1, 1)

        for j in range(ATT_G):
            cols = slice(j * W2, (j + 1) * W2)
            on = acc_ref[:HEAD_W, cols] * (1.0 / acc_ref[HEAD_W:HEAD_W + 1, cols])
            o = on[:, :ATT_BQ] - lam * on[:, ATT_BQ:]
            ms = jnp.mean(o * o, axis=0, keepdims=True)
            y = o * lax.rsqrt(ms + EPS) * w_ref[...] * out_scale
            o_ref[pl.ds(q0 + j * ATT_BQ, ATT_BQ), :] = y.T.astype(BF16)
        return carry

    lax.fori_loop(0, S // (ATT_G * ATT_BQ), group, 0)


def _attention(q, k, vt, rel_bias, subln_w, lam, out_scale):
    B, S, _ = q.shape
    nkb = S // ATT_BK
    GW = ATT_G * 2 * ATT_BQ
    assert ATT_GK == 2 and ATT_BK == 2 * ATT_BQ
    bkt = jnp.asarray(_attn_bucket_tiles())
    head = pl.BlockSpec((None, S, HEAD_W), lambda h, b: (b, 0, h))
    smem = pl.BlockSpec(memory_space=pltpu.SMEM)
    return pl.pallas_call(
        functools.partial(_attn_kernel, out_scale=out_scale),
        grid=(ATT_HEADS, B),
        in_specs=[
            smem, smem, _const_spec(bkt.shape),
            head, head,
            pl.BlockSpec((None, nkb, ATT_VR, ATT_BK), lambda h, b: (b, 0, h, 0)),
            pl.BlockSpec((HEAD_W, 1), lambda h, b: (0, 0)),
        ],
        out_specs=head,
        out_shape=jax.ShapeDtypeStruct((B, S, ATT_HEADS * HEAD_W), BF16),
        scratch_shapes=[
            pltpu.VMEM((ATT_GK + 1, ATT_BK, GW), F32),
            pltpu.VMEM((GW, HEAD_W), BF16),
            pltpu.VMEM((ATT_BK, GW), F32),
            pltpu.VMEM((ATT_BK, GW), F32),
            pltpu.VMEM((ATT_BK, GW), BF16),
            pltpu.VMEM((ATT_BK, GW), BF16),
            pltpu.VMEM((1, GW), F32),
            pltpu.VMEM((1, GW), F32),
            pltpu.VMEM((1, GW), F32),
            pltpu.VMEM((ATT_VR, GW), F32),
        ],
        compiler_params=_cparams(2),
        name="diffattn",
    )(lam, rel_bias, bkt, q, k, vt, subln_w)


def _hgrn_kernel(hq_ref, kk_ref, hi_ref, sog_ref, lf_ref, w_ref, y_ref,
                 state_ref, kbuf_ref, bbuf_ref, *, tc):
    C, H, W = HG_C, HG_HEADS, HEAD_W
    PAD = 8

    @pl.when(pl.program_id(1) == 0)
    def _():
        state_ref[...] = jnp.zeros_like(state_ref)

    kbuf_ref[:, 0:PAD, :] = jnp.zeros((H, PAD, W), F32)
    bbuf_ref[:, 0:PAD, :] = jnp.zeros((H, PAD, W), F32)

    row = lax.broadcasted_iota(jnp.int32, (C, W), 0)
    r2 = lax.broadcasted_iota(jnp.int32, (C, C), 0)
    c2 = lax.broadcasted_iota(jnp.int32, (C, C), 1)
    tri = (c2 <= r2).astype(BF16)
    is_q = {hs: ((row // hs) % 2) == 1 for hs in HG_LEVELS}
    sign = {hs: jnp.where(is_q[hs], 1.0, -1.0).astype(F32) for hs in HG_LEVELS}
    keep = {hs: ((r2 // (2 * hs)) == (c2 // (2 * hs)))
            & (((r2 // hs) % 2) == 1) & (((c2 // hs) % 2) == 0) for hs in HG_LEVELS}
    same_blk = (r2 // HG_DIAG) == (c2 // HG_DIAG)
    on_diag = {d: ((r2 - c2) == d) & same_blk for d in range(HG_DIAG)}

    def level_ref(b, hs):
        if hs >= PAD:
            parts = [jnp.broadcast_to(b[g * 2 * hs + hs - 1:g * 2 * hs + hs, :], (2 * hs, W))
                     for g in range(C // (2 * hs))]
            return parts[0] if len(parts) == 1 else jnp.concatenate(parts, axis=0)
        b3 = b.reshape(C // (2 * hs), 2 * hs, W)
        return jnp.broadcast_to(b3[:, hs - 1:hs, :], b3.shape).reshape(C, W)

    def chunk(c, carry):
        r0 = pl.multiple_of(c * C, C)
        lf = lf_ref[pl.ds(r0, C), :]
        p0 = lf.astype(BF16)
        p1 = (lf - p0.astype(F32)).astype(BF16)
        b_all = (jnp.dot(tri, p0, preferred_element_type=F32)
                 + jnp.dot(tri, p1, preferred_element_type=F32))

        heads = range(H)
        cols = [slice(h * W, (h + 1) * W) for h in heads]
        b = [b_all[:, cols[h]] for h in heads]
        qf = [hq_ref[pl.ds(r0, C), cols[h]].astype(F32) for h in heads]
        kf = [kk_ref[pl.ds(r0, C), cols[h]].astype(F32) for h in heads]
        iv = [hi_ref[pl.ds(r0, C), cols[h]] for h in heads]
        st = [state_ref[h] for h in heads]
        b_last = [b[h][C - 1:C, :] for h in heads]

        inter = [lax.dot_general((qf[h] * jnp.exp2(b[h])).astype(BF16), st[h].astype(BF16),
                                 (((1,), (1,)), ((), ())), preferred_element_type=F32)
                 for h in heads]

        a = [None] * H
        for hs in HG_LEVELS:
            for h in heads:
                e = jnp.exp2((b[h] - level_ref(b[h], hs)) * sign[hs])
                y = (jnp.where(is_q[hs], qf[h], kf[h]) * e).astype(BF16)
                sc = lax.dot_general(y, y, (((1,), (1,)), ((), ())),
                                     preferred_element_type=F32)
                sc = jnp.where(keep[hs], sc, 0.0)
                a[h] = sc if a[h] is None else a[h] + sc
        for h in heads:
            kbuf_ref[h, PAD:, :] = kf[h]
            bbuf_ref[h, PAD:, :] = b[h]
        for d in range(HG_DIAG):
            for h in heads:
                if d == 0:
                    term = qf[h] * kf[h]
                else:
                    ks = kbuf_ref[h, PAD - d:PAD - d + C, :]
                    bs = bbuf_ref[h, PAD - d:PAD - d + C, :]
                    term = qf[h] * ks * jnp.exp2(b[h] - bs)
                dsum = jnp.sum(term, axis=-1, keepdims=True)
                a[h] = a[h] + jnp.where(on_diag[d], dsum, 0.0)

        o = [inter[h] + jnp.dot(a[h].astype(BF16), iv[h], preferred_element_type=F32)
             for h in heads]

        for h in heads:
            k_end = (kf[h] * jnp.exp2(b_last[h] - b[h])).astype(BF16)
            upd = lax.dot_general(iv[h], k_end, (((0,), (0,)), ((), ())),
                                  preferred_element_type=F32)
            state_ref[h] = st[h] * jnp.exp2(b_last[h]) + upd

        for h in heads:
            ms = jnp.mean(o[h] * o[h], axis=-1, keepdims=True)
            y = o[h] * lax.rsqrt(ms + EPS) * w_ref[...]
            y = y * sog_ref[pl.ds(r0, C), cols[h]].astype(F32)
            y_ref[pl.ds(r0, C), cols[h]] = y.astype(BF16)
        return carry

    lax.fori_loop(0, tc // C, chunk, 0)


def _hgrn(hq, kk, hi, sog, logf, norm_w, tc=512):
    B, S, W = hq.shape
    blk = pl.BlockSpec((None, tc, W), lambda b, s: (b, s, 0))
    return pl.pallas_call(
        functools.partial(_hgrn_kernel, tc=tc),
        grid=(B, S // tc),
        in_specs=[blk, blk, blk, blk, blk, pl.BlockSpec((1, HEAD_W), lambda b, s: (0, 0))],
        out_specs=blk,
        out_shape=jax.ShapeDtypeStruct((B, S, W), BF16),
        scratch_shapes=[pltpu.VMEM((HG_HEADS, HEAD_W, HEAD_W), F32),
                        pltpu.VMEM((HG_HEADS, HG_C + 8, HEAD_W), F32),
                        pltpu.VMEM((HG_HEADS, HG_C + 8, HEAD_W), F32)],
        compiler_params=_cparams(2),
        name="hgrn2",
    )(hq, kk, hi, sog, logf, norm_w)


def _layer_norm(z, g, b):
    mu = jnp.mean(z, axis=-1, keepdims=True)
    zc = z - mu
    var = jnp.mean(zc * zc, axis=-1, keepdims=True)
    return zc * lax.rsqrt(var + EPS) * g + b


def _merge_kernel(ya_ref, yh_ref, gates_ref, x_ref, wa_ref, wh_ref, wo_ref,
                  g_ref, b_ref, o_ref, *, alpha):
    D = x_ref.shape[-1]
    tm = x_ref.shape[0]

    def branches(r):
        rows = slice(r * ROW_CHUNK, (r + 1) * ROW_CHUNK)
        pa = jnp.dot(ya_ref[rows, :], wa_ref[...], preferred_element_type=F32)
        ph = jnp.dot(yh_ref[rows, :], wh_ref[...], preferred_element_type=F32)
        return pa, ph

    n_chunks = tm // ROW_CHUNK
    nxt = branches(0)
    for r in range(n_chunks):
        rows = slice(r * ROW_CHUNK, (r + 1) * ROW_CHUNK)
        pa, ph = nxt
        if r + 1 < n_chunks:
            nxt = branches(r + 1)
        merged = (gates_ref[rows, :D].astype(F32) * pa
                  + gates_ref[rows, D:].astype(F32) * ph)
        mix = jnp.dot(merged.astype(BF16), wo_ref[...], preferred_element_type=F32)
        o_ref[rows, :] = _layer_norm(alpha * x_ref[rows, :] + mix, g_ref[...], b_ref[...])


def _merge(ya, yh, gates, x, wa, wh, wo, g, b, alpha, tm=1024):
    N, D = x.shape
    row = lambda w: pl.BlockSpec((tm, w), lambda i: (i, 0))
    return pl.pallas_call(
        functools.partial(_merge_kernel, alpha=alpha),
        grid=(N // tm,),
        in_specs=[row(ya.shape[1]), row(yh.shape[1]), row(gates.shape[1]), row(D),
                  _const_spec(wa.shape), _const_spec(wh.shape), _const_spec(wo.shape),
                  _const_spec(g.shape), _const_spec(b.shape)],
        out_specs=row(D),
        out_shape=jax.ShapeDtypeStruct((N, D), F32),
        compiler_params=_cparams(1),
        name="merge_ln",
    )(ya, yh, gates, x, wa, wh, wo, g, b)


def _ffn_kernel(h_ref, wg_ref, wu_ref, cw_ref, cb_ref, wd_ref, g_ref, b_ref, o_ref,
                tail_ref, gbuf_ref, *, alpha, tm):
    d_ff = wg_ref.shape[1]
    PAD = 8

    @pl.when(pl.program_id(1) == 0)
    def _():
        tail_ref[...] = jnp.zeros_like(tail_ref)

    RC = ROW_CHUNK
    nfc = d_ff // FFN_FC
    steps = [(r, fc) for r in range(tm // RC) for fc in range(nfc)]
    hb = {}

    def gate_up(r, fc):
        if r not in hb:
            hb[r] = h_ref[r * RC:(r + 1) * RC, :].astype(BF16)
        cols = slice(fc * FFN_FC, (fc + 1) * FFN_FC)
        return (jnp.dot(hb[r], wg_ref[:, cols], preferred_element_type=F32),
                jnp.dot(hb[r], wu_ref[:, cols], preferred_element_type=F32))

    nxt = gate_up(*steps[0])
    acc = None
    for t, (r, fc) in enumerate(steps):
        rows = slice(r * RC, (r + 1) * RC)
        cols = slice(fc * FFN_FC, (fc + 1) * FFN_FC)
        gte, up = nxt
        if t + 1 < len(steps):
            nxt = gate_up(*steps[t + 1])
        gbuf = gbuf_ref.at[t % 2]
        gbuf[0:PAD, :] = tail_ref[:, cols]
        gbuf[PAD:, :] = gte
        tail_ref[:, cols] = gte[RC - PAD:, :]
        conv = (gte * cw_ref[2:3, cols]
                + gbuf[PAD - 1:PAD - 1 + RC, :] * cw_ref[1:2, cols]
                + gbuf[PAD - 2:PAD - 2 + RC, :] * cw_ref[0:1, cols]
                + cb_ref[:, cols])
        act = jax.nn.gelu(conv) * up
        down = jnp.dot(act.astype(BF16), wd_ref[cols, :], preferred_element_type=F32)
        acc = down if fc == 0 else acc + down
        if fc == nfc - 1:
            o_ref[rows, :] = _layer_norm(alpha * h_ref[rows, :] + acc, g_ref[...], b_ref[...])


def _ffn(h, wg, wu, cw, cb, wd, g, b, alpha, tm=1024):
    B, S, D = h.shape
    d_ff = wg.shape[1]
    row = pl.BlockSpec((None, tm, D), lambda bi, s: (bi, s, 0))
    return pl.pallas_call(
        functools.partial(_ffn_kernel, alpha=alpha, tm=tm),
        grid=(B, S // tm),
        in_specs=[row, _const_spec(wg.shape), _const_spec(wu.shape), _const_spec(cw.shape),
                  _const_spec(cb.shape), _const_spec(wd.shape),
                  _const_spec(g.shape), _const_spec(b.shape)],
        out_specs=row,
        out_shape=jax.ShapeDtypeStruct((B, S, D), F32),
        scratch_shapes=[pltpu.VMEM((8, d_ff), F32),
                        pltpu.VMEM((2, ROW_CHUNK + 8, FFN_FC), F32)],
        compiler_params=_cparams(2),
        name="ffn_ln",
    )(h, wg, wu, cw, cb, wd, g, b)


def kernel(x, w_in, b_gate, lambda_q1, lambda_k1, lambda_q2, lambda_k2, attn_subln_w,
           rel_bias, hgrn_lb_logits, hgrn_norm_w, w_branch_attn, w_branch_hgrn, w_out,
           ln1_g, ln1_b, w_ffn_gate, w_ffn_up, ffn_conv_w, ffn_conv_b, w_ffn_down,
           ln2_g, ln2_b):
    B, S, D = x.shape
    depth = w_in.shape[0]
    att_qk = ATT_HEADS * 2 * ATT_DH
    att_w = ATT_HEADS * HEAD_W
    hg_w = HG_HEADS * HEAD_W
    offs = {"aq": 0, "ak": att_qk, "av": 2 * att_qk}
    offs["hq"] = offs["av"] + att_w
    offs["hf"] = offs["hq"] + hg_w
    offs["hi"] = offs["hf"] + hg_w
    offs["hg"] = offs["hi"] + hg_w
    offs["gate"] = offs["hg"] + hg_w
    alpha = (2 * depth) ** 0.25

    lb_all = jnp.cumsum(jax.nn.softmax(hgrn_lb_logits.astype(F32), axis=0), axis=0)

    h = x
    for l in range(depth):
        lam_init = 0.8 - 0.6 * math.exp(-0.3 * l)
        lam = (jnp.exp(jnp.sum(lambda_q1[l].astype(F32) * lambda_k1[l].astype(F32)))
               - jnp.exp(jnp.sum(lambda_q2[l].astype(F32) * lambda_k2[l].astype(F32)))
               + lam_init).reshape(1)

        w_l = w_in[l]
        w_bf = w_l.astype(BF16)
        wvt_bf = w_l[:, offs["av"]:offs["hq"]].T.astype(BF16)
        (q, k, vt, hq, kk, hi, sog, logf, gates) = _inproj(
            h, w_bf, wvt_bf, b_gate[l][None, :], lb_all[l][None, :], offs)

        ya = _attention(q, k, vt, rel_bias.astype(F32),
                        attn_subln_w[l][:, None].astype(F32), lam, 1.0 - lam_init)
        yh = _hgrn(hq, kk, hi, sog, logf, hgrn_norm_w[l][None, :].astype(F32))

        h1 = _merge(ya.reshape(B * S, att_w), yh.reshape(B * S, hg_w),
                    gates.reshape(B * S, 2 * D), h.reshape(B * S, D),
                    w_branch_attn[l].astype(BF16), w_branch_hgrn[l].astype(BF16),
                    w_out[l].astype(BF16), ln1_g[l][None, :], ln1_b[l][None, :], alpha)

        h = _ffn(h1.reshape(B, S, D), w_ffn_gate[l].astype(BF16), w_ffn_up[l].astype(BF16),
                 ffn_conv_w[l], ffn_conv_b[l][None, :], w_ffn_down[l].astype(BF16),
                 ln2_g[l][None, :], ln2_b[l][None, :], alpha)
    return h
```

```python
import functools
import math

import numpy as np
import jax
import jax.numpy as jnp
from jax import lax
from jax.experimental import pallas as pl
from jax.experimental.pallas import tpu as pltpu

F32 = jnp.float32
BF16 = jnp.bfloat16

CHUNK = 64
ATT_HEADS = 4
ATT_DH = 64
HG_HEADS = 4
HEAD_W = 128
NUM_BUCKETS = 32
MAX_DISTANCE = 128
EPS = 1e-5
NEG = -1e30

ATT_BQ = 128
ATT_BK = 256
ATT_G = 4
ATT_GK = ATT_G * ATT_BQ // ATT_BK
ATT_NEAR = (0, -1, -2)
ATT_VR = HEAD_W + 16
LOG2E = math.log2(math.e)
HG_C = 128
HG_LEVELS = (64, 32, 16, 8, 4)
HG_DIAG = 4
FFN_FC = 256
ROW_CHUNK = 256

VMEM_LIMIT = 56 * 1024 * 1024


def _cparams(n_axes):
    return pltpu.CompilerParams(
        dimension_semantics=("arbitrary",) * n_axes,
        vmem_limit_bytes=VMEM_LIMIT)


def _const_spec(shape):
    nd = len(shape)
    return pl.BlockSpec(shape, lambda *_: (0,) * nd, pipeline_mode=pl.Buffered(1))


def _inproj_kernel(x_ref, w_ref, lb_ref,
                   q_ref, k_ref, vt_ref, hq_ref, kk_ref, hi_ref, sog_ref, logf_ref,
                   *, tm, offs):
    lb = lb_ref[...]
    for r in range(tm // ROW_CHUNK):
        rows = slice(r * ROW_CHUNK, (r + 1) * ROW_CHUNK)
        xb = x_ref[rows, :].astype(BF16)

        def mm(off, xb=xb):
            return jnp.dot(xb, w_ref[:, off:off + 512], preferred_element_type=F32)

        q_ref[rows, :] = (mm(offs["aq"]) * (ATT_DH ** -0.5 * LOG2E)).astype(BF16)
        k_ref[rows, :] = mm(offs["ak"]).astype(BF16)
        v = mm(offs["av"])
        for hd in range(ATT_HEADS):
            vt_ref[r, hd * ATT_VR:hd * ATT_VR + HEAD_W, :] = (
                v[:, hd * HEAD_W:(hd + 1) * HEAD_W].T.astype(BF16))
            vt_ref[r, hd * ATT_VR + HEAD_W:(hd + 1) * ATT_VR, :] = jnp.ones(
                (ATT_VR - HEAD_W, ATT_BK), BF16)
        hq_ref[rows, :] = mm(offs["hq"]).astype(BF16)
        sig = jax.nn.sigmoid(mm(offs["hf"]))
        logf_ref[rows, :] = jnp.log2(lb + (1.0 - lb) * sig)
        kk_ref[rows, :] = ((1.0 - lb) * (1.0 - sig)).astype(BF16)
        hi_ref[rows, :] = mm(offs["hi"]).astype(BF16)
        hog = mm(offs["hg"])
        sog_ref[rows, :] = (hog * jax.nn.sigmoid(hog)).astype(BF16)


def _inproj(x, w_bf, lb, offs, tm=1024):
    B, S, D = x.shape
    assert ROW_CHUNK == ATT_BK
    row = lambda w: pl.BlockSpec((None, tm, w), lambda b, s: (b, s, 0))
    out_shape = (
        jax.ShapeDtypeStruct((B, S, 512), BF16),
        jax.ShapeDtypeStruct((B, S, 512), BF16),
        jax.ShapeDtypeStruct((B, S // ATT_BK, ATT_HEADS * ATT_VR, ATT_BK), BF16),
        jax.ShapeDtypeStruct((B, S, 512), BF16),
        jax.ShapeDtypeStruct((B, S, 512), BF16),
        jax.ShapeDtypeStruct((B, S, 512), BF16),
        jax.ShapeDtypeStruct((B, S, 512), BF16),
        jax.ShapeDtypeStruct((B, S, 512), F32),
    )
    out_specs = (
        row(512), row(512),
        pl.BlockSpec((None, tm // ATT_BK, ATT_HEADS * ATT_VR, ATT_BK),
                     lambda b, s: (b, s, 0, 0)),
        row(512), row(512), row(512), row(512), row(512),
    )
    return pl.pallas_call(
        functools.partial(_inproj_kernel, tm=tm, offs=offs),
        grid=(B, S // tm),
        in_specs=[row(D), _const_spec(w_bf.shape), _const_spec(lb.shape)],
        out_specs=out_specs,
        out_shape=out_shape,
        compiler_params=_cparams(2),
        name="inproj",
    )(x, w_bf, lb)


def _rel_bucket_np(rel):
    nb = NUM_BUCKETS // 2
    max_exact = nb // 2
    ret = np.where(rel > 0, nb, 0)
    n = np.abs(rel)
    nf = np.maximum(n, 1).astype(np.float32)
    large = max_exact + (np.log(nf / np.float32(max_exact))
                         / np.float32(math.log(MAX_DISTANCE / max_exact))
                         * np.float32(nb - max_exact)).astype(np.int32)
    large = np.minimum(large, nb - 1)
    return ret + np.where(n < max_exact, n, large)


def _attn_bucket_tiles():
    i = np.arange(ATT_BQ)[None, :]
    j = np.arange(ATT_BK)[:, None]
    tiles = []
    for d in ATT_NEAR:
        kpos = d * ATT_BQ + j
        live = (kpos // CHUNK) <= (i // CHUNK)
        tiles.append(np.where(live, _rel_bucket_np(kpos - i), -1))
    far = _rel_bucket_np(np.asarray((min(ATT_NEAR) - 1) * ATT_BQ + ATT_BK - 1))
    assert int(far) == NUM_BUCKETS // 2 - 1
    return np.stack(tiles).astype(np.int32)


def _attn_kernel(lam_ref, rb_ref, bkt_ref, q_ref, k_ref, vt_ref, w_ref, o_ref,
                 bias_ref, qs_ref, s0_ref, s1_ref, p0_ref, p1_ref, a0_ref, a1_ref,
                 m_ref, acc_ref, *, out_scale):
    S = q_ref.shape[0]
    nkb = S // ATT_BK
    W2 = 2 * ATT_BQ
    near = tuple(range(ATT_GK)) + (-1,)
    lam = lam_ref[0]
    h = pl.program_id(0)

    @pl.when(pl.program_id(1) == 0)
    def _():
        far = rb_ref[NUM_BUCKETS // 2 - 1, h]
        for n, kbrel in enumerate(near):
            for j in range(ATT_G):
                d = (kbrel * ATT_BK) // ATT_BQ - j
                cols = slice(j * W2, (j + 1) * W2)
                if d > max(ATT_NEAR):
                    bias_ref[n, :, cols] = jnp.full((ATT_BK, W2), NEG, F32)
                elif d < min(ATT_NEAR):
                    bias_ref[n, :, cols] = jnp.zeros((ATT_BK, W2), F32)
        for t, dt in enumerate(ATT_NEAR):
            bk = bkt_ref[t]
            tile = jnp.full(bk.shape, NEG, F32)
            for bucket in range(NUM_BUCKETS):
                tile = jnp.where(bk == bucket, (rb_ref[bucket, h] - far) * LOG2E, tile)
            tile2 = jnp.concatenate([tile, tile], axis=1)
            for n, kbrel in enumerate(near):
                for j in range(ATT_G):
                    if (kbrel * ATT_BK) // ATT_BQ - j == dt:
                        bias_ref[n, :, j * W2:(j + 1) * W2] = tile2

    lane = lax.broadcasted_iota(jnp.int32, (ATT_BQ, HEAD_W), 1)
    first_map = lane < ATT_DH

    s_bufs, p_bufs, a_bufs = (s0_ref, s1_ref), (p0_ref, p1_ref), (a0_ref, a1_ref)

    def scores(step, j, buf):
        kb, bias_idx = step
        cols = slice(j * W2, (j + 1) * W2)
        kblk = k_ref[pl.ds(pl.multiple_of(kb * ATT_BK, ATT_BK), ATT_BK), :]
        s = lax.dot_general(kblk, qs_ref[cols, :], (((1,), (1,)), ((), ())),
                            preferred_element_type=F32)
        if bias_idx is not None:
            s = s + bias_ref[bias_idx, :, cols]
        s_bufs[buf][:, cols] = s

    def soft(j, buf, first=False):
        cols = slice(j * W2, (j + 1) * W2)
        s = s_bufs[buf][:, cols]
        m_new = jnp.max(s, axis=0, keepdims=True)
        if first:
            a_bufs[buf][:, cols] = jnp.ones_like(m_new)
        else:
            m_old = m_ref[:, cols]
            m_new = jnp.maximum(m_old, m_new)
            a_bufs[buf][:, cols] = jnp.exp2(m_old - m_new)
        m_ref[:, cols] = m_new
        p_bufs[buf][:, cols] = jnp.exp2(s - m_new).astype(BF16)

    def accum(kb, j, buf):
        cols = slice(j * W2, (j + 1) * W2)
        acc_ref[:, cols] = a_bufs[buf][:, cols] * acc_ref[:, cols] + jnp.dot(
            vt_ref[kb], p_bufs[buf][:, cols], preferred_element_type=F32)

    AHEAD = 3

    def two_steps(steps, first=False, prev_kb=None, dead=()):
        for k in range(2 * ATT_G):
            ua = k + AHEAD
            if (ua // ATT_G, ua % ATT_G) not in dead:
                scores(steps[ua // ATT_G], ua % ATT_G, (ua // ATT_G) % 2)
            uc = k - 1
            if k >= 1 and (uc // ATT_G, uc % ATT_G) not in dead:
                accum(steps[uc // ATT_G][0], uc % ATT_G, uc // ATT_G)
            elif k == 0 and prev_kb is not None:
                accum(prev_kb, ATT_G - 1, 1)
            if (k // ATT_G, k % ATT_G) not in dead:
                soft(k % ATT_G, k // ATT_G, first=first and k < ATT_G)

    def group(g, carry):
        q0 = pl.multiple_of(g * (ATT_G * ATT_BQ), ATT_G * ATT_BQ)
        for j in range(ATT_G):
            q = q_ref[pl.ds(q0 + j * ATT_BQ, ATT_BQ), :]
            zero = jnp.zeros_like(q)
            qs_ref[j * W2:j * W2 + ATT_BQ, :] = jnp.where(first_map, q, zero)
            qs_ref[j * W2 + ATT_BQ:(j + 1) * W2, :] = jnp.where(first_map, zero, q)
        kb0 = g * ATT_GK
        acc_ref[...] = jnp.zeros(acc_ref.shape, F32)

        n_steps = jnp.where(g > 0, ATT_GK + 1 + (kb0 - 1), ATT_GK)
        inside = [(kb0 + i, i) for i in range(ATT_GK)]
        before = (jnp.maximum(kb0 - 1, 0), ATT_GK)

        def far(n):
            return (jnp.clip(n - (ATT_GK + 1), 0, nkb - 1), None)

        for j in range(AHEAD):
            scores(inside[0], j, 0)
        dead = {(i, j) for i in range(ATT_GK) for j in range(ATT_G)
                if (i * ATT_BK) // ATT_BQ - j > max(ATT_NEAR)}
        two_steps(inside + [before], first=True, dead=dead)

        @pl.when(g > 0)
        def _():
            two_steps([before, far(3), far(4)], prev_kb=inside[1][0])

        def far_steps(i, c):
            n = 2 * i
            two_steps([far(n), far(n + 1), far(n + 2)], prev_kb=far(n - 1)[0])
            return c

        lax.fori_loop(2, n_steps // 2, far_steps, 0)

        last_kb = jnp.where(g > 0, far(n_steps - 1)[0], inside[1][0])
        accum(last_kb, ATT_G - 1, 1)

        for j in range(ATT_G):
            cols = slice(j * W2, (j + 1) * W2)
            on = acc_ref[:HEAD_W, cols] * (1.0 / acc_ref[HEAD_W:HEAD_W + 1, cols])
            o = on[:, :ATT_BQ] - lam * on[:, ATT_BQ:]
            ms = jnp.mean(o * o, axis=0, keepdims=True)
            y = o * lax.rsqrt(ms + EPS) * w_ref[...] * out_scale
            o_ref[pl.ds(q0 + j * ATT_BQ, ATT_BQ), :] = y.T.astype(BF16)
        return carry

    lax.fori_loop(0, S // (ATT_G * ATT_BQ), group, 0)


def _attention(q, k, vt, rel_bias, subln_w, lam, out_scale):
    B, S, _ = q.shape
    nkb = S // ATT_BK
    GW = ATT_G * 2 * ATT_BQ
    assert ATT_GK == 2 and ATT_BK == 2 * ATT_BQ
    bkt = jnp.asarray(_attn_bucket_tiles())
    head = pl.BlockSpec((None, S, HEAD_W), lambda h, b: (b, 0, h))
    smem = pl.BlockSpec(memory_space=pltpu.SMEM)
    return pl.pallas_call(
        functools.partial(_attn_kernel, out_scale=out_scale),
        grid=(ATT_HEADS, B),
        in_specs=[
            smem, smem, _const_spec(bkt.shape),
            head, head,
            pl.BlockSpec((None, nkb, ATT_VR, ATT_BK), lambda h, b: (b, 0, h, 0)),
            pl.BlockSpec((HEAD_W, 1), lambda h, b: (0, 0)),
        ],
        out_specs=head,
        out_shape=jax.ShapeDtypeStruct((B, S, ATT_HEADS * HEAD_W), BF16),
        scratch_shapes=[
            pltpu.VMEM((ATT_GK + 1, ATT_BK, GW), F32),
            pltpu.VMEM((GW, HEAD_W), BF16),
            pltpu.VMEM((ATT_BK, GW), F32),
            pltpu.VMEM((ATT_BK, GW), F32),
            pltpu.VMEM((ATT_BK, GW), BF16),
            pltpu.VMEM((ATT_BK, GW), BF16),
            pltpu.VMEM((1, GW), F32),
            pltpu.VMEM((1, GW), F32),
            pltpu.VMEM((1, GW), F32),
            pltpu.VMEM((ATT_VR, GW), F32),
        ],
        compiler_params=_cparams(2),
        name="diffattn",
    )(lam, rel_bias, bkt, q, k, vt, subln_w)


def _hgrn_kernel(hq_ref, kk_ref, hi_ref, sog_ref, lf_ref, w_ref, y_ref,
                 state_ref, kbuf_ref, bbuf_ref, *, tc):
    C, H, W = HG_C, HG_HEADS, HEAD_W
    PAD = 8

    @pl.when(pl.program_id(1) == 0)
    def _():
        state_ref[...] = jnp.zeros_like(state_ref)

    kbuf_ref[:, 0:PAD, :] = jnp.zeros((H, PAD, W), F32)
    bbuf_ref[:, 0:PAD, :] = jnp.zeros((H, PAD, W), F32)

    row = lax.broadcasted_iota(jnp.int32, (C, W), 0)
    r2 = lax.broadcasted_iota(jnp.int32, (C, C), 0)
    c2 = lax.broadcasted_iota(jnp.int32, (C, C), 1)
    tri = (c2 <= r2).astype(BF16)
    is_q = {hs: ((row // hs) % 2) == 1 for hs in HG_LEVELS}
    sign = {hs: jnp.where(is_q[hs], 1.0, -1.0).astype(F32) for hs in HG_LEVELS}
    keep = {hs: ((r2 // (2 * hs)) == (c2 // (2 * hs)))
            & (((r2 // hs) % 2) == 1) & (((c2 // hs) % 2) == 0) for hs in HG_LEVELS}
    same_blk = (r2 // HG_DIAG) == (c2 // HG_DIAG)
    on_diag = {d: ((r2 - c2) == d) & same_blk for d in range(HG_DIAG)}

    def level_ref(b, hs):
        if hs >= PAD:
            parts = [jnp.broadcast_to(b[g * 2 * hs + hs - 1:g * 2 * hs + hs, :], (2 * hs, W))
                     for g in range(C // (2 * hs))]
            return parts[0] if len(parts) == 1 else jnp.concatenate(parts, axis=0)
        b3 = b.reshape(C // (2 * hs), 2 * hs, W)
        return jnp.broadcast_to(b3[:, hs - 1:hs, :], b3.shape).reshape(C, W)

    def chunk(c, carry):
        r0 = pl.multiple_of(c * C, C)
        lf = lf_ref[pl.ds(r0, C), :]
        p0 = lf.astype(BF16)
        p1 = (lf - p0.astype(F32)).astype(BF16)
        b_all = (jnp.dot(tri, p0, preferred_element_type=F32)
                 + jnp.dot(tri, p1, preferred_element_type=F32))

        heads = range(H)
        cols = [slice(h * W, (h + 1) * W) for h in heads]
        b = [b_all[:, cols[h]] for h in heads]
        qf = [hq_ref[pl.ds(r0, C), cols[h]].astype(F32) for h in heads]
        kf = [kk_ref[pl.ds(r0, C), cols[h]].astype(F32) for h in heads]
        iv = [hi_ref[pl.ds(r0, C), cols[h]] for h in heads]
        st = [state_ref[h] for h in heads]
        b_last = [b[h][C - 1:C, :] for h in heads]

        inter = [lax.dot_general((qf[h] * jnp.exp2(b[h])).astype(BF16), st[h].astype(BF16),
                                 (((1,), (1,)), ((), ())), preferred_element_type=F32)
                 for h in heads]

        a = [None] * H
        for hs in HG_LEVELS:
            for h in heads:
                e = jnp.exp2((b[h] - level_ref(b[h], hs)) * sign[hs])
                y = (jnp.where(is_q[hs], qf[h], kf[h]) * e).astype(BF16)
                sc = lax.dot_general(y, y, (((1,), (1,)), ((), ())),
                                     preferred_element_type=F32)
                sc = jnp.where(keep[hs], sc, 0.0)
                a[h] = sc if a[h] is None else a[h] + sc
        for h in heads:
            kbuf_ref[h, PAD:, :] = kf[h]
            bbuf_ref[h, PAD:, :] = b[h]
        for d in range(HG_DIAG):
            for h in heads:
                if d == 0:
                    term = qf[h] * kf[h]
                else:
                    ks = kbuf_ref[h, PAD - d:PAD - d + C, :]
                    bs = bbuf_ref[h, PAD - d:PAD - d + C, :]
                    term = qf[h] * ks * jnp.exp2(b[h] - bs)
                dsum = jnp.sum(term, axis=-1, keepdims=True)
                a[h] = a[h] + jnp.where(on_diag[d], dsum, 0.0)

        o = [inter[h] + jnp.dot(a[h].astype(BF16), iv[h], preferred_element_type=F32)
             for h in heads]

        for h in heads:
            k_end = (kf[h] * jnp.exp2(b_last[h] - b[h])).astype(BF16)
            upd = lax.dot_general(iv[h], k_end, (((0,), (0,)), ((), ())),
                                  preferred_element_type=F32)
            state_ref[h] = st[h] * jnp.exp2(b_last[h]) + upd

        for h in heads:
            ms = jnp.mean(o[h] * o[h], axis=-1, keepdims=True)
            y = o[h] * lax.rsqrt(ms + EPS) * w_ref[...]
            y = y * sog_ref[pl.ds(r0, C), cols[h]].astype(F32)
            y_ref[pl.ds(r0, C), cols[h]] = y.astype(BF16)
        return carry

    lax.fori_loop(0, tc // C, chunk, 0)


def _hgrn(hq, kk, hi, sog, logf, norm_w, tc=512):
    B, S, W = hq.shape
    blk = pl.BlockSpec((None, tc, W), lambda b, s: (b, s, 0))
    return pl.pallas_call(
        functools.partial(_hgrn_kernel, tc=tc),
        grid=(B, S // tc),
        in_specs=[blk, blk, blk, blk, blk, pl.BlockSpec((1, HEAD_W), lambda b, s: (0, 0))],
        out_specs=blk,
        out_shape=jax.ShapeDtypeStruct((B, S, W), BF16),
        scratch_shapes=[pltpu.VMEM((HG_HEADS, HEAD_W, HEAD_W), F32),
                        pltpu.VMEM((HG_HEADS, HG_C + 8, HEAD_W), F32),
                        pltpu.VMEM((HG_HEADS, HG_C + 8, HEAD_W), F32)],
        compiler_params=_cparams(2),
        name="hgrn2",
    )(hq, kk, hi, sog, logf, norm_w)


def _layer_norm(z, g, b):
    mu = jnp.mean(z, axis=-1, keepdims=True)
    zc = z - mu
    var = jnp.mean(zc * zc, axis=-1, keepdims=True)
    return zc * lax.rsqrt(var + EPS) * g + b


def _merge_kernel(ya_ref, yh_ref, x_ref, wa_ref, wh_ref, wg_ref, bg_ref, wo_ref,
                  g_ref, b_ref, o_ref, *, alpha):
    D = x_ref.shape[-1]
    tm = x_ref.shape[0]

    def branches(r):
        rows = slice(r * ROW_CHUNK, (r + 1) * ROW_CHUNK)
        xb = x_ref[rows, :].astype(BF16)
        pa = jnp.dot(ya_ref[rows, :], wa_ref[...], preferred_element_type=F32)
        ph = jnp.dot(yh_ref[rows, :], wh_ref[...], preferred_element_type=F32)
        ga = jnp.dot(xb, wg_ref[:, :D], preferred_element_type=F32)
        gh = jnp.dot(xb, wg_ref[:, D:], preferred_element_type=F32)
        return pa, ph, ga, gh

    n_chunks = tm // ROW_CHUNK
    nxt = branches(0)
    for r in range(n_chunks):
        rows = slice(r * ROW_CHUNK, (r + 1) * ROW_CHUNK)
        pa, ph, ga, gh = nxt
        if r + 1 < n_chunks:
            nxt = branches(r + 1)
        merged = (jax.nn.sigmoid(ga + bg_ref[:, :D]) * pa
                  + jax.nn.sigmoid(gh + bg_ref[:, D:]) * ph)
        mix = jnp.dot(merged.astype(BF16), wo_ref[...], preferred_element_type=F32)
        o_ref[rows, :] = _layer_norm(alpha * x_ref[rows, :] + mix, g_ref[...], b_ref[...])


def _merge(ya, yh, x, wa, wh, wg, bg, wo, g, b, alpha, tm=1024):
    N, D = x.shape
    row = lambda w: pl.BlockSpec((tm, w), lambda i: (i, 0))
    return pl.pallas_call(
        functools.partial(_merge_kernel, alpha=alpha),
        grid=(N // tm,),
        in_specs=[row(ya.shape[1]), row(yh.shape[1]), row(D),
                  _const_spec(wa.shape), _const_spec(wh.shape), _const_spec(wg.shape),
                  _const_spec(bg.shape), _const_spec(wo.shape),
                  _const_spec(g.shape), _const_spec(b.shape)],
        out_specs=row(D),
        out_shape=jax.ShapeDtypeStruct((N, D), F32),
        compiler_params=_cparams(1),
        name="merge_ln",
    )(ya, yh, x, wa, wh, wg, bg, wo, g, b)


def _ffn_kernel(h_ref, wg_ref, wu_ref, cw_ref, cb_ref, wd_ref, g_ref, b_ref, o_ref,
                tail_ref, gbuf_ref, *, alpha, tm):
    d_ff = wg_ref.shape[1]
    PAD = 8

    @pl.when(pl.program_id(1) == 0)
    def _():
        tail_ref[...] = jnp.zeros_like(tail_ref)

    RC = ROW_CHUNK
    nfc = d_ff // FFN_FC
    steps = [(r, fc) for r in range(tm // RC) for fc in range(nfc)]
    hb = {}

    def gate_up(r, fc):
        if r not in hb:
            hb[r] = h_ref[r * RC:(r + 1) * RC, :].astype(BF16)
        cols = slice(fc * FFN_FC, (fc + 1) * FFN_FC)
        return (jnp.dot(hb[r], wg_ref[:, cols], preferred_element_type=F32),
                jnp.dot(hb[r], wu_ref[:, cols], preferred_element_type=F32))

    nxt = gate_up(*steps[0])
    acc = None
    for t, (r, fc) in enumerate(steps):
        rows = slice(r * RC, (r + 1) * RC)
        cols = slice(fc * FFN_FC, (fc + 1) * FFN_FC)
        gte, up = nxt
        if t + 1 < len(steps):
            nxt = gate_up(*steps[t + 1])
        gbuf = gbuf_ref.at[t % 2]
        gbuf[0:PAD, :] = tail_ref[:, cols]
        gbuf[PAD:, :] = gte
        tail_ref[:, cols] = gte[RC - PAD:, :]
        conv = (gte * cw_ref[2:3, cols]
                + gbuf[PAD - 1:PAD - 1 + RC, :] * cw_ref[1:2, cols]
                + gbuf[PAD - 2:PAD - 2 + RC, :] * cw_ref[0:1, cols]
                + cb_ref[:, cols])
        act = jax.nn.gelu(conv) * up
        down = jnp.dot(act.astype(BF16), wd_ref[cols, :], preferred_element_type=F32)
        acc = down if fc == 0 else acc + down
        if fc == nfc - 1:
            o_ref[rows, :] = _layer_norm(alpha * h_ref[rows, :] + acc, g_ref[...], b_ref[...])


def _ffn(h, wg, wu, cw, cb, wd, g, b, alpha, tm=1024):
    B, S, D = h.shape
    d_ff = wg.shape[1]
    row = pl.BlockSpec((None, tm, D), lambda bi, s: (bi, s, 0))
    return pl.pallas_call(
        functools.partial(_ffn_kernel, alpha=alpha, tm=tm),
        grid=(B, S // tm),
        in_specs=[row, _const_spec(wg.shape), _const_spec(wu.shape), _const_spec(cw.shape),
                  _const_spec(cb.shape), _const_spec(wd.shape),
                  _const_spec(g.shape), _const_spec(b.shape)],
        out_specs=row,
        out_shape=jax.ShapeDtypeStruct((B, S, D), F32),
        scratch_shapes=[pltpu.VMEM((8, d_ff), F32),
                        pltpu.VMEM((2, ROW_CHUNK + 8, FFN_FC), F32)],
        compiler_params=_cparams(2),
        name="ffn_ln",
    )(h, wg, wu, cw, cb, wd, g, b)


def kernel(x, w_in, b_gate, lambda_q1, lambda_k1, lambda_q2, lambda_k2, attn_subln_w,
           rel_bias, hgrn_lb_logits, hgrn_norm_w, w_branch_attn, w_branch_hgrn, w_out,
           ln1_g, ln1_b, w_ffn_gate, w_ffn_up, ffn_conv_w, ffn_conv_b, w_ffn_down,
           ln2_g, ln2_b):
    B, S, D = x.shape
    depth = w_in.shape[0]
    att_qk = ATT_HEADS * 2 * ATT_DH
    att_w = ATT_HEADS * HEAD_W
    hg_w = HG_HEADS * HEAD_W
    offs = {"aq": 0, "ak": att_qk, "av": 2 * att_qk}
    offs["hq"] = offs["av"] + att_w
    offs["hf"] = offs["hq"] + hg_w
    offs["hi"] = offs["hf"] + hg_w
    offs["hg"] = offs["hi"] + hg_w
    offs["gate"] = offs["hg"] + hg_w
    alpha = (2 * depth) ** 0.25

    lb_all = jnp.cumsum(jax.nn.softmax(hgrn_lb_logits.astype(F32), axis=0), axis=0)

    h = x
    for l in range(depth):
        lam_init = 0.8 - 0.6 * math.exp(-0.3 * l)
        lam = (jnp.exp(jnp.sum(lambda_q1[l].astype(F32) * lambda_k1[l].astype(F32)))
               - jnp.exp(jnp.sum(lambda_q2[l].astype(F32) * lambda_k2[l].astype(F32)))
               + lam_init).reshape(1)

        w_mix_bf = w_in[l, :, :offs["gate"]].astype(BF16)
        w_gate_bf = w_in[l, :, offs["gate"]:].astype(BF16)
        (q, k, vt, hq, kk, hi, sog, logf) = _inproj(h, w_mix_bf, lb_all[l][None, :], offs)

        ya = _attention(q, k, vt, rel_bias.astype(F32),
                        attn_subln_w[l][:, None].astype(F32), lam, 1.0 - lam_init)
        yh = _hgrn(hq, kk, hi, sog, logf, hgrn_norm_w[l][None, :].astype(F32))

        h1 = _merge(ya.reshape(B * S, att_w), yh.reshape(B * S, hg_w), h.reshape(B * S, D),
                    w_branch_attn[l].astype(BF16), w_branch_hgrn[l].astype(BF16),
                    w_gate_bf, b_gate[l][None, :], w_out[l].astype(BF16),
                    ln1_g[l][None, :], ln1_b[l][None, :], alpha)

        h = _ffn(h1.reshape(B, S, D), w_ffn_gate[l].astype(BF16), w_ffn_up[l].astype(BF16),
                 ffn_conv_w[l], ffn_conv_b[l][None, :], w_ffn_down[l].astype(BF16),
                 ln2_g[l][None, :], ln2_b[l][None, :], alpha)
    return h
```

```python
import functools
import math

import numpy as np
import jax
import jax.numpy as jnp
from jax import lax
from jax.experimental import pallas as pl
from jax.experimental.pallas import tpu as pltpu

F32 = jnp.float32
BF16 = jnp.bfloat16

CHUNK = 64
ATT_HEADS = 4
ATT_DH = 64
HG_HEADS = 4
HEAD_W = 128
NUM_BUCKETS = 32
MAX_DISTANCE = 128
EPS = 1e-5
NEG = -1e30

ATT_BQ = 128
ATT_BK = 256
ATT_G = 4
ATT_GK = ATT_G * ATT_BQ // ATT_BK
ATT_NEAR = (0, -1, -2)
ATT_VR = HEAD_W + 16
LOG2E = math.log2(math.e)
HG_C = 128
HG_LEVELS = (64, 32, 16, 8, 4)
HG_DIAG = 4
FFN_FC = 256
ROW_CHUNK = 256

VMEM_LIMIT = 56 * 1024 * 1024


def _cparams(n_axes):
    return pltpu.CompilerParams(
        dimension_semantics=("arbitrary",) * n_axes,
        vmem_limit_bytes=VMEM_LIMIT)


def _const_spec(shape):
    nd = len(shape)
    return pl.BlockSpec(shape, lambda *_: (0,) * nd, pipeline_mode=pl.Buffered(1))


def _inproj_kernel(x_ref, w_ref, lb_ref,
                   q_ref, k_ref, vt_ref, hq_ref, kk_ref, hi_ref, sog_ref, logf_ref,
                   *, tm, offs):
    lb = lb_ref[...]
    for r in range(tm // ROW_CHUNK):
        rows = slice(r * ROW_CHUNK, (r + 1) * ROW_CHUNK)
        xb = x_ref[rows, :].astype(BF16)

        def mm(off, xb=xb):
            return jnp.dot(xb, w_ref[:, off:off + 512], preferred_element_type=F32)

        q_ref[rows, :] = (mm(offs["aq"]) * (ATT_DH ** -0.5 * LOG2E)).astype(BF16)
        k_ref[rows, :] = mm(offs["ak"]).astype(BF16)
        v = mm(offs["av"])
        for hd in range(ATT_HEADS):
            vt_ref[r, hd * ATT_VR:hd * ATT_VR + HEAD_W, :] = (
                v[:, hd * HEAD_W:(hd + 1) * HEAD_W].T.astype(BF16))
            vt_ref[r, hd * ATT_VR + HEAD_W:(hd + 1) * ATT_VR, :] = jnp.ones(
                (ATT_VR - HEAD_W, ATT_BK), BF16)
        hq_ref[rows, :] = mm(offs["hq"]).astype(BF16)
        sig = jax.nn.sigmoid(mm(offs["hf"]))
        logf_ref[rows, :] = jnp.log2(lb + (1.0 - lb) * sig)
        kk_ref[rows, :] = ((1.0 - lb) * (1.0 - sig)).astype(BF16)
        hi_ref[rows, :] = mm(offs["hi"]).astype(BF16)
        hog = mm(offs["hg"])
        sog_ref[rows, :] = (hog * jax.nn.sigmoid(hog)).astype(BF16)


def _inproj(x, w_bf, lb, offs, tm=1024):
    B, S, D = x.shape
    assert ROW_CHUNK == ATT_BK
    row = lambda w: pl.BlockSpec((None, tm, w), lambda b, s: (b, s, 0))
    out_shape = (
        jax.ShapeDtypeStruct((B, S, 512), BF16),
        jax.ShapeDtypeStruct((B, S, 512), BF16),
        jax.ShapeDtypeStruct((B, S // ATT_BK, ATT_HEADS * ATT_VR, ATT_BK), BF16),
        jax.ShapeDtypeStruct((B, S, 512), BF16),
        jax.ShapeDtypeStruct((B, S, 512), BF16),
        jax.ShapeDtypeStruct((B, S, 512), BF16),
        jax.ShapeDtypeStruct((B, S, 512), BF16),
        jax.ShapeDtypeStruct((B, S, 512), F32),
    )
    out_specs = (
        row(512), row(512),
        pl.BlockSpec((None, tm // ATT_BK, ATT_HEADS * ATT_VR, ATT_BK),
                     lambda b, s: (b, s, 0, 0)),
        row(512), row(512), row(512), row(512), row(512),
    )
    return pl.pallas_call(
        functools.partial(_inproj_kernel, tm=tm, offs=offs),
        grid=(B, S // tm),
        in_specs=[row(D), _const_spec(w_bf.shape), _const_spec(lb.shape)],
        out_specs=out_specs,
        out_shape=out_shape,
        compiler_params=_cparams(2),
        name="inproj",
    )(x, w_bf, lb)


def _rel_bucket_np(rel):
    nb = NUM_BUCKETS // 2
    max_exact = nb // 2
    ret = np.where(rel > 0, nb, 0)
    n = np.abs(rel)
    nf = np.maximum(n, 1).astype(np.float32)
    large = max_exact + (np.log(nf / np.float32(max_exact))
                         / np.float32(math.log(MAX_DISTANCE / max_exact))
                         * np.float32(nb - max_exact)).astype(np.int32)
    large = np.minimum(large, nb - 1)
    return ret + np.where(n < max_exact, n, large)


def _attn_bucket_tiles():
    i = np.arange(ATT_BQ)[None, :]
    j = np.arange(ATT_BK)[:, None]
    tiles = []
    for d in ATT_NEAR:
        kpos = d * ATT_BQ + j
        live = (kpos // CHUNK) <= (i // CHUNK)
        tiles.append(np.where(live, _rel_bucket_np(kpos - i), -1))
    far = _rel_bucket_np(np.asarray((min(ATT_NEAR) - 1) * ATT_BQ + ATT_BK - 1))
    assert int(far) == NUM_BUCKETS // 2 - 1
    return np.stack(tiles).astype(np.int32)


def _attn_kernel(lam_ref, rb_ref, bkt_ref, q_ref, k_ref, vt_ref, w_ref, o_ref,
                 bias_ref, qs_ref, s0_ref, s1_ref, p0_ref, p1_ref, a0_ref, a1_ref,
                 m_ref, acc_ref, *, out_scale):
    S = q_ref.shape[0]
    nkb = S // ATT_BK
    W2 = 2 * ATT_BQ
    near = tuple(range(ATT_GK)) + (-1,)
    lam = lam_ref[0]
    h = pl.program_id(0)

    @pl.when(pl.program_id(1) == 0)
    def _():
        far = rb_ref[NUM_BUCKETS // 2 - 1, h]
        for n, kbrel in enumerate(near):
            for j in range(ATT_G):
                d = (kbrel * ATT_BK) // ATT_BQ - j
                cols = slice(j * W2, (j + 1) * W2)
                if d > max(ATT_NEAR):
                    bias_ref[n, :, cols] = jnp.full((ATT_BK, W2), NEG, F32)
                elif d < min(ATT_NEAR):
                    bias_ref[n, :, cols] = jnp.zeros((ATT_BK, W2), F32)
        for t, dt in enumerate(ATT_NEAR):
            bk = bkt_ref[t]
            tile = jnp.full(bk.shape, NEG, F32)
            for bucket in range(NUM_BUCKETS):
                tile = jnp.where(bk == bucket, (rb_ref[bucket, h] - far) * LOG2E, tile)
            tile2 = jnp.concatenate([tile, tile], axis=1)
            for n, kbrel in enumerate(near):
                for j in range(ATT_G):
                    if (kbrel * ATT_BK) // ATT_BQ - j == dt:
                        bias_ref[n, :, j * W2:(j + 1) * W2] = tile2

    lane = lax.broadcasted_iota(jnp.int32, (ATT_BQ, HEAD_W), 1)
    first_map = lane < ATT_DH

    s_bufs, p_bufs, a_bufs = (s0_ref, s1_ref), (p0_ref, p1_ref), (a0_ref, a1_ref)

    def scores(step, j, buf):
        kb, bias_idx = step
        cols = slice(j * W2, (j + 1) * W2)
        kblk = k_ref[pl.ds(pl.multiple_of(kb * ATT_BK, ATT_BK), ATT_BK), :]
        s = lax.dot_general(kblk, qs_ref[cols, :], (((1,), (1,)), ((), ())),
                            preferred_element_type=F32)
        if bias_idx is not None:
            s = s + bias_ref[bias_idx, :, cols]
        s_bufs[buf][:, cols] = s

    def soft(j, buf, first=False):
        cols = slice(j * W2, (j + 1) * W2)
        s = s_bufs[buf][:, cols]
        m_new = jnp.max(s, axis=0, keepdims=True)
        if first:
            a_bufs[buf][:, cols] = jnp.ones_like(m_new)
        else:
            m_old = m_ref[:, cols]
            m_new = jnp.maximum(m_old, m_new)
            a_bufs[buf][:, cols] = jnp.exp2(m_old - m_new)
        m_ref[:, cols] = m_new
        p_bufs[buf][:, cols] = jnp.exp2(s - m_new).astype(BF16)

    def accum(kb, j, buf, first=False):
        cols = slice(j * W2, (j + 1) * W2)
        pv = jnp.dot(vt_ref[kb], p_bufs[buf][:, cols], preferred_element_type=F32)
        if first:
            acc_ref[:, cols] = pv
        else:
            acc_ref[:, cols] = a_bufs[buf][:, cols] * acc_ref[:, cols] + pv

    AHEAD = 3

    def two_steps(steps, first=False, prev_kb=None, dead=()):
        for k in range(2 * ATT_G):
            ua = k + AHEAD
            if (ua // ATT_G, ua % ATT_G) not in dead:
                scores(steps[ua // ATT_G], ua % ATT_G, (ua // ATT_G) % 2)
            uc = k - 1
            if k >= 1 and (uc // ATT_G, uc % ATT_G) not in dead:
                accum(steps[uc // ATT_G][0], uc % ATT_G, uc // ATT_G,
                      first=first and uc < ATT_G)
            elif k == 0 and prev_kb is not None:
                accum(prev_kb, ATT_G - 1, 1)
            if (k // ATT_G, k % ATT_G) not in dead:
                soft(k % ATT_G, k // ATT_G, first=first and k < ATT_G)

    def inside(g):
        return [(g * ATT_GK + i, i) for i in range(ATT_GK)]

    def before(g):
        return (jnp.maximum(g * ATT_GK - 1, 0), ATT_GK)

    def far(n):
        return (jnp.clip(n - (ATT_GK + 1), 0, nkb - 1), None)

    dead = {(i, j) for i in range(ATT_GK) for j in range(ATT_G)
            if (i * ATT_BK) // ATT_BQ - j > max(ATT_NEAR)}

    def open_group(g):
        q0 = pl.multiple_of(g * (ATT_G * ATT_BQ), ATT_G * ATT_BQ)
        for j in range(ATT_G):
            q = q_ref[pl.ds(q0 + j * ATT_BQ, ATT_BQ), :]
            zero = jnp.zeros_like(q)
            qs_ref[j * W2:j * W2 + ATT_BQ, :] = jnp.where(first_map, q, zero)
            qs_ref[j * W2 + ATT_BQ:(j + 1) * W2, :] = jnp.where(first_map, zero, q)
        for j in range(AHEAD):
            scores(inside(g)[0], j, 0)

    def close_group(g, n_steps):
        last_kb = jnp.where(g > 0, far(n_steps - 1)[0], inside(g)[1][0])
        accum(last_kb, ATT_G - 1, 1)

    def emit_group(g):
        q0 = pl.multiple_of(g * (ATT_G * ATT_BQ), ATT_G * ATT_BQ)
        for j in range(ATT_G):
            cols = slice(j * W2, (j + 1) * W2)
            on = acc_ref[:HEAD_W, cols] * (1.0 / acc_ref[HEAD_W:HEAD_W + 1, cols])
            o = on[:, :ATT_BQ] - lam * on[:, ATT_BQ:]
            ms = jnp.mean(o * o, axis=0, keepdims=True)
            y = o * lax.rsqrt(ms + EPS) * w_ref[...] * out_scale
            o_ref[pl.ds(q0 + j * ATT_BQ, ATT_BQ), :] = y.T.astype(BF16)

    n_groups = S // (ATT_G * ATT_BQ)
    open_group(0)
    two_steps(inside(0) + [before(0)], first=True, dead=dead)

    def group(g, carry):
        n_steps = jnp.where(g > 0, ATT_GK + 1 + (g * ATT_GK - 1), ATT_GK)

        @pl.when(g > 0)
        def _():
            two_steps([before(g), far(3), far(4)], prev_kb=inside(g)[1][0])

        def far_steps(i, c):
            n = 2 * i
            two_steps([far(n), far(n + 1), far(n + 2)], prev_kb=far(n - 1)[0])
            return c

        lax.fori_loop(2, n_steps // 2, far_steps, 0)

        @pl.when(g + 1 < n_groups)
        def _():
            close_group(g, n_steps)
            open_group(g + 1)
            emit_group(g)
            two_steps(inside(g + 1) + [before(g + 1)], first=True, dead=dead)

        @pl.when(g + 1 == n_groups)
        def _():
            close_group(g, n_steps)
            emit_group(g)

        return carry

    lax.fori_loop(0, n_groups, group, 0)


def _attention(q, k, vt, rel_bias, subln_w, lam, out_scale):
    B, S, _ = q.shape
    nkb = S // ATT_BK
    GW = ATT_G * 2 * ATT_BQ
    assert ATT_GK == 2 and ATT_BK == 2 * ATT_BQ
    bkt = jnp.asarray(_attn_bucket_tiles())
    head = pl.BlockSpec((None, S, HEAD_W), lambda h, b: (b, 0, h))
    smem = pl.BlockSpec(memory_space=pltpu.SMEM)
    return pl.pallas_call(
        functools.partial(_attn_kernel, out_scale=out_scale),
        grid=(ATT_HEADS, B),
        in_specs=[
            smem, smem, _const_spec(bkt.shape),
            head, head,
            pl.BlockSpec((None, nkb, ATT_VR, ATT_BK), lambda h, b: (b, 0, h, 0)),
            pl.BlockSpec((HEAD_W, 1), lambda h, b: (0, 0)),
        ],
        out_specs=head,
        out_shape=jax.ShapeDtypeStruct((B, S, ATT_HEADS * HEAD_W), BF16),
        scratch_shapes=[
            pltpu.VMEM((ATT_GK + 1, ATT_BK, GW), F32),
            pltpu.VMEM((GW, HEAD_W), BF16),
            pltpu.VMEM((ATT_BK, GW), F32),
            pltpu.VMEM((ATT_BK, GW), F32),
            pltpu.VMEM((ATT_BK, GW), BF16),
            pltpu.VMEM((ATT_BK, GW), BF16),
            pltpu.VMEM((1, GW), F32),
            pltpu.VMEM((1, GW), F32),
            pltpu.VMEM((1, GW), F32),
            pltpu.VMEM((ATT_VR, GW), F32),
        ],
        compiler_params=_cparams(2),
        name="diffattn",
    )(lam, rel_bias, bkt, q, k, vt, subln_w)


def _hgrn_kernel(hq_ref, kk_ref, hi_ref, sog_ref, lf_ref, w_ref, y_ref,
                 state_ref, kbuf_ref, bbuf_ref, *, tc):
    C, H, W = HG_C, HG_HEADS, HEAD_W
    PAD = 8

    @pl.when(pl.program_id(1) == 0)
    def _():
        state_ref[...] = jnp.zeros_like(state_ref)

    kbuf_ref[:, 0:PAD, :] = jnp.zeros((H, PAD, W), F32)
    bbuf_ref[:, 0:PAD, :] = jnp.zeros((H, PAD, W), F32)

    row = lax.broadcasted_iota(jnp.int32, (C, W), 0)
    r2 = lax.broadcasted_iota(jnp.int32, (C, C), 0)
    c2 = lax.broadcasted_iota(jnp.int32, (C, C), 1)
    tri = (c2 <= r2).astype(BF16)
    is_q = {hs: ((row // hs) % 2) == 1 for hs in HG_LEVELS}
    sign = {hs: jnp.where(is_q[hs], 1.0, -1.0).astype(F32) for hs in HG_LEVELS}
    keep = {hs: ((r2 // (2 * hs)) == (c2 // (2 * hs)))
            & (((r2 // hs) % 2) == 1) & (((c2 // hs) % 2) == 0) for hs in HG_LEVELS}
    same_blk = (r2 // HG_DIAG) == (c2 // HG_DIAG)
    on_diag = {d: ((r2 - c2) == d) & same_blk for d in range(HG_DIAG)}

    def level_ref(b, hs):
        if hs >= PAD:
            parts = [jnp.broadcast_to(b[g * 2 * hs + hs - 1:g * 2 * hs + hs, :], (2 * hs, W))
                     for g in range(C // (2 * hs))]
            return parts[0] if len(parts) == 1 else jnp.concatenate(parts, axis=0)
        b3 = b.reshape(C // (2 * hs), 2 * hs, W)
        return jnp.broadcast_to(b3[:, hs - 1:hs, :], b3.shape).reshape(C, W)

    def chunk(c, carry):
        r0 = pl.multiple_of(c * C, C)
        lf = lf_ref[pl.ds(r0, C), :]
        p0 = lf.astype(BF16)
        p1 = (lf - p0.astype(F32)).astype(BF16)
        b_all = (jnp.dot(tri, p0, preferred_element_type=F32)
                 + jnp.dot(tri, p1, preferred_element_type=F32))

        heads = range(H)
        cols = [slice(h * W, (h + 1) * W) for h in heads]
        b = [b_all[:, cols[h]] for h in heads]
        qf = [hq_ref[pl.ds(r0, C), cols[h]].astype(F32) for h in heads]
        kf = [kk_ref[pl.ds(r0, C), cols[h]].astype(F32) for h in heads]
        iv = [hi_ref[pl.ds(r0, C), cols[h]] for h in heads]
        st = [state_ref[h] for h in heads]
        b_last = [b[h][C - 1:C, :] for h in heads]

        inter = [lax.dot_general((qf[h] * jnp.exp2(b[h])).astype(BF16), st[h].astype(BF16),
                                 (((1,), (1,)), ((), ())), preferred_element_type=F32)
                 for h in heads]

        a = [None] * H
        for hs in HG_LEVELS:
            for h in heads:
                e = jnp.exp2((b[h] - level_ref(b[h], hs)) * sign[hs])
                y = (jnp.where(is_q[hs], qf[h], kf[h]) * e).astype(BF16)
                sc = lax.dot_general(y, y, (((1,), (1,)), ((), ())),
                                     preferred_element_type=F32)
                sc = jnp.where(keep[hs], sc, 0.0)
                a[h] = sc if a[h] is None else a[h] + sc
        for h in heads:
            kbuf_ref[h, PAD:, :] = kf[h]
            bbuf_ref[h, PAD:, :] = b[h]
        for d in range(HG_DIAG):
            for h in heads:
                if d == 0:
                    term = qf[h] * kf[h]
                else:
                    ks = kbuf_ref[h, PAD - d:PAD - d + C, :]
                    bs = bbuf_ref[h, PAD - d:PAD - d + C, :]
                    term = qf[h] * ks * jnp.exp2(b[h] - bs)
                dsum = jnp.sum(term, axis=-1, keepdims=True)
                a[h] = a[h] + jnp.where(on_diag[d], dsum, 0.0)

        o = [inter[h] + jnp.dot(a[h].astype(BF16), iv[h], preferred_element_type=F32)
             for h in heads]

        for h in heads:
            k_end = (kf[h] * jnp.exp2(b_last[h] - b[h])).astype(BF16)
            upd = lax.dot_general(iv[h], k_end, (((0,), (0,)), ((), ())),
                                  preferred_element_type=F32)
            state_ref[h] = st[h] * jnp.exp2(b_last[h]) + upd

        for h in heads:
            ms = jnp.mean(o[h] * o[h], axis=-1, keepdims=True)
            y = o[h] * lax.rsqrt(ms + EPS) * w_ref[...]
            y = y * sog_ref[pl.ds(r0, C), cols[h]].astype(F32)
            y_ref[pl.ds(r0, C), cols[h]] = y.astype(BF16)
        return carry

    lax.fori_loop(0, tc // C, chunk, 0)


def _hgrn(hq, kk, hi, sog, logf, norm_w, tc=512):
    B, S, W = hq.shape
    blk = pl.BlockSpec((None, tc, W), lambda b, s: (b, s, 0))
    return pl.pallas_call(
        functools.partial(_hgrn_kernel, tc=tc),
        grid=(B, S // tc),
        in_specs=[blk, blk, blk, blk, blk, pl.BlockSpec((1, HEAD_W), lambda b, s: (0, 0))],
        out_specs=blk,
        out_shape=jax.ShapeDtypeStruct((B, S, W), BF16),
        scratch_shapes=[pltpu.VMEM((HG_HEADS, HEAD_W, HEAD_W), F32),
                        pltpu.VMEM((HG_HEADS, HG_C + 8, HEAD_W), F32),
                        pltpu.VMEM((HG_HEADS, HG_C + 8, HEAD_W), F32)],
        compiler_params=_cparams(2),
        name="hgrn2",
    )(hq, kk, hi, sog, logf, norm_w)


def _layer_norm(z, g, b):
    mu = jnp.mean(z, axis=-1, keepdims=True)
    zc = z - mu
    var = jnp.mean(zc * zc, axis=-1, keepdims=True)
    return zc * lax.rsqrt(var + EPS) * g + b


def _merge_kernel(ya_ref, yh_ref, x_ref, wa_ref, wh_ref, wg_ref, bg_ref, wo_ref,
                  g_ref, b_ref, o_ref, *, alpha):
    D = x_ref.shape[-1]
    tm = x_ref.shape[0]

    def branches(r):
        rows = slice(r * ROW_CHUNK, (r + 1) * ROW_CHUNK)
        xb = x_ref[rows, :].astype(BF16)
        pa = jnp.dot(ya_ref[rows, :], wa_ref[...], preferred_element_type=F32)
        ph = jnp.dot(yh_ref[rows, :], wh_ref[...], preferred_element_type=F32)
        ga = jnp.dot(xb, wg_ref[:, :D], preferred_element_type=F32)
        gh = jnp.dot(xb, wg_ref[:, D:], preferred_element_type=F32)
        return pa, ph, ga, gh

    n_chunks = tm // ROW_CHUNK
    nxt = branches(0)
    for r in range(n_chunks):
        rows = slice(r * ROW_CHUNK, (r + 1) * ROW_CHUNK)
        pa, ph, ga, gh = nxt
        if r + 1 < n_chunks:
            nxt = branches(r + 1)
        merged = (jax.nn.sigmoid(ga + bg_ref[:, :D]) * pa
                  + jax.nn.sigmoid(gh + bg_ref[:, D:]) * ph)
        mix = jnp.dot(merged.astype(BF16), wo_ref[...], preferred_element_type=F32)
        o_ref[rows, :] = _layer_norm(alpha * x_ref[rows, :] + mix, g_ref[...], b_ref[...])


def _merge(ya, yh, x, wa, wh, wg, bg, wo, g, b, alpha, tm=1024):
    N, D = x.shape
    row = lambda w: pl.BlockSpec((tm, w), lambda i: (i, 0))
    return pl.pallas_call(
        functools.partial(_merge_kernel, alpha=alpha),
        grid=(N // tm,),
        in_specs=[row(ya.shape[1]), row(yh.shape[1]), row(D),
                  _const_spec(wa.shape), _const_spec(wh.shape), _const_spec(wg.shape),
                  _const_spec(bg.shape), _const_spec(wo.shape),
                  _const_spec(g.shape), _const_spec(b.shape)],
        out_specs=row(D),
        out_shape=jax.ShapeDtypeStruct((N, D), F32),
        compiler_params=_cparams(1),
        name="merge_ln",
    )(ya, yh, x, wa, wh, wg, bg, wo, g, b)


def _ffn_kernel(h_ref, wg_ref, wu_ref, cw_ref, cb_ref, wd_ref, g_ref, b_ref, o_ref,
                tail_ref, gbuf_ref, *, alpha, tm):
    d_ff = wg_ref.shape[1]
    PAD = 8

    @pl.when(pl.program_id(1) == 0)
    def _():
        tail_ref[...] = jnp.zeros_like(tail_ref)

    RC = ROW_CHUNK
    nfc = d_ff // FFN_FC
    steps = [(r, fc) for r in range(tm // RC) for fc in range(nfc)]
    hb = {}

    def gate_up(r, fc):
        if r not in hb:
            hb[r] = h_ref[r * RC:(r + 1) * RC, :].astype(BF16)
        cols = slice(fc * FFN_FC, (fc + 1) * FFN_FC)
        return (jnp.dot(hb[r], wg_ref[:, cols], preferred_element_type=F32),
                jnp.dot(hb[r], wu_ref[:, cols], preferred_element_type=F32))

    nxt = gate_up(*steps[0])
    acc = None
    for t, (r, fc) in enumerate(steps):
        rows = slice(r * RC, (r + 1) * RC)
        cols = slice(fc * FFN_FC, (fc + 1) * FFN_FC)
        gte, up = nxt
        if t + 1 < len(steps):
            nxt = gate_up(*steps[t + 1])
        gbuf = gbuf_ref.at[t % 2]
        gbuf[0:PAD, :] = tail_ref[:, cols]
        gbuf[PAD:, :] = gte
        tail_ref[:, cols] = gte[RC - PAD:, :]
        conv = (gte * cw_ref[2:3, cols]
                + gbuf[PAD - 1:PAD - 1 + RC, :] * cw_ref[1:2, cols]
                + gbuf[PAD - 2:PAD - 2 + RC, :] * cw_ref[0:1, cols]
                + cb_ref[:, cols])
        act = jax.nn.gelu(conv) * up
        down = jnp.dot(act.astype(BF16), wd_ref[cols, :], preferred_element_type=F32)
        acc = down if fc == 0 else acc + down
        if fc == nfc - 1:
            o_ref[rows, :] = _layer_norm(alpha * h_ref[rows, :] + acc, g_ref[...], b_ref[...])


def _ffn(h, wg, wu, cw, cb, wd, g, b, alpha, tm=1024):
    B, S, D = h.shape
    d_ff = wg.shape[1]
    row = pl.BlockSpec((None, tm, D), lambda bi, s: (bi, s, 0))
    return pl.pallas_call(
        functools.partial(_ffn_kernel, alpha=alpha, tm=tm),
        grid=(B, S // tm),
        in_specs=[row, _const_spec(wg.shape), _const_spec(wu.shape), _const_spec(cw.shape),
                  _const_spec(cb.shape), _const_spec(wd.shape),
                  _const_spec(g.shape), _const_spec(b.shape)],
        out_specs=row,
        out_shape=jax.ShapeDtypeStruct((B, S, D), F32),
        scratch_shapes=[pltpu.VMEM((8, d_ff), F32),
                        pltpu.VMEM((2, ROW_CHUNK + 8, FFN_FC), F32)],
        compiler_params=_cparams(2),
        name="ffn_ln",
    )(h, wg, wu, cw, cb, wd, g, b)


def kernel(x, w_in, b_gate, lambda_q1, lambda_k1, lambda_q2, lambda_k2, attn_subln_w,
           rel_bias, hgrn_lb_logits, hgrn_norm_w, w_branch_attn, w_branch_hgrn, w_out,
           ln1_g, ln1_b, w_ffn_gate, w_ffn_up, ffn_conv_w, ffn_conv_b, w_ffn_down,
           ln2_g, ln2_b):
    B, S, D = x.shape
    depth = w_in.shape[0]
    att_qk = ATT_HEADS * 2 * ATT_DH
    att_w = ATT_HEADS * HEAD_W
    hg_w = HG_HEADS * HEAD_W
    offs = {"aq": 0, "ak": att_qk, "av": 2 * att_qk}
    offs["hq"] = offs["av"] + att_w
    offs["hf"] = offs["hq"] + hg_w
    offs["hi"] = offs["hf"] + hg_w
    offs["hg"] = offs["hi"] + hg_w
    offs["gate"] = offs["hg"] + hg_w
    alpha = (2 * depth) ** 0.25

    lb_all = jnp.cumsum(jax.nn.softmax(hgrn_lb_logits.astype(F32), axis=0), axis=0)

    h = x
    for l in range(depth):
        lam_init = 0.8 - 0.6 * math.exp(-0.3 * l)
        lam = (jnp.exp(jnp.sum(lambda_q1[l].astype(F32) * lambda_k1[l].astype(F32)))
               - jnp.exp(jnp.sum(lambda_q2[l].astype(F32) * lambda_k2[l].astype(F32)))
               + lam_init).reshape(1)

        w_mix_bf = w_in[l, :, :offs["gate"]].astype(BF16)
        w_gate_bf = w_in[l, :, offs["gate"]:].astype(BF16)
        (q, k, vt, hq, kk, hi, sog, logf) = _inproj(h, w_mix_bf, lb_all[l][None, :], offs)

        ya = _attention(q, k, vt, rel_bias.astype(F32),
                        attn_subln_w[l][:, None].astype(F32), lam, 1.0 - lam_init)
        yh = _hgrn(hq, kk, hi, sog, logf, hgrn_norm_w[l][None, :].astype(F32))

        h1 = _merge(ya.reshape(B * S, att_w), yh.reshape(B * S, hg_w), h.reshape(B * S, D),
                    w_branch_attn[l].astype(BF16), w_branch_hgrn[l].astype(BF16),
                    w_gate_bf, b_gate[l][None, :], w_out[l].astype(BF16),
                    ln1_g[l][None, :], ln1_b[l][None, :], alpha)

        h = _ffn(h1.reshape(B, S, D), w_ffn_gate[l].astype(BF16), w_ffn_up[l].astype(BF16),
                 ffn_conv_w[l], ffn_conv_b[l][None, :], w_ffn_down[l].astype(BF16),
                 ln2_g[l][None, :], ln2_b[l][None, :], alpha)
    return h
```

```python
import functools
import math

import numpy as np
import jax
import jax.numpy as jnp
from jax import lax
from jax.experimental import pallas as pl
from jax.experimental.pallas import tpu as pltpu

F32 = jnp.float32
BF16 = jnp.bfloat16

CHUNK = 64
ATT_HEADS = 4
ATT_DH = 64
HG_HEADS = 4
HEAD_W = 128
NUM_BUCKETS = 32
MAX_DISTANCE = 128
EPS = 1e-5
NEG = -1e30

ATT_BQ = 128
ATT_BK = 256
ATT_G = 4
ATT_GK = ATT_G * ATT_BQ // ATT_BK
ATT_NEAR = (0, -1, -2)
ATT_VR = HEAD_W + 16
LOG2E = math.log2(math.e)
HG_C = 128
HG_LEVELS = (64, 32, 16, 8, 4)
HG_DIAG = 4
FFN_FC = 256
ROW_CHUNK = 256

VMEM_LIMIT = 56 * 1024 * 1024


def _cparams(n_axes):
    return pltpu.CompilerParams(
        dimension_semantics=("arbitrary",) * n_axes,
        vmem_limit_bytes=VMEM_LIMIT)


def _const_spec(shape):
    nd = len(shape)
    return pl.BlockSpec(shape, lambda *_: (0,) * nd, pipeline_mode=pl.Buffered(1))


def _inproj_kernel(x_ref, w_ref, lb_ref,
                   q_ref, k_ref, vt_ref, hq_ref, kk_ref, hi_ref, sog_ref, logf_ref,
                   *, tm, offs):
    lb = lb_ref[...]
    for r in range(tm // ROW_CHUNK):
        rows = slice(r * ROW_CHUNK, (r + 1) * ROW_CHUNK)
        xb = x_ref[rows, :].astype(BF16)

        def mm(off, xb=xb):
            return jnp.dot(xb, w_ref[:, off:off + 512], preferred_element_type=F32)

        q_ref[rows, :] = (mm(offs["aq"]) * (ATT_DH ** -0.5 * LOG2E)).astype(BF16)
        k_ref[rows, :] = mm(offs["ak"]).astype(BF16)
        v = mm(offs["av"])
        for hd in range(ATT_HEADS):
            vt_ref[r, hd * ATT_VR:hd * ATT_VR + HEAD_W, :] = (
                v[:, hd * HEAD_W:(hd + 1) * HEAD_W].T.astype(BF16))
            vt_ref[r, hd * ATT_VR + HEAD_W:(hd + 1) * ATT_VR, :] = jnp.ones(
                (ATT_VR - HEAD_W, ATT_BK), BF16)
        hq_ref[rows, :] = mm(offs["hq"]).astype(BF16)
        sig = jax.nn.sigmoid(mm(offs["hf"]))
        logf_ref[rows, :] = jnp.log2(lb + (1.0 - lb) * sig)
        kk_ref[rows, :] = ((1.0 - lb) * (1.0 - sig)).astype(BF16)
        hi_ref[rows, :] = mm(offs["hi"]).astype(BF16)
        hog = mm(offs["hg"])
        sog_ref[rows, :] = (hog * jax.nn.sigmoid(hog)).astype(BF16)


def _inproj(x, w_bf, lb, offs, tm=1024):
    B, S, D = x.shape
    assert ROW_CHUNK == ATT_BK
    row = lambda w: pl.BlockSpec((None, tm, w), lambda b, s: (b, s, 0))
    out_shape = (
        jax.ShapeDtypeStruct((B, S, 512), BF16),
        jax.ShapeDtypeStruct((B, S, 512), BF16),
        jax.ShapeDtypeStruct((B, S // ATT_BK, ATT_HEADS * ATT_VR, ATT_BK), BF16),
        jax.ShapeDtypeStruct((B, S, 512), BF16),
        jax.ShapeDtypeStruct((B, S, 512), BF16),
        jax.ShapeDtypeStruct((B, S, 512), BF16),
        jax.ShapeDtypeStruct((B, S, 512), BF16),
        jax.ShapeDtypeStruct((B, S, 512), F32),
    )
    out_specs = (
        row(512), row(512),
        pl.BlockSpec((None, tm // ATT_BK, ATT_HEADS * ATT_VR, ATT_BK),
                     lambda b, s: (b, s, 0, 0)),
        row(512), row(512), row(512), row(512), row(512),
    )
    return pl.pallas_call(
        functools.partial(_inproj_kernel, tm=tm, offs=offs),
        grid=(B, S // tm),
        in_specs=[row(D), _const_spec(w_bf.shape), _const_spec(lb.shape)],
        out_specs=out_specs,
        out_shape=out_shape,
        compiler_params=_cparams(2),
        name="inproj",
    )(x, w_bf, lb)


def _rel_bucket_np(rel):
    nb = NUM_BUCKETS // 2
    max_exact = nb // 2
    ret = np.where(rel > 0, nb, 0)
    n = np.abs(rel)
    nf = np.maximum(n, 1).astype(np.float32)
    large = max_exact + (np.log(nf / np.float32(max_exact))
                         / np.float32(math.log(MAX_DISTANCE / max_exact))
                         * np.float32(nb - max_exact)).astype(np.int32)
    large = np.minimum(large, nb - 1)
    return ret + np.where(n < max_exact, n, large)


def _attn_bucket_tiles():
    i = np.arange(ATT_BQ)[None, :]
    j = np.arange(ATT_BK)[:, None]
    tiles = []
    for d in ATT_NEAR:
        kpos = d * ATT_BQ + j
        live = (kpos // CHUNK) <= (i // CHUNK)
        tiles.append(np.where(live, _rel_bucket_np(kpos - i), -1))
    far = _rel_bucket_np(np.asarray((min(ATT_NEAR) - 1) * ATT_BQ + ATT_BK - 1))
    assert int(far) == NUM_BUCKETS // 2 - 1
    return np.stack(tiles).astype(np.int32)


def _attn_kernel(lam_ref, rb_ref, bkt_ref, q_ref, k_ref, vt_ref, w_ref, o_ref,
                 bias_ref, qs_ref, s0_ref, s1_ref, p0_ref, p1_ref, a0_ref, a1_ref,
                 m_ref, acc_ref, *, out_scale):
    S = q_ref.shape[0]
    nkb = S // ATT_BK
    W2 = 2 * ATT_BQ
    near = tuple(range(ATT_GK)) + (-1,)
    lam = lam_ref[0]
    h = pl.program_id(0)

    @pl.when(pl.program_id(1) == 0)
    def _():
        far = rb_ref[NUM_BUCKETS // 2 - 1, h]
        for n, kbrel in enumerate(near):
            for j in range(ATT_G):
                d = (kbrel * ATT_BK) // ATT_BQ - j
                if d > max(ATT_NEAR):
                    bias_ref[n, j] = jnp.full((ATT_BK, W2), NEG, F32)
                elif d < min(ATT_NEAR):
                    bias_ref[n, j] = jnp.zeros((ATT_BK, W2), F32)
        for t, dt in enumerate(ATT_NEAR):
            bk = bkt_ref[t]
            tile = jnp.full(bk.shape, NEG, F32)
            for bucket in range(NUM_BUCKETS):
                tile = jnp.where(bk == bucket, (rb_ref[bucket, h] - far) * LOG2E, tile)
            tile2 = jnp.concatenate([tile, tile], axis=1)
            for n, kbrel in enumerate(near):
                for j in range(ATT_G):
                    if (kbrel * ATT_BK) // ATT_BQ - j == dt:
                        bias_ref[n, j] = tile2

    lane = lax.broadcasted_iota(jnp.int32, (ATT_BQ, HEAD_W), 1)
    first_map = lane < ATT_DH

    s_bufs, p_bufs, a_bufs = (s0_ref, s1_ref), (p0_ref, p1_ref), (a0_ref, a1_ref)

    def scores(step, j, buf):
        kb, bias_idx = step
        kblk = k_ref[pl.ds(pl.multiple_of(kb * ATT_BK, ATT_BK), ATT_BK), :]
        s = lax.dot_general(kblk, qs_ref[j], (((1,), (1,)), ((), ())),
                            preferred_element_type=F32)
        if bias_idx is not None:
            s = s + bias_ref[bias_idx, j]
        s_bufs[buf][j] = s

    def soft(j, buf, first=False):
        s = s_bufs[buf][j]
        m_new = jnp.max(s, axis=0, keepdims=True)
        if first:
            a_bufs[buf][j] = jnp.ones_like(m_new)
        else:
            m_old = m_ref[j]
            m_new = jnp.maximum(m_old, m_new)
            a_bufs[buf][j] = jnp.exp2(m_old - m_new)
        m_ref[j] = m_new
        p_bufs[buf][j] = jnp.exp2(s - m_new).astype(BF16)

    def accum(kb, j, buf, first=False):
        pv = jnp.dot(vt_ref[kb], p_bufs[buf][j], preferred_element_type=F32)
        if first:
            acc_ref[j] = pv
        else:
            acc_ref[j] = a_bufs[buf][j] * acc_ref[j] + pv

    AHEAD = 3

    def two_steps(steps, first=False, prev_kb=None, dead=()):
        for k in range(2 * ATT_G):
            ua = k + AHEAD
            if (ua // ATT_G, ua % ATT_G) not in dead:
                scores(steps[ua // ATT_G], ua % ATT_G, (ua // ATT_G) % 2)
            uc = k - 1
            if k >= 1 and (uc // ATT_G, uc % ATT_G) not in dead:
                accum(steps[uc // ATT_G][0], uc % ATT_G, uc // ATT_G,
                      first=first and uc < ATT_G)
            elif k == 0 and prev_kb is not None:
                accum(prev_kb, ATT_G - 1, 1)
            if (k // ATT_G, k % ATT_G) not in dead:
                soft(k % ATT_G, k // ATT_G, first=first and k < ATT_G)

    def inside(g):
        return [(g * ATT_GK + i, i) for i in range(ATT_GK)]

    def before(g):
        return (jnp.maximum(g * ATT_GK - 1, 0), ATT_GK)

    def far(n):
        return (jnp.clip(n - (ATT_GK + 1), 0, nkb - 1), None)

    dead = {(i, j) for i in range(ATT_GK) for j in range(ATT_G)
            if (i * ATT_BK) // ATT_BQ - j > max(ATT_NEAR)}

    def open_group(g):
        q0 = pl.multiple_of(g * (ATT_G * ATT_BQ), ATT_G * ATT_BQ)
        for j in range(ATT_G):
            q = q_ref[pl.ds(q0 + j * ATT_BQ, ATT_BQ), :]
            zero = jnp.zeros_like(q)
            qs_ref[j, :ATT_BQ, :] = jnp.where(first_map, q, zero)
            qs_ref[j, ATT_BQ:, :] = jnp.where(first_map, zero, q)
        for j in range(AHEAD):
            scores(inside(g)[0], j, 0)

    def close_group(g, n_steps):
        last_kb = jnp.where(g > 0, far(n_steps - 1)[0], inside(g)[1][0])
        accum(last_kb, ATT_G - 1, 1)

    def emit_group(g):
        q0 = pl.multiple_of(g * (ATT_G * ATT_BQ), ATT_G * ATT_BQ)
        for j in range(ATT_G):
            on = acc_ref[j, :HEAD_W, :] * (1.0 / acc_ref[j, HEAD_W:HEAD_W + 1, :])
            o = on[:, :ATT_BQ] - lam * on[:, ATT_BQ:]
            ms = jnp.mean(o * o, axis=0, keepdims=True)
            y = o * lax.rsqrt(ms + EPS) * w_ref[...] * out_scale
            o_ref[pl.ds(q0 + j * ATT_BQ, ATT_BQ), :] = y.T.astype(BF16)

    n_groups = S // (ATT_G * ATT_BQ)
    open_group(0)
    two_steps(inside(0) + [before(0)], first=True, dead=dead)

    def group(g, carry):
        n_steps = jnp.where(g > 0, ATT_GK + 1 + (g * ATT_GK - 1), ATT_GK)

        @pl.when(g > 0)
        def _():
            two_steps([before(g), far(3), far(4)], prev_kb=inside(g)[1][0])

        def far_steps(i, c):
            n = 2 * i
            two_steps([far(n), far(n + 1), far(n + 2)], prev_kb=far(n - 1)[0])
            return c

        lax.fori_loop(2, n_steps // 2, far_steps, 0)

        @pl.when(g + 1 < n_groups)
        def _():
            close_group(g, n_steps)
            open_group(g + 1)
            emit_group(g)
            two_steps(inside(g + 1) + [before(g + 1)], first=True, dead=dead)

        @pl.when(g + 1 == n_groups)
        def _():
            close_group(g, n_steps)
            emit_group(g)

        return carry

    lax.fori_loop(0, n_groups, group, 0)


def _attention(q, k, vt, rel_bias, subln_w, lam, out_scale):
    B, S, _ = q.shape
    nkb = S // ATT_BK
    W2 = 2 * ATT_BQ
    assert ATT_GK == 2 and ATT_BK == 2 * ATT_BQ
    bkt = jnp.asarray(_attn_bucket_tiles())
    head = pl.BlockSpec((None, S, HEAD_W), lambda h, b: (b, 0, h))
    smem = pl.BlockSpec(memory_space=pltpu.SMEM)
    return pl.pallas_call(
        functools.partial(_attn_kernel, out_scale=out_scale),
        grid=(ATT_HEADS, B),
        in_specs=[
            smem, smem, _const_spec(bkt.shape),
            head, head,
            pl.BlockSpec((None, nkb, ATT_VR, ATT_BK), lambda h, b: (b, 0, h, 0)),
            pl.BlockSpec((HEAD_W, 1), lambda h, b: (0, 0)),
        ],
        out_specs=head,
        out_shape=jax.ShapeDtypeStruct((B, S, ATT_HEADS * HEAD_W), BF16),
        scratch_shapes=[
            pltpu.VMEM((ATT_GK + 1, ATT_G, ATT_BK, W2), F32),
            pltpu.VMEM((ATT_G, W2, HEAD_W), BF16),
            pltpu.VMEM((ATT_G, ATT_BK, W2), F32),
            pltpu.VMEM((ATT_G, ATT_BK, W2), F32),
            pltpu.VMEM((ATT_G, ATT_BK, W2), BF16),
            pltpu.VMEM((ATT_G, ATT_BK, W2), BF16),
            pltpu.VMEM((ATT_G, 1, W2), F32),
            pltpu.VMEM((ATT_G, 1, W2), F32),
            pltpu.VMEM((ATT_G, 1, W2), F32),
            pltpu.VMEM((ATT_G, ATT_VR, W2), F32),
        ],
        compiler_params=_cparams(2),
        name="diffattn",
    )(lam, rel_bias, bkt, q, k, vt, subln_w)


def _hgrn_kernel(hq_ref, kk_ref, hi_ref, sog_ref, lf_ref, w_ref, y_ref,
                 state_ref, kbuf_ref, bbuf_ref, *, tc):
    C, H, W = HG_C, HG_HEADS, HEAD_W
    PAD = 8

    @pl.when(pl.program_id(1) == 0)
    def _():
        state_ref[...] = jnp.zeros_like(state_ref)

    kbuf_ref[:, 0:PAD, :] = jnp.zeros((H, PAD, W), F32)
    bbuf_ref[:, 0:PAD, :] = jnp.zeros((H, PAD, W), F32)

    row = lax.broadcasted_iota(jnp.int32, (C, W), 0)
    r2 = lax.broadcasted_iota(jnp.int32, (C, C), 0)
    c2 = lax.broadcasted_iota(jnp.int32, (C, C), 1)
    tri = (c2 <= r2).astype(BF16)
    is_q = {hs: ((row // hs) % 2) == 1 for hs in HG_LEVELS}
    sign = {hs: jnp.where(is_q[hs], 1.0, -1.0).astype(F32) for hs in HG_LEVELS}
    keep = {hs: ((r2 // (2 * hs)) == (c2 // (2 * hs)))
            & (((r2 // hs) % 2) == 1) & (((c2 // hs) % 2) == 0) for hs in HG_LEVELS}
    same_blk = (r2 // HG_DIAG) == (c2 // HG_DIAG)
    on_diag = {d: ((r2 - c2) == d) & same_blk for d in range(HG_DIAG)}

    def level_ref(b, hs):
        if hs >= PAD:
            parts = [jnp.broadcast_to(b[g * 2 * hs + hs - 1:g * 2 * hs + hs, :], (2 * hs, W))
                     for g in range(C // (2 * hs))]
            return parts[0] if len(parts) == 1 else jnp.concatenate(parts, axis=0)
        b3 = b.reshape(C // (2 * hs), 2 * hs, W)
        return jnp.broadcast_to(b3[:, hs - 1:hs, :], b3.shape).reshape(C, W)

    def chunk(c, carry):
        r0 = pl.multiple_of(c * C, C)
        lf = lf_ref[pl.ds(r0, C), :]
        p0 = lf.astype(BF16)
        p1 = (lf - p0.astype(F32)).astype(BF16)
        b_all = (jnp.dot(tri, p0, preferred_element_type=F32)
                 + jnp.dot(tri, p1, preferred_element_type=F32))

        heads = range(H)
        cols = [slice(h * W, (h + 1) * W) for h in heads]
        b = [b_all[:, cols[h]] for h in heads]
        qf = [hq_ref[pl.ds(r0, C), cols[h]].astype(F32) for h in heads]
        kf = [kk_ref[pl.ds(r0, C), cols[h]].astype(F32) for h in heads]
        iv = [hi_ref[pl.ds(r0, C), cols[h]] for h in heads]
        st = [state_ref[h] for h in heads]
        b_last = [b[h][C - 1:C, :] for h in heads]

        inter = [lax.dot_general((qf[h] * jnp.exp2(b[h])).astype(BF16), st[h].astype(BF16),
                                 (((1,), (1,)), ((), ())), preferred_element_type=F32)
                 for h in heads]

        a = [None] * H
        for hs in HG_LEVELS:
            for h in heads:
                e = jnp.exp2((b[h] - level_ref(b[h], hs)) * sign[hs])
                y = (jnp.where(is_q[hs], qf[h], kf[h]) * e).astype(BF16)
                sc = lax.dot_general(y, y, (((1,), (1,)), ((), ())),
                                     preferred_element_type=F32)
                sc = jnp.where(keep[hs], sc, 0.0)
                a[h] = sc if a[h] is None else a[h] + sc
        for h in heads:
            kbuf_ref[h, PAD:, :] = kf[h]
            bbuf_ref[h, PAD:, :] = b[h]
        for d in range(HG_DIAG):
            for h in heads:
                if d == 0:
                    term = qf[h] * kf[h]
                else:
                    ks = kbuf_ref[h, PAD - d:PAD - d + C, :]
                    bs = bbuf_ref[h, PAD - d:PAD - d + C, :]
                    term = qf[h] * ks * jnp.exp2(b[h] - bs)
                dsum = jnp.sum(term, axis=-1, keepdims=True)
                a[h] = a[h] + jnp.where(on_diag[d], dsum, 0.0)

        o = [inter[h] + jnp.dot(a[h].astype(BF16), iv[h], preferred_element_type=F32)
             for h in heads]

        for h in heads:
            k_end = (kf[h] * jnp.exp2(b_last[h] - b[h])).astype(BF16)
            upd = lax.dot_general(iv[h], k_end, (((0,), (0,)), ((), ())),
                                  preferred_element_type=F32)
            state_ref[h] = st[h] * jnp.exp2(b_last[h]) + upd

        for h in heads:
            ms = jnp.mean(o[h] * o[h], axis=-1, keepdims=True)
            y = o[h] * lax.rsqrt(ms + EPS) * w_ref[...]
            y = y * sog_ref[pl.ds(r0, C), cols[h]].astype(F32)
            y_ref[pl.ds(r0, C), cols[h]] = y.astype(BF16)
        return carry

    lax.fori_loop(0, tc // C, chunk, 0)


def _hgrn(hq, kk, hi, sog, logf, norm_w, tc=512):
    B, S, W = hq.shape
    blk = pl.BlockSpec((None, tc, W), lambda b, s: (b, s, 0))
    return pl.pallas_call(
        functools.partial(_hgrn_kernel, tc=tc),
        grid=(B, S // tc),
        in_specs=[blk, blk, blk, blk, blk, pl.BlockSpec((1, HEAD_W), lambda b, s: (0, 0))],
        out_specs=blk,
        out_shape=jax.ShapeDtypeStruct((B, S, W), BF16),
        scratch_shapes=[pltpu.VMEM((HG_HEADS, HEAD_W, HEAD_W), F32),
                        pltpu.VMEM((HG_HEADS, HG_C + 8, HEAD_W), F32),
                        pltpu.VMEM((HG_HEADS, HG_C + 8, HEAD_W), F32)],
        compiler_params=_cparams(2),
        name="hgrn2",
    )(hq, kk, hi, sog, logf, norm_w)


def _layer_norm(z, g, b):
    mu = jnp.mean(z, axis=-1, keepdims=True)
    zc = z - mu
    var = jnp.mean(zc * zc, axis=-1, keepdims=True)
    return zc * lax.rsqrt(var + EPS) * g + b


def _merge_kernel(ya_ref, yh_ref, x_ref, wa_ref, wh_ref, wg_ref, bg_ref, wo_ref,
                  g_ref, b_ref, o_ref, *, alpha):
    D = x_ref.shape[-1]
    tm = x_ref.shape[0]

    def branches(r):
        rows = slice(r * ROW_CHUNK, (r + 1) * ROW_CHUNK)
        xb = x_ref[rows, :].astype(BF16)
        pa = jnp.dot(ya_ref[rows, :], wa_ref[...], preferred_element_type=F32)
        ph = jnp.dot(yh_ref[rows, :], wh_ref[...], preferred_element_type=F32)
        ga = jnp.dot(xb, wg_ref[:, :D], preferred_element_type=F32)
        gh = jnp.dot(xb, wg_ref[:, D:], preferred_element_type=F32)
        return pa, ph, ga, gh

    n_chunks = tm // ROW_CHUNK
    nxt = branches(0)
    for r in range(n_chunks):
        rows = slice(r * ROW_CHUNK, (r + 1) * ROW_CHUNK)
        pa, ph, ga, gh = nxt
        if r + 1 < n_chunks:
            nxt = branches(r + 1)
        merged = (jax.nn.sigmoid(ga + bg_ref[:, :D]) * pa
                  + jax.nn.sigmoid(gh + bg_ref[:, D:]) * ph)
        mix = jnp.dot(merged.astype(BF16), wo_ref[...], preferred_element_type=F32)
        o_ref[rows, :] = _layer_norm(alpha * x_ref[rows, :] + mix, g_ref[...], b_ref[...])


def _merge(ya, yh, x, wa, wh, wg, bg, wo, g, b, alpha, tm=1024):
    N, D = x.shape
    row = lambda w: pl.BlockSpec((tm, w), lambda i: (i, 0))
    return pl.pallas_call(
        functools.partial(_merge_kernel, alpha=alpha),
        grid=(N // tm,),
        in_specs=[row(ya.shape[1]), row(yh.shape[1]), row(D),
                  _const_spec(wa.shape), _const_spec(wh.shape), _const_spec(wg.shape),
                  _const_spec(bg.shape), _const_spec(wo.shape),
                  _const_spec(g.shape), _const_spec(b.shape)],
        out_specs=row(D),
        out_shape=jax.ShapeDtypeStruct((N, D), F32),
        compiler_params=_cparams(1),
        name="merge_ln",
    )(ya, yh, x, wa, wh, wg, bg, wo, g, b)


def _ffn_kernel(h_ref, wg_ref, wu_ref, cw_ref, cb_ref, wd_ref, g_ref, b_ref, o_ref,
                tail_ref, gbuf_ref, *, alpha, tm):
    d_ff = wg_ref.shape[1]
    PAD = 8

    @pl.when(pl.program_id(1) == 0)
    def _():
        tail_ref[...] = jnp.zeros_like(tail_ref)

    RC = ROW_CHUNK
    nfc = d_ff // FFN_FC
    steps = [(r, fc) for r in range(tm // RC) for fc in range(nfc)]
    hb = {}

    def gate_up(r, fc):
        if r not in hb:
            hb[r] = h_ref[r * RC:(r + 1) * RC, :].astype(BF16)
        cols = slice(fc * FFN_FC, (fc + 1) * FFN_FC)
        return (jnp.dot(hb[r], wg_ref[:, cols], preferred_element_type=F32),
                jnp.dot(hb[r], wu_ref[:, cols], preferred_element_type=F32))

    nxt = gate_up(*steps[0])
    acc = None
    for t, (r, fc) in enumerate(steps):
        rows = slice(r * RC, (r + 1) * RC)
        cols = slice(fc * FFN_FC, (fc + 1) * FFN_FC)
        gte, up = nxt
        if t + 1 < len(steps):
            nxt = gate_up(*steps[t + 1])
        gbuf = gbuf_ref.at[t % 2]
        gbuf[0:PAD, :] = tail_ref[:, cols]
        gbuf[PAD:, :] = gte
        tail_ref[:, cols] = gte[RC - PAD:, :]
        conv = (gte * cw_ref[2:3, cols]
                + gbuf[PAD - 1:PAD - 1 + RC, :] * cw_ref[1:2, cols]
                + gbuf[PAD - 2:PAD - 2 + RC, :] * cw_ref[0:1, cols]
                + cb_ref[:, cols])
        act = jax.nn.gelu(conv) * up
        down = jnp.dot(act.astype(BF16), wd_ref[cols, :], preferred_element_type=F32)
        acc = down if fc == 0 else acc + down
        if fc == nfc - 1:
            o_ref[rows, :] = _layer_norm(alpha * h_ref[rows, :] + acc, g_ref[...], b_ref[...])


def _ffn(h, wg, wu, cw, cb, wd, g, b, alpha, tm=1024):
    B, S, D = h.shape
    d_ff = wg.shape[1]
    row = pl.BlockSpec((None, tm, D), lambda bi, s: (bi, s, 0))
    return pl.pallas_call(
        functools.partial(_ffn_kernel, alpha=alpha, tm=tm),
        grid=(B, S // tm),
        in_specs=[row, _const_spec(wg.shape), _const_spec(wu.shape), _const_spec(cw.shape),
                  _const_spec(cb.shape), _const_spec(wd.shape),
                  _const_spec(g.shape), _const_spec(b.shape)],
        out_specs=row,
        out_shape=jax.ShapeDtypeStruct((B, S, D), F32),
        scratch_shapes=[pltpu.VMEM((8, d_ff), F32),
                        pltpu.VMEM((2, ROW_CHUNK + 8, FFN_FC), F32)],
        compiler_params=_cparams(2),
        name="ffn_ln",
    )(h, wg, wu, cw, cb, wd, g, b)


def kernel(x, w_in, b_gate, lambda_q1, lambda_k1, lambda_q2, lambda_k2, attn_subln_w,
           rel_bias, hgrn_lb_logits, hgrn_norm_w, w_branch_attn, w_branch_hgrn, w_out,
           ln1_g, ln1_b, w_ffn_gate, w_ffn_up, ffn_conv_w, ffn_conv_b, w_ffn_down,
           ln2_g, ln2_b):
    B, S, D = x.shape
    depth = w_in.shape[0]
    att_qk = ATT_HEADS * 2 * ATT_DH
    att_w = ATT_HEADS * HEAD_W
    hg_w = HG_HEADS * HEAD_W
    offs = {"aq": 0, "ak": att_qk, "av": 2 * att_qk}
    offs["hq"] = offs["av"] + att_w
    offs["hf"] = offs["hq"] + hg_w
    offs["hi"] = offs["hf"] + hg_w
    offs["hg"] = offs["hi"] + hg_w
    offs["gate"] = offs["hg"] + hg_w
    alpha = (2 * depth) ** 0.25

    lb_all = jnp.cumsum(jax.nn.softmax(hgrn_lb_logits.astype(F32), axis=0), axis=0)

    h = x
    for l in range(depth):
        lam_init = 0.8 - 0.6 * math.exp(-0.3 * l)
        lam = (jnp.exp(jnp.sum(lambda_q1[l].astype(F32) * lambda_k1[l].astype(F32)))
               - jnp.exp(jnp.sum(lambda_q2[l].astype(F32) * lambda_k2[l].astype(F32)))
               + lam_init).reshape(1)

        w_mix_bf = w_in[l, :, :offs["gate"]].astype(BF16)
        w_gate_bf = w_in[l, :, offs["gate"]:].astype(BF16)
        (q, k, vt, hq, kk, hi, sog, logf) = _inproj(h, w_mix_bf, lb_all[l][None, :], offs)

        ya = _attention(q, k, vt, rel_bias.astype(F32),
                        attn_subln_w[l][:, None].astype(F32), lam, 1.0 - lam_init)
        yh = _hgrn(hq, kk, hi, sog, logf, hgrn_norm_w[l][None, :].astype(F32))

        h1 = _merge(ya.reshape(B * S, att_w), yh.reshape(B * S, hg_w), h.reshape(B * S, D),
                    w_branch_attn[l].astype(BF16), w_branch_hgrn[l].astype(BF16),
                    w_gate_bf, b_gate[l][None, :], w_out[l].astype(BF16),
                    ln1_g[l][None, :], ln1_b[l][None, :], alpha)

        h = _ffn(h1.reshape(B, S, D), w_ffn_gate[l].astype(BF16), w_ffn_up[l].astype(BF16),
                 ffn_conv_w[l], ffn_conv_b[l][None, :], w_ffn_down[l].astype(BF16),
                 ln2_g[l][None, :], ln2_b[l][None, :], alpha)
    return h
```

```python
import functools
import math

import numpy as np
import jax
import jax.numpy as jnp
from jax import lax
from jax.experimental import pallas as pl
from jax.experimental.pallas import tpu as pltpu

F32 = jnp.float32
BF16 = jnp.bfloat16

CHUNK = 64
ATT_HEADS = 4
ATT_DH = 64
HG_HEADS = 4
HEAD_W = 128
NUM_BUCKETS = 32
MAX_DISTANCE = 128
EPS = 1e-5
NEG = -1e30

ATT_BQ = 128
ATT_BK = 256
ATT_G = 4
ATT_GK = ATT_G * ATT_BQ // ATT_BK
ATT_NEAR = (0, -1, -2)
ATT_VR = HEAD_W + 16
LOG2E = math.log2(math.e)
HG_C = 128
HG_LEVELS = (64, 32, 16, 8, 4)
HG_DIAG = 4
FFN_FC = 256
ROW_CHUNK = 256

VMEM_LIMIT = 56 * 1024 * 1024


def _cparams(n_axes):
    return pltpu.CompilerParams(
        dimension_semantics=("arbitrary",) * n_axes,
        vmem_limit_bytes=VMEM_LIMIT)


def _const_spec(shape):
    nd = len(shape)
    return pl.BlockSpec(shape, lambda *_: (0,) * nd, pipeline_mode=pl.Buffered(1))


def _inproj_kernel(x_ref, w_ref, lb_ref,
                   q_ref, k_ref, vt_ref, hq_ref, kk_ref, hi_ref, sog_ref, logf_ref,
                   *, tm, offs):
    lb = lb_ref[...]
    for r in range(tm // ROW_CHUNK):
        rows = slice(r * ROW_CHUNK, (r + 1) * ROW_CHUNK)
        xb = x_ref[rows, :].astype(BF16)

        def mm(off, xb=xb):
            return jnp.dot(xb, w_ref[:, off:off + 512], preferred_element_type=F32)

        q_ref[rows, :] = (mm(offs["aq"]) * (ATT_DH ** -0.5 * LOG2E)).astype(BF16)
        k_ref[rows, :] = mm(offs["ak"]).astype(BF16)
        v = mm(offs["av"])
        for hd in range(ATT_HEADS):
            vt_ref[r, hd * ATT_VR:hd * ATT_VR + HEAD_W, :] = (
                v[:, hd * HEAD_W:(hd + 1) * HEAD_W].T.astype(BF16))
            vt_ref[r, hd * ATT_VR + HEAD_W:(hd + 1) * ATT_VR, :] = jnp.ones(
                (ATT_VR - HEAD_W, ATT_BK), BF16)
        hq_ref[rows, :] = mm(offs["hq"]).astype(BF16)
        sig = jax.nn.sigmoid(mm(offs["hf"]))
        logf_ref[rows, :] = jnp.log2(lb + (1.0 - lb) * sig)
        kk_ref[rows, :] = ((1.0 - lb) * (1.0 - sig)).astype(BF16)
        hi_ref[rows, :] = mm(offs["hi"]).astype(BF16)
        hog = mm(offs["hg"])
        sog_ref[rows, :] = (hog * jax.nn.sigmoid(hog)).astype(BF16)


def _inproj(x, w_bf, lb, offs, tm=1024):
    B, S, D = x.shape
    assert ROW_CHUNK == ATT_BK
    row = lambda w: pl.BlockSpec((None, tm, w), lambda b, s: (b, s, 0))
    out_shape = (
        jax.ShapeDtypeStruct((B, S, 512), BF16),
        jax.ShapeDtypeStruct((B, S, 512), BF16),
        jax.ShapeDtypeStruct((B, S // ATT_BK, ATT_HEADS * ATT_VR, ATT_BK), BF16),
        jax.ShapeDtypeStruct((B, S, 512), BF16),
        jax.ShapeDtypeStruct((B, S, 512), BF16),
        jax.ShapeDtypeStruct((B, S, 512), BF16),
        jax.ShapeDtypeStruct((B, S, 512), BF16),
        jax.ShapeDtypeStruct((B, S, 512), F32),
    )
    out_specs = (
        row(512), row(512),
        pl.BlockSpec((None, tm // ATT_BK, ATT_HEADS * ATT_VR, ATT_BK),
                     lambda b, s: (b, s, 0, 0)),
        row(512), row(512), row(512), row(512), row(512),
    )
    return pl.pallas_call(
        functools.partial(_inproj_kernel, tm=tm, offs=offs),
        grid=(B, S // tm),
        in_specs=[row(D), _const_spec(w_bf.shape), _const_spec(lb.shape)],
        out_specs=out_specs,
        out_shape=out_shape,
        compiler_params=_cparams(2),
        name="inproj",
    )(x, w_bf, lb)


def _rel_bucket_np(rel):
    nb = NUM_BUCKETS // 2
    max_exact = nb // 2
    ret = np.where(rel > 0, nb, 0)
    n = np.abs(rel)
    nf = np.maximum(n, 1).astype(np.float32)
    large = max_exact + (np.log(nf / np.float32(max_exact))
                         / np.float32(math.log(MAX_DISTANCE / max_exact))
                         * np.float32(nb - max_exact)).astype(np.int32)
    large = np.minimum(large, nb - 1)
    return ret + np.where(n < max_exact, n, large)


def _attn_bucket_tiles():
    i = np.arange(ATT_BQ)[None, :]
    j = np.arange(ATT_BK)[:, None]
    tiles = []
    for d in ATT_NEAR:
        kpos = d * ATT_BQ + j
        live = (kpos // CHUNK) <= (i // CHUNK)
        tiles.append(np.where(live, _rel_bucket_np(kpos - i), -1))
    far = _rel_bucket_np(np.asarray((min(ATT_NEAR) - 1) * ATT_BQ + ATT_BK - 1))
    assert int(far) == NUM_BUCKETS // 2 - 1
    return np.stack(tiles).astype(np.int32)


def _attn_kernel(lam_ref, rb_ref, bkt_ref, q_ref, k_ref, vt_ref, w_ref, o_ref,
                 bias_ref, qs_ref, s0_ref, s1_ref, p0_ref, p1_ref, a0_ref, a1_ref,
                 m_ref, acc_ref, *, out_scale):
    S = q_ref.shape[0]
    nkb = S // ATT_BK
    W2 = 2 * ATT_BQ
    near = tuple(range(ATT_GK)) + (-1,)
    lam = lam_ref[0]
    h = pl.program_id(0)

    @pl.when(pl.program_id(1) == 0)
    def _():
        far = rb_ref[NUM_BUCKETS // 2 - 1, h]
        for n, kbrel in enumerate(near):
            for j in range(ATT_G):
                d = (kbrel * ATT_BK) // ATT_BQ - j
                if d > max(ATT_NEAR):
                    bias_ref[n, j] = jnp.full((ATT_BK, W2), NEG, F32)
                elif d < min(ATT_NEAR):
                    bias_ref[n, j] = jnp.zeros((ATT_BK, W2), F32)
        for t, dt in enumerate(ATT_NEAR):
            bk = bkt_ref[t]
            tile = jnp.full(bk.shape, NEG, F32)
            for bucket in range(NUM_BUCKETS):
                tile = jnp.where(bk == bucket, (rb_ref[bucket, h] - far) * LOG2E, tile)
            tile2 = jnp.concatenate([tile, tile], axis=1)
            for n, kbrel in enumerate(near):
                for j in range(ATT_G):
                    if (kbrel * ATT_BK) // ATT_BQ - j == dt:
                        bias_ref[n, j] = tile2

    lane = lax.broadcasted_iota(jnp.int32, (ATT_BQ, HEAD_W), 1)
    first_map = lane < ATT_DH

    s_bufs, p_bufs, a_bufs = (s0_ref, s1_ref), (p0_ref, p1_ref), (a0_ref, a1_ref)

    def scores(step, j, buf):
        kb, bias_idx = step
        kblk = k_ref[pl.ds(pl.multiple_of(kb * ATT_BK, ATT_BK), ATT_BK), :]
        s = lax.dot_general(kblk, qs_ref[j], (((1,), (1,)), ((), ())),
                            preferred_element_type=F32)
        if bias_idx is not None:
            s = s + bias_ref[bias_idx, j]
        s_bufs[buf][j] = s

    def soft(j, buf, first=False):
        s = s_bufs[buf][j]
        m_new = jnp.max(s, axis=0, keepdims=True)
        if first:
            a_bufs[buf][j] = jnp.ones_like(m_new)
        else:
            m_old = m_ref[j]
            m_new = jnp.maximum(m_old, m_new)
            a_bufs[buf][j] = jnp.exp2(m_old - m_new)
        m_ref[j] = m_new
        p_bufs[buf][j] = jnp.exp2(s - m_new).astype(BF16)

    def accum(kb, j, buf, first=False):
        pv = jnp.dot(vt_ref[kb], p_bufs[buf][j], preferred_element_type=F32)
        if first:
            acc_ref[j] = pv
        else:
            acc_ref[j] = a_bufs[buf][j] * acc_ref[j] + pv

    AHEAD = 4

    def two_steps(steps, first=False, prev_kb=None, dead=()):
        for k in range(2 * ATT_G):
            ua = k + AHEAD
            if (ua // ATT_G, ua % ATT_G) not in dead:
                scores(steps[ua // ATT_G], ua % ATT_G, (ua // ATT_G) % 2)
            uc = k - 1
            if k >= 1 and (uc // ATT_G, uc % ATT_G) not in dead:
                accum(steps[uc // ATT_G][0], uc % ATT_G, uc // ATT_G,
                      first=first and uc < ATT_G)
            elif k == 0 and prev_kb is not None:
                accum(prev_kb, ATT_G - 1, 1)
            if (k // ATT_G, k % ATT_G) not in dead:
                soft(k % ATT_G, k // ATT_G, first=first and k < ATT_G)

    def inside(g):
        return [(g * ATT_GK + i, i) for i in range(ATT_GK)]

    def before(g):
        return (jnp.maximum(g * ATT_GK - 1, 0), ATT_GK)

    def far(n):
        return (jnp.clip(n - (ATT_GK + 1), 0, nkb - 1), None)

    dead = {(i, j) for i in range(ATT_GK) for j in range(ATT_G)
            if (i * ATT_BK) // ATT_BQ - j > max(ATT_NEAR)}

    def open_group(g):
        q0 = pl.multiple_of(g * (ATT_G * ATT_BQ), ATT_G * ATT_BQ)
        for j in range(ATT_G):
            q = q_ref[pl.ds(q0 + j * ATT_BQ, ATT_BQ), :]
            zero = jnp.zeros_like(q)
            qs_ref[j, :ATT_BQ, :] = jnp.where(first_map, q, zero)
            qs_ref[j, ATT_BQ:, :] = jnp.where(first_map, zero, q)
        for j in range(AHEAD):
            scores(inside(g)[0], j, 0)

    def close_group(g, n_steps):
        last_kb = jnp.where(g > 0, far(n_steps - 1)[0], inside(g)[1][0])
        accum(last_kb, ATT_G - 1, 1)

    def emit_group(g):
        q0 = pl.multiple_of(g * (ATT_G * ATT_BQ), ATT_G * ATT_BQ)
        for j in range(ATT_G):
            on = acc_ref[j, :HEAD_W, :] * (1.0 / acc_ref[j, HEAD_W:HEAD_W + 1, :])
            o = on[:, :ATT_BQ] - lam * on[:, ATT_BQ:]
            ms = jnp.mean(o * o, axis=0, keepdims=True)
            y = o * lax.rsqrt(ms + EPS) * w_ref[...] * out_scale
            o_ref[pl.ds(q0 + j * ATT_BQ, ATT_BQ), :] = y.T.astype(BF16)

    n_groups = S // (ATT_G * ATT_BQ)
    open_group(0)
    two_steps(inside(0) + [before(0)], first=True, dead=dead)

    def group(g, carry):
        n_steps = jnp.where(g > 0, ATT_GK + 1 + (g * ATT_GK - 1), ATT_GK)

        @pl.when(g > 0)
        def _():
            two_steps([before(g), far(3), far(4)], prev_kb=inside(g)[1][0])

        def far_steps(n):
            two_steps([far(n), far(n + 1), far(n + 2)], prev_kb=far(n - 1)[0])

        n_far_pairs = jnp.maximum(n_steps // 2 - 2, 0)
        first_far = ATT_GK + 2

        def four_far_steps(i, c):
            far_steps(first_far + 4 * i)
            far_steps(first_far + 4 * i + 2)
            return c

        lax.fori_loop(0, n_far_pairs // 2, four_far_steps, 0)

        @pl.when(n_far_pairs % 2 == 1)
        def _():
            far_steps(first_far + 4 * (n_far_pairs // 2))

        @pl.when(g + 1 < n_groups)
        def _():
            close_group(g, n_steps)
            open_group(g + 1)
            emit_group(g)
            two_steps(inside(g + 1) + [before(g + 1)], first=True, dead=dead)

        @pl.when(g + 1 == n_groups)
        def _():
            close_group(g, n_steps)
            emit_group(g)

        return carry

    lax.fori_loop(0, n_groups, group, 0)


def _attention(q, k, vt, rel_bias, subln_w, lam, out_scale):
    B, S, _ = q.shape
    nkb = S // ATT_BK
    W2 = 2 * ATT_BQ
    assert ATT_GK == 2 and ATT_BK == 2 * ATT_BQ
    bkt = jnp.asarray(_attn_bucket_tiles())
    head = pl.BlockSpec((None, S, HEAD_W), lambda h, b: (b, 0, h))
    smem = pl.BlockSpec(memory_space=pltpu.SMEM)
    return pl.pallas_call(
        functools.partial(_attn_kernel, out_scale=out_scale),
        grid=(ATT_HEADS, B),
        in_specs=[
            smem, smem, _const_spec(bkt.shape),
            head, head,
            pl.BlockSpec((None, nkb, ATT_VR, ATT_BK), lambda h, b: (b, 0, h, 0)),
            pl.BlockSpec((HEAD_W, 1), lambda h, b: (0, 0)),
        ],
        out_specs=head,
        out_shape=jax.ShapeDtypeStruct((B, S, ATT_HEADS * HEAD_W), BF16),
        scratch_shapes=[
            pltpu.VMEM((ATT_GK + 1, ATT_G, ATT_BK, W2), F32),
            pltpu.VMEM((ATT_G, W2, HEAD_W), BF16),
            pltpu.VMEM((ATT_G, ATT_BK, W2), F32),
            pltpu.VMEM((ATT_G, ATT_BK, W2), F32),
            pltpu.VMEM((ATT_G, ATT_BK, W2), BF16),
            pltpu.VMEM((ATT_G, ATT_BK, W2), BF16),
            pltpu.VMEM((ATT_G, 1, W2), F32),
            pltpu.VMEM((ATT_G, 1, W2), F32),
            pltpu.VMEM((ATT_G, 1, W2), F32),
            pltpu.VMEM((ATT_G, ATT_VR, W2), F32),
        ],
        compiler_params=_cparams(2),
        name="diffattn",
    )(lam, rel_bias, bkt, q, k, vt, subln_w)


def _hgrn_kernel(hq_ref, kk_ref, hi_ref, sog_ref, lf_ref, w_ref, y_ref,
                 state_ref, kbuf_ref, bbuf_ref, *, tc):
    C, H, W = HG_C, HG_HEADS, HEAD_W
    PAD = 8

    @pl.when(pl.program_id(1) == 0)
    def _():
        state_ref[...] = jnp.zeros_like(state_ref)

    kbuf_ref[:, 0:PAD, :] = jnp.zeros((H, PAD, W), F32)
    bbuf_ref[:, 0:PAD, :] = jnp.zeros((H, PAD, W), F32)

    row = lax.broadcasted_iota(jnp.int32, (C, W), 0)
    r2 = lax.broadcasted_iota(jnp.int32, (C, C), 0)
    c2 = lax.broadcasted_iota(jnp.int32, (C, C), 1)
    tri = (c2 <= r2).astype(BF16)
    is_q = {hs: ((row // hs) % 2) == 1 for hs in HG_LEVELS}
    sign = {hs: jnp.where(is_q[hs], 1.0, -1.0).astype(F32) for hs in HG_LEVELS}
    keep = {hs: ((r2 // (2 * hs)) == (c2 // (2 * hs)))
            & (((r2 // hs) % 2) == 1) & (((c2 // hs) % 2) == 0) for hs in HG_LEVELS}
    same_blk = (r2 // HG_DIAG) == (c2 // HG_DIAG)
    on_diag = {d: ((r2 - c2) == d) & same_blk for d in range(HG_DIAG)}

    def level_ref(b, hs):
        if hs >= PAD:
            parts = [jnp.broadcast_to(b[g * 2 * hs + hs - 1:g * 2 * hs + hs, :], (2 * hs, W))
                     for g in range(C // (2 * hs))]
            return parts[0] if len(parts) == 1 else jnp.concatenate(parts, axis=0)
        b3 = b.reshape(C // (2 * hs), 2 * hs, W)
        return jnp.broadcast_to(b3[:, hs - 1:hs, :], b3.shape).reshape(C, W)

    def chunk(c, carry):
        r0 = pl.multiple_of(c * C, C)
        lf = lf_ref[pl.ds(r0, C), :]
        p0 = lf.astype(BF16)
        p1 = (lf - p0.astype(F32)).astype(BF16)
        b_all = (jnp.dot(tri, p0, preferred_element_type=F32)
                 + jnp.dot(tri, p1, preferred_element_type=F32))

        heads = range(H)
        cols = [slice(h * W, (h + 1) * W) for h in heads]
        b = [b_all[:, cols[h]] for h in heads]
        qf = [hq_ref[pl.ds(r0, C), cols[h]].astype(F32) for h in heads]
        kf = [kk_ref[pl.ds(r0, C), cols[h]].astype(F32) for h in heads]
        iv = [hi_ref[pl.ds(r0, C), cols[h]] for h in heads]
        st = [state_ref[h] for h in heads]
        b_last = [b[h][C - 1:C, :] for h in heads]

        inter = [lax.dot_general((qf[h] * jnp.exp2(b[h])).astype(BF16), st[h].astype(BF16),
                                 (((1,), (1,)), ((), ())), preferred_element_type=F32)
                 for h in heads]

        a = [None] * H
        for hs in HG_LEVELS:
            for h in heads:
                e = jnp.exp2((b[h] - level_ref(b[h], hs)) * sign[hs])
                y = (jnp.where(is_q[hs], qf[h], kf[h]) * e).astype(BF16)
                sc = lax.dot_general(y, y, (((1,), (1,)), ((), ())),
                                     preferred_element_type=F32)
                sc = jnp.where(keep[hs], sc, 0.0)
                a[h] = sc if a[h] is None else a[h] + sc
        for h in heads:
            kbuf_ref[h, PAD:, :] = kf[h]
            bbuf_ref[h, PAD:, :] = b[h]
        for d in range(HG_DIAG):
            for h in heads:
                if d == 0:
                    term = qf[h] * kf[h]
                else:
                    ks = kbuf_ref[h, PAD - d:PAD - d + C, :]
                    bs = bbuf_ref[h, PAD - d:PAD - d + C, :]
                    term = qf[h] * ks * jnp.exp2(b[h] - bs)
                dsum = jnp.sum(term, axis=-1, keepdims=True)
                a[h] = a[h] + jnp.where(on_diag[d], dsum, 0.0)

        o = [inter[h] + jnp.dot(a[h].astype(BF16), iv[h], preferred_element_type=F32)
             for h in heads]

        for h in heads:
            k_end = (kf[h] * jnp.exp2(b_last[h] - b[h])).astype(BF16)
            upd = lax.dot_general(iv[h], k_end, (((0,), (0,)), ((), ())),
                                  preferred_element_type=F32)
            state_ref[h] = st[h] * jnp.exp2(b_last[h]) + upd

        for h in heads:
            ms = jnp.mean(o[h] * o[h], axis=-1, keepdims=True)
            y = o[h] * lax.rsqrt(ms + EPS) * w_ref[...]
            y = y * sog_ref[pl.ds(r0, C), cols[h]].astype(F32)
            y_ref[pl.ds(r0, C), cols[h]] = y.astype(BF16)
        return carry

    lax.fori_loop(0, tc // C, chunk, 0)


def _hgrn(hq, kk, hi, sog, logf, norm_w, tc=512):
    B, S, W = hq.shape
    blk = pl.BlockSpec((None, tc, W), lambda b, s: (b, s, 0))
    return pl.pallas_call(
        functools.partial(_hgrn_kernel, tc=tc),
        grid=(B, S // tc),
        in_specs=[blk, blk, blk, blk, blk, pl.BlockSpec((1, HEAD_W), lambda b, s: (0, 0))],
        out_specs=blk,
        out_shape=jax.ShapeDtypeStruct((B, S, W), BF16),
        scratch_shapes=[pltpu.VMEM((HG_HEADS, HEAD_W, HEAD_W), F32),
                        pltpu.VMEM((HG_HEADS, HG_C + 8, HEAD_W), F32),
                        pltpu.VMEM((HG_HEADS, HG_C + 8, HEAD_W), F32)],
        compiler_params=_cparams(2),
        name="hgrn2",
    )(hq, kk, hi, sog, logf, norm_w)


def _layer_norm(z, g, b):
    mu = jnp.mean(z, axis=-1, keepdims=True)
    zc = z - mu
    var = jnp.mean(zc * zc, axis=-1, keepdims=True)
    return zc * lax.rsqrt(var + EPS) * g + b


def _merge_kernel(ya_ref, yh_ref, x_ref, wa_ref, wh_ref, wg_ref, bg_ref, wo_ref,
                  g_ref, b_ref, o_ref, *, alpha):
    D = x_ref.shape[-1]
    tm = x_ref.shape[0]

    def branches(r):
        rows = slice(r * ROW_CHUNK, (r + 1) * ROW_CHUNK)
        xb = x_ref[rows, :].astype(BF16)
        pa = jnp.dot(ya_ref[rows, :], wa_ref[...], preferred_element_type=F32)
        ph = jnp.dot(yh_ref[rows, :], wh_ref[...], preferred_element_type=F32)
        ga = jnp.dot(xb, wg_ref[:, :D], preferred_element_type=F32)
        gh = jnp.dot(xb, wg_ref[:, D:], preferred_element_type=F32)
        return pa, ph, ga, gh

    n_chunks = tm // ROW_CHUNK
    nxt = branches(0)
    for r in range(n_chunks):
        rows = slice(r * ROW_CHUNK, (r + 1) * ROW_CHUNK)
        pa, ph, ga, gh = nxt
        if r + 1 < n_chunks:
            nxt = branches(r + 1)
        merged = (jax.nn.sigmoid(ga + bg_ref[:, :D]) * pa
                  + jax.nn.sigmoid(gh + bg_ref[:, D:]) * ph)
        mix = jnp.dot(merged.astype(BF16), wo_ref[...], preferred_element_type=F32)
        o_ref[rows, :] = _layer_norm(alpha * x_ref[rows, :] + mix, g_ref[...], b_ref[...])


def _merge(ya, yh, x, wa, wh, wg, bg, wo, g, b, alpha, tm=1024):
    N, D = x.shape
    row = lambda w: pl.BlockSpec((tm, w), lambda i: (i, 0))
    return pl.pallas_call(
        functools.partial(_merge_kernel, alpha=alpha),
        grid=(N // tm,),
        in_specs=[row(ya.shape[1]), row(yh.shape[1]), row(D),
                  _const_spec(wa.shape), _const_spec(wh.shape), _const_spec(wg.shape),
                  _const_spec(bg.shape), _const_spec(wo.shape),
                  _const_spec(g.shape), _const_spec(b.shape)],
        out_specs=row(D),
        out_shape=jax.ShapeDtypeStruct((N, D), F32),
        compiler_params=_cparams(1),
        name="merge_ln",
    )(ya, yh, x, wa, wh, wg, bg, wo, g, b)


def _ffn_kernel(h_ref, wg_ref, wu_ref, cw_ref, cb_ref, wd_ref, g_ref, b_ref, o_ref,
                tail_ref, gbuf_ref, *, alpha, tm):
    d_ff = wg_ref.shape[1]
    PAD = 8

    @pl.when(pl.program_id(1) == 0)
    def _():
        tail_ref[...] = jnp.zeros_like(tail_ref)

    RC = ROW_CHUNK
    nfc = d_ff // FFN_FC
    steps = [(r, fc) for r in range(tm // RC) for fc in range(nfc)]
    hb = {}

    def gate_up(r, fc):
        if r not in hb:
            hb[r] = h_ref[r * RC:(r + 1) * RC, :].astype(BF16)
        cols = slice(fc * FFN_FC, (fc + 1) * FFN_FC)
        return (jnp.dot(hb[r], wg_ref[:, cols], preferred_element_type=F32),
                jnp.dot(hb[r], wu_ref[:, cols], preferred_element_type=F32))

    nxt = gate_up(*steps[0])
    acc = None
    for t, (r, fc) in enumerate(steps):
        rows = slice(r * RC, (r + 1) * RC)
        cols = slice(fc * FFN_FC, (fc + 1) * FFN_FC)
        gte, up = nxt
        if t + 1 < len(steps):
            nxt = gate_up(*steps[t + 1])
        gbuf = gbuf_ref.at[t % 2]
        gbuf[0:PAD, :] = tail_ref[:, cols]
        gbuf[PAD:, :] = gte
        tail_ref[:, cols] = gte[RC - PAD:, :]
        conv = (gte * cw_ref[2:3, cols]
                + gbuf[PAD - 1:PAD - 1 + RC, :] * cw_ref[1:2, cols]
                + gbuf[PAD - 2:PAD - 2 + RC, :] * cw_ref[0:1, cols]
                + cb_ref[:, cols])
        act = jax.nn.gelu(conv) * up
        down = jnp.dot(act.astype(BF16), wd_ref[cols, :], preferred_element_type=F32)
        acc = down if fc == 0 else acc + down
        if fc == nfc - 1:
            o_ref[rows, :] = _layer_norm(alpha * h_ref[rows, :] + acc, g_ref[...], b_ref[...])


def _ffn(h, wg, wu, cw, cb, wd, g, b, alpha, tm=1024):
    B, S, D = h.shape
    d_ff = wg.shape[1]
    row = pl.BlockSpec((None, tm, D), lambda bi, s: (bi, s, 0))
    return pl.pallas_call(
        functools.partial(_ffn_kernel, alpha=alpha, tm=tm),
        grid=(B, S // tm),
        in_specs=[row, _const_spec(wg.shape), _const_spec(wu.shape), _const_spec(cw.shape),
                  _const_spec(cb.shape), _const_spec(wd.shape),
                  _const_spec(g.shape), _const_spec(b.shape)],
        out_specs=row,
        out_shape=jax.ShapeDtypeStruct((B, S, D), F32),
        scratch_shapes=[pltpu.VMEM((8, d_ff), F32),
                        pltpu.VMEM((2, ROW_CHUNK + 8, FFN_FC), F32)],
        compiler_params=_cparams(2),
        name="ffn_ln",
    )(h, wg, wu, cw, cb, wd, g, b)


def kernel(x, w_in, b_gate, lambda_q1, lambda_k1, lambda_q2, lambda_k2, attn_subln_w,
           rel_bias, hgrn_lb_logits, hgrn_norm_w, w_branch_attn, w_branch_hgrn, w_out,
           ln1_g, ln1_b, w_ffn_gate, w_ffn_up, ffn_conv_w, ffn_conv_b, w_ffn_down,
           ln2_g, ln2_b):
    B, S, D = x.shape
    depth = w_in.shape[0]
    att_qk = ATT_HEADS * 2 * ATT_DH
    att_w = ATT_HEADS * HEAD_W
    hg_w = HG_HEADS * HEAD_W
    offs = {"aq": 0, "ak": att_qk, "av": 2 * att_qk}
    offs["hq"] = offs["av"] + att_w
    offs["hf"] = offs["hq"] + hg_w
    offs["hi"] = offs["hf"] + hg_w
    offs["hg"] = offs["hi"] + hg_w
    offs["gate"] = offs["hg"] + hg_w
    alpha = (2 * depth) ** 0.25

    lb_all = jnp.cumsum(jax.nn.softmax(hgrn_lb_logits.astype(F32), axis=0), axis=0)

    h = x
    for l in range(depth):
        lam_init = 0.8 - 0.6 * math.exp(-0.3 * l)
        lam = (jnp.exp(jnp.sum(lambda_q1[l].astype(F32) * lambda_k1[l].astype(F32)))
               - jnp.exp(jnp.sum(lambda_q2[l].astype(F32) * lambda_k2[l].astype(F32)))
               + lam_init).reshape(1)

        w_mix_bf = w_in[l, :, :offs["gate"]].astype(BF16)
        w_gate_bf = w_in[l, :, offs["gate"]:].astype(BF16)
        (q, k, vt, hq, kk, hi, sog, logf) = _inproj(h, w_mix_bf, lb_all[l][None, :], offs)

        ya = _attention(q, k, vt, rel_bias.astype(F32),
                        attn_subln_w[l][:, None].astype(F32), lam, 1.0 - lam_init)
        yh = _hgrn(hq, kk, hi, sog, logf, hgrn_norm_w[l][None, :].astype(F32))

        h1 = _merge(ya.reshape(B * S, att_w), yh.reshape(B * S, hg_w), h.reshape(B * S, D),
                    w_branch_attn[l].astype(BF16), w_branch_hgrn[l].astype(BF16),
                    w_gate_bf, b_gate[l][None, :], w_out[l].astype(BF16),
                    ln1_g[l][None, :], ln1_b[l][None, :], alpha)

        h = _ffn(h1.reshape(B, S, D), w_ffn_gate[l].astype(BF16), w_ffn_up[l].astype(BF16),
                 ffn_conv_w[l], ffn_conv_b[l][None, :], w_ffn_down[l].astype(BF16),
                 ln2_g[l][None, :], ln2_b[l][None, :], alpha)
    return h
```

```python
import functools
import math

import numpy as np
import jax
import jax.numpy as jnp
from jax import lax
from jax.experimental import pallas as pl
from jax.experimental.pallas import tpu as pltpu

F32 = jnp.float32
BF16 = jnp.bfloat16

CHUNK = 64
ATT_HEADS = 4
ATT_DH = 64
HG_HEADS = 4
HEAD_W = 128
NUM_BUCKETS = 32
MAX_DISTANCE = 128
EPS = 1e-5
NEG = -1e30

ATT_BQ = 128
ATT_BK = 256
ATT_G = 4
ATT_GK = ATT_G * ATT_BQ // ATT_BK
ATT_NEAR = (0, -1, -2)
ATT_VR = HEAD_W + 16
LOG2E = math.log2(math.e)
HG_C = 128
HG_LEVELS = (64, 32, 16, 8, 4)
HG_DIAG = 4
FFN_FC = 256
ROW_CHUNK = 256

VMEM_LIMIT = 56 * 1024 * 1024


def _cparams(n_axes):
    return pltpu.CompilerParams(
        dimension_semantics=("arbitrary",) * n_axes,
        vmem_limit_bytes=VMEM_LIMIT)


def _const_spec(shape):
    nd = len(shape)
    return pl.BlockSpec(shape, lambda *_: (0,) * nd, pipeline_mode=pl.Buffered(1))


def _inproj_kernel(x_ref, w_ref, lb_ref,
                   q_ref, k_ref, vt_ref, hq_ref, kk_ref, hi_ref, sog_ref, logf_ref,
                   *, tm, offs):
    lb = lb_ref[...]
    for r in range(tm // ROW_CHUNK):
        rows = slice(r * ROW_CHUNK, (r + 1) * ROW_CHUNK)
        xb = x_ref[rows, :].astype(BF16)

        def mm(off, xb=xb):
            return jnp.dot(xb, w_ref[:, off:off + 512], preferred_element_type=F32)

        q_ref[rows, :] = (mm(offs["aq"]) * (ATT_DH ** -0.5 * LOG2E)).astype(BF16)
        k_ref[rows, :] = mm(offs["ak"]).astype(BF16)
        v = mm(offs["av"])
        for hd in range(ATT_HEADS):
            vt_ref[r, hd * ATT_VR:hd * ATT_VR + HEAD_W, :] = (
                v[:, hd * HEAD_W:(hd + 1) * HEAD_W].T.astype(BF16))
            vt_ref[r, hd * ATT_VR + HEAD_W:(hd + 1) * ATT_VR, :] = jnp.ones(
                (ATT_VR - HEAD_W, ATT_BK), BF16)
        hq_ref[rows, :] = mm(offs["hq"]).astype(BF16)
        sig = jax.nn.sigmoid(mm(offs["hf"]))
        logf_ref[rows, :] = jnp.log2(lb + (1.0 - lb) * sig)
        kk_ref[rows, :] = ((1.0 - lb) * (1.0 - sig)).astype(BF16)
        hi_ref[rows, :] = mm(offs["hi"]).astype(BF16)
        hog = mm(offs["hg"])
        sog_ref[rows, :] = (hog * jax.nn.sigmoid(hog)).astype(BF16)


def _inproj(x, w_bf, lb, offs, tm=1024):
    B, S, D = x.shape
    assert ROW_CHUNK == ATT_BK
    row = lambda w: pl.BlockSpec((None, tm, w), lambda b, s: (b, s, 0))
    out_shape = (
        jax.ShapeDtypeStruct((B, S, 512), BF16),
        jax.ShapeDtypeStruct((B, S, 512), BF16),
        jax.ShapeDtypeStruct((B, S // ATT_BK, ATT_HEADS * ATT_VR, ATT_BK), BF16),
        jax.ShapeDtypeStruct((B, S, 512), BF16),
        jax.ShapeDtypeStruct((B, S, 512), BF16),
        jax.ShapeDtypeStruct((B, S, 512), BF16),
        jax.ShapeDtypeStruct((B, S, 512), BF16),
        jax.ShapeDtypeStruct((B, S, 512), F32),
    )
    out_specs = (
        row(512), row(512),
        pl.BlockSpec((None, tm // ATT_BK, ATT_HEADS * ATT_VR, ATT_BK),
                     lambda b, s: (b, s, 0, 0)),
        row(512), row(512), row(512), row(512), row(512),
    )
    return pl.pallas_call(
        functools.partial(_inproj_kernel, tm=tm, offs=offs),
        grid=(B, S // tm),
        in_specs=[row(D), _const_spec(w_bf.shape), _const_spec(lb.shape)],
        out_specs=out_specs,
        out_shape=out_shape,
        compiler_params=_cparams(2),
        name="inproj",
    )(x, w_bf, lb)


def _rel_bucket_np(rel):
    nb = NUM_BUCKETS // 2
    max_exact = nb // 2
    ret = np.where(rel > 0, nb, 0)
    n = np.abs(rel)
    nf = np.maximum(n, 1).astype(np.float32)
    large = max_exact + (np.log(nf / np.float32(max_exact))
                         / np.float32(math.log(MAX_DISTANCE / max_exact))
                         * np.float32(nb - max_exact)).astype(np.int32)
    large = np.minimum(large, nb - 1)
    return ret + np.where(n < max_exact, n, large)


def _attn_bucket_tiles():
    i = np.arange(ATT_BQ)[None, :]
    j = np.arange(ATT_BK)[:, None]
    tiles = []
    for d in ATT_NEAR:
        kpos = d * ATT_BQ + j
        live = (kpos // CHUNK) <= (i // CHUNK)
        tiles.append(np.where(live, _rel_bucket_np(kpos - i), -1))
    far = _rel_bucket_np(np.asarray((min(ATT_NEAR) - 1) * ATT_BQ + ATT_BK - 1))
    assert int(far) == NUM_BUCKETS // 2 - 1
    return np.stack(tiles).astype(np.int32)


def _attn_kernel(lam_ref, rb_ref, bkt_ref, q_ref, k_ref, vt_ref, w_ref, o_ref,
                 bias_ref, qs_ref, s0_ref, s1_ref, p0_ref, p1_ref, a0_ref, a1_ref,
                 m_ref, acc_ref, *, out_scale):
    S = q_ref.shape[0]
    nkb = S // ATT_BK
    W2 = 2 * ATT_BQ
    near = tuple(range(ATT_GK)) + (-1,)
    lam = lam_ref[0]
    h = pl.program_id(0)

    @pl.when(pl.program_id(1) == 0)
    def _():
        far = rb_ref[NUM_BUCKETS // 2 - 1, h]
        for n, kbrel in enumerate(near):
            for j in range(ATT_G):
                d = (kbrel * ATT_BK) // ATT_BQ - j
                if d > max(ATT_NEAR):
                    bias_ref[n, j] = jnp.full((ATT_BK, W2), NEG, F32)
                elif d < min(ATT_NEAR):
                    bias_ref[n, j] = jnp.zeros((ATT_BK, W2), F32)
        for t, dt in enumerate(ATT_NEAR):
            bk = bkt_ref[t]
            tile = jnp.full(bk.shape, NEG, F32)
            for bucket in range(NUM_BUCKETS):
                tile = jnp.where(bk == bucket, (rb_ref[bucket, h] - far) * LOG2E, tile)
            tile2 = jnp.concatenate([tile, tile], axis=1)
            for n, kbrel in enumerate(near):
                for j in range(ATT_G):
                    if (kbrel * ATT_BK) // ATT_BQ - j == dt:
                        bias_ref[n, j] = tile2

    lane = lax.broadcasted_iota(jnp.int32, (ATT_BQ, HEAD_W), 1)
    first_map = lane < ATT_DH

    s_bufs, p_bufs, a_bufs = (s0_ref, s1_ref), (p0_ref, p1_ref), (a0_ref, a1_ref)

    def scores(step, j, buf):
        kb, bias_idx = step
        kblk = k_ref[pl.ds(pl.multiple_of(kb * ATT_BK, ATT_BK), ATT_BK), :]
        s = lax.dot_general(kblk, qs_ref[j], (((1,), (1,)), ((), ())),
                            preferred_element_type=F32)
        if bias_idx is not None:
            s = s + bias_ref[bias_idx, j]
        s_bufs[buf][j] = s

    def soft(j, buf, first=False):
        s = s_bufs[buf][j]
        m_new = jnp.max(s, axis=0, keepdims=True)
        if first:
            a_bufs[buf][j] = jnp.ones_like(m_new)
        else:
            m_old = m_ref[j]
            m_new = jnp.maximum(m_old, m_new)
            a_bufs[buf][j] = jnp.exp2(m_old - m_new)
        m_ref[j] = m_new
        p_bufs[buf][j] = jnp.exp2(s - m_new).astype(BF16)

    def accum(kb, j, buf, first=False):
        pv = jnp.dot(vt_ref[kb], p_bufs[buf][j], preferred_element_type=F32)
        if first:
            acc_ref[j] = pv
        else:
            acc_ref[j] = a_bufs[buf][j] * acc_ref[j] + pv

    AHEAD = 4

    def two_steps(steps, first=False, prev_kb=None, dead=()):
        for k in range(2 * ATT_G):
            ua = k + AHEAD
            if (ua // ATT_G, ua % ATT_G) not in dead:
                scores(steps[ua // ATT_G], ua % ATT_G, (ua // ATT_G) % 2)
            uc = k - 1
            if k >= 1 and (uc // ATT_G, uc % ATT_G) not in dead:
                accum(steps[uc // ATT_G][0], uc % ATT_G, uc // ATT_G,
                      first=first and uc < ATT_G)
            elif k == 0 and prev_kb is not None:
                accum(prev_kb, ATT_G - 1, 1)
            if (k // ATT_G, k % ATT_G) not in dead:
                soft(k % ATT_G, k // ATT_G, first=first and k < ATT_G)

    def inside(g):
        return [(g * ATT_GK + i, i) for i in range(ATT_GK)]

    def before(g):
        return (jnp.maximum(g * ATT_GK - 1, 0), ATT_GK)

    def far(n):
        return (jnp.clip(n - (ATT_GK + 1), 0, nkb - 1), None)

    dead = {(i, j) for i in range(ATT_GK) for j in range(ATT_G)
            if (i * ATT_BK) // ATT_BQ - j > max(ATT_NEAR)}

    def open_group(g):
        q0 = pl.multiple_of(g * (ATT_G * ATT_BQ), ATT_G * ATT_BQ)
        for j in range(ATT_G):
            q = q_ref[pl.ds(q0 + j * ATT_BQ, ATT_BQ), :]
            zero = jnp.zeros_like(q)
            qs_ref[j, :ATT_BQ, :] = jnp.where(first_map, q, zero)
            qs_ref[j, ATT_BQ:, :] = jnp.where(first_map, zero, q)
        for j in range(AHEAD):
            scores(inside(g)[0], j, 0)

    def close_group(g, n_steps):
        last_kb = jnp.where(g > 0, far(n_steps - 1)[0], inside(g)[1][0])
        accum(last_kb, ATT_G - 1, 1)

    def emit_group(g):
        q0 = pl.multiple_of(g * (ATT_G * ATT_BQ), ATT_G * ATT_BQ)
        for j in range(ATT_G):
            on = acc_ref[j, :HEAD_W, :] * (1.0 / acc_ref[j, HEAD_W:HEAD_W + 1, :])
            o = on[:, :ATT_BQ] - lam * on[:, ATT_BQ:]
            ms = jnp.mean(o * o, axis=0, keepdims=True)
            y = o * lax.rsqrt(ms + EPS) * w_ref[...] * out_scale
            o_ref[pl.ds(q0 + j * ATT_BQ, ATT_BQ), :] = y.T.astype(BF16)

    n_groups = S // (ATT_G * ATT_BQ)
    open_group(0)
    two_steps(inside(0) + [before(0)], first=True, dead=dead)

    def group(g, carry):
        n_steps = jnp.where(g > 0, ATT_GK + 1 + (g * ATT_GK - 1), ATT_GK)

        def far_steps(n):
            two_steps([far(n), far(n + 1), far(n + 2)], prev_kb=far(n - 1)[0])

        n_far_pairs = jnp.maximum(n_steps // 2 - 2, 0)
        first_far = ATT_GK + 2

        def four_far_steps(i, c):
            far_steps(first_far + 4 * i)
            far_steps(first_far + 4 * i + 2)
            return c

        lax.fori_loop(0, n_far_pairs // 2, four_far_steps, 0)

        @pl.when(n_far_pairs % 2 == 1)
        def _():
            far_steps(first_far + 4 * (n_far_pairs // 2))

        @pl.when(g + 1 < n_groups)
        def _():
            close_group(g, n_steps)
            open_group(g + 1)
            emit_group(g)
            two_steps(inside(g + 1) + [before(g + 1)], first=True, dead=dead)
            two_steps([before(g + 1), far(3), far(4)], prev_kb=inside(g + 1)[1][0])

        @pl.when(g + 1 == n_groups)
        def _():
            close_group(g, n_steps)
            emit_group(g)

        return carry

    lax.fori_loop(0, n_groups, group, 0)


def _attention(q, k, vt, rel_bias, subln_w, lam, out_scale):
    B, S, _ = q.shape
    nkb = S // ATT_BK
    W2 = 2 * ATT_BQ
    assert ATT_GK == 2 and ATT_BK == 2 * ATT_BQ
    bkt = jnp.asarray(_attn_bucket_tiles())
    head = pl.BlockSpec((None, S, HEAD_W), lambda h, b: (b, 0, h))
    smem = pl.BlockSpec(memory_space=pltpu.SMEM)
    return pl.pallas_call(
        functools.partial(_attn_kernel, out_scale=out_scale),
        grid=(ATT_HEADS, B),
        in_specs=[
            smem, smem, _const_spec(bkt.shape),
            head, head,
            pl.BlockSpec((None, nkb, ATT_VR, ATT_BK), lambda h, b: (b, 0, h, 0)),
            pl.BlockSpec((HEAD_W, 1), lambda h, b: (0, 0)),
        ],
        out_specs=head,
        out_shape=jax.ShapeDtypeStruct((B, S, ATT_HEADS * HEAD_W), BF16),
        scratch_shapes=[
            pltpu.VMEM((ATT_GK + 1, ATT_G, ATT_BK, W2), F32),
            pltpu.VMEM((ATT_G, W2, HEAD_W), BF16),
            pltpu.VMEM((ATT_G, ATT_BK, W2), F32),
            pltpu.VMEM((ATT_G, ATT_BK, W2), F32),
            pltpu.VMEM((ATT_G, ATT_BK, W2), BF16),
            pltpu.VMEM((ATT_G, ATT_BK, W2), BF16),
            pltpu.VMEM((ATT_G, 1, W2), F32),
            pltpu.VMEM((ATT_G, 1, W2), F32),
            pltpu.VMEM((ATT_G, 1, W2), F32),
            pltpu.VMEM((ATT_G, ATT_VR, W2), F32),
        ],
        compiler_params=_cparams(2),
        name="diffattn",
    )(lam, rel_bias, bkt, q, k, vt, subln_w)


def _hgrn_kernel(hq_ref, kk_ref, hi_ref, sog_ref, lf_ref, w_ref, y_ref,
                 state_ref, kbuf_ref, bbuf_ref, *, tc):
    C, H, W = HG_C, HG_HEADS, HEAD_W
    PAD = 8

    @pl.when(pl.program_id(1) == 0)
    def _():
        state_ref[...] = jnp.zeros_like(state_ref)

    kbuf_ref[:, 0:PAD, :] = jnp.zeros((H, PAD, W), F32)
    bbuf_ref[:, 0:PAD, :] = jnp.zeros((H, PAD, W), F32)

    row = lax.broadcasted_iota(jnp.int32, (C, W), 0)
    r2 = lax.broadcasted_iota(jnp.int32, (C, C), 0)
    c2 = lax.broadcasted_iota(jnp.int32, (C, C), 1)
    tri = (c2 <= r2).astype(BF16)
    is_q = {hs: ((row // hs) % 2) == 1 for hs in HG_LEVELS}
    sign = {hs: jnp.where(is_q[hs], 1.0, -1.0).astype(F32) for hs in HG_LEVELS}
    keep = {hs: ((r2 // (2 * hs)) == (c2 // (2 * hs)))
            & (((r2 // hs) % 2) == 1) & (((c2 // hs) % 2) == 0) for hs in HG_LEVELS}
    same_blk = (r2 // HG_DIAG) == (c2 // HG_DIAG)
    on_diag = {d: ((r2 - c2) == d) & same_blk for d in range(HG_DIAG)}

    def level_ref(b, hs):
        if hs >= PAD:
            parts = [jnp.broadcast_to(b[g * 2 * hs + hs - 1:g * 2 * hs + hs, :], (2 * hs, W))
                     for g in range(C // (2 * hs))]
            return parts[0] if len(parts) == 1 else jnp.concatenate(parts, axis=0)
        b3 = b.reshape(C // (2 * hs), 2 * hs, W)
        return jnp.broadcast_to(b3[:, hs - 1:hs, :], b3.shape).reshape(C, W)

    def chunk(c, carry):
        r0 = c * C
        lf = lf_ref[pl.ds(r0, C), :]
        p0 = lf.astype(BF16)
        p1 = (lf - p0.astype(F32)).astype(BF16)
        b_all = (jnp.dot(tri, p0, preferred_element_type=F32)
                 + jnp.dot(tri, p1, preferred_element_type=F32))

        heads = range(H)
        cols = [slice(h * W, (h + 1) * W) for h in heads]
        b = [b_all[:, cols[h]] for h in heads]
        qf = [hq_ref[pl.ds(r0, C), cols[h]].astype(F32) for h in heads]
        kf = [kk_ref[pl.ds(r0, C), cols[h]].astype(F32) for h in heads]
        iv = [hi_ref[pl.ds(r0, C), cols[h]] for h in heads]
        st = [state_ref[h] for h in heads]
        b_last = [b[h][C - 1:C, :] for h in heads]

        inter = [lax.dot_general((qf[h] * jnp.exp2(b[h])).astype(BF16), st[h].astype(BF16),
                                 (((1,), (1,)), ((), ())), preferred_element_type=F32)
                 for h in heads]

        a = [None] * H
        for hs in HG_LEVELS:
            for h in heads:
                e = jnp.exp2((b[h] - level_ref(b[h], hs)) * sign[hs])
                y = (jnp.where(is_q[hs], qf[h], kf[h]) * e).astype(BF16)
                sc = lax.dot_general(y, y, (((1,), (1,)), ((), ())),
                                     preferred_element_type=F32)
                sc = jnp.where(keep[hs], sc, 0.0)
                a[h] = sc if a[h] is None else a[h] + sc
        for h in heads:
            kbuf_ref[h, PAD:, :] = kf[h]
            bbuf_ref[h, PAD:, :] = b[h]
        for d in range(HG_DIAG):
            for h in heads:
                if d == 0:
                    term = qf[h] * kf[h]
                else:
                    ks = kbuf_ref[h, PAD - d:PAD - d + C, :]
                    bs = bbuf_ref[h, PAD - d:PAD - d + C, :]
                    term = qf[h] * ks * jnp.exp2(b[h] - bs)
                dsum = jnp.sum(term, axis=-1, keepdims=True)
                a[h] = a[h] + jnp.where(on_diag[d], dsum, 0.0)

        o = [inter[h] + jnp.dot(a[h].astype(BF16), iv[h], preferred_element_type=F32)
             for h in heads]

        for h in heads:
            k_end = (kf[h] * jnp.exp2(b_last[h] - b[h])).astype(BF16)
            upd = lax.dot_general(iv[h], k_end, (((0,), (0,)), ((), ())),
                                  preferred_element_type=F32)
            state_ref[h] = st[h] * jnp.exp2(b_last[h]) + upd

        for h in heads:
            ms = jnp.mean(o[h] * o[h], axis=-1, keepdims=True)
            y = o[h] * lax.rsqrt(ms + EPS) * w_ref[...]
            y = y * sog_ref[pl.ds(r0, C), cols[h]].astype(F32)
            y_ref[pl.ds(r0, C), cols[h]] = y.astype(BF16)
        return carry

    for c in range(tc // C):
        chunk(c, 0)


def _hgrn(hq, kk, hi, sog, logf, norm_w, tc=512):
    B, S, W = hq.shape
    blk = pl.BlockSpec((None, tc, W), lambda b, s: (b, s, 0))
    return pl.pallas_call(
        functools.partial(_hgrn_kernel, tc=tc),
        grid=(B, S // tc),
        in_specs=[blk, blk, blk, blk, blk, pl.BlockSpec((1, HEAD_W), lambda b, s: (0, 0))],
        out_specs=blk,
        out_shape=jax.ShapeDtypeStruct((B, S, W), BF16),
        scratch_shapes=[pltpu.VMEM((HG_HEADS, HEAD_W, HEAD_W), F32),
                        pltpu.VMEM((HG_HEADS, HG_C + 8, HEAD_W), F32),
                        pltpu.VMEM((HG_HEADS, HG_C + 8, HEAD_W), F32)],
        compiler_params=_cparams(2),
        name="hgrn2",
    )(hq, kk, hi, sog, logf, norm_w)


def _layer_norm(z, g, b):
    mu = jnp.mean(z, axis=-1, keepdims=True)
    zc = z - mu
    var = jnp.mean(zc * zc, axis=-1, keepdims=True)
    return zc * lax.rsqrt(var + EPS) * g + b


def _merge_kernel(ya_ref, yh_ref, x_ref, wa_ref, wh_ref, wg_ref, bg_ref, wo_ref,
                  g_ref, b_ref, o_ref, *, alpha):
    D = x_ref.shape[-1]
    tm = x_ref.shape[0]

    def branches(r):
        rows = slice(r * ROW_CHUNK, (r + 1) * ROW_CHUNK)
        xb = x_ref[rows, :].astype(BF16)
        pa = jnp.dot(ya_ref[rows, :], wa_ref[...], preferred_element_type=F32)
        ph = jnp.dot(yh_ref[rows, :], wh_ref[...], preferred_element_type=F32)
        ga = jnp.dot(xb, wg_ref[:, :D], preferred_element_type=F32)
        gh = jnp.dot(xb, wg_ref[:, D:], preferred_element_type=F32)
        return pa, ph, ga, gh

    n_chunks = tm // ROW_CHUNK
    nxt = branches(0)
    for r in range(n_chunks):
        rows = slice(r * ROW_CHUNK, (r + 1) * ROW_CHUNK)
        pa, ph, ga, gh = nxt
        if r + 1 < n_chunks:
            nxt = branches(r + 1)
        merged = (jax.nn.sigmoid(ga + bg_ref[:, :D]) * pa
                  + jax.nn.sigmoid(gh + bg_ref[:, D:]) * ph)
        mix = jnp.dot(merged.astype(BF16), wo_ref[...], preferred_element_type=F32)
        o_ref[rows, :] = _layer_norm(alpha * x_ref[rows, :] + mix, g_ref[...], b_ref[...])


def _merge(ya, yh, x, wa, wh, wg, bg, wo, g, b, alpha, tm=1024):
    N, D = x.shape
    row = lambda w: pl.BlockSpec((tm, w), lambda i: (i, 0))
    return pl.pallas_call(
        functools.partial(_merge_kernel, alpha=alpha),
        grid=(N // tm,),
        in_specs=[row(ya.shape[1]), row(yh.shape[1]), row(D),
                  _const_spec(wa.shape), _const_spec(wh.shape), _const_spec(wg.shape),
                  _const_spec(bg.shape), _const_spec(wo.shape),
                  _const_spec(g.shape), _const_spec(b.shape)],
        out_specs=row(D),
        out_shape=jax.ShapeDtypeStruct((N, D), F32),
        compiler_params=_cparams(1),
        name="merge_ln",
    )(ya, yh, x, wa, wh, wg, bg, wo, g, b)


def _ffn_kernel(h_ref, wg_ref, wu_ref, cw_ref, cb_ref, wd_ref, g_ref, b_ref, o_ref,
                tail_ref, gbuf_ref, *, alpha, tm):
    d_ff = wg_ref.shape[1]
    PAD = 8

    @pl.when(pl.program_id(1) == 0)
    def _():
        tail_ref[...] = jnp.zeros_like(tail_ref)

    RC = ROW_CHUNK
    nfc = d_ff // FFN_FC
    steps = [(r, fc) for r in range(tm // RC) for fc in range(nfc)]
    hb = {}

    def gate_up(r, fc):
        if r not in hb:
            hb[r] = h_ref[r * RC:(r + 1) * RC, :].astype(BF16)
        cols = slice(fc * FFN_FC, (fc + 1) * FFN_FC)
        return (jnp.dot(hb[r], wg_ref[:, cols], preferred_element_type=F32),
                jnp.dot(hb[r], wu_ref[:, cols], preferred_element_type=F32))

    nxt = gate_up(*steps[0])
    acc = None
    for t, (r, fc) in enumerate(steps):
        rows = slice(r * RC, (r + 1) * RC)
        cols = slice(fc * FFN_FC, (fc + 1) * FFN_FC)
        gte, up = nxt
        if t + 1 < len(steps):
            nxt = gate_up(*steps[t + 1])
        gbuf = gbuf_ref.at[t % 2]
        gbuf[0:PAD, :] = tail_ref[:, cols]
        gbuf[PAD:, :] = gte
        tail_ref[:, cols] = gte[RC - PAD:, :]
        conv = (gte * cw_ref[2:3, cols]
                + gbuf[PAD - 1:PAD - 1 + RC, :] * cw_ref[1:2, cols]
                + gbuf[PAD - 2:PAD - 2 + RC, :] * cw_ref[0:1, cols]
                + cb_ref[:, cols])
        act = jax.nn.gelu(conv) * up
        down = jnp.dot(act.astype(BF16), wd_ref[cols, :], preferred_element_type=F32)
        acc = down if fc == 0 else acc + down
        if fc == nfc - 1:
            o_ref[rows, :] = _layer_norm(alpha * h_ref[rows, :] + acc, g_ref[...], b_ref[...])


def _ffn(h, wg, wu, cw, cb, wd, g, b, alpha, tm=1024):
    B, S, D = h.shape
    d_ff = wg.shape[1]
    row = pl.BlockSpec((None, tm, D), lambda bi, s: (bi, s, 0))
    return pl.pallas_call(
        functools.partial(_ffn_kernel, alpha=alpha, tm=tm),
        grid=(B, S // tm),
        in_specs=[row, _const_spec(wg.shape), _const_spec(wu.shape), _const_spec(cw.shape),
                  _const_spec(cb.shape), _const_spec(wd.shape),
                  _const_spec(g.shape), _const_spec(b.shape)],
        out_specs=row,
        out_shape=jax.ShapeDtypeStruct((B, S, D), F32),
        scratch_shapes=[pltpu.VMEM((8, d_ff), F32),
                        pltpu.VMEM((2, ROW_CHUNK + 8, FFN_FC), F32)],
        compiler_params=_cparams(2),
        name="ffn_ln",
    )(h, wg, wu, cw, cb, wd, g, b)


def kernel(x, w_in, b_gate, lambda_q1, lambda_k1, lambda_q2, lambda_k2, attn_subln_w,
           rel_bias, hgrn_lb_logits, hgrn_norm_w, w_branch_attn, w_branch_hgrn, w_out,
           ln1_g, ln1_b, w_ffn_gate, w_ffn_up, ffn_conv_w, ffn_conv_b, w_ffn_down,
           ln2_g, ln2_b):
    B, S, D = x.shape
    depth = w_in.shape[0]
    att_qk = ATT_HEADS * 2 * ATT_DH
    att_w = ATT_HEADS * HEAD_W
    hg_w = HG_HEADS * HEAD_W
    offs = {"aq": 0, "ak": att_qk, "av": 2 * att_qk}
    offs["hq"] = offs["av"] + att_w
    offs["hf"] = offs["hq"] + hg_w
    offs["hi"] = offs["hf"] + hg_w
    offs["hg"] = offs["hi"] + hg_w
    offs["gate"] = offs["hg"] + hg_w
    alpha = (2 * depth) ** 0.25

    lb_all = jnp.cumsum(jax.nn.softmax(hgrn_lb_logits.astype(F32), axis=0), axis=0)

    h = x
    for l in range(depth):
        lam_init = 0.8 - 0.6 * math.exp(-0.3 * l)
        lam = (jnp.exp(jnp.sum(lambda_q1[l].astype(F32) * lambda_k1[l].astype(F32)))
               - jnp.exp(jnp.sum(lambda_q2[l].astype(F32) * lambda_k2[l].astype(F32)))
               + lam_init).reshape(1)

        w_mix_bf = w_in[l, :, :offs["gate"]].astype(BF16)
        w_gate_bf = w_in[l, :, offs["gate"]:].astype(BF16)
        (q, k, vt, hq, kk, hi, sog, logf) = _inproj(h, w_mix_bf, lb_all[l][None, :], offs)

        ya = _attention(q, k, vt, rel_bias.astype(F32),
                        attn_subln_w[l][:, None].astype(F32), lam, 1.0 - lam_init)
        yh = _hgrn(hq, kk, hi, sog, logf, hgrn_norm_w[l][None, :].astype(F32))

        h1 = _merge(ya.reshape(B * S, att_w), yh.reshape(B * S, hg_w), h.reshape(B * S, D),
                    w_branch_attn[l].astype(BF16), w_branch_hgrn[l].astype(BF16),
                    w_gate_bf, b_gate[l][None, :], w_out[l].astype(BF16),
                    ln1_g[l][None, :], ln1_b[l][None, :], alpha)

        h = _ffn(h1.reshape(B, S, D), w_ffn_gate[l].astype(BF16), w_ffn_up[l].astype(BF16),
                 ffn_conv_w[l], ffn_conv_b[l][None, :], w_ffn_down[l].astype(BF16),
                 ln2_g[l][None, :], ln2_b[l][None, :], alpha)
    return h
```

```python
import functools
import math

import numpy as np
import jax
import jax.numpy as jnp
from jax import lax
from jax.experimental import pallas as pl
from jax.experimental.pallas import tpu as pltpu

F32 = jnp.float32
BF16 = jnp.bfloat16

CHUNK = 64
ATT_HEADS = 4
ATT_DH = 64
HG_HEADS = 4
HEAD_W = 128
NUM_BUCKETS = 32
MAX_DISTANCE = 128
EPS = 1e-5
NEG = -1e30

ATT_BQ = 128
ATT_BK = 256
ATT_G = 4
ATT_GK = ATT_G * ATT_BQ // ATT_BK
ATT_NEAR = (0, -1, -2)
ATT_VR = HEAD_W + 16
LOG2E = math.log2(math.e)
HG_C = 128
HG_LEVELS = (64, 32, 16, 8, 4)
HG_DIAG = 4
FFN_FC = 256
ROW_CHUNK = 256

VMEM_LIMIT = 56 * 1024 * 1024


def _cparams(n_axes):
    return pltpu.CompilerParams(
        dimension_semantics=("arbitrary",) * n_axes,
        vmem_limit_bytes=VMEM_LIMIT)


def _const_spec(shape):
    nd = len(shape)
    return pl.BlockSpec(shape, lambda *_: (0,) * nd, pipeline_mode=pl.Buffered(1))


def _mix_kernel(x_ref, w_ref, lb_ref, nw_ref, q_ref, k_ref, vt_ref, yh_ref,
                state_ref, kbuf_ref, bbuf_ref, *, tm, offs):
    C, H, W, RC, PAD = HG_C, HG_HEADS, HEAD_W, ROW_CHUNK, 8

    @pl.when(pl.program_id(1) == 0)
    def _():
        state_ref[...] = jnp.zeros_like(state_ref)

    kbuf_ref[:, 0:PAD, :] = jnp.zeros((H, PAD, W), F32)
    bbuf_ref[:, 0:PAD, :] = jnp.zeros((H, PAD, W), F32)

    lb = lb_ref[...]
    row = lax.broadcasted_iota(jnp.int32, (C, W), 0)
    r2 = lax.broadcasted_iota(jnp.int32, (C, C), 0)
    c2 = lax.broadcasted_iota(jnp.int32, (C, C), 1)
    tri = (c2 <= r2).astype(BF16)
    is_q = {hs: ((row // hs) % 2) == 1 for hs in HG_LEVELS}
    sign = {hs: jnp.where(is_q[hs], 1.0, -1.0).astype(F32) for hs in HG_LEVELS}
    keep = {hs: ((r2 // (2 * hs)) == (c2 // (2 * hs)))
            & (((r2 // hs) % 2) == 1) & (((c2 // hs) % 2) == 0) for hs in HG_LEVELS}
    same_blk = (r2 // HG_DIAG) == (c2 // HG_DIAG)
    on_diag = {d: ((r2 - c2) == d) & same_blk for d in range(HG_DIAG)}
    heads = range(H)
    cols = [slice(h * W, (h + 1) * W) for h in heads]

    def level_ref(b, hs):
        if hs >= PAD:
            parts = [jnp.broadcast_to(b[g * 2 * hs + hs - 1:g * 2 * hs + hs, :], (2 * hs, W))
                     for g in range(C // (2 * hs))]
            return parts[0] if len(parts) == 1 else jnp.concatenate(parts, axis=0)
        b3 = b.reshape(C // (2 * hs), 2 * hs, W)
        return jnp.broadcast_to(b3[:, hs - 1:hs, :], b3.shape).reshape(C, W)

    def projection(r):
        rows = slice(r * RC, (r + 1) * RC)
        out = {}

        def mm(off):
            return jnp.dot(out["xb"], w_ref[:, off:off + 512], preferred_element_type=F32)

        def sec_q():
            out["xb"] = x_ref[rows, :].astype(BF16)
            q_ref[rows, :] = (mm(offs["aq"]) * (ATT_DH ** -0.5 * LOG2E)).astype(BF16)

        def sec_k():
            k_ref[rows, :] = mm(offs["ak"]).astype(BF16)

        def sec_v():
            v = mm(offs["av"])
            for hd in range(ATT_HEADS):
                vt_ref[r, hd * ATT_VR:hd * ATT_VR + HEAD_W, :] = (
                    v[:, hd * HEAD_W:(hd + 1) * HEAD_W].T.astype(BF16))
                vt_ref[r, hd * ATT_VR + HEAD_W:(hd + 1) * ATT_VR, :] = jnp.ones(
                    (ATT_VR - HEAD_W, ATT_BK), BF16)

        def sec_hq():
            out["hq"] = mm(offs["hq"])

        def sec_hf():
            sig = jax.nn.sigmoid(mm(offs["hf"]))
            out["lf"] = jnp.log2(lb + (1.0 - lb) * sig)
            out["kk"] = (1.0 - lb) * (1.0 - sig)

        def sec_hi():
            out["hi"] = mm(offs["hi"]).astype(BF16)

        def sec_hg():
            hog = mm(offs["hg"])
            out["sog"] = hog * jax.nn.sigmoid(hog)

        return [sec_q, sec_k, sec_v, sec_hq, sec_hf, sec_hi, sec_hg], out

    def recurrence(r, vals):
        pieces = []
        for sub in range(RC // C):
            sl = slice(sub * C, (sub + 1) * C)
            t0 = r * RC + sub * C
            st = {}

            def cumsum(sl=sl, st=st):
                lf = vals["lf"][sl]
                p0 = lf.astype(BF16)
                p1 = (lf - p0.astype(F32)).astype(BF16)
                st["b_all"] = (jnp.dot(tri, p0, preferred_element_type=F32)
                               + jnp.dot(tri, p1, preferred_element_type=F32))

            def inter(sl=sl, st=st):
                st["b"] = [st["b_all"][:, cols[h]] for h in heads]
                st["qf"] = [vals["hq"][sl, cols[h]] for h in heads]
                st["kf"] = [vals["kk"][sl, cols[h]] for h in heads]
                st["iv"] = [vals["hi"][sl, cols[h]] for h in heads]
                st["st"] = [state_ref[h] for h in heads]
                st["inter"] = [
                    lax.dot_general((st["qf"][h] * jnp.exp2(st["b"][h])).astype(BF16),
                                    st["st"][h].astype(BF16), (((1,), (1,)), ((), ())),
                                    preferred_element_type=F32) for h in heads]
                st["a"] = [None] * H

            def level(hs, st=st):
                for h in heads:
                    b = st["b"][h]
                    e = jnp.exp2((b - level_ref(b, hs)) * sign[hs])
                    y = (jnp.where(is_q[hs], st["qf"][h], st["kf"][h]) * e).astype(BF16)
                    sc = lax.dot_general(y, y, (((1,), (1,)), ((), ())),
                                         preferred_element_type=F32)
                    sc = jnp.where(keep[hs], sc, 0.0)
                    st["a"][h] = sc if st["a"][h] is None else st["a"][h] + sc

            def diagonals(st=st):
                for h in heads:
                    kbuf_ref[h, PAD:, :] = st["kf"][h]
                    bbuf_ref[h, PAD:, :] = st["b"][h]
                for d in range(HG_DIAG):
                    for h in heads:
                        if d == 0:
                            term = st["qf"][h] * st["kf"][h]
                        else:
                            ks = kbuf_ref[h, PAD - d:PAD - d + C, :]
                            bs = bbuf_ref[h, PAD - d:PAD - d + C, :]
                            term = st["qf"][h] * ks * jnp.exp2(st["b"][h] - bs)
                        dsum = jnp.sum(term, axis=-1, keepdims=True)
                        st["a"][h] = st["a"][h] + jnp.where(on_diag[d], dsum, 0.0)

            def output(st=st):
                st["o"] = [st["inter"][h] + jnp.dot(st["a"][h].astype(BF16), st["iv"][h],
                                                    preferred_element_type=F32)
                           for h in heads]

            def update(st=st):
                for h in heads:
                    b_last = st["b"][h][C - 1:C, :]
                    k_end = (st["kf"][h] * jnp.exp2(b_last - st["b"][h])).astype(BF16)
                    upd = lax.dot_general(st["iv"][h], k_end, (((0,), (0,)), ((), ())),
                                          preferred_element_type=F32)
                    state_ref[h] = st["st"][h] * jnp.exp2(b_last) + upd

            def emit(sl=sl, st=st, t0=t0):
                for h in heads:
                    o = st["o"][h]
                    ms = jnp.mean(o * o, axis=-1, keepdims=True)
                    y = o * lax.rsqrt(ms + EPS) * nw_ref[...] * vals["sog"][sl, cols[h]]
                    yh_ref[t0:t0 + C, cols[h]] = y.astype(BF16)

            pieces += [cumsum, inter]
            pieces += [functools.partial(level, hs) for hs in HG_LEVELS]
            pieces += [diagonals, output, update, emit]
        return pieces

    n_chunks = tm // RC
    pending = []
    for r in range(n_chunks + 1):
        proj, vals = projection(r) if r < n_chunks else ([], None)
        per = -(-len(pending) // max(len(proj), 1))
        for i in range(max(len(proj), 1)):
            if i < len(proj):
                proj[i]()
            for piece in pending[i * per:(i + 1) * per]:
                piece()
        pending = recurrence(r, vals) if r < n_chunks else []


def _mix(x, w_bf, lb, norm_w, offs, tm=1024):
    B, S, D = x.shape
    assert ROW_CHUNK == ATT_BK and ROW_CHUNK % HG_C == 0
    row = lambda w: pl.BlockSpec((None, tm, w), lambda b, s: (b, s, 0))
    out_shape = (
        jax.ShapeDtypeStruct((B, S, 512), BF16),
        jax.ShapeDtypeStruct((B, S, 512), BF16),
        jax.ShapeDtypeStruct((B, S // ATT_BK, ATT_HEADS * ATT_VR, ATT_BK), BF16),
        jax.ShapeDtypeStruct((B, S, HG_HEADS * HEAD_W), BF16),
    )
    out_specs = (
        row(512), row(512),
        pl.BlockSpec((None, tm // ATT_BK, ATT_HEADS * ATT_VR, ATT_BK),
                     lambda b, s: (b, s, 0, 0)),
        row(HG_HEADS * HEAD_W),
    )
    return pl.pallas_call(
        functools.partial(_mix_kernel, tm=tm, offs=offs),
        grid=(B, S // tm),
        in_specs=[row(D), _const_spec(w_bf.shape), _const_spec(lb.shape),
                  _const_spec(norm_w.shape)],
        out_specs=out_specs,
        out_shape=out_shape,
        scratch_shapes=[pltpu.VMEM((HG_HEADS, HEAD_W, HEAD_W), F32),
                        pltpu.VMEM((HG_HEADS, HG_C + 8, HEAD_W), F32),
                        pltpu.VMEM((HG_HEADS, HG_C + 8, HEAD_W), F32)],
        compiler_params=_cparams(2),
        name="inproj_hgrn2",
    )(x, w_bf, lb, norm_w)


def _rel_bucket_np(rel):
    nb = NUM_BUCKETS // 2
    max_exact = nb // 2
    ret = np.where(rel > 0, nb, 0)
    n = np.abs(rel)
    nf = np.maximum(n, 1).astype(np.float32)
    large = max_exact + (np.log(nf / np.float32(max_exact))
                         / np.float32(math.log(MAX_DISTANCE / max_exact))
                         * np.float32(nb - max_exact)).astype(np.int32)
    large = np.minimum(large, nb - 1)
    return ret + np.where(n < max_exact, n, large)


def _attn_bucket_tiles():
    i = np.arange(ATT_BQ)[None, :]
    j = np.arange(ATT_BK)[:, None]
    tiles = []
    for d in ATT_NEAR:
        kpos = d * ATT_BQ + j
        live = (kpos // CHUNK) <= (i // CHUNK)
        tiles.append(np.where(live, _rel_bucket_np(kpos - i), -1))
    far = _rel_bucket_np(np.asarray((min(ATT_NEAR) - 1) * ATT_BQ + ATT_BK - 1))
    assert int(far) == NUM_BUCKETS // 2 - 1
    return np.stack(tiles).astype(np.int32)


def _attn_kernel(lam_ref, rb_ref, bkt_ref, q_ref, k_ref, vt_ref, w_ref, o_ref,
                 bias_ref, qs_ref, s0_ref, s1_ref, p0_ref, p1_ref, a0_ref, a1_ref,
                 m_ref, acc_ref, *, out_scale):
    S = q_ref.shape[0]
    nkb = S // ATT_BK
    W2 = 2 * ATT_BQ
    near = tuple(range(ATT_GK)) + (-1,)
    lam = lam_ref[0]
    h = pl.program_id(0)

    @pl.when(pl.program_id(1) == 0)
    def _():
        far = rb_ref[NUM_BUCKETS // 2 - 1, h]
        for n, kbrel in enumerate(near):
            for j in range(ATT_G):
                d = (kbrel * ATT_BK) // ATT_BQ - j
                if d > max(ATT_NEAR):
                    bias_ref[n, j] = jnp.full((ATT_BK, W2), NEG, F32)
                elif d < min(ATT_NEAR):
                    bias_ref[n, j] = jnp.zeros((ATT_BK, W2), F32)
        for t, dt in enumerate(ATT_NEAR):
            bk = bkt_ref[t]
            tile = jnp.full(bk.shape, NEG, F32)
            for bucket in range(NUM_BUCKETS):
                tile = jnp.where(bk == bucket, (rb_ref[bucket, h] - far) * LOG2E, tile)
            tile2 = jnp.concatenate([tile, tile], axis=1)
            for n, kbrel in enumerate(near):
                for j in range(ATT_G):
                    if (kbrel * ATT_BK) // ATT_BQ - j == dt:
                        bias_ref[n, j] = tile2

    lane = lax.broadcasted_iota(jnp.int32, (ATT_BQ, HEAD_W), 1)
    first_map = lane < ATT_DH

    s_bufs, p_bufs, a_bufs = (s0_ref, s1_ref), (p0_ref, p1_ref), (a0_ref, a1_ref)

    def scores(step, j, buf):
        kb, bias_idx = step
        kblk = k_ref[pl.ds(pl.multiple_of(kb * ATT_BK, ATT_BK), ATT_BK), :]
        s = lax.dot_general(kblk, qs_ref[j], (((1,), (1,)), ((), ())),
                            preferred_element_type=F32)
        if bias_idx is not None:
            s = s + bias_ref[bias_idx, j]
        s_bufs[buf][j] = s

    def soft(j, buf, first=False):
        s = s_bufs[buf][j]
        m_new = jnp.max(s, axis=0, keepdims=True)
        if first:
            a_bufs[buf][j] = jnp.ones_like(m_new)
        else:
            m_old = m_ref[j]
            m_new = jnp.maximum(m_old, m_new)
            a_bufs[buf][j] = jnp.exp2(m_old - m_new)
        m_ref[j] = m_new
        p_bufs[buf][j] = jnp.exp2(s - m_new).astype(BF16)

    def accum(kb, j, buf, first=False):
        pv = jnp.dot(vt_ref[kb], p_bufs[buf][j], preferred_element_type=F32)
        if first:
            acc_ref[j] = pv
        else:
            acc_ref[j] = a_bufs[buf][j] * acc_ref[j] + pv

    AHEAD = 4

    def two_steps(steps, first=False, prev_kb=None, dead=()):
        for k in range(2 * ATT_G):
            ua = k + AHEAD
            if (ua // ATT_G, ua % ATT_G) not in dead:
                scores(steps[ua // ATT_G], ua % ATT_G, (ua // ATT_G) % 2)
            uc = k - 1
            if k >= 1 and (uc // ATT_G, uc % ATT_G) not in dead:
                accum(steps[uc // ATT_G][0], uc % ATT_G, uc // ATT_G,
                      first=first and uc < ATT_G)
            elif k == 0 and prev_kb is not None:
                accum(prev_kb, ATT_G - 1, 1)
            if (k // ATT_G, k % ATT_G) not in dead:
                soft(k % ATT_G, k // ATT_G, first=first and k < ATT_G)

    def inside(g):
        return [(g * ATT_GK + i, i) for i in range(ATT_GK)]

    def before(g):
        return (jnp.maximum(g * ATT_GK - 1, 0), ATT_GK)

    def far(n):
        return (jnp.clip(n - (ATT_GK + 1), 0, nkb - 1), None)

    dead = {(i, j) for i in range(ATT_GK) for j in range(ATT_G)
            if (i * ATT_BK) // ATT_BQ - j > max(ATT_NEAR)}

    def open_group(g):
        q0 = pl.multiple_of(g * (ATT_G * ATT_BQ), ATT_G * ATT_BQ)
        for j in range(ATT_G):
            q = q_ref[pl.ds(q0 + j * ATT_BQ, ATT_BQ), :]
            zero = jnp.zeros_like(q)
            qs_ref[j, :ATT_BQ, :] = jnp.where(first_map, q, zero)
            qs_ref[j, ATT_BQ:, :] = jnp.where(first_map, zero, q)
        for j in range(AHEAD):
            scores(inside(g)[0], j, 0)

    def close_group(g, n_steps):
        last_kb = jnp.where(g > 0, far(n_steps - 1)[0], inside(g)[1][0])
        accum(last_kb, ATT_G - 1, 1)

    def emit_group(g):
        q0 = pl.multiple_of(g * (ATT_G * ATT_BQ), ATT_G * ATT_BQ)
        for j in range(ATT_G):
            on = acc_ref[j, :HEAD_W, :] * (1.0 / acc_ref[j, HEAD_W:HEAD_W + 1, :])
            o = on[:, :ATT_BQ] - lam * on[:, ATT_BQ:]
            ms = jnp.mean(o * o, axis=0, keepdims=True)
            y = o * lax.rsqrt(ms + EPS) * w_ref[...] * out_scale
            o_ref[pl.ds(q0 + j * ATT_BQ, ATT_BQ), :] = y.T.astype(BF16)

    n_groups = S // (ATT_G * ATT_BQ)
    open_group(0)
    two_steps(inside(0) + [before(0)], first=True, dead=dead)

    def group(g, carry):
        n_steps = jnp.where(g > 0, ATT_GK + 1 + (g * ATT_GK - 1), ATT_GK)

        def far_steps(n):
            two_steps([far(n), far(n + 1), far(n + 2)], prev_kb=far(n - 1)[0])

        n_far_pairs = jnp.maximum(n_steps // 2 - 2, 0)
        first_far = ATT_GK + 2

        def four_far_steps(i, c):
            far_steps(first_far + 4 * i)
            far_steps(first_far + 4 * i + 2)
            return c

        lax.fori_loop(0, n_far_pairs // 2, four_far_steps, 0)

        @pl.when(n_far_pairs % 2 == 1)
        def _():
            far_steps(first_far + 4 * (n_far_pairs // 2))

        @pl.when(g + 1 < n_groups)
        def _():
            close_group(g, n_steps)
            open_group(g + 1)
            emit_group(g)
            two_steps(inside(g + 1) + [before(g + 1)], first=True, dead=dead)
            two_steps([before(g + 1), far(3), far(4)], prev_kb=inside(g + 1)[1][0])

        @pl.when(g + 1 == n_groups)
        def _():
            close_group(g, n_steps)
            emit_group(g)

        return carry

    lax.fori_loop(0, n_groups, group, 0)


def _attention(q, k, vt, rel_bias, subln_w, lam, out_scale):
    B, S, _ = q.shape
    nkb = S // ATT_BK
    W2 = 2 * ATT_BQ
    assert ATT_GK == 2 and ATT_BK == 2 * ATT_BQ
    bkt = jnp.asarray(_attn_bucket_tiles())
    head = pl.BlockSpec((None, S, HEAD_W), lambda h, b: (b, 0, h))
    smem = pl.BlockSpec(memory_space=pltpu.SMEM)
    return pl.pallas_call(
        functools.partial(_attn_kernel, out_scale=out_scale),
        grid=(ATT_HEADS, B),
        in_specs=[
            smem, smem, _const_spec(bkt.shape),
            head, head,
            pl.BlockSpec((None, nkb, ATT_VR, ATT_BK), lambda h, b: (b, 0, h, 0)),
            pl.BlockSpec((HEAD_W, 1), lambda h, b: (0, 0)),
        ],
        out_specs=head,
        out_shape=jax.ShapeDtypeStruct((B, S, ATT_HEADS * HEAD_W), BF16),
        scratch_shapes=[
            pltpu.VMEM((ATT_GK + 1, ATT_G, ATT_BK, W2), F32),
            pltpu.VMEM((ATT_G, W2, HEAD_W), BF16),
            pltpu.VMEM((ATT_G, ATT_BK, W2), F32),
            pltpu.VMEM((ATT_G, ATT_BK, W2), F32),
            pltpu.VMEM((ATT_G, ATT_BK, W2), BF16),
            pltpu.VMEM((ATT_G, ATT_BK, W2), BF16),
            pltpu.VMEM((ATT_G, 1, W2), F32),
            pltpu.VMEM((ATT_G, 1, W2), F32),
            pltpu.VMEM((ATT_G, 1, W2), F32),
            pltpu.VMEM((ATT_G, ATT_VR, W2), F32),
        ],
        compiler_params=_cparams(2),
        name="diffattn",
    )(lam, rel_bias, bkt, q, k, vt, subln_w)


def _layer_norm(z, g, b):
    mu = jnp.mean(z, axis=-1, keepdims=True)
    zc = z - mu
    var = jnp.mean(zc * zc, axis=-1, keepdims=True)
    return zc * lax.rsqrt(var + EPS) * g + b


def _merge_kernel(ya_ref, yh_ref, x_ref, wa_ref, wh_ref, wg_ref, bg_ref, wo_ref,
                  g_ref, b_ref, o_ref, *, alpha):
    D = x_ref.shape[-1]
    tm = x_ref.shape[0]

    def branches(r):
        rows = slice(r * ROW_CHUNK, (r + 1) * ROW_CHUNK)
        xb = x_ref[rows, :].astype(BF16)
        pa = jnp.dot(ya_ref[rows, :], wa_ref[...], preferred_element_type=F32)
        ph = jnp.dot(yh_ref[rows, :], wh_ref[...], preferred_element_type=F32)
        ga = jnp.dot(xb, wg_ref[:, :D], preferred_element_type=F32)
        gh = jnp.dot(xb, wg_ref[:, D:], preferred_element_type=F32)
        return pa, ph, ga, gh

    n_chunks = tm // ROW_CHUNK
    nxt = branches(0)
    for r in range(n_chunks):
        rows = slice(r * ROW_CHUNK, (r + 1) * ROW_CHUNK)
        pa, ph, ga, gh = nxt
        if r + 1 < n_chunks:
            nxt = branches(r + 1)
        merged = (jax.nn.sigmoid(ga + bg_ref[:, :D]) * pa
                  + jax.nn.sigmoid(gh + bg_ref[:, D:]) * ph)
        mix = jnp.dot(merged.astype(BF16), wo_ref[...], preferred_element_type=F32)
        o_ref[rows, :] = _layer_norm(alpha * x_ref[rows, :] + mix, g_ref[...], b_ref[...])


def _merge(ya, yh, x, wa, wh, wg, bg, wo, g, b, alpha, tm=1024):
    N, D = x.shape
    row = lambda w: pl.BlockSpec((tm, w), lambda i: (i, 0))
    return pl.pallas_call(
        functools.partial(_merge_kernel, alpha=alpha),
        grid=(N // tm,),
        in_specs=[row(ya.shape[1]), row(yh.shape[1]), row(D),
                  _const_spec(wa.shape), _const_spec(wh.shape), _const_spec(wg.shape),
                  _const_spec(bg.shape), _const_spec(wo.shape),
                  _const_spec(g.shape), _const_spec(b.shape)],
        out_specs=row(D),
        out_shape=jax.ShapeDtypeStruct((N, D), F32),
        compiler_params=_cparams(1),
        name="merge_ln",
    )(ya, yh, x, wa, wh, wg, bg, wo, g, b)


def _ffn_kernel(h_ref, wg_ref, wu_ref, cw_ref, cb_ref, wd_ref, g_ref, b_ref, o_ref,
                tail_ref, gbuf_ref, *, alpha, tm):
    d_ff = wg_ref.shape[1]
    PAD = 8

    @pl.when(pl.program_id(1) == 0)
    def _():
        tail_ref[...] = jnp.zeros_like(tail_ref)

    RC = ROW_CHUNK
    nfc = d_ff // FFN_FC
    steps = [(r, fc) for r in range(tm // RC) for fc in range(nfc)]
    hb = {}

    def gate_up(r, fc):
        if r not in hb:
            hb[r] = h_ref[r * RC:(r + 1) * RC, :].astype(BF16)
        cols = slice(fc * FFN_FC, (fc + 1) * FFN_FC)
        return (jnp.dot(hb[r], wg_ref[:, cols], preferred_element_type=F32),
                jnp.dot(hb[r], wu_ref[:, cols], preferred_element_type=F32))

    nxt = gate_up(*steps[0])
    acc = None
    for t, (r, fc) in enumerate(steps):
        rows = slice(r * RC, (r + 1) * RC)
        cols = slice(fc * FFN_FC, (fc + 1) * FFN_FC)
        gte, up = nxt
        if t + 1 < len(steps):
            nxt = gate_up(*steps[t + 1])
        gbuf = gbuf_ref.at[t % 2]
        gbuf[0:PAD, :] = tail_ref[:, cols]
        gbuf[PAD:, :] = gte
        tail_ref[:, cols] = gte[RC - PAD:, :]
        conv = (gte * cw_ref[2:3, cols]
                + gbuf[PAD - 1:PAD - 1 + RC, :] * cw_ref[1:2, cols]
                + gbuf[PAD - 2:PAD - 2 + RC, :] * cw_ref[0:1, cols]
                + cb_ref[:, cols])
        act = jax.nn.gelu(conv) * up
        down = jnp.dot(act.astype(BF16), wd_ref[cols, :], preferred_element_type=F32)
        acc = down if fc == 0 else acc + down
        if fc == nfc - 1:
            o_ref[rows, :] = _layer_norm(alpha * h_ref[rows, :] + acc, g_ref[...], b_ref[...])


def _ffn(h, wg, wu, cw, cb, wd, g, b, alpha, tm=1024):
    B, S, D = h.shape
    d_ff = wg.shape[1]
    row = pl.BlockSpec((None, tm, D), lambda bi, s: (bi, s, 0))
    return pl.pallas_call(
        functools.partial(_ffn_kernel, alpha=alpha, tm=tm),
        grid=(B, S // tm),
        in_specs=[row, _const_spec(wg.shape), _const_spec(wu.shape), _const_spec(cw.shape),
                  _const_spec(cb.shape), _const_spec(wd.shape),
                  _const_spec(g.shape), _const_spec(b.shape)],
        out_specs=row,
        out_shape=jax.ShapeDtypeStruct((B, S, D), F32),
        scratch_shapes=[pltpu.VMEM((8, d_ff), F32),
                        pltpu.VMEM((2, ROW_CHUNK + 8, FFN_FC), F32)],
        compiler_params=_cparams(2),
        name="ffn_ln",
    )(h, wg, wu, cw, cb, wd, g, b)


def kernel(x, w_in, b_gate, lambda_q1, lambda_k1, lambda_q2, lambda_k2, attn_subln_w,
           rel_bias, hgrn_lb_logits, hgrn_norm_w, w_branch_attn, w_branch_hgrn, w_out,
           ln1_g, ln1_b, w_ffn_gate, w_ffn_up, ffn_conv_w, ffn_conv_b, w_ffn_down,
           ln2_g, ln2_b):
    B, S, D = x.shape
    depth = w_in.shape[0]
    att_qk = ATT_HEADS * 2 * ATT_DH
    att_w = ATT_HEADS * HEAD_W
    hg_w = HG_HEADS * HEAD_W
    offs = {"aq": 0, "ak": att_qk, "av": 2 * att_qk}
    offs["hq"] = offs["av"] + att_w
    offs["hf"] = offs["hq"] + hg_w
    offs["hi"] = offs["hf"] + hg_w
    offs["hg"] = offs["hi"] + hg_w
    offs["gate"] = offs["hg"] + hg_w
    alpha = (2 * depth) ** 0.25

    lb_all = jnp.cumsum(jax.nn.softmax(hgrn_lb_logits.astype(F32), axis=0), axis=0)

    h = x
    for l in range(depth):
        lam_init = 0.8 - 0.6 * math.exp(-0.3 * l)
        lam = (jnp.exp(jnp.sum(lambda_q1[l].astype(F32) * lambda_k1[l].astype(F32)))
               - jnp.exp(jnp.sum(lambda_q2[l].astype(F32) * lambda_k2[l].astype(F32)))
               + lam_init).reshape(1)

        w_mix_bf = w_in[l, :, :offs["gate"]].astype(BF16)
        w_gate_bf = w_in[l, :, offs["gate"]:].astype(BF16)
        q, k, vt, yh = _mix(h, w_mix_bf, lb_all[l][None, :],
                            hgrn_norm_w[l][None, :].astype(F32), offs)

        ya = _attention(q, k, vt, rel_bias.astype(F32),
                        attn_subln_w[l][:, None].astype(F32), lam, 1.0 - lam_init)

        h1 = _merge(ya.reshape(B * S, att_w), yh.reshape(B * S, hg_w), h.reshape(B * S, D),
                    w_branch_attn[l].astype(BF16), w_branch_hgrn[l].astype(BF16),
                    w_gate_bf, b_gate[l][None, :], w_out[l].astype(BF16),
                    ln1_g[l][None, :], ln1_b[l][None, :], alpha)

        h = _ffn(h1.reshape(B, S, D), w_ffn_gate[l].astype(BF16), w_ffn_up[l].astype(BF16),
                 ffn_conv_w[l], ffn_conv_b[l][None, :], w_ffn_down[l].astype(BF16),
                 ln2_g[l][None, :], ln2_b[l][None, :], alpha)
    return h
```

```python
import functools
import math

import numpy as np
import jax
import jax.numpy as jnp
from jax import lax
from jax.experimental import pallas as pl
from jax.experimental.pallas import tpu as pltpu

F32 = jnp.float32
BF16 = jnp.bfloat16

CHUNK = 64
ATT_HEADS = 4
ATT_DH = 64
HG_HEADS = 4
HEAD_W = 128
NUM_BUCKETS = 32
MAX_DISTANCE = 128
EPS = 1e-5
NEG = -1e30

ATT_BQ = 128
ATT_BK = 256
ATT_G = 4
ATT_GK = ATT_G * ATT_BQ // ATT_BK
ATT_NEAR = (0, -1, -2)
ATT_VR = HEAD_W + 16
LOG2E = math.log2(math.e)
HG_C = 128
HG_LEVELS = (64, 32, 16, 8, 4)
HG_DIAG = 4
FFN_FC = 256
ROW_CHUNK = 256

VMEM_LIMIT = 56 * 1024 * 1024


def _cparams(n_axes):
    return pltpu.CompilerParams(
        dimension_semantics=("arbitrary",) * n_axes,
        vmem_limit_bytes=VMEM_LIMIT)


def _const_spec(shape):
    nd = len(shape)
    return pl.BlockSpec(shape, lambda *_: (0,) * nd, pipeline_mode=pl.Buffered(1))


def _mix_kernel(x_ref, w_ref, lb_ref, nw_ref, q_ref, k_ref, vt_ref, yh_ref,
                state_ref, kbuf_ref, bbuf_ref, *, tm, offs):
    C, H, W, RC, PAD = HG_C, HG_HEADS, HEAD_W, ROW_CHUNK, 8

    @pl.when(pl.program_id(1) == 0)
    def _():
        state_ref[...] = jnp.zeros_like(state_ref)

    kbuf_ref[:, 0:PAD, :] = jnp.zeros((H, PAD, W), F32)
    bbuf_ref[:, 0:PAD, :] = jnp.zeros((H, PAD, W), F32)

    lb = lb_ref[...]
    row = lax.broadcasted_iota(jnp.int32, (C, W), 0)
    r2 = lax.broadcasted_iota(jnp.int32, (C, C), 0)
    c2 = lax.broadcasted_iota(jnp.int32, (C, C), 1)
    tri = (c2 <= r2).astype(BF16)
    is_q = {hs: ((row // hs) % 2) == 1 for hs in HG_LEVELS}
    sign = {hs: jnp.where(is_q[hs], 1.0, -1.0).astype(F32) for hs in HG_LEVELS}
    keep = {hs: ((r2 // (2 * hs)) == (c2 // (2 * hs)))
            & (((r2 // hs) % 2) == 1) & (((c2 // hs) % 2) == 0) for hs in HG_LEVELS}
    same_blk = (r2 // HG_DIAG) == (c2 // HG_DIAG)
    on_diag = {d: ((r2 - c2) == d) & same_blk for d in range(HG_DIAG)}
    heads = range(H)
    cols = [slice(h * W, (h + 1) * W) for h in heads]

    def level_ref(b, hs):
        if hs >= PAD:
            parts = [jnp.broadcast_to(b[g * 2 * hs + hs - 1:g * 2 * hs + hs, :], (2 * hs, W))
                     for g in range(C // (2 * hs))]
            return parts[0] if len(parts) == 1 else jnp.concatenate(parts, axis=0)
        b3 = b.reshape(C // (2 * hs), 2 * hs, W)
        return jnp.broadcast_to(b3[:, hs - 1:hs, :], b3.shape).reshape(C, W)

    def projection(r):
        rows = slice(r * RC, (r + 1) * RC)
        out = {}

        def mm(off):
            return jnp.dot(out["xb"], w_ref[:, off:off + 512], preferred_element_type=F32)

        def sec_q():
            out["xb"] = x_ref[rows, :].astype(BF16)
            q_ref[rows, :] = (mm(offs["aq"]) * (ATT_DH ** -0.5 * LOG2E)).astype(BF16)

        def sec_k():
            k_ref[rows, :] = mm(offs["ak"]).astype(BF16)

        def sec_v():
            v = mm(offs["av"])
            for hd in range(ATT_HEADS):
                vt_ref[r, hd * ATT_VR:hd * ATT_VR + HEAD_W, :] = (
                    v[:, hd * HEAD_W:(hd + 1) * HEAD_W].T.astype(BF16))
                vt_ref[r, hd * ATT_VR + HEAD_W:(hd + 1) * ATT_VR, :] = jnp.ones(
                    (ATT_VR - HEAD_W, ATT_BK), BF16)

        def sec_hq():
            out["hq"] = mm(offs["hq"])

        def sec_hf():
            sig = jax.nn.sigmoid(mm(offs["hf"]))
            out["lf"] = jnp.log2(lb + (1.0 - lb) * sig)
            out["kk"] = (1.0 - lb) * (1.0 - sig)

        def sec_hi():
            out["hi"] = mm(offs["hi"]).astype(BF16)

        def sec_hg():
            hog = mm(offs["hg"])
            out["sog"] = hog * jax.nn.sigmoid(hog)

        return [sec_q, sec_k, sec_v, sec_hq, sec_hf, sec_hi, sec_hg], out

    def recurrence(r, vals):
        pieces = []
        for sub in range(RC // C):
            sl = slice(sub * C, (sub + 1) * C)
            t0 = r * RC + sub * C
            st = {}
            s0 = 3 * sub

            def cumsum_prep(sl=sl, st=st):
                lf = vals["lf"][sl]
                st["p0"] = lf.astype(BF16)
                st["p1"] = (lf - st["p0"].astype(F32)).astype(BF16)

            def cumsum_fire(st=st):
                b_all = (jnp.dot(tri, st["p0"], preferred_element_type=F32)
                         + jnp.dot(tri, st["p1"], preferred_element_type=F32))
                st["b"] = [b_all[:, cols[h]] for h in heads]

            def inter_prep(sl=sl, st=st):
                st["qf"] = [vals["hq"][sl, cols[h]] for h in heads]
                st["kf"] = [vals["kk"][sl, cols[h]] for h in heads]
                st["iv"] = [vals["hi"][sl, cols[h]] for h in heads]
                st["st"] = [state_ref[h] for h in heads]
                st["qt"] = [(st["qf"][h] * jnp.exp2(st["b"][h])).astype(BF16) for h in heads]
                st["stb"] = [st["st"][h].astype(BF16) for h in heads]
                st["a"] = [None] * H

            def inter_fire(st=st):
                st["inter"] = [lax.dot_general(st["qt"][h], st["stb"][h],
                                               (((1,), (1,)), ((), ())),
                                               preferred_element_type=F32) for h in heads]

            def level_prep(hs, st=st):
                ys = []
                for h in heads:
                    b = st["b"][h]
                    e = jnp.exp2((b - level_ref(b, hs)) * sign[hs])
                    ys.append((jnp.where(is_q[hs], st["qf"][h], st["kf"][h]) * e).astype(BF16))
                st["y", hs] = ys

            def level_fire(hs, st=st):
                for h in heads:
                    y = st["y", hs][h]
                    sc = lax.dot_general(y, y, (((1,), (1,)), ((), ())),
                                         preferred_element_type=F32)
                    sc = jnp.where(keep[hs], sc, 0.0)
                    st["a"][h] = sc if st["a"][h] is None else st["a"][h] + sc

            def diagonals(st=st):
                for h in heads:
                    kbuf_ref[h, PAD:, :] = st["kf"][h]
                    bbuf_ref[h, PAD:, :] = st["b"][h]
                diag = [None] * H
                for d in range(HG_DIAG):
                    for h in heads:
                        if d == 0:
                            term = st["qf"][h] * st["kf"][h]
                        else:
                            ks = kbuf_ref[h, PAD - d:PAD - d + C, :]
                            bs = bbuf_ref[h, PAD - d:PAD - d + C, :]
                            term = st["qf"][h] * ks * jnp.exp2(st["b"][h] - bs)
                        dsum = jnp.sum(term, axis=-1, keepdims=True)
                        placed = jnp.where(on_diag[d], dsum, 0.0)
                        diag[h] = placed if diag[h] is None else diag[h] + placed
                st["diag"] = diag

            def output_prep(st=st):
                st["ab"] = [(st["a"][h] + st["diag"][h]).astype(BF16) for h in heads]

            def output_fire(st=st):
                st["o"] = [st["inter"][h] + jnp.dot(st["ab"][h], st["iv"][h],
                                                    preferred_element_type=F32)
                           for h in heads]

            def update_prep(st=st):
                st["b_last"] = [st["b"][h][C - 1:C, :] for h in heads]
                st["k_end"] = [(st["kf"][h] * jnp.exp2(st["b_last"][h] - st["b"][h])
                                ).astype(BF16) for h in heads]

            def update_fire(st=st):
                for h in heads:
                    upd = lax.dot_general(st["iv"][h], st["k_end"][h],
                                          (((0,), (0,)), ((), ())),
                                          preferred_element_type=F32)
                    state_ref[h] = st["st"][h] * jnp.exp2(st["b_last"][h]) + upd

            def emit(sl=sl, st=st, t0=t0):
                for h in heads:
                    o = st["o"][h]
                    ms = jnp.mean(o * o, axis=-1, keepdims=True)
                    y = o * lax.rsqrt(ms + EPS) * nw_ref[...] * vals["sog"][sl, cols[h]]
                    yh_ref[t0:t0 + C, cols[h]] = y.astype(BF16)

            levels = [(functools.partial(level_prep, hs), functools.partial(level_fire, hs))
                      for hs in HG_LEVELS]
            pieces.append((0, cumsum_prep, cumsum_fire))
            pieces.append((s0 + 1, inter_prep, inter_fire))
            pieces += [(s0 + 1, p, f) for p, f in levels[:2]]
            pieces += [(s0 + 2, p, f) for p, f in levels[2:]]
            pieces.append((s0 + 3, update_prep, update_fire))
            pieces.append((s0 + 3, diagonals, None))
            pieces.append((s0 + 4, output_prep, output_fire))
            pieces.append((s0 + 5, emit, None))
        return pieces

    n_chunks = tm // RC
    sections = []
    schedule = {}
    for r in range(n_chunks):
        proj, vals = projection(r)
        sections += proj
        base = (r + 1) * len(proj)
        for slot, prep, fire in recurrence(r, vals):
            schedule.setdefault(base + slot, []).append((prep, fire))
    for t in range(max(schedule) + 2):
        for _, fire in schedule.get(t - 1, []):
            if fire is not None:
                fire()
        if t < len(sections):
            sections[t]()
        for prep, _ in schedule.get(t, []):
            prep()


def _mix(x, w_bf, lb, norm_w, offs, tm=1024):
    B, S, D = x.shape
    assert ROW_CHUNK == ATT_BK and ROW_CHUNK % HG_C == 0
    row = lambda w: pl.BlockSpec((None, tm, w), lambda b, s: (b, s, 0))
    out_shape = (
        jax.ShapeDtypeStruct((B, S, 512), BF16),
        jax.ShapeDtypeStruct((B, S, 512), BF16),
        jax.ShapeDtypeStruct((B, S // ATT_BK, ATT_HEADS * ATT_VR, ATT_BK), BF16),
        jax.ShapeDtypeStruct((B, S, HG_HEADS * HEAD_W), BF16),
    )
    out_specs = (
        row(512), row(512),
        pl.BlockSpec((None, tm // ATT_BK, ATT_HEADS * ATT_VR, ATT_BK),
                     lambda b, s: (b, s, 0, 0)),
        row(HG_HEADS * HEAD_W),
    )
    return pl.pallas_call(
        functools.partial(_mix_kernel, tm=tm, offs=offs),
        grid=(B, S // tm),
        in_specs=[row(D), _const_spec(w_bf.shape), _const_spec(lb.shape),
                  _const_spec(norm_w.shape)],
        out_specs=out_specs,
        out_shape=out_shape,
        scratch_shapes=[pltpu.VMEM((HG_HEADS, HEAD_W, HEAD_W), F32),
                        pltpu.VMEM((HG_HEADS, HG_C + 8, HEAD_W), F32),
                        pltpu.VMEM((HG_HEADS, HG_C + 8, HEAD_W), F32)],
        compiler_params=_cparams(2),
        name="inproj_hgrn2",
    )(x, w_bf, lb, norm_w)


def _rel_bucket_np(rel):
    nb = NUM_BUCKETS // 2
    max_exact = nb // 2
    ret = np.where(rel > 0, nb, 0)
    n = np.abs(rel)
    nf = np.maximum(n, 1).astype(np.float32)
    large = max_exact + (np.log(nf / np.float32(max_exact))
                         / np.float32(math.log(MAX_DISTANCE / max_exact))
                         * np.float32(nb - max_exact)).astype(np.int32)
    large = np.minimum(large, nb - 1)
    return ret + np.where(n < max_exact, n, large)


def _attn_bucket_tiles():
    i = np.arange(ATT_BQ)[None, :]
    j = np.arange(ATT_BK)[:, None]
    tiles = []
    for d in ATT_NEAR:
        kpos = d * ATT_BQ + j
        live = (kpos // CHUNK) <= (i // CHUNK)
        tiles.append(np.where(live, _rel_bucket_np(kpos - i), -1))
    far = _rel_bucket_np(np.asarray((min(ATT_NEAR) - 1) * ATT_BQ + ATT_BK - 1))
    assert int(far) == NUM_BUCKETS // 2 - 1
    return np.stack(tiles).astype(np.int32)


def _attn_kernel(lam_ref, rb_ref, bkt_ref, q_ref, k_ref, vt_ref, w_ref, o_ref,
                 bias_ref, qs_ref, s0_ref, s1_ref, p0_ref, p1_ref, a0_ref, a1_ref,
                 m_ref, acc_ref, *, out_scale):
    S = q_ref.shape[0]
    nkb = S // ATT_BK
    W2 = 2 * ATT_BQ
    near = tuple(range(ATT_GK)) + (-1,)
    lam = lam_ref[0]
    h = pl.program_id(0)

    @pl.when(pl.program_id(1) == 0)
    def _():
        far = rb_ref[NUM_BUCKETS // 2 - 1, h]
        for n, kbrel in enumerate(near):
            for j in range(ATT_G):
                d = (kbrel * ATT_BK) // ATT_BQ - j
                if d > max(ATT_NEAR):
                    bias_ref[n, j] = jnp.full((ATT_BK, W2), NEG, F32)
                elif d < min(ATT_NEAR):
                    bias_ref[n, j] = jnp.zeros((ATT_BK, W2), F32)
        for t, dt in enumerate(ATT_NEAR):
            bk = bkt_ref[t]
            tile = jnp.full(bk.shape, NEG, F32)
            for bucket in range(NUM_BUCKETS):
                tile = jnp.where(bk == bucket, (rb_ref[bucket, h] - far) * LOG2E, tile)
            tile2 = jnp.concatenate([tile, tile], axis=1)
            for n, kbrel in enumerate(near):
                for j in range(ATT_G):
                    if (kbrel * ATT_BK) // ATT_BQ - j == dt:
                        bias_ref[n, j] = tile2

    lane = lax.broadcasted_iota(jnp.int32, (ATT_BQ, HEAD_W), 1)
    first_map = lane < ATT_DH

    s_bufs, p_bufs, a_bufs = (s0_ref, s1_ref), (p0_ref, p1_ref), (a0_ref, a1_ref)

    def scores(step, j, buf):
        kb, bias_idx = step
        kblk = k_ref[pl.ds(pl.multiple_of(kb * ATT_BK, ATT_BK), ATT_BK), :]
        s = lax.dot_general(kblk, qs_ref[j], (((1,), (1,)), ((), ())),
                            preferred_element_type=F32)
        if bias_idx is not None:
            s = s + bias_ref[bias_idx, j]
        s_bufs[buf][j] = s

    def soft(j, buf, first=False):
        s = s_bufs[buf][j]
        m_new = jnp.max(s, axis=0, keepdims=True)
        if first:
            a_bufs[buf][j] = jnp.ones_like(m_new)
        else:
            m_old = m_ref[j]
            m_new = jnp.maximum(m_old, m_new)
            a_bufs[buf][j] = jnp.exp2(m_old - m_new)
        m_ref[j] = m_new
        p_bufs[buf][j] = jnp.exp2(s - m_new).astype(BF16)

    def accum(kb, j, buf, first=False):
        pv = jnp.dot(vt_ref[kb], p_bufs[buf][j], preferred_element_type=F32)
        if first:
            acc_ref[j] = pv
        else:
            acc_ref[j] = a_bufs[buf][j] * acc_ref[j] + pv

    AHEAD = 4

    def two_steps(steps, first=False, prev_kb=None, dead=()):
        for k in range(2 * ATT_G):
            ua = k + AHEAD
            if (ua // ATT_G, ua % ATT_G) not in dead:
                scores(steps[ua // ATT_G], ua % ATT_G, (ua // ATT_G) % 2)
            uc = k - 1
            if k >= 1 and (uc // ATT_G, uc % ATT_G) not in dead:
                accum(steps[uc // ATT_G][0], uc % ATT_G, uc // ATT_G,
                      first=first and uc < ATT_G)
            elif k == 0 and prev_kb is not None:
                accum(prev_kb, ATT_G - 1, 1)
            if (k // ATT_G, k % ATT_G) not in dead:
                soft(k % ATT_G, k // ATT_G, first=first and k < ATT_G)

    def inside(g):
        return [(g * ATT_GK + i, i) for i in range(ATT_GK)]

    def before(g):
        return (jnp.maximum(g * ATT_GK - 1, 0), ATT_GK)

    def far(n):
        return (jnp.clip(n - (ATT_GK + 1), 0, nkb - 1), None)

    dead = {(i, j) for i in range(ATT_GK) for j in range(ATT_G)
            if (i * ATT_BK) // ATT_BQ - j > max(ATT_NEAR)}

    def open_group(g):
        q0 = pl.multiple_of(g * (ATT_G * ATT_BQ), ATT_G * ATT_BQ)
        for j in range(ATT_G):
            q = q_ref[pl.ds(q0 + j * ATT_BQ, ATT_BQ), :]
            zero = jnp.zeros_like(q)
            qs_ref[j, :ATT_BQ, :] = jnp.where(first_map, q, zero)
            qs_ref[j, ATT_BQ:, :] = jnp.where(first_map, zero, q)
        for j in range(AHEAD):
            scores(inside(g)[0], j, 0)

    def close_group(g, n_steps):
        last_kb = jnp.where(g > 0, far(n_steps - 1)[0], inside(g)[1][0])
        accum(last_kb, ATT_G - 1, 1)

    def emit_group(g):
        q0 = pl.multiple_of(g * (ATT_G * ATT_BQ), ATT_G * ATT_BQ)
        for j in range(ATT_G):
            on = acc_ref[j, :HEAD_W, :] * (1.0 / acc_ref[j, HEAD_W:HEAD_W + 1, :])
            o = on[:, :ATT_BQ] - lam * on[:, ATT_BQ:]
            ms = jnp.mean(o * o, axis=0, keepdims=True)
            y = o * lax.rsqrt(ms + EPS) * w_ref[...] * out_scale
            o_ref[pl.ds(q0 + j * ATT_BQ, ATT_BQ), :] = y.T.astype(BF16)

    n_groups = S // (ATT_G * ATT_BQ)
    open_group(0)
    two_steps(inside(0) + [before(0)], first=True, dead=dead)

    def group(g, carry):
        n_steps = jnp.where(g > 0, ATT_GK + 1 + (g * ATT_GK - 1), ATT_GK)

        def far_steps(n):
            two_steps([far(n), far(n + 1), far(n + 2)], prev_kb=far(n - 1)[0])

        n_far_pairs = jnp.maximum(n_steps // 2 - 2, 0)
        first_far = ATT_GK + 2

        def four_far_steps(i, c):
            far_steps(first_far + 4 * i)
            far_steps(first_far + 4 * i + 2)
            return c

        lax.fori_loop(0, n_far_pairs // 2, four_far_steps, 0)

        @pl.when(n_far_pairs % 2 == 1)
        def _():
            far_steps(first_far + 4 * (n_far_pairs // 2))

        @pl.when(g + 1 < n_groups)
        def _():
            close_group(g, n_steps)
            open_group(g + 1)
            emit_group(g)
            two_steps(inside(g + 1) + [before(g + 1)], first=True, dead=dead)
            two_steps([before(g + 1), far(3), far(4)], prev_kb=inside(g + 1)[1][0])

        @pl.when(g + 1 == n_groups)
        def _():
            close_group(g, n_steps)
            emit_group(g)

        return carry

    lax.fori_loop(0, n_groups, group, 0)


def _attention(q, k, vt, rel_bias, subln_w, lam, out_scale):
    B, S, _ = q.shape
    nkb = S // ATT_BK
    W2 = 2 * ATT_BQ
    assert ATT_GK == 2 and ATT_BK == 2 * ATT_BQ
    bkt = jnp.asarray(_attn_bucket_tiles())
    head = pl.BlockSpec((None, S, HEAD_W), lambda h, b: (b, 0, h))
    smem = pl.BlockSpec(memory_space=pltpu.SMEM)
    return pl.pallas_call(
        functools.partial(_attn_kernel, out_scale=out_scale),
        grid=(ATT_HEADS, B),
        in_specs=[
            smem, smem, _const_spec(bkt.shape),
            head, head,
            pl.BlockSpec((None, nkb, ATT_VR, ATT_BK), lambda h, b: (b, 0, h, 0)),
            pl.BlockSpec((HEAD_W, 1), lambda h, b: (0, 0)),
        ],
        out_specs=head,
        out_shape=jax.ShapeDtypeStruct((B, S, ATT_HEADS * HEAD_W), BF16),
        scratch_shapes=[
            pltpu.VMEM((ATT_GK + 1, ATT_G, ATT_BK, W2), F32),
            pltpu.VMEM((ATT_G, W2, HEAD_W), BF16),
            pltpu.VMEM((ATT_G, ATT_BK, W2), F32),
            pltpu.VMEM((ATT_G, ATT_BK, W2), F32),
            pltpu.VMEM((ATT_G, ATT_BK, W2), BF16),
            pltpu.VMEM((ATT_G, ATT_BK, W2), BF16),
            pltpu.VMEM((ATT_G, 1, W2), F32),
            pltpu.VMEM((ATT_G, 1, W2), F32),
            pltpu.VMEM((ATT_G, 1, W2), F32),
            pltpu.VMEM((ATT_G, ATT_VR, W2), F32),
        ],
        compiler_params=_cparams(2),
        name="diffattn",
    )(lam, rel_bias, bkt, q, k, vt, subln_w)


def _layer_norm(z, g, b):
    mu = jnp.mean(z, axis=-1, keepdims=True)
    zc = z - mu
    var = jnp.mean(zc * zc, axis=-1, keepdims=True)
    return zc * lax.rsqrt(var + EPS) * g + b


def _merge_kernel(ya_ref, yh_ref, x_ref, wa_ref, wh_ref, wg_ref, bg_ref, wo_ref,
                  g_ref, b_ref, o_ref, *, alpha):
    D = x_ref.shape[-1]
    tm = x_ref.shape[0]

    def branches(r):
        rows = slice(r * ROW_CHUNK, (r + 1) * ROW_CHUNK)
        xb = x_ref[rows, :].astype(BF16)
        pa = jnp.dot(ya_ref[rows, :], wa_ref[...], preferred_element_type=F32)
        ph = jnp.dot(yh_ref[rows, :], wh_ref[...], preferred_element_type=F32)
        ga = jnp.dot(xb, wg_ref[:, :D], preferred_element_type=F32)
        gh = jnp.dot(xb, wg_ref[:, D:], preferred_element_type=F32)
        return pa, ph, ga, gh

    n_chunks = tm // ROW_CHUNK
    nxt = branches(0)
    for r in range(n_chunks):
        rows = slice(r * ROW_CHUNK, (r + 1) * ROW_CHUNK)
        pa, ph, ga, gh = nxt
        if r + 1 < n_chunks:
            nxt = branches(r + 1)
        merged = (jax.nn.sigmoid(ga + bg_ref[:, :D]) * pa
                  + jax.nn.sigmoid(gh + bg_ref[:, D:]) * ph)
        mix = jnp.dot(merged.astype(BF16), wo_ref[...], preferred_element_type=F32)
        o_ref[rows, :] = _layer_norm(alpha * x_ref[rows, :] + mix, g_ref[...], b_ref[...])


def _merge(ya, yh, x, wa, wh, wg, bg, wo, g, b, alpha, tm=1024):
    N, D = x.shape
    row = lambda w: pl.BlockSpec((tm, w), lambda i: (i, 0))
    return pl.pallas_call(
        functools.partial(_merge_kernel, alpha=alpha),
        grid=(N // tm,),
        in_specs=[row(ya.shape[1]), row(yh.shape[1]), row(D),
                  _const_spec(wa.shape), _const_spec(wh.shape), _const_spec(wg.shape),
                  _const_spec(bg.shape), _const_spec(wo.shape),
                  _const_spec(g.shape), _const_spec(b.shape)],
        out_specs=row(D),
        out_shape=jax.ShapeDtypeStruct((N, D), F32),
        compiler_params=_cparams(1),
        name="merge_ln",
    )(ya, yh, x, wa, wh, wg, bg, wo, g, b)


def _ffn_kernel(h_ref, wg_ref, wu_ref, cw_ref, cb_ref, wd_ref, g_ref, b_ref, o_ref,
                tail_ref, gbuf_ref, *, alpha, tm):
    d_ff = wg_ref.shape[1]
    PAD = 8

    @pl.when(pl.program_id(1) == 0)
    def _():
        tail_ref[...] = jnp.zeros_like(tail_ref)

    RC = ROW_CHUNK
    nfc = d_ff // FFN_FC
    steps = [(r, fc) for r in range(tm // RC) for fc in range(nfc)]
    hb = {}

    def gate_up(r, fc):
        if r not in hb:
            hb[r] = h_ref[r * RC:(r + 1) * RC, :].astype(BF16)
        cols = slice(fc * FFN_FC, (fc + 1) * FFN_FC)
        return (jnp.dot(hb[r], wg_ref[:, cols], preferred_element_type=F32),
                jnp.dot(hb[r], wu_ref[:, cols], preferred_element_type=F32))

    nxt = gate_up(*steps[0])
    acc = None
    for t, (r, fc) in enumerate(steps):
        rows = slice(r * RC, (r + 1) * RC)
        cols = slice(fc * FFN_FC, (fc + 1) * FFN_FC)
        gte, up = nxt
        if t + 1 < len(steps):
            nxt = gate_up(*steps[t + 1])
        gbuf = gbuf_ref.at[t % 2]
        gbuf[0:PAD, :] = tail_ref[:, cols]
        gbuf[PAD:, :] = gte
        tail_ref[:, cols] = gte[RC - PAD:, :]
        conv = (gte * cw_ref[2:3, cols]
                + gbuf[PAD - 1:PAD - 1 + RC, :] * cw_ref[1:2, cols]
                + gbuf[PAD - 2:PAD - 2 + RC, :] * cw_ref[0:1, cols]
                + cb_ref[:, cols])
        act = jax.nn.gelu(conv) * up
        down = jnp.dot(act.astype(BF16), wd_ref[cols, :], preferred_element_type=F32)
        acc = down if fc == 0 else acc + down
        if fc == nfc - 1:
            o_ref[rows, :] = _layer_norm(alpha * h_ref[rows, :] + acc, g_ref[...], b_ref[...])


def _ffn(h, wg, wu, cw, cb, wd, g, b, alpha, tm=1024):
    B, S, D = h.shape
    d_ff = wg.shape[1]
    row = pl.BlockSpec((None, tm, D), lambda bi, s: (bi, s, 0))
    return pl.pallas_call(
        functools.partial(_ffn_kernel, alpha=alpha, tm=tm),
        grid=(B, S // tm),
        in_specs=[row, _const_spec(wg.shape), _const_spec(wu.shape), _const_spec(cw.shape),
                  _const_spec(cb.shape), _const_spec(wd.shape),
                  _const_spec(g.shape), _const_spec(b.shape)],
        out_specs=row,
        out_shape=jax.ShapeDtypeStruct((B, S, D), F32),
        scratch_shapes=[pltpu.VMEM((8, d_ff), F32),
                        pltpu.VMEM((2, ROW_CHUNK + 8, FFN_FC), F32)],
        compiler_params=_cparams(2),
        name="ffn_ln",
    )(h, wg, wu, cw, cb, wd, g, b)


def kernel(x, w_in, b_gate, lambda_q1, lambda_k1, lambda_q2, lambda_k2, attn_subln_w,
           rel_bias, hgrn_lb_logits, hgrn_norm_w, w_branch_attn, w_branch_hgrn, w_out,
           ln1_g, ln1_b, w_ffn_gate, w_ffn_up, ffn_conv_w, ffn_conv_b, w_ffn_down,
           ln2_g, ln2_b):
    B, S, D = x.shape
    depth = w_in.shape[0]
    att_qk = ATT_HEADS * 2 * ATT_DH
    att_w = ATT_HEADS * HEAD_W
    hg_w = HG_HEADS * HEAD_W
    offs = {"aq": 0, "ak": att_qk, "av": 2 * att_qk}
    offs["hq"] = offs["av"] + att_w
    offs["hf"] = offs["hq"] + hg_w
    offs["hi"] = offs["hf"] + hg_w
    offs["hg"] = offs["hi"] + hg_w
    offs["gate"] = offs["hg"] + hg_w
    alpha = (2 * depth) ** 0.25

    lb_all = jnp.cumsum(jax.nn.softmax(hgrn_lb_logits.astype(F32), axis=0), axis=0)

    h = x
    for l in range(depth):
        lam_init = 0.8 - 0.6 * math.exp(-0.3 * l)
        lam = (jnp.exp(jnp.sum(lambda_q1[l].astype(F32) * lambda_k1[l].astype(F32)))
               - jnp.exp(jnp.sum(lambda_q2[l].astype(F32) * lambda_k2[l].astype(F32)))
               + lam_init).reshape(1)

        w_mix_bf = w_in[l, :, :offs["gate"]].astype(BF16)
        w_gate_bf = w_in[l, :, offs["gate"]:].astype(BF16)
        q, k, vt, yh = _mix(h, w_mix_bf, lb_all[l][None, :],
                            hgrn_norm_w[l][None, :].astype(F32), offs)

        ya = _attention(q, k, vt, rel_bias.astype(F32),
                        attn_subln_w[l][:, None].astype(F32), lam, 1.0 - lam_init)

        h1 = _merge(ya.reshape(B * S, att_w), yh.reshape(B * S, hg_w), h.reshape(B * S, D),
                    w_branch_attn[l].astype(BF16), w_branch_hgrn[l].astype(BF16),
                    w_gate_bf, b_gate[l][None, :], w_out[l].astype(BF16),
                    ln1_g[l][None, :], ln1_b[l][None, :], alpha)

        h = _ffn(h1.reshape(B, S, D), w_ffn_gate[l].astype(BF16), w_ffn_up[l].astype(BF16),
                 ffn_conv_w[l], ffn_conv_b[l][None, :], w_ffn_down[l].astype(BF16),
                 ln2_g[l][None, :], ln2_b[l][None, :], alpha)
    return h
```

```python
import functools
import math

import numpy as np
import jax
import jax.numpy as jnp
from jax import lax
from jax.experimental import pallas as pl
from jax.experimental.pallas import tpu as pltpu

F32 = jnp.float32
BF16 = jnp.bfloat16

CHUNK = 64
ATT_HEADS = 4
ATT_DH = 64
HG_HEADS = 4
HEAD_W = 128
NUM_BUCKETS = 32
MAX_DISTANCE = 128
EPS = 1e-5
NEG = -1e30

ATT_BQ = 128
ATT_BK = 256
ATT_G = 4
ATT_GK = ATT_G * ATT_BQ // ATT_BK
ATT_NEAR = (0, -1, -2)
ATT_VR = HEAD_W + 16
LOG2E = math.log2(math.e)
HG_C = 128
HG_LEVELS = (64, 32, 16, 8, 4)
HG_DIAG = 4
FFN_FC = 256
ROW_CHUNK = 256

VMEM_LIMIT = 56 * 1024 * 1024


def _cparams(n_axes):
    return pltpu.CompilerParams(
        dimension_semantics=("arbitrary",) * n_axes,
        vmem_limit_bytes=VMEM_LIMIT)


def _const_spec(shape):
    nd = len(shape)
    return pl.BlockSpec(shape, lambda *_: (0,) * nd, pipeline_mode=pl.Buffered(1))


def _mix_kernel(x_ref, w_ref, lb_ref, nw_ref, q_ref, k_ref, vt_ref, yh_ref,
                state_ref, kbuf_ref, bbuf_ref, *, tm, offs):
    C, H, W, RC, PAD = HG_C, HG_HEADS, HEAD_W, ROW_CHUNK, 8

    @pl.when(pl.program_id(1) == 0)
    def _():
        state_ref[...] = jnp.zeros_like(state_ref)

    kbuf_ref[:, 0:PAD, :] = jnp.zeros((H, PAD, W), F32)
    bbuf_ref[:, 0:PAD, :] = jnp.zeros((H, PAD, W), F32)

    lb = lb_ref[...]
    row = lax.broadcasted_iota(jnp.int32, (C, W), 0)
    r2 = lax.broadcasted_iota(jnp.int32, (C, C), 0)
    c2 = lax.broadcasted_iota(jnp.int32, (C, C), 1)
    tri = (c2 <= r2).astype(BF16)
    is_q = {hs: ((row // hs) % 2) == 1 for hs in HG_LEVELS}
    sign = {hs: jnp.where(is_q[hs], 1.0, -1.0).astype(F32) for hs in HG_LEVELS}
    keep = {hs: ((r2 // (2 * hs)) == (c2 // (2 * hs)))
            & (((r2 // hs) % 2) == 1) & (((c2 // hs) % 2) == 0) for hs in HG_LEVELS}
    same_blk = (r2 // HG_DIAG) == (c2 // HG_DIAG)
    on_diag = {d: ((r2 - c2) == d) & same_blk for d in range(HG_DIAG)}
    heads = range(H)
    cols = [slice(h * W, (h + 1) * W) for h in heads]

    def level_ref(b, hs):
        if hs >= PAD:
            parts = [jnp.broadcast_to(b[g * 2 * hs + hs - 1:g * 2 * hs + hs, :], (2 * hs, W))
                     for g in range(C // (2 * hs))]
            return parts[0] if len(parts) == 1 else jnp.concatenate(parts, axis=0)
        b3 = b.reshape(C // (2 * hs), 2 * hs, W)
        return jnp.broadcast_to(b3[:, hs - 1:hs, :], b3.shape).reshape(C, W)

    def projection(r):
        rows = slice(r * RC, (r + 1) * RC)
        out = {}

        def mm(off):
            return jnp.dot(out["xb"], w_ref[:, off:off + 512], preferred_element_type=F32)

        def sec_q():
            out["xb"] = x_ref[rows, :].astype(BF16)
            q_ref[rows, :] = (mm(offs["aq"]) * (ATT_DH ** -0.5 * LOG2E)).astype(BF16)

        def sec_k():
            k_ref[rows, :] = mm(offs["ak"]).astype(BF16)

        def sec_v():
            v = mm(offs["av"])
            for hd in range(ATT_HEADS):
                vt_ref[r, hd * ATT_VR:hd * ATT_VR + HEAD_W, :] = (
                    v[:, hd * HEAD_W:(hd + 1) * HEAD_W].T.astype(BF16))
                vt_ref[r, hd * ATT_VR + HEAD_W:(hd + 1) * ATT_VR, :] = jnp.ones(
                    (ATT_VR - HEAD_W, ATT_BK), BF16)

        def sec_hq():
            out["hq"] = mm(offs["hq"])

        def sec_hf():
            sig = jax.nn.sigmoid(mm(offs["hf"]))
            out["lf"] = jnp.log2(lb + (1.0 - lb) * sig)
            out["kk"] = (1.0 - lb) * (1.0 - sig)

        def sec_hi():
            out["hi"] = mm(offs["hi"]).astype(BF16)

        def sec_hg():
            hog = mm(offs["hg"])
            out["sog"] = hog * jax.nn.sigmoid(hog)

        return [sec_q, sec_k, sec_v, sec_hq, sec_hf, sec_hi, sec_hg], out

    def recurrence(r, vals):
        pieces = []
        for sub in range(RC // C):
            sl = slice(sub * C, (sub + 1) * C)
            t0 = r * RC + sub * C
            st = {}
            s0 = 3 * sub

            def cumsum_prep(sl=sl, st=st):
                lf = vals["lf"][sl]
                st["p0"] = lf.astype(BF16)
                st["p1"] = (lf - st["p0"].astype(F32)).astype(BF16)

            def cumsum_fire(st=st):
                b_all = (jnp.dot(tri, st["p0"], preferred_element_type=F32)
                         + jnp.dot(tri, st["p1"], preferred_element_type=F32))
                st["b"] = [b_all[:, cols[h]] for h in heads]

            def inter_prep(sl=sl, st=st):
                st["qf"] = [vals["hq"][sl, cols[h]] for h in heads]
                st["kf"] = [vals["kk"][sl, cols[h]] for h in heads]
                st["iv"] = [vals["hi"][sl, cols[h]] for h in heads]
                st["st"] = [state_ref[h] for h in heads]
                st["qt"] = [(st["qf"][h] * jnp.exp2(st["b"][h])).astype(BF16) for h in heads]
                st["stb"] = [st["st"][h].astype(BF16) for h in heads]
                st["a"] = [None] * H

            def inter_fire(st=st):
                st["inter"] = [lax.dot_general(st["qt"][h], st["stb"][h],
                                               (((1,), (1,)), ((), ())),
                                               preferred_element_type=F32) for h in heads]

            def level_prep(hs, st=st):
                ys = []
                for h in heads:
                    b = st["b"][h]
                    e = jnp.exp2((b - level_ref(b, hs)) * sign[hs])
                    ys.append((jnp.where(is_q[hs], st["qf"][h], st["kf"][h]) * e).astype(BF16))
                st["y", hs] = ys

            def level_fire(hs, st=st):
                for h in heads:
                    y = st["y", hs][h]
                    sc = lax.dot_general(y, y, (((1,), (1,)), ((), ())),
                                         preferred_element_type=F32)
                    sc = jnp.where(keep[hs], sc, 0.0)
                    st["a"][h] = sc if st["a"][h] is None else st["a"][h] + sc

            def diagonals(st=st):
                for h in heads:
                    kbuf_ref[h, PAD:, :] = st["kf"][h]
                    bbuf_ref[h, PAD:, :] = st["b"][h]
                diag = [None] * H
                for d in range(HG_DIAG):
                    for h in heads:
                        if d == 0:
                            term = st["qf"][h] * st["kf"][h]
                        else:
                            ks = kbuf_ref[h, PAD - d:PAD - d + C, :]
                            bs = bbuf_ref[h, PAD - d:PAD - d + C, :]
                            term = st["qf"][h] * ks * jnp.exp2(st["b"][h] - bs)
                        dsum = jnp.sum(term, axis=-1, keepdims=True)
                        placed = jnp.where(on_diag[d], dsum, 0.0)
                        diag[h] = placed if diag[h] is None else diag[h] + placed
                st["diag"] = diag

            def output_prep(st=st):
                st["ab"] = [(st["a"][h] + st["diag"][h]).astype(BF16) for h in heads]

            def output_fire(st=st):
                st["o"] = [st["inter"][h] + jnp.dot(st["ab"][h], st["iv"][h],
                                                    preferred_element_type=F32)
                           for h in heads]

            def update_prep(st=st):
                st["b_last"] = [st["b"][h][C - 1:C, :] for h in heads]
                st["k_end"] = [(st["kf"][h] * jnp.exp2(st["b_last"][h] - st["b"][h])
                                ).astype(BF16) for h in heads]

            def update_fire(st=st):
                for h in heads:
                    upd = lax.dot_general(st["iv"][h], st["k_end"][h],
                                          (((0,), (0,)), ((), ())),
                                          preferred_element_type=F32)
                    state_ref[h] = st["st"][h] * jnp.exp2(st["b_last"][h]) + upd

            def emit(sl=sl, st=st, t0=t0):
                for h in heads:
                    o = st["o"][h]
                    ms = jnp.mean(o * o, axis=-1, keepdims=True)
                    y = o * lax.rsqrt(ms + EPS) * nw_ref[...] * vals["sog"][sl, cols[h]]
                    yh_ref[t0:t0 + C, cols[h]] = y.astype(BF16)

            levels = [(functools.partial(level_prep, hs), functools.partial(level_fire, hs))
                      for hs in HG_LEVELS]
            pieces.append((0, cumsum_prep, cumsum_fire))
            pieces.append((s0 + 1, inter_prep, inter_fire))
            pieces += [(s0 + 1, p, f) for p, f in levels[:2]]
            pieces += [(s0 + 2, p, f) for p, f in levels[2:]]
            pieces.append((s0 + 3, update_prep, update_fire))
            pieces.append((s0 + 3, diagonals, None))
            pieces.append((s0 + 4, output_prep, output_fire))
            pieces.append((s0 + 5, emit, None))
        return pieces

    n_chunks = tm // RC
    sections = []
    schedule = {}
    for r in range(n_chunks):
        proj, vals = projection(r)
        sections += proj
        base = (r + 1) * len(proj)
        for slot, prep, fire in recurrence(r, vals):
            schedule.setdefault(base + slot, []).append((prep, fire))
    for t in range(max(schedule) + 2):
        for _, fire in schedule.get(t - 1, []):
            if fire is not None:
                fire()
        if t < len(sections):
            sections[t]()
        for prep, _ in schedule.get(t, []):
            prep()


def _mix(x, w_bf, lb, norm_w, offs, tm=1024):
    B, S, D = x.shape
    assert ROW_CHUNK == ATT_BK and ROW_CHUNK % HG_C == 0
    row = lambda w: pl.BlockSpec((None, tm, w), lambda b, s: (b, s, 0))
    out_shape = (
        jax.ShapeDtypeStruct((B, S, 512), BF16),
        jax.ShapeDtypeStruct((B, S, 512), BF16),
        jax.ShapeDtypeStruct((B, S // ATT_BK, ATT_HEADS * ATT_VR, ATT_BK), BF16),
        jax.ShapeDtypeStruct((B, S, HG_HEADS * HEAD_W), BF16),
    )
    out_specs = (
        row(512), row(512),
        pl.BlockSpec((None, tm // ATT_BK, ATT_HEADS * ATT_VR, ATT_BK),
                     lambda b, s: (b, s, 0, 0)),
        row(HG_HEADS * HEAD_W),
    )
    return pl.pallas_call(
        functools.partial(_mix_kernel, tm=tm, offs=offs),
        grid=(B, S // tm),
        in_specs=[row(D), _const_spec(w_bf.shape), _const_spec(lb.shape),
                  _const_spec(norm_w.shape)],
        out_specs=out_specs,
        out_shape=out_shape,
        scratch_shapes=[pltpu.VMEM((HG_HEADS, HEAD_W, HEAD_W), F32),
                        pltpu.VMEM((HG_HEADS, HG_C + 8, HEAD_W), F32),
                        pltpu.VMEM((HG_HEADS, HG_C + 8, HEAD_W), F32)],
        compiler_params=_cparams(2),
        name="inproj_hgrn2",
    )(x, w_bf, lb, norm_w)


def _rel_bucket_np(rel):
    nb = NUM_BUCKETS // 2
    max_exact = nb // 2
    ret = np.where(rel > 0, nb, 0)
    n = np.abs(rel)
    nf = np.maximum(n, 1).astype(np.float32)
    large = max_exact + (np.log(nf / np.float32(max_exact))
                         / np.float32(math.log(MAX_DISTANCE / max_exact))
                         * np.float32(nb - max_exact)).astype(np.int32)
    large = np.minimum(large, nb - 1)
    return ret + np.where(n < max_exact, n, large)


def _attn_bucket_tiles():
    i = np.arange(ATT_BQ)[None, :]
    j = np.arange(ATT_BK)[:, None]
    tiles = []
    for d in ATT_NEAR:
        kpos = d * ATT_BQ + j
        live = (kpos // CHUNK) <= (i // CHUNK)
        tiles.append(np.where(live, _rel_bucket_np(kpos - i), -1))
    far = _rel_bucket_np(np.asarray((min(ATT_NEAR) - 1) * ATT_BQ + ATT_BK - 1))
    assert int(far) == NUM_BUCKETS // 2 - 1
    return np.stack(tiles).astype(np.int32)


def _attn_kernel(lam_ref, rb_ref, bkt_ref, q_ref, k_ref, vt_ref, w_ref, o_ref,
                 bias_ref, qs_ref, s0_ref, s1_ref, p0_ref, p1_ref, a0_ref, a1_ref,
                 m_ref, acc_ref, *, out_scale):
    S = q_ref.shape[0]
    nkb = S // ATT_BK
    W2 = 2 * ATT_BQ
    near = tuple(range(ATT_GK)) + (-1,)
    lam = lam_ref[0]
    h = pl.program_id(0)

    @pl.when(pl.program_id(1) == 0)
    def _():
        far = rb_ref[NUM_BUCKETS // 2 - 1, h]
        for n, kbrel in enumerate(near):
            for j in range(ATT_G):
                d = (kbrel * ATT_BK) // ATT_BQ - j
                if d > max(ATT_NEAR):
                    bias_ref[n, j] = jnp.full((ATT_BK, W2), NEG, F32)
                elif d < min(ATT_NEAR):
                    bias_ref[n, j] = jnp.zeros((ATT_BK, W2), F32)
        for t, dt in enumerate(ATT_NEAR):
            bk = bkt_ref[t]
            tile = jnp.full(bk.shape, NEG, F32)
            for bucket in range(NUM_BUCKETS):
                tile = jnp.where(bk == bucket, (rb_ref[bucket, h] - far) * LOG2E, tile)
            tile2 = jnp.concatenate([tile, tile], axis=1)
            for n, kbrel in enumerate(near):
                for j in range(ATT_G):
                    if (kbrel * ATT_BK) // ATT_BQ - j == dt:
                        bias_ref[n, j] = tile2

    lane = lax.broadcasted_iota(jnp.int32, (ATT_BQ, HEAD_W), 1)
    first_map = lane < ATT_DH

    s_bufs, p_bufs, a_bufs = (s0_ref, s1_ref), (p0_ref, p1_ref), (a0_ref, a1_ref)

    def scores(step, j, buf):
        kb, bias_idx = step
        kblk = k_ref[pl.ds(pl.multiple_of(kb * ATT_BK, ATT_BK), ATT_BK), :]
        s = lax.dot_general(kblk, qs_ref[j], (((1,), (1,)), ((), ())),
                            preferred_element_type=F32)
        if bias_idx is not None:
            s = s + bias_ref[bias_idx, j]
        s_bufs[buf][j] = s

    def soft(j, buf, first=False):
        s = s_bufs[buf][j]
        m_new = jnp.max(s, axis=0, keepdims=True)
        if first:
            a_bufs[buf][j] = jnp.ones_like(m_new)
        else:
            m_old = m_ref[j]
            m_new = jnp.maximum(m_old, m_new)
            a_bufs[buf][j] = jnp.exp2(m_old - m_new)
        m_ref[j] = m_new
        p_bufs[buf][j] = jnp.exp2(s - m_new).astype(BF16)

    def accum(kb, j, buf, first=False):
        pv = jnp.dot(vt_ref[kb], p_bufs[buf][j], preferred_element_type=F32)
        if first:
            acc_ref[j] = pv
        else:
            acc_ref[j] = a_bufs[buf][j] * acc_ref[j] + pv

    AHEAD = 4

    def two_steps(steps, first=False, prev_kb=None, dead=()):
        for k in range(2 * ATT_G):
            ua = k + AHEAD
            if (ua // ATT_G, ua % ATT_G) not in dead:
                scores(steps[ua // ATT_G], ua % ATT_G, (ua // ATT_G) % 2)
            uc = k - 1
            if k >= 1 and (uc // ATT_G, uc % ATT_G) not in dead:
                accum(steps[uc // ATT_G][0], uc % ATT_G, uc // ATT_G,
                      first=first and uc < ATT_G)
            elif k == 0 and prev_kb is not None:
                accum(prev_kb, ATT_G - 1, 1)
            if (k // ATT_G, k % ATT_G) not in dead:
                soft(k % ATT_G, k // ATT_G, first=first and k < ATT_G)

    def inside(g):
        return [(g * ATT_GK + i, i) for i in range(ATT_GK)]

    def before(g):
        return (jnp.maximum(g * ATT_GK - 1, 0), ATT_GK)

    def far(n):
        return (jnp.clip(n - (ATT_GK + 1), 0, nkb - 1), None)

    dead = {(i, j) for i in range(ATT_GK) for j in range(ATT_G)
            if (i * ATT_BK) // ATT_BQ - j > max(ATT_NEAR)}

    def open_group(g):
        q0 = pl.multiple_of(g * (ATT_G * ATT_BQ), ATT_G * ATT_BQ)
        for j in range(ATT_G):
            q = q_ref[pl.ds(q0 + j * ATT_BQ, ATT_BQ), :]
            zero = jnp.zeros_like(q)
            qs_ref[j, :ATT_BQ, :] = jnp.where(first_map, q, zero)
            qs_ref[j, ATT_BQ:, :] = jnp.where(first_map, zero, q)
        for j in range(AHEAD):
            scores(inside(g)[0], j, 0)

    def close_group(g, n_steps):
        last_kb = jnp.where(g > 0, far(n_steps - 1)[0], inside(g)[1][0])
        accum(last_kb, ATT_G - 1, 1)

    def emit_group(g):
        q0 = pl.multiple_of(g * (ATT_G * ATT_BQ), ATT_G * ATT_BQ)
        for j in range(ATT_G):
            on = acc_ref[j, :HEAD_W, :] * (1.0 / acc_ref[j, HEAD_W:HEAD_W + 1, :])
            o = on[:, :ATT_BQ] - lam * on[:, ATT_BQ:]
            ms = jnp.mean(o * o, axis=0, keepdims=True)
            y = o * lax.rsqrt(ms + EPS) * w_ref[...] * out_scale
            o_ref[pl.ds(q0 + j * ATT_BQ, ATT_BQ), :] = y.T.astype(BF16)

    n_groups = S // (ATT_G * ATT_BQ)
    open_group(0)
    two_steps(inside(0) + [before(0)], first=True, dead=dead)

    def group(g, carry):
        n_steps = jnp.where(g > 0, ATT_GK + 1 + (g * ATT_GK - 1), ATT_GK)

        def far_steps(n):
            two_steps([far(n), far(n + 1), far(n + 2)], prev_kb=far(n - 1)[0])

        n_far_pairs = jnp.maximum(n_steps // 2 - 2, 0)
        first_far = ATT_GK + 2

        def far_block(n, pairs):
            for i in range(pairs):
                far_steps(n + 2 * i)

        def six_far_steps(i, c):
            far_block(first_far + 6 * i, 3)
            return c

        n_six = n_far_pairs // 3
        lax.fori_loop(0, n_six, six_far_steps, 0)
        for left in (1, 2):
            @pl.when(n_far_pairs - 3 * n_six == left)
            def _(left=left):
                far_block(first_far + 6 * n_six, left)

        @pl.when(g + 1 < n_groups)
        def _():
            close_group(g, n_steps)
            open_group(g + 1)
            emit_group(g)
            two_steps(inside(g + 1) + [before(g + 1)], first=True, dead=dead)
            two_steps([before(g + 1), far(3), far(4)], prev_kb=inside(g + 1)[1][0])

        @pl.when(g + 1 == n_groups)
        def _():
            close_group(g, n_steps)
            emit_group(g)

        return carry

    lax.fori_loop(0, n_groups, group, 0)


def _attention(q, k, vt, rel_bias, subln_w, lam, out_scale):
    B, S, _ = q.shape
    nkb = S // ATT_BK
    W2 = 2 * ATT_BQ
    assert ATT_GK == 2 and ATT_BK == 2 * ATT_BQ
    bkt = jnp.asarray(_attn_bucket_tiles())
    head = pl.BlockSpec((None, S, HEAD_W), lambda h, b: (b, 0, h))
    smem = pl.BlockSpec(memory_space=pltpu.SMEM)
    return pl.pallas_call(
        functools.partial(_attn_kernel, out_scale=out_scale),
        grid=(ATT_HEADS, B),
        in_specs=[
            smem, smem, _const_spec(bkt.shape),
            head, head,
            pl.BlockSpec((None, nkb, ATT_VR, ATT_BK), lambda h, b: (b, 0, h, 0)),
            pl.BlockSpec((HEAD_W, 1), lambda h, b: (0, 0)),
        ],
        out_specs=head,
        out_shape=jax.ShapeDtypeStruct((B, S, ATT_HEADS * HEAD_W), BF16),
        scratch_shapes=[
            pltpu.VMEM((ATT_GK + 1, ATT_G, ATT_BK, W2), F32),
            pltpu.VMEM((ATT_G, W2, HEAD_W), BF16),
            pltpu.VMEM((ATT_G, ATT_BK, W2), F32),
            pltpu.VMEM((ATT_G, ATT_BK, W2), F32),
            pltpu.VMEM((ATT_G, ATT_BK, W2), BF16),
            pltpu.VMEM((ATT_G, ATT_BK, W2), BF16),
            pltpu.VMEM((ATT_G, 1, W2), F32),
            pltpu.VMEM((ATT_G, 1, W2), F32),
            pltpu.VMEM((ATT_G, 1, W2), F32),
            pltpu.VMEM((ATT_G, ATT_VR, W2), F32),
        ],
        compiler_params=_cparams(2),
        name="diffattn",
    )(lam, rel_bias, bkt, q, k, vt, subln_w)


def _layer_norm(z, g, b):
    mu = jnp.mean(z, axis=-1, keepdims=True)
    zc = z - mu
    var = jnp.mean(zc * zc, axis=-1, keepdims=True)
    return zc * lax.rsqrt(var + EPS) * g + b


def _merge_kernel(ya_ref, yh_ref, x_ref, wa_ref, wh_ref, wg_ref, bg_ref, wo_ref,
                  g_ref, b_ref, o_ref, *, alpha):
    D = x_ref.shape[-1]
    tm = x_ref.shape[0]

    def branches(r):
        rows = slice(r * ROW_CHUNK, (r + 1) * ROW_CHUNK)
        xb = x_ref[rows, :].astype(BF16)
        pa = jnp.dot(ya_ref[rows, :], wa_ref[...], preferred_element_type=F32)
        ph = jnp.dot(yh_ref[rows, :], wh_ref[...], preferred_element_type=F32)
        ga = jnp.dot(xb, wg_ref[:, :D], preferred_element_type=F32)
        gh = jnp.dot(xb, wg_ref[:, D:], preferred_element_type=F32)
        return pa, ph, ga, gh

    n_chunks = tm // ROW_CHUNK
    nxt = branches(0)
    for r in range(n_chunks):
        rows = slice(r * ROW_CHUNK, (r + 1) * ROW_CHUNK)
        pa, ph, ga, gh = nxt
        if r + 1 < n_chunks:
            nxt = branches(r + 1)
        merged = (jax.nn.sigmoid(ga + bg_ref[:, :D]) * pa
                  + jax.nn.sigmoid(gh + bg_ref[:, D:]) * ph)
        mix = jnp.dot(merged.astype(BF16), wo_ref[...], preferred_element_type=F32)
        o_ref[rows, :] = _layer_norm(alpha * x_ref[rows, :] + mix, g_ref[...], b_ref[...])


def _merge(ya, yh, x, wa, wh, wg, bg, wo, g, b, alpha, tm=1024):
    N, D = x.shape
    row = lambda w: pl.BlockSpec((tm, w), lambda i: (i, 0))
    return pl.pallas_call(
        functools.partial(_merge_kernel, alpha=alpha),
        grid=(N // tm,),
        in_specs=[row(ya.shape[1]), row(yh.shape[1]), row(D),
                  _const_spec(wa.shape), _const_spec(wh.shape), _const_spec(wg.shape),
                  _const_spec(bg.shape), _const_spec(wo.shape),
                  _const_spec(g.shape), _const_spec(b.shape)],
        out_specs=row(D),
        out_shape=jax.ShapeDtypeStruct((N, D), F32),
        compiler_params=_cparams(1),
        name="merge_ln",
    )(ya, yh, x, wa, wh, wg, bg, wo, g, b)


def _ffn_kernel(h_ref, wg_ref, wu_ref, cw_ref, cb_ref, wd_ref, g_ref, b_ref, o_ref,
                tail_ref, gbuf_ref, *, alpha, tm):
    d_ff = wg_ref.shape[1]
    PAD = 8

    @pl.when(pl.program_id(1) == 0)
    def _():
        tail_ref[...] = jnp.zeros_like(tail_ref)

    RC = ROW_CHUNK
    nfc = d_ff // FFN_FC
    steps = [(r, fc) for r in range(tm // RC) for fc in range(nfc)]
    hb = {}

    def gate_up(r, fc):
        if r not in hb:
            hb[r] = h_ref[r * RC:(r + 1) * RC, :].astype(BF16)
        cols = slice(fc * FFN_FC, (fc + 1) * FFN_FC)
        return (jnp.dot(hb[r], wg_ref[:, cols], preferred_element_type=F32),
                jnp.dot(hb[r], wu_ref[:, cols], preferred_element_type=F32))

    nxt = gate_up(*steps[0])
    acc = None
    for t, (r, fc) in enumerate(steps):
        rows = slice(r * RC, (r + 1) * RC)
        cols = slice(fc * FFN_FC, (fc + 1) * FFN_FC)
        gte, up = nxt
        if t + 1 < len(steps):
            nxt = gate_up(*steps[t + 1])
        gbuf = gbuf_ref.at[t % 2]
        gbuf[0:PAD, :] = tail_ref[:, cols]
        gbuf[PAD:, :] = gte
        tail_ref[:, cols] = gte[RC - PAD:, :]
        conv = (gte * cw_ref[2:3, cols]
                + gbuf[PAD - 1:PAD - 1 + RC, :] * cw_ref[1:2, cols]
                + gbuf[PAD - 2:PAD - 2 + RC, :] * cw_ref[0:1, cols]
                + cb_ref[:, cols])
        act = jax.nn.gelu(conv) * up
        down = jnp.dot(act.astype(BF16), wd_ref[cols, :], preferred_element_type=F32)
        acc = down if fc == 0 else acc + down
        if fc == nfc - 1:
            o_ref[rows, :] = _layer_norm(alpha * h_ref[rows, :] + acc, g_ref[...], b_ref[...])


def _ffn(h, wg, wu, cw, cb, wd, g, b, alpha, tm=1024):
    B, S, D = h.shape
    d_ff = wg.shape[1]
    row = pl.BlockSpec((None, tm, D), lambda bi, s: (bi, s, 0))
    return pl.pallas_call(
        functools.partial(_ffn_kernel, alpha=alpha, tm=tm),
        grid=(B, S // tm),
        in_specs=[row, _const_spec(wg.shape), _const_spec(wu.shape), _const_spec(cw.shape),
                  _const_spec(cb.shape), _const_spec(wd.shape),
                  _const_spec(g.shape), _const_spec(b.shape)],
        out_specs=row,
        out_shape=jax.ShapeDtypeStruct((B, S, D), F32),
        scratch_shapes=[pltpu.VMEM((8, d_ff), F32),
                        pltpu.VMEM((2, ROW_CHUNK + 8, FFN_FC), F32)],
        compiler_params=_cparams(2),
        name="ffn_ln",
    )(h, wg, wu, cw, cb, wd, g, b)


def kernel(x, w_in, b_gate, lambda_q1, lambda_k1, lambda_q2, lambda_k2, attn_subln_w,
           rel_bias, hgrn_lb_logits, hgrn_norm_w, w_branch_attn, w_branch_hgrn, w_out,
           ln1_g, ln1_b, w_ffn_gate, w_ffn_up, ffn_conv_w, ffn_conv_b, w_ffn_down,
           ln2_g, ln2_b):
    B, S, D = x.shape
    depth = w_in.shape[0]
    att_qk = ATT_HEADS * 2 * ATT_DH
    att_w = ATT_HEADS * HEAD_W
    hg_w = HG_HEADS * HEAD_W
    offs = {"aq": 0, "ak": att_qk, "av": 2 * att_qk}
    offs["hq"] = offs["av"] + att_w
    offs["hf"] = offs["hq"] + hg_w
    offs["hi"] = offs["hf"] + hg_w
    offs["hg"] = offs["hi"] + hg_w
    offs["gate"] = offs["hg"] + hg_w
    alpha = (2 * depth) ** 0.25

    lb_all = jnp.cumsum(jax.nn.softmax(hgrn_lb_logits.astype(F32), axis=0), axis=0)

    h = x
    for l in range(depth):
        lam_init = 0.8 - 0.6 * math.exp(-0.3 * l)
        lam = (jnp.exp(jnp.sum(lambda_q1[l].astype(F32) * lambda_k1[l].astype(F32)))
               - jnp.exp(jnp.sum(lambda_q2[l].astype(F32) * lambda_k2[l].astype(F32)))
               + lam_init).reshape(1)

        w_mix_bf = w_in[l, :, :offs["gate"]].astype(BF16)
        w_gate_bf = w_in[l, :, offs["gate"]:].astype(BF16)
        q, k, vt, yh = _mix(h, w_mix_bf, lb_all[l][None, :],
                            hgrn_norm_w[l][None, :].astype(F32), offs)

        ya = _attention(q, k, vt, rel_bias.astype(F32),
                        attn_subln_w[l][:, None].astype(F32), lam, 1.0 - lam_init)

        h1 = _merge(ya.reshape(B * S, att_w), yh.reshape(B * S, hg_w), h.reshape(B * S, D),
                    w_branch_attn[l].astype(BF16), w_branch_hgrn[l].astype(BF16),
                    w_gate_bf, b_gate[l][None, :], w_out[l].astype(BF16),
                    ln1_g[l][None, :], ln1_b[l][None, :], alpha)

        h = _ffn(h1.reshape(B, S, D), w_ffn_gate[l].astype(BF16), w_ffn_up[l].astype(BF16),
                 ffn_conv_w[l], ffn_conv_b[l][None, :], w_ffn_down[l].astype(BF16),
                 ln2_g[l][None, :], ln2_b[l][None, :], alpha)
    return h
```

```python
import functools
import math

import numpy as np
import jax
import jax.numpy as jnp
from jax import lax
from jax.experimental import pallas as pl
from jax.experimental.pallas import tpu as pltpu

F32 = jnp.float32
BF16 = jnp.bfloat16

CHUNK = 64
ATT_HEADS = 4
ATT_DH = 64
HG_HEADS = 4
HEAD_W = 128
NUM_BUCKETS = 32
MAX_DISTANCE = 128
EPS = 1e-5
NEG = -1e30

ATT_BQ = 128
ATT_BK = 256
ATT_G = 4
ATT_GK = ATT_G * ATT_BQ // ATT_BK
ATT_NEAR = (0, -1, -2)
ATT_VR = HEAD_W + 16
LOG2E = math.log2(math.e)
HG_C = 128
HG_LEVELS = (64, 32, 16, 8, 4)
HG_DIAG = 4
FFN_FC = 256
ROW_CHUNK = 256

VMEM_LIMIT = 56 * 1024 * 1024


def _cparams(n_axes):
    return pltpu.CompilerParams(
        dimension_semantics=("arbitrary",) * n_axes,
        vmem_limit_bytes=VMEM_LIMIT)


def _const_spec(shape):
    nd = len(shape)
    return pl.BlockSpec(shape, lambda *_: (0,) * nd, pipeline_mode=pl.Buffered(1))


def _mix_kernel(x_ref, w_ref, lb_ref, nw_ref, q_ref, k_ref, vt_ref, yh_ref,
                state_ref, kbuf_ref, bbuf_ref, *, tm, offs):
    C, H, W, RC, PAD = HG_C, HG_HEADS, HEAD_W, ROW_CHUNK, 8

    @pl.when(pl.program_id(1) == 0)
    def _():
        state_ref[...] = jnp.zeros_like(state_ref)

    kbuf_ref[:, 0:PAD, :] = jnp.zeros((H, PAD, W), F32)
    bbuf_ref[:, 0:PAD, :] = jnp.zeros((H, PAD, W), F32)

    lb = lb_ref[...]
    row = lax.broadcasted_iota(jnp.int32, (C, W), 0)
    r2 = lax.broadcasted_iota(jnp.int32, (C, C), 0)
    c2 = lax.broadcasted_iota(jnp.int32, (C, C), 1)
    tri = (c2 <= r2).astype(BF16)
    is_q = {hs: ((row // hs) % 2) == 1 for hs in HG_LEVELS}
    sign = {hs: jnp.where(is_q[hs], 1.0, -1.0).astype(F32) for hs in HG_LEVELS}
    keep = {hs: ((r2 // (2 * hs)) == (c2 // (2 * hs)))
            & (((r2 // hs) % 2) == 1) & (((c2 // hs) % 2) == 0) for hs in HG_LEVELS}
    same_blk = (r2 // HG_DIAG) == (c2 // HG_DIAG)
    on_diag = {d: ((r2 - c2) == d) & same_blk for d in range(HG_DIAG)}
    heads = range(H)
    cols = [slice(h * W, (h + 1) * W) for h in heads]

    def level_ref(b, hs):
        if hs >= PAD:
            parts = [jnp.broadcast_to(b[g * 2 * hs + hs - 1:g * 2 * hs + hs, :], (2 * hs, W))
                     for g in range(C // (2 * hs))]
            return parts[0] if len(parts) == 1 else jnp.concatenate(parts, axis=0)
        b3 = b.reshape(C // (2 * hs), 2 * hs, W)
        return jnp.broadcast_to(b3[:, hs - 1:hs, :], b3.shape).reshape(C, W)

    def projection(r):
        rows = slice(r * RC, (r + 1) * RC)
        out = {}

        def mm(off):
            if "xb" not in out:
                out["xb"] = x_ref[rows, :].astype(BF16)
            return jnp.dot(out["xb"], w_ref[:, off:off + 512], preferred_element_type=F32)

        def sec_q():
            q_ref[rows, :] = (mm(offs["aq"]) * (ATT_DH ** -0.5 * LOG2E)).astype(BF16)

        def sec_k():
            k_ref[rows, :] = mm(offs["ak"]).astype(BF16)

        def sec_v():
            v = mm(offs["av"])
            for hd in range(ATT_HEADS):
                vt_ref[r, hd * ATT_VR:hd * ATT_VR + HEAD_W, :] = (
                    v[:, hd * HEAD_W:(hd + 1) * HEAD_W].T.astype(BF16))
                vt_ref[r, hd * ATT_VR + HEAD_W:(hd + 1) * ATT_VR, :] = jnp.ones(
                    (ATT_VR - HEAD_W, ATT_BK), BF16)

        def sec_hq():
            out["hq"] = mm(offs["hq"])

        def sec_hf():
            sig = jax.nn.sigmoid(mm(offs["hf"]))
            out["lf"] = jnp.log2(lb + (1.0 - lb) * sig)
            out["kk"] = (1.0 - lb) * (1.0 - sig)

        def sec_hi():
            out["hi"] = mm(offs["hi"]).astype(BF16)

        def sec_hg():
            hog = mm(offs["hg"])
            out["sog"] = hog * jax.nn.sigmoid(hog)

        return [sec_hq, sec_hf, sec_hi, sec_hg, sec_q, sec_k, sec_v], out

    def recurrence(r, vals):
        pieces = []
        for sub in range(RC // C):
            sl = slice(sub * C, (sub + 1) * C)
            t0 = r * RC + sub * C
            st = {}
            s0 = 3 * sub

            def cumsum_prep(sl=sl, st=st):
                lf = vals["lf"][sl]
                st["p0"] = lf.astype(BF16)
                st["p1"] = (lf - st["p0"].astype(F32)).astype(BF16)

            def cumsum_fire(st=st):
                b_all = (jnp.dot(tri, st["p0"], preferred_element_type=F32)
                         + jnp.dot(tri, st["p1"], preferred_element_type=F32))
                st["b"] = [b_all[:, cols[h]] for h in heads]

            def inter_prep(sl=sl, st=st):
                st["qf"] = [vals["hq"][sl, cols[h]] for h in heads]
                st["kf"] = [vals["kk"][sl, cols[h]] for h in heads]
                st["iv"] = [vals["hi"][sl, cols[h]] for h in heads]
                st["st"] = [state_ref[h] for h in heads]
                st["qt"] = [(st["qf"][h] * jnp.exp2(st["b"][h])).astype(BF16) for h in heads]
                st["stb"] = [st["st"][h].astype(BF16) for h in heads]
                st["a"] = [None] * H

            def inter_fire(st=st):
                st["inter"] = [lax.dot_general(st["qt"][h], st["stb"][h],
                                               (((1,), (1,)), ((), ())),
                                               preferred_element_type=F32) for h in heads]

            def level_prep(hs, st=st):
                ys = []
                for h in heads:
                    b = st["b"][h]
                    e = jnp.exp2((b - level_ref(b, hs)) * sign[hs])
                    ys.append((jnp.where(is_q[hs], st["qf"][h], st["kf"][h]) * e).astype(BF16))
                st["y", hs] = ys

            def level_fire(hs, st=st):
                for h in heads:
                    y = st["y", hs][h]
                    sc = lax.dot_general(y, y, (((1,), (1,)), ((), ())),
                                         preferred_element_type=F32)
                    sc = jnp.where(keep[hs], sc, 0.0)
                    st["a"][h] = sc if st["a"][h] is None else st["a"][h] + sc

            def diagonals(st=st):
                for h in heads:
                    kbuf_ref[h, PAD:, :] = st["kf"][h]
                    bbuf_ref[h, PAD:, :] = st["b"][h]
                diag = [None] * H
                for d in range(HG_DIAG):
                    for h in heads:
                        if d == 0:
                            term = st["qf"][h] * st["kf"][h]
                        else:
                            ks = kbuf_ref[h, PAD - d:PAD - d + C, :]
                            bs = bbuf_ref[h, PAD - d:PAD - d + C, :]
                            term = st["qf"][h] * ks * jnp.exp2(st["b"][h] - bs)
                        dsum = jnp.sum(term, axis=-1, keepdims=True)
                        placed = jnp.where(on_diag[d], dsum, 0.0)
                        diag[h] = placed if diag[h] is None else diag[h] + placed
                st["diag"] = diag

            def output_prep(st=st):
                st["ab"] = [(st["a"][h] + st["diag"][h]).astype(BF16) for h in heads]

            def output_fire(st=st):
                st["o"] = [st["inter"][h] + jnp.dot(st["ab"][h], st["iv"][h],
                                                    preferred_element_type=F32)
                           for h in heads]

            def update_prep(st=st):
                st["b_last"] = [st["b"][h][C - 1:C, :] for h in heads]
                st["k_end"] = [(st["kf"][h] * jnp.exp2(st["b_last"][h] - st["b"][h])
                                ).astype(BF16) for h in heads]

            def update_fire(st=st):
                for h in heads:
                    upd = lax.dot_general(st["iv"][h], st["k_end"][h],
                                          (((0,), (0,)), ((), ())),
                                          preferred_element_type=F32)
                    state_ref[h] = st["st"][h] * jnp.exp2(st["b_last"][h]) + upd

            def emit(sl=sl, st=st, t0=t0):
                for h in heads:
                    o = st["o"][h]
                    ms = jnp.mean(o * o, axis=-1, keepdims=True)
                    y = o * lax.rsqrt(ms + EPS) * nw_ref[...] * vals["sog"][sl, cols[h]]
                    yh_ref[t0:t0 + C, cols[h]] = y.astype(BF16)

            levels = [(functools.partial(level_prep, hs), functools.partial(level_fire, hs))
                      for hs in HG_LEVELS]
            pieces.append((0, cumsum_prep, cumsum_fire))
            pieces.append((s0 + 1, inter_prep, inter_fire))
            pieces += [(s0 + 1, p, f) for p, f in levels[:2]]
            pieces += [(s0 + 2, p, f) for p, f in levels[2:]]
            pieces.append((s0 + 3, update_prep, update_fire))
            pieces.append((s0 + 3, diagonals, None))
            pieces.append((s0 + 4, output_prep, output_fire))
            pieces.append((s0 + 5, emit, None))
        return pieces

    N_RECURRENCE_INPUTS = 4
    n_chunks = tm // RC
    sections = []
    schedule = {}
    for r in range(n_chunks):
        proj, vals = projection(r)
        sections += proj
        base = r * len(proj) + N_RECURRENCE_INPUTS
        for slot, prep, fire in recurrence(r, vals):
            schedule.setdefault(base + slot, []).append((prep, fire))
    for t in range(max(schedule) + 2):
        for _, fire in schedule.get(t - 1, []):
            if fire is not None:
                fire()
        if t < len(sections):
            sections[t]()
        for prep, _ in schedule.get(t, []):
            prep()


def _mix(x, w_bf, lb, norm_w, offs, tm=1024):
    B, S, D = x.shape
    assert ROW_CHUNK == ATT_BK and ROW_CHUNK % HG_C == 0
    row = lambda w: pl.BlockSpec((None, tm, w), lambda b, s: (b, s, 0))
    out_shape = (
        jax.ShapeDtypeStruct((B, S, 512), BF16),
        jax.ShapeDtypeStruct((B, S, 512), BF16),
        jax.ShapeDtypeStruct((B, S // ATT_BK, ATT_HEADS * ATT_VR, ATT_BK), BF16),
        jax.ShapeDtypeStruct((B, S, HG_HEADS * HEAD_W), BF16),
    )
    out_specs = (
        row(512), row(512),
        pl.BlockSpec((None, tm // ATT_BK, ATT_HEADS * ATT_VR, ATT_BK),
                     lambda b, s: (b, s, 0, 0)),
        row(HG_HEADS * HEAD_W),
    )
    return pl.pallas_call(
        functools.partial(_mix_kernel, tm=tm, offs=offs),
        grid=(B, S // tm),
        in_specs=[row(D), _const_spec(w_bf.shape), _const_spec(lb.shape),
                  _const_spec(norm_w.shape)],
        out_specs=out_specs,
        out_shape=out_shape,
        scratch_shapes=[pltpu.VMEM((HG_HEADS, HEAD_W, HEAD_W), F32),
                        pltpu.VMEM((HG_HEADS, HG_C + 8, HEAD_W), F32),
                        pltpu.VMEM((HG_HEADS, HG_C + 8, HEAD_W), F32)],
        compiler_params=_cparams(2),
        name="inproj_hgrn2",
    )(x, w_bf, lb, norm_w)


def _rel_bucket_np(rel):
    nb = NUM_BUCKETS // 2
    max_exact = nb // 2
    ret = np.where(rel > 0, nb, 0)
    n = np.abs(rel)
    nf = np.maximum(n, 1).astype(np.float32)
    large = max_exact + (np.log(nf / np.float32(max_exact))
                         / np.float32(math.log(MAX_DISTANCE / max_exact))
                         * np.float32(nb - max_exact)).astype(np.int32)
    large = np.minimum(large, nb - 1)
    return ret + np.where(n < max_exact, n, large)


def _attn_bucket_tiles():
    i = np.arange(ATT_BQ)[None, :]
    j = np.arange(ATT_BK)[:, None]
    tiles = []
    for d in ATT_NEAR:
        kpos = d * ATT_BQ + j
        live = (kpos // CHUNK) <= (i // CHUNK)
        tiles.append(np.where(live, _rel_bucket_np(kpos - i), -1))
    far = _rel_bucket_np(np.asarray((min(ATT_NEAR) - 1) * ATT_BQ + ATT_BK - 1))
    assert int(far) == NUM_BUCKETS // 2 - 1
    return np.stack(tiles).astype(np.int32)


def _attn_kernel(lam_ref, rb_ref, bkt_ref, q_ref, k_ref, vt_ref, w_ref, o_ref,
                 bias_ref, qs_ref, s0_ref, s1_ref, p0_ref, p1_ref, a0_ref, a1_ref,
                 m_ref, acc_ref, *, out_scale):
    S = q_ref.shape[0]
    nkb = S // ATT_BK
    W2 = 2 * ATT_BQ
    near = tuple(range(ATT_GK)) + (-1,)
    lam = lam_ref[0]
    h = pl.program_id(0)

    @pl.when(pl.program_id(1) == 0)
    def _():
        far = rb_ref[NUM_BUCKETS // 2 - 1, h]
        for n, kbrel in enumerate(near):
            for j in range(ATT_G):
                d = (kbrel * ATT_BK) // ATT_BQ - j
                if d > max(ATT_NEAR):
                    bias_ref[n, j] = jnp.full((ATT_BK, W2), NEG, F32)
                elif d < min(ATT_NEAR):
                    bias_ref[n, j] = jnp.zeros((ATT_BK, W2), F32)
        for t, dt in enumerate(ATT_NEAR):
            bk = bkt_ref[t]
            tile = jnp.full(bk.shape, NEG, F32)
            for bucket in range(NUM_BUCKETS):
                tile = jnp.where(bk == bucket, (rb_ref[bucket, h] - far) * LOG2E, tile)
            tile2 = jnp.concatenate([tile, tile], axis=1)
            for n, kbrel in enumerate(near):
                for j in range(ATT_G):
                    if (kbrel * ATT_BK) // ATT_BQ - j == dt:
                        bias_ref[n, j] = tile2

    lane = lax.broadcasted_iota(jnp.int32, (ATT_BQ, HEAD_W), 1)
    first_map = lane < ATT_DH

    s_bufs, p_bufs, a_bufs = (s0_ref, s1_ref), (p0_ref, p1_ref), (a0_ref, a1_ref)

    def scores(step, j, buf):
        kb, bias_idx = step
        kblk = k_ref[pl.ds(pl.multiple_of(kb * ATT_BK, ATT_BK), ATT_BK), :]
        s = lax.dot_general(kblk, qs_ref[j], (((1,), (1,)), ((), ())),
                            preferred_element_type=F32)
        if bias_idx is not None:
            s = s + bias_ref[bias_idx, j]
        s_bufs[buf][j] = s

    def soft(j, buf, first=False):
        s = s_bufs[buf][j]
        m_new = jnp.max(s, axis=0, keepdims=True)
        if first:
            a_bufs[buf][j] = jnp.ones_like(m_new)
        else:
            m_old = m_ref[j]
            m_new = jnp.maximum(m_old, m_new)
            a_bufs[buf][j] = jnp.exp2(m_old - m_new)
        m_ref[j] = m_new
        p_bufs[buf][j] = jnp.exp2(s - m_new).astype(BF16)

    def accum(kb, j, buf, first=False):
        pv = jnp.dot(vt_ref[kb], p_bufs[buf][j], preferred_element_type=F32)
        if first:
            acc_ref[j] = pv
        else:
            acc_ref[j] = a_bufs[buf][j] * acc_ref[j] + pv

    AHEAD = 4

    def two_steps(steps, first=False, prev_kb=None, dead=()):
        for k in range(2 * ATT_G):
            ua = k + AHEAD
            if (ua // ATT_G, ua % ATT_G) not in dead:
                scores(steps[ua // ATT_G], ua % ATT_G, (ua // ATT_G) % 2)
            uc = k - 1
            if k >= 1 and (uc // ATT_G, uc % ATT_G) not in dead:
                accum(steps[uc // ATT_G][0], uc % ATT_G, uc // ATT_G,
                      first=first and uc < ATT_G)
            elif k == 0 and prev_kb is not None:
                accum(prev_kb, ATT_G - 1, 1)
            if (k // ATT_G, k % ATT_G) not in dead:
                soft(k % ATT_G, k // ATT_G, first=first and k < ATT_G)

    def inside(g):
        return [(g * ATT_GK + i, i) for i in range(ATT_GK)]

    def before(g):
        return (jnp.maximum(g * ATT_GK - 1, 0), ATT_GK)

    def far(n):
        return (jnp.clip(n - (ATT_GK + 1), 0, nkb - 1), None)

    dead = {(i, j) for i in range(ATT_GK) for j in range(ATT_G)
            if (i * ATT_BK) // ATT_BQ - j > max(ATT_NEAR)}

    def open_group(g):
        q0 = pl.multiple_of(g * (ATT_G * ATT_BQ), ATT_G * ATT_BQ)
        for j in range(ATT_G):
            q = q_ref[pl.ds(q0 + j * ATT_BQ, ATT_BQ), :]
            zero = jnp.zeros_like(q)
            qs_ref[j, :ATT_BQ, :] = jnp.where(first_map, q, zero)
            qs_ref[j, ATT_BQ:, :] = jnp.where(first_map, zero, q)
        for j in range(AHEAD):
            scores(inside(g)[0], j, 0)

    def close_group(g, n_steps):
        last_kb = jnp.where(g > 0, far(n_steps - 1)[0], inside(g)[1][0])
        accum(last_kb, ATT_G - 1, 1)

    def emit_group(g):
        q0 = pl.multiple_of(g * (ATT_G * ATT_BQ), ATT_G * ATT_BQ)
        for j in range(ATT_G):
            on = acc_ref[j, :HEAD_W, :] * (1.0 / acc_ref[j, HEAD_W:HEAD_W + 1, :])
            o = on[:, :ATT_BQ] - lam * on[:, ATT_BQ:]
            ms = jnp.mean(o * o, axis=0, keepdims=True)
            y = o * lax.rsqrt(ms + EPS) * w_ref[...] * out_scale
            o_ref[pl.ds(q0 + j * ATT_BQ, ATT_BQ), :] = y.T.astype(BF16)

    n_groups = S // (ATT_G * ATT_BQ)
    open_group(0)
    two_steps(inside(0) + [before(0)], first=True, dead=dead)

    def group(g, carry):
        n_steps = jnp.where(g > 0, ATT_GK + 1 + (g * ATT_GK - 1), ATT_GK)

        def far_steps(n):
            two_steps([far(n), far(n + 1), far(n + 2)], prev_kb=far(n - 1)[0])

        n_far_pairs = jnp.maximum(n_steps // 2 - 2, 0)
        first_far = ATT_GK + 2

        def far_block(n, pairs):
            for i in range(pairs):
                far_steps(n + 2 * i)

        def six_far_steps(i, c):
            far_block(first_far + 6 * i, 3)
            return c

        n_six = n_far_pairs // 3
        lax.fori_loop(0, n_six, six_far_steps, 0)
        for left in (1, 2):
            @pl.when(n_far_pairs - 3 * n_six == left)
            def _(left=left):
                far_block(first_far + 6 * n_six, left)

        @pl.when(g + 1 < n_groups)
        def _():
            close_group(g, n_steps)
            open_group(g + 1)
            emit_group(g)
            two_steps(inside(g + 1) + [before(g + 1)], first=True, dead=dead)
            two_steps([before(g + 1), far(3), far(4)], prev_kb=inside(g + 1)[1][0])

        @pl.when(g + 1 == n_groups)
        def _():
            close_group(g, n_steps)
            emit_group(g)

        return carry

    lax.fori_loop(0, n_groups, group, 0)


def _attention(q, k, vt, rel_bias, subln_w, lam, out_scale):
    B, S, _ = q.shape
    nkb = S // ATT_BK
    W2 = 2 * ATT_BQ
    assert ATT_GK == 2 and ATT_BK == 2 * ATT_BQ
    bkt = jnp.asarray(_attn_bucket_tiles())
    head = pl.BlockSpec((None, S, HEAD_W), lambda h, b: (b, 0, h))
    smem = pl.BlockSpec(memory_space=pltpu.SMEM)
    return pl.pallas_call(
        functools.partial(_attn_kernel, out_scale=out_scale),
        grid=(ATT_HEADS, B),
        in_specs=[
            smem, smem, _const_spec(bkt.shape),
            head, head,
            pl.BlockSpec((None, nkb, ATT_VR, ATT_BK), lambda h, b: (b, 0, h, 0)),
            pl.BlockSpec((HEAD_W, 1), lambda h, b: (0, 0)),
        ],
        out_specs=head,
        out_shape=jax.ShapeDtypeStruct((B, S, ATT_HEADS * HEAD_W), BF16),
        scratch_shapes=[
            pltpu.VMEM((ATT_GK + 1, ATT_G, ATT_BK, W2), F32),
            pltpu.VMEM((ATT_G, W2, HEAD_W), BF16),
            pltpu.VMEM((ATT_G, ATT_BK, W2), F32),
            pltpu.VMEM((ATT_G, ATT_BK, W2), F32),
            pltpu.VMEM((ATT_G, ATT_BK, W2), BF16),
            pltpu.VMEM((ATT_G, ATT_BK, W2), BF16),
            pltpu.VMEM((ATT_G, 1, W2), F32),
            pltpu.VMEM((ATT_G, 1, W2), F32),
            pltpu.VMEM((ATT_G, 1, W2), F32),
            pltpu.VMEM((ATT_G, ATT_VR, W2), F32),
        ],
        compiler_params=_cparams(2),
        name="diffattn",
    )(lam, rel_bias, bkt, q, k, vt, subln_w)


def _layer_norm(z, g, b):
    mu = jnp.mean(z, axis=-1, keepdims=True)
    zc = z - mu
    var = jnp.mean(zc * zc, axis=-1, keepdims=True)
    return zc * lax.rsqrt(var + EPS) * g + b


def _merge_kernel(ya_ref, yh_ref, x_ref, wa_ref, wh_ref, wg_ref, bg_ref, wo_ref,
                  g_ref, b_ref, o_ref, *, alpha):
    D = x_ref.shape[-1]
    tm = x_ref.shape[0]

    def branches(r):
        rows = slice(r * ROW_CHUNK, (r + 1) * ROW_CHUNK)
        xb = x_ref[rows, :].astype(BF16)
        pa = jnp.dot(ya_ref[rows, :], wa_ref[...], preferred_element_type=F32)
        ph = jnp.dot(yh_ref[rows, :], wh_ref[...], preferred_element_type=F32)
        ga = jnp.dot(xb, wg_ref[:, :D], preferred_element_type=F32)
        gh = jnp.dot(xb, wg_ref[:, D:], preferred_element_type=F32)
        return pa, ph, ga, gh

    n_chunks = tm // ROW_CHUNK
    nxt = branches(0)
    for r in range(n_chunks):
        rows = slice(r * ROW_CHUNK, (r + 1) * ROW_CHUNK)
        pa, ph, ga, gh = nxt
        if r + 1 < n_chunks:
            nxt = branches(r + 1)
        merged = (jax.nn.sigmoid(ga + bg_ref[:, :D]) * pa
                  + jax.nn.sigmoid(gh + bg_ref[:, D:]) * ph)
        mix = jnp.dot(merged.astype(BF16), wo_ref[...], preferred_element_type=F32)
        o_ref[rows, :] = _layer_norm(alpha * x_ref[rows, :] + mix, g_ref[...], b_ref[...])


def _merge(ya, yh, x, wa, wh, wg, bg, wo, g, b, alpha, tm=1024):
    N, D = x.shape
    row = lambda w: pl.BlockSpec((tm, w), lambda i: (i, 0))
    return pl.pallas_call(
        functools.partial(_merge_kernel, alpha=alpha),
        grid=(N // tm,),
        in_specs=[row(ya.shape[1]), row(yh.shape[1]), row(D),
                  _const_spec(wa.shape), _const_spec(wh.shape), _const_spec(wg.shape),
                  _const_spec(bg.shape), _const_spec(wo.shape),
                  _const_spec(g.shape), _const_spec(b.shape)],
        out_specs=row(D),
        out_shape=jax.ShapeDtypeStruct((N, D), F32),
        compiler_params=_cparams(1),
        name="merge_ln",
    )(ya, yh, x, wa, wh, wg, bg, wo, g, b)


def _ffn_kernel(h_ref, wg_ref, wu_ref, cw_ref, cb_ref, wd_ref, g_ref, b_ref, o_ref,
                tail_ref, gbuf_ref, *, alpha, tm):
    d_ff = wg_ref.shape[1]
    PAD = 8

    @pl.when(pl.program_id(1) == 0)
    def _():
        tail_ref[...] = jnp.zeros_like(tail_ref)

    RC = ROW_CHUNK
    nfc = d_ff // FFN_FC
    steps = [(r, fc) for r in range(tm // RC) for fc in range(nfc)]
    hb = {}

    def gate_up(r, fc):
        if r not in hb:
            hb[r] = h_ref[r * RC:(r + 1) * RC, :].astype(BF16)
        cols = slice(fc * FFN_FC, (fc + 1) * FFN_FC)
        return (jnp.dot(hb[r], wg_ref[:, cols], preferred_element_type=F32),
                jnp.dot(hb[r], wu_ref[:, cols], preferred_element_type=F32))

    nxt = gate_up(*steps[0])
    acc = None
    for t, (r, fc) in enumerate(steps):
        rows = slice(r * RC, (r + 1) * RC)
        cols = slice(fc * FFN_FC, (fc + 1) * FFN_FC)
        gte, up = nxt
        if t + 1 < len(steps):
            nxt = gate_up(*steps[t + 1])
        gbuf = gbuf_ref.at[t % 2]
        gbuf[0:PAD, :] = tail_ref[:, cols]
        gbuf[PAD:, :] = gte
        tail_ref[:, cols] = gte[RC - PAD:, :]
        conv = (gte * cw_ref[2:3, cols]
                + gbuf[PAD - 1:PAD - 1 + RC, :] * cw_ref[1:2, cols]
                + gbuf[PAD - 2:PAD - 2 + RC, :] * cw_ref[0:1, cols]
                + cb_ref[:, cols])
        act = jax.nn.gelu(conv) * up
        down = jnp.dot(act.astype(BF16), wd_ref[cols, :], preferred_element_type=F32)
        acc = down if fc == 0 else acc + down
        if fc == nfc - 1:
            o_ref[rows, :] = _layer_norm(alpha * h_ref[rows, :] + acc, g_ref[...], b_ref[...])


def _ffn(h, wg, wu, cw, cb, wd, g, b, alpha, tm=1024):
    B, S, D = h.shape
    d_ff = wg.shape[1]
    row = pl.BlockSpec((None, tm, D), lambda bi, s: (bi, s, 0))
    return pl.pallas_call(
        functools.partial(_ffn_kernel, alpha=alpha, tm=tm),
        grid=(B, S // tm),
        in_specs=[row, _const_spec(wg.shape), _const_spec(wu.shape), _const_spec(cw.shape),
                  _const_spec(cb.shape), _const_spec(wd.shape),
                  _const_spec(g.shape), _const_spec(b.shape)],
        out_specs=row,
        out_shape=jax.ShapeDtypeStruct((B, S, D), F32),
        scratch_shapes=[pltpu.VMEM((8, d_ff), F32),
                        pltpu.VMEM((2, ROW_CHUNK + 8, FFN_FC), F32)],
        compiler_params=_cparams(2),
        name="ffn_ln",
    )(h, wg, wu, cw, cb, wd, g, b)


def kernel(x, w_in, b_gate, lambda_q1, lambda_k1, lambda_q2, lambda_k2, attn_subln_w,
           rel_bias, hgrn_lb_logits, hgrn_norm_w, w_branch_attn, w_branch_hgrn, w_out,
           ln1_g, ln1_b, w_ffn_gate, w_ffn_up, ffn_conv_w, ffn_conv_b, w_ffn_down,
           ln2_g, ln2_b):
    B, S, D = x.shape
    depth = w_in.shape[0]
    att_qk = ATT_HEADS * 2 * ATT_DH
    att_w = ATT_HEADS * HEAD_W
    hg_w = HG_HEADS * HEAD_W
    offs = {"aq": 0, "ak": att_qk, "av": 2 * att_qk}
    offs["hq"] = offs["av"] + att_w
    offs["hf"] = offs["hq"] + hg_w
    offs["hi"] = offs["hf"] + hg_w
    offs["hg"] = offs["hi"] + hg_w
    offs["gate"] = offs["hg"] + hg_w
    alpha = (2 * depth) ** 0.25

    lb_all = jnp.cumsum(jax.nn.softmax(hgrn_lb_logits.astype(F32), axis=0), axis=0)

    h = x
    for l in range(depth):
        lam_init = 0.8 - 0.6 * math.exp(-0.3 * l)
        lam = (jnp.exp(jnp.sum(lambda_q1[l].astype(F32) * lambda_k1[l].astype(F32)))
               - jnp.exp(jnp.sum(lambda_q2[l].astype(F32) * lambda_k2[l].astype(F32)))
               + lam_init).reshape(1)

        w_mix_bf = w_in[l, :, :offs["gate"]].astype(BF16)
        w_gate_bf = w_in[l, :, offs["gate"]:].astype(BF16)
        q, k, vt, yh = _mix(h, w_mix_bf, lb_all[l][None, :],
                            hgrn_norm_w[l][None, :].astype(F32), offs)

        ya = _attention(q, k, vt, rel_bias.astype(F32),
                        attn_subln_w[l][:, None].astype(F32), lam, 1.0 - lam_init)

        h1 = _merge(ya.reshape(B * S, att_w), yh.reshape(B * S, hg_w), h.reshape(B * S, D),
                    w_branch_attn[l].astype(BF16), w_branch_hgrn[l].astype(BF16),
                    w_gate_bf, b_gate[l][None, :], w_out[l].astype(BF16),
                    ln1_g[l][None, :], ln1_b[l][None, :], alpha)

        h = _ffn(h1.reshape(B, S, D), w_ffn_gate[l].astype(BF16), w_ffn_up[l].astype(BF16),
                 ffn_conv_w[l], ffn_conv_b[l][None, :], w_ffn_down[l].astype(BF16),
                 ln2_g[l][None, :], ln2_b[l][None, :], alpha)
    return h
```

```python
import functools
import math

import numpy as np
import jax
import jax.numpy as jnp
from jax import lax
from jax.experimental import pallas as pl
from jax.experimental.pallas import tpu as pltpu

F32 = jnp.float32
BF16 = jnp.bfloat16

CHUNK = 64
ATT_HEADS = 4
ATT_DH = 64
HG_HEADS = 4
HEAD_W = 128
NUM_BUCKETS = 32
MAX_DISTANCE = 128
EPS = 1e-5
NEG = -1e30

ATT_BQ = 128
ATT_BK = 256
ATT_G = 4
ATT_GK = ATT_G * ATT_BQ // ATT_BK
ATT_NEAR = (0, -1, -2)
ATT_VR = HEAD_W + 16
LOG2E = math.log2(math.e)
HG_C = 128
HG_LEVELS = (64, 32, 16, 8, 4)
HG_DIAG = 4
FFN_FC = 256
ROW_CHUNK = 256

VMEM_LIMIT = 56 * 1024 * 1024


def _cparams(n_axes):
    return pltpu.CompilerParams(
        dimension_semantics=("arbitrary",) * n_axes,
        vmem_limit_bytes=VMEM_LIMIT)


def _const_spec(shape):
    nd = len(shape)
    return pl.BlockSpec(shape, lambda *_: (0,) * nd, pipeline_mode=pl.Buffered(1))


def _mix_kernel(x_ref, w_ref, lb_ref, nw_ref, q_ref, k_ref, vt_ref, yh_ref,
                state_ref, kbuf_ref, bbuf_ref, *, tm, offs):
    C, H, W, RC, PAD = HG_C, HG_HEADS, HEAD_W, ROW_CHUNK, 8

    @pl.when(pl.program_id(1) == 0)
    def _():
        state_ref[...] = jnp.zeros_like(state_ref)

    kbuf_ref[:, 0:PAD, :] = jnp.zeros((H, PAD, W), F32)
    bbuf_ref[:, 0:PAD, :] = jnp.zeros((H, PAD, W), F32)

    lb = lb_ref[...]
    row = lax.broadcasted_iota(jnp.int32, (C, W), 0)
    r2 = lax.broadcasted_iota(jnp.int32, (C, C), 0)
    c2 = lax.broadcasted_iota(jnp.int32, (C, C), 1)
    tri = (c2 <= r2).astype(BF16)
    is_q = {hs: ((row // hs) % 2) == 1 for hs in HG_LEVELS}
    sign = {hs: jnp.where(is_q[hs], 1.0, -1.0).astype(F32) for hs in HG_LEVELS}
    keep = {hs: ((r2 // (2 * hs)) == (c2 // (2 * hs)))
            & (((r2 // hs) % 2) == 1) & (((c2 // hs) % 2) == 0) for hs in HG_LEVELS}
    same_blk = (r2 // HG_DIAG) == (c2 // HG_DIAG)
    on_diag = {d: ((r2 - c2) == d) & same_blk for d in range(HG_DIAG)}
    heads = range(H)
    cols = [slice(h * W, (h + 1) * W) for h in heads]

    def level_ref(b, hs):
        if hs >= PAD:
            parts = [jnp.broadcast_to(b[g * 2 * hs + hs - 1:g * 2 * hs + hs, :], (2 * hs, W))
                     for g in range(C // (2 * hs))]
            return parts[0] if len(parts) == 1 else jnp.concatenate(parts, axis=0)
        b3 = b.reshape(C // (2 * hs), 2 * hs, W)
        return jnp.broadcast_to(b3[:, hs - 1:hs, :], b3.shape).reshape(C, W)

    def projection(r):
        rows = slice(r * RC, (r + 1) * RC)
        out = {}

        def mm(off):
            return jnp.dot(out["xb"], w_ref[:, off:off + 512], preferred_element_type=F32)

        def sec_q():
            out["xb"] = x_ref[rows, :].astype(BF16)
            q_ref[rows, :] = (mm(offs["aq"]) * (ATT_DH ** -0.5 * LOG2E)).astype(BF16)

        def sec_k():
            k_ref[rows, :] = mm(offs["ak"]).astype(BF16)

        def sec_v():
            v = mm(offs["av"])
            for hd in range(ATT_HEADS):
                vt_ref[r, hd * ATT_VR:hd * ATT_VR + HEAD_W, :] = (
                    v[:, hd * HEAD_W:(hd + 1) * HEAD_W].T.astype(BF16))
                vt_ref[r, hd * ATT_VR + HEAD_W:(hd + 1) * ATT_VR, :] = jnp.ones(
                    (ATT_VR - HEAD_W, ATT_BK), BF16)

        def sec_hq():
            out["hq"] = mm(offs["hq"])

        def sec_hf():
            sig = jax.nn.sigmoid(mm(offs["hf"]))
            out["lf"] = jnp.log2(lb + (1.0 - lb) * sig)
            out["kk"] = (1.0 - lb) * (1.0 - sig)

        def sec_hi():
            out["hi"] = mm(offs["hi"]).astype(BF16)

        def sec_hg():
            hog = mm(offs["hg"])
            out["sog"] = hog * jax.nn.sigmoid(hog)

        return [sec_q, sec_k, sec_v, sec_hq, sec_hf, sec_hi, sec_hg], out

    def recurrence(r, vals):
        pieces = []
        for sub in range(RC // C):
            sl = slice(sub * C, (sub + 1) * C)
            t0 = r * RC + sub * C
            st = {}
            s0 = 3 * sub

            def cumsum_prep(sl=sl, st=st):
                lf = vals["lf"][sl]
                st["p0"] = lf.astype(BF16)
                st["p1"] = (lf - st["p0"].astype(F32)).astype(BF16)

            def cumsum_fire(st=st):
                b_all = (jnp.dot(tri, st["p0"], preferred_element_type=F32)
                         + jnp.dot(tri, st["p1"], preferred_element_type=F32))
                st["b"] = [b_all[:, cols[h]] for h in heads]

            def inter_prep(sl=sl, st=st):
                st["qf"] = [vals["hq"][sl, cols[h]] for h in heads]
                st["kf"] = [vals["kk"][sl, cols[h]] for h in heads]
                st["iv"] = [vals["hi"][sl, cols[h]] for h in heads]
                st["st"] = [state_ref[h] for h in heads]
                st["qt"] = [(st["qf"][h] * jnp.exp2(st["b"][h])).astype(BF16) for h in heads]
                st["stb"] = [st["st"][h].astype(BF16) for h in heads]
                st["a"] = [None] * H

            def inter_fire(st=st):
                st["inter"] = [lax.dot_general(st["qt"][h], st["stb"][h],
                                               (((1,), (1,)), ((), ())),
                                               preferred_element_type=F32) for h in heads]

            def level_prep(hs, st=st):
                ys = []
                for h in heads:
                    b = st["b"][h]
                    e = jnp.exp2((b - level_ref(b, hs)) * sign[hs])
                    ys.append((jnp.where(is_q[hs], st["qf"][h], st["kf"][h]) * e).astype(BF16))
                st["y", hs] = ys

            def level_fire(hs, st=st):
                for h in heads:
                    y = st["y", hs][h]
                    sc = lax.dot_general(y, y, (((1,), (1,)), ((), ())),
                                         preferred_element_type=F32)
                    sc = jnp.where(keep[hs], sc, 0.0)
                    st["a"][h] = sc if st["a"][h] is None else st["a"][h] + sc

            def diagonals(st=st):
                for h in heads:
                    kbuf_ref[h, PAD:, :] = st["kf"][h]
                    bbuf_ref[h, PAD:, :] = st["b"][h]
                diag = [None] * H
                for d in range(HG_DIAG):
                    for h in heads:
                        if d == 0:
                            term = st["qf"][h] * st["kf"][h]
                        else:
                            ks = kbuf_ref[h, PAD - d:PAD - d + C, :]
                            bs = bbuf_ref[h, PAD - d:PAD - d + C, :]
                            term = st["qf"][h] * ks * jnp.exp2(st["b"][h] - bs)
                        dsum = jnp.sum(term, axis=-1, keepdims=True)
                        placed = jnp.where(on_diag[d], dsum, 0.0)
                        diag[h] = placed if diag[h] is None else diag[h] + placed
                st["diag"] = diag

            def output_prep(st=st):
                st["ab"] = [(st["a"][h] + st["diag"][h]).astype(BF16) for h in heads]

            def output_fire(st=st):
                st["o"] = [st["inter"][h] + jnp.dot(st["ab"][h], st["iv"][h],
                                                    preferred_element_type=F32)
                           for h in heads]

            def update_prep(st=st):
                st["b_last"] = [st["b"][h][C - 1:C, :] for h in heads]
                st["k_end"] = [(st["kf"][h] * jnp.exp2(st["b_last"][h] - st["b"][h])
                                ).astype(BF16) for h in heads]

            def update_fire(st=st):
                for h in heads:
                    upd = lax.dot_general(st["iv"][h], st["k_end"][h],
                                          (((0,), (0,)), ((), ())),
                                          preferred_element_type=F32)
                    state_ref[h] = st["st"][h] * jnp.exp2(st["b_last"][h]) + upd

            def emit(sl=sl, st=st, t0=t0):
                for h in heads:
                    o = st["o"][h]
                    ms = jnp.mean(o * o, axis=-1, keepdims=True)
                    y = o * lax.rsqrt(ms + EPS) * nw_ref[...] * vals["sog"][sl, cols[h]]
                    yh_ref[t0:t0 + C, cols[h]] = y.astype(BF16)

            levels = [(functools.partial(level_prep, hs), functools.partial(level_fire, hs))
                      for hs in HG_LEVELS]
            pieces.append((0, cumsum_prep, cumsum_fire))
            pieces.append((s0 + 1, inter_prep, inter_fire))
            pieces += [(s0 + 1, p, f) for p, f in levels[:2]]
            pieces += [(s0 + 2, p, f) for p, f in levels[2:]]
            pieces.append((s0 + 3, update_prep, update_fire))
            pieces.append((s0 + 3, diagonals, None))
            pieces.append((s0 + 4, output_prep, output_fire))
            pieces.append((s0 + 5, emit, None))
        return pieces

    n_chunks = tm // RC
    sections = []
    schedule = {}
    for r in range(n_chunks):
        proj, vals = projection(r)
        sections += proj
        base = (r + 1) * len(proj)
        for slot, prep, fire in recurrence(r, vals):
            schedule.setdefault(base + slot, []).append((prep, fire))
    for t in range(max(schedule) + 2):
        for _, fire in schedule.get(t - 1, []):
            if fire is not None:
                fire()
        if t < len(sections):
            sections[t]()
        for prep, _ in schedule.get(t, []):
            prep()


def _mix(x, w_bf, lb, norm_w, offs, tm=1024):
    B, S, D = x.shape
    assert ROW_CHUNK == ATT_BK and ROW_CHUNK % HG_C == 0
    row = lambda w: pl.BlockSpec((None, tm, w), lambda b, s: (b, s, 0))
    out_shape = (
        jax.ShapeDtypeStruct((B, S, 512), BF16),
        jax.ShapeDtypeStruct((B, S, 512), BF16),
        jax.ShapeDtypeStruct((B, S // ATT_BK, ATT_HEADS * ATT_VR, ATT_BK), BF16),
        jax.ShapeDtypeStruct((B, S, HG_HEADS * HEAD_W), BF16),
    )
    out_specs = (
        row(512), row(512),
        pl.BlockSpec((None, tm // ATT_BK, ATT_HEADS * ATT_VR, ATT_BK),
                     lambda b, s: (b, s, 0, 0)),
        row(HG_HEADS * HEAD_W),
    )
    return pl.pallas_call(
        functools.partial(_mix_kernel, tm=tm, offs=offs),
        grid=(B, S // tm),
        in_specs=[row(D), _const_spec(w_bf.shape), _const_spec(lb.shape),
                  _const_spec(norm_w.shape)],
        out_specs=out_specs,
        out_shape=out_shape,
        scratch_shapes=[pltpu.VMEM((HG_HEADS, HEAD_W, HEAD_W), F32),
                        pltpu.VMEM((HG_HEADS, HG_C + 8, HEAD_W), F32),
                        pltpu.VMEM((HG_HEADS, HG_C + 8, HEAD_W), F32)],
        compiler_params=_cparams(2),
        name="inproj_hgrn2",
    )(x, w_bf, lb, norm_w)


def _rel_bucket_np(rel):
    nb = NUM_BUCKETS // 2
    max_exact = nb // 2
    ret = np.where(rel > 0, nb, 0)
    n = np.abs(rel)
    nf = np.maximum(n, 1).astype(np.float32)
    large = max_exact + (np.log(nf / np.float32(max_exact))
                         / np.float32(math.log(MAX_DISTANCE / max_exact))
                         * np.float32(nb - max_exact)).astype(np.int32)
    large = np.minimum(large, nb - 1)
    return ret + np.where(n < max_exact, n, large)


def _attn_bucket_tiles():
    i = np.arange(ATT_BQ)[None, :]
    j = np.arange(ATT_BK)[:, None]
    tiles = []
    for d in ATT_NEAR:
        kpos = d * ATT_BQ + j
        live = (kpos // CHUNK) <= (i // CHUNK)
        tiles.append(np.where(live, _rel_bucket_np(kpos - i), -1))
    far = _rel_bucket_np(np.asarray((min(ATT_NEAR) - 1) * ATT_BQ + ATT_BK - 1))
    assert int(far) == NUM_BUCKETS // 2 - 1
    return np.stack(tiles).astype(np.int32)


def _attn_kernel(lam_ref, rb_ref, bkt_ref, q_ref, k_ref, vt_ref, w_ref, o_ref,
                 bias_ref, qs_ref, s0_ref, s1_ref, p0_ref, p1_ref, a_ref,
                 m_ref, acc_ref, *, out_scale):
    S = q_ref.shape[0]
    nkb = S // ATT_BK
    W2 = 2 * ATT_BQ
    near = tuple(range(ATT_GK)) + (-1,)
    lam = lam_ref[0]
    h = pl.program_id(0)

    @pl.when(pl.program_id(1) == 0)
    def _():
        far = rb_ref[NUM_BUCKETS // 2 - 1, h]
        for n, kbrel in enumerate(near):
            for j in range(ATT_G):
                d = (kbrel * ATT_BK) // ATT_BQ - j
                if d > max(ATT_NEAR):
                    bias_ref[n, j] = jnp.full((ATT_BK, W2), NEG, F32)
                elif d < min(ATT_NEAR):
                    bias_ref[n, j] = jnp.zeros((ATT_BK, W2), F32)
        for t, dt in enumerate(ATT_NEAR):
            bk = bkt_ref[t]
            tile = jnp.full(bk.shape, NEG, F32)
            for bucket in range(NUM_BUCKETS):
                tile = jnp.where(bk == bucket, (rb_ref[bucket, h] - far) * LOG2E, tile)
            tile2 = jnp.concatenate([tile, tile], axis=1)
            for n, kbrel in enumerate(near):
                for j in range(ATT_G):
                    if (kbrel * ATT_BK) // ATT_BQ - j == dt:
                        bias_ref[n, j] = tile2

    lane = lax.broadcasted_iota(jnp.int32, (ATT_BQ, HEAD_W), 1)
    first_map = lane < ATT_DH

    s_bufs, p_bufs = (s0_ref, s1_ref), (p0_ref, p1_ref)

    def scores(step, j, buf):
        kb, bias_idx = step
        kblk = k_ref[pl.ds(pl.multiple_of(kb * ATT_BK, ATT_BK), ATT_BK), :]
        s = lax.dot_general(kblk, qs_ref[j], (((1,), (1,)), ((), ())),
                            preferred_element_type=F32)
        if bias_idx is not None:
            s = s + bias_ref[bias_idx, j]
        s_bufs[buf][j] = s

    def soft(j, bufs, first=False):
        tiles = [s_bufs[b][j] for b in bufs]
        m_new = functools.reduce(
            jnp.maximum, [jnp.max(s, axis=0, keepdims=True) for s in tiles])
        if first:
            a_ref[j] = jnp.ones_like(m_new)
        else:
            m_old = m_ref[j]
            m_new = jnp.maximum(m_old, m_new)
            a_ref[j] = jnp.exp2(m_old - m_new)
        m_ref[j] = m_new
        for b, s in zip(bufs, tiles):
            p_bufs[b][j] = jnp.exp2(s - m_new).astype(BF16)

    def accum(kbs, j, bufs, first=False):
        pv = functools.reduce(
            lambda x, y: x + y,
            [jnp.dot(vt_ref[kb], p_bufs[b][j], preferred_element_type=F32)
             for kb, b in zip(kbs, bufs)])
        if first:
            acc_ref[j] = pv
        else:
            acc_ref[j] = a_ref[j] * acc_ref[j] + pv

    AHEAD = 2

    def two_steps(steps, first=False, prev_kbs=None, dead=()):
        def live(j):
            return [i for i in (0, 1) if (i, j) not in dead]

        for k in range(ATT_G):
            ua = k + AHEAD
            for i in (0, 1):
                si = 2 * (ua // ATT_G) + i
                if (si, ua % ATT_G) not in dead:
                    scores(steps[si], ua % ATT_G, i)
            if k >= 1:
                accum([steps[i][0] for i in live(k - 1)], k - 1, live(k - 1), first=first)
            elif prev_kbs is not None:
                accum(prev_kbs, ATT_G - 1, (0, 1))
            soft(k, live(k), first=first)

    def inside(g):
        return [(g * ATT_GK + i, i) for i in range(ATT_GK)]

    def before(g):
        return (jnp.maximum(g * ATT_GK - 1, 0), ATT_GK)

    def far(n):
        return (jnp.clip(n - (ATT_GK + 1), 0, nkb - 1), None)

    def step_kb(g, n):
        return jnp.where(n == ATT_GK, before(g)[0], far(n)[0])

    dead = {(i, j) for i in range(ATT_GK) for j in range(ATT_G)
            if (i * ATT_BK) // ATT_BQ - j > max(ATT_NEAR)}

    def open_group(g):
        q0 = pl.multiple_of(g * (ATT_G * ATT_BQ), ATT_G * ATT_BQ)
        for j in range(ATT_G):
            q = q_ref[pl.ds(q0 + j * ATT_BQ, ATT_BQ), :]
            zero = jnp.zeros_like(q)
            qs_ref[j, :ATT_BQ, :] = jnp.where(first_map, q, zero)
            qs_ref[j, ATT_BQ:, :] = jnp.where(first_map, zero, q)
        for j in range(AHEAD):
            for i in range(ATT_GK):
                if (i, j) not in dead:
                    scores(inside(g)[i], j, i)

    def first_steps(g):
        acc_ref[ATT_G - 1] = jnp.zeros(acc_ref.shape[1:], F32)
        two_steps(inside(g) + [before(g), far(ATT_GK + 1)], first=True, dead=dead)

    def close_group(g, n_steps):
        kbs = [jnp.where(g > 0, step_kb(g, n_steps - 2 + i), inside(g)[i][0]) for i in (0, 1)]
        accum(kbs, ATT_G - 1, (0, 1))

    def emit_group(g):
        q0 = pl.multiple_of(g * (ATT_G * ATT_BQ), ATT_G * ATT_BQ)
        for j in range(ATT_G):
            on = acc_ref[j, :HEAD_W, :] * (1.0 / acc_ref[j, HEAD_W:HEAD_W + 1, :])
            o = on[:, :ATT_BQ] - lam * on[:, ATT_BQ:]
            ms = jnp.mean(o * o, axis=0, keepdims=True)
            y = o * lax.rsqrt(ms + EPS) * w_ref[...] * out_scale
            o_ref[pl.ds(q0 + j * ATT_BQ, ATT_BQ), :] = y.T.astype(BF16)

    n_groups = S // (ATT_G * ATT_BQ)
    open_group(0)
    first_steps(0)

    def group(g, carry):
        n_steps = jnp.where(g > 0, ATT_GK + 1 + (g * ATT_GK - 1), ATT_GK)

        def far_steps(n):
            two_steps([far(n + i) for i in range(4)],
                      prev_kbs=[step_kb(g, n - 2), step_kb(g, n - 1)])

        n_far_pairs = jnp.maximum(n_steps // 2 - 2, 0)
        first_far = ATT_GK + 2

        def far_block(n, pairs):
            for i in range(pairs):
                far_steps(n + 2 * i)

        def six_far_steps(i, c):
            far_block(first_far + 6 * i, 3)
            return c

        n_six = n_far_pairs // 3
        lax.fori_loop(0, n_six, six_far_steps, 0)
        for left in (1, 2):
            @pl.when(n_far_pairs - 3 * n_six == left)
            def _(left=left):
                far_block(first_far + 6 * n_six, left)

        @pl.when(g + 1 < n_groups)
        def _():
            close_group(g, n_steps)
            open_group(g + 1)
            emit_group(g)
            first_steps(g + 1)
            two_steps([before(g + 1)] + [far(ATT_GK + 1 + i) for i in range(3)],
                      prev_kbs=[kb for kb, _ in inside(g + 1)])

        @pl.when(g + 1 == n_groups)
        def _():
            close_group(g, n_steps)
            emit_group(g)

        return carry

    lax.fori_loop(0, n_groups, group, 0)


def _attention(q, k, vt, rel_bias, subln_w, lam, out_scale):
    B, S, _ = q.shape
    nkb = S // ATT_BK
    W2 = 2 * ATT_BQ
    assert ATT_GK == 2 and ATT_BK == 2 * ATT_BQ
    bkt = jnp.asarray(_attn_bucket_tiles())
    head = pl.BlockSpec((None, S, HEAD_W), lambda h, b: (b, 0, h))
    smem = pl.BlockSpec(memory_space=pltpu.SMEM)
    return pl.pallas_call(
        functools.partial(_attn_kernel, out_scale=out_scale),
        grid=(ATT_HEADS, B),
        in_specs=[
            smem, smem, _const_spec(bkt.shape),
            head, head,
            pl.BlockSpec((None, nkb, ATT_VR, ATT_BK), lambda h, b: (b, 0, h, 0)),
            pl.BlockSpec((HEAD_W, 1), lambda h, b: (0, 0)),
        ],
        out_specs=head,
        out_shape=jax.ShapeDtypeStruct((B, S, ATT_HEADS * HEAD_W), BF16),
        scratch_shapes=[
            pltpu.VMEM((ATT_GK + 1, ATT_G, ATT_BK, W2), F32),
            pltpu.VMEM((ATT_G, W2, HEAD_W), BF16),
            pltpu.VMEM((ATT_G, ATT_BK, W2), F32),
            pltpu.VMEM((ATT_G, ATT_BK, W2), F32),
            pltpu.VMEM((ATT_G, ATT_BK, W2), BF16),
            pltpu.VMEM((ATT_G, ATT_BK, W2), BF16),
            pltpu.VMEM((ATT_G, 1, W2), F32),
            pltpu.VMEM((ATT_G, 1, W2), F32),
            pltpu.VMEM((ATT_G, ATT_VR, W2), F32),
        ],
        compiler_params=_cparams(2),
        name="diffattn",
    )(lam, rel_bias, bkt, q, k, vt, subln_w)


def _layer_norm(z, g, b):
    mu = jnp.mean(z, axis=-1, keepdims=True)
    zc = z - mu
    var = jnp.mean(zc * zc, axis=-1, keepdims=True)
    return zc * lax.rsqrt(var + EPS) * g + b


def _merge_kernel(ya_ref, yh_ref, x_ref, wa_ref, wh_ref, wg_ref, bg_ref, wo_ref,
                  g_ref, b_ref, o_ref, *, alpha):
    D = x_ref.shape[-1]
    tm = x_ref.shape[0]

    def branches(r):
        rows = slice(r * ROW_CHUNK, (r + 1) * ROW_CHUNK)
        xb = x_ref[rows, :].astype(BF16)
        pa = jnp.dot(ya_ref[rows, :], wa_ref[...], preferred_element_type=F32)
        ph = jnp.dot(yh_ref[rows, :], wh_ref[...], preferred_element_type=F32)
        ga = jnp.dot(xb, wg_ref[:, :D], preferred_element_type=F32)
        gh = jnp.dot(xb, wg_ref[:, D:], preferred_element_type=F32)
        return pa, ph, ga, gh

    n_chunks = tm // ROW_CHUNK
    nxt = branches(0)
    for r in range(n_chunks):
        rows = slice(r * ROW_CHUNK, (r + 1) * ROW_CHUNK)
        pa, ph, ga, gh = nxt
        if r + 1 < n_chunks:
            nxt = branches(r + 1)
        merged = (jax.nn.sigmoid(ga + bg_ref[:, :D]) * pa
                  + jax.nn.sigmoid(gh + bg_ref[:, D:]) * ph)
        mix = jnp.dot(merged.astype(BF16), wo_ref[...], preferred_element_type=F32)
        o_ref[rows, :] = _layer_norm(alpha * x_ref[rows, :] + mix, g_ref[...], b_ref[...])


def _merge(ya, yh, x, wa, wh, wg, bg, wo, g, b, alpha, tm=1024):
    N, D = x.shape
    row = lambda w: pl.BlockSpec((tm, w), lambda i: (i, 0))
    return pl.pallas_call(
        functools.partial(_merge_kernel, alpha=alpha),
        grid=(N // tm,),
        in_specs=[row(ya.shape[1]), row(yh.shape[1]), row(D),
                  _const_spec(wa.shape), _const_spec(wh.shape), _const_spec(wg.shape),
                  _const_spec(bg.shape), _const_spec(wo.shape),
                  _const_spec(g.shape), _const_spec(b.shape)],
        out_specs=row(D),
        out_shape=jax.ShapeDtypeStruct((N, D), F32),
        compiler_params=_cparams(1),
        name="merge_ln",
    )(ya, yh, x, wa, wh, wg, bg, wo, g, b)


def _ffn_kernel(h_ref, wg_ref, wu_ref, cw_ref, cb_ref, wd_ref, g_ref, b_ref, o_ref,
                tail_ref, gbuf_ref, *, alpha, tm):
    d_ff = wg_ref.shape[1]
    PAD = 8

    @pl.when(pl.program_id(1) == 0)
    def _():
        tail_ref[...] = jnp.zeros_like(tail_ref)

    RC = ROW_CHUNK
    nfc = d_ff // FFN_FC
    steps = [(r, fc) for r in range(tm // RC) for fc in range(nfc)]
    hb = {}

    def gate_up(r, fc):
        if r not in hb:
            hb[r] = h_ref[r * RC:(r + 1) * RC, :].astype(BF16)
        cols = slice(fc * FFN_FC, (fc + 1) * FFN_FC)
        return (jnp.dot(hb[r], wg_ref[:, cols], preferred_element_type=F32),
                jnp.dot(hb[r], wu_ref[:, cols], preferred_element_type=F32))

    nxt = gate_up(*steps[0])
    acc = None
    for t, (r, fc) in enumerate(steps):
        rows = slice(r * RC, (r + 1) * RC)
        cols = slice(fc * FFN_FC, (fc + 1) * FFN_FC)
        gte, up = nxt
        if t + 1 < len(steps):
            nxt = gate_up(*steps[t + 1])
        gbuf = gbuf_ref.at[t % 2]
        gbuf[0:PAD, :] = tail_ref[:, cols]
        gbuf[PAD:, :] = gte
        tail_ref[:, cols] = gte[RC - PAD:, :]
        conv = (gte * cw_ref[2:3, cols]
                + gbuf[PAD - 1:PAD - 1 + RC, :] * cw_ref[1:2, cols]
                + gbuf[PAD - 2:PAD - 2 + RC, :] * cw_ref[0:1, cols]
                + cb_ref[:, cols])
        act = jax.nn.gelu(conv) * up
        down = jnp.dot(act.astype(BF16), wd_ref[cols, :], preferred_element_type=F32)
        acc = down if fc == 0 else acc + down
        if fc == nfc - 1:
            o_ref[rows, :] = _layer_norm(alpha * h_ref[rows, :] + acc, g_ref[...], b_ref[...])


def _ffn(h, wg, wu, cw, cb, wd, g, b, alpha, tm=1024):
    B, S, D = h.shape
    d_ff = wg.shape[1]
    row = pl.BlockSpec((None, tm, D), lambda bi, s: (bi, s, 0))
    return pl.pallas_call(
        functools.partial(_ffn_kernel, alpha=alpha, tm=tm),
        grid=(B, S // tm),
        in_specs=[row, _const_spec(wg.shape), _const_spec(wu.shape), _const_spec(cw.shape),
                  _const_spec(cb.shape), _const_spec(wd.shape),
                  _const_spec(g.shape), _const_spec(b.shape)],
        out_specs=row,
        out_shape=jax.ShapeDtypeStruct((B, S, D), F32),
        scratch_shapes=[pltpu.VMEM((8, d_ff), F32),
                        pltpu.VMEM((2, ROW_CHUNK + 8, FFN_FC), F32)],
        compiler_params=_cparams(2),
        name="ffn_ln",
    )(h, wg, wu, cw, cb, wd, g, b)


def kernel(x, w_in, b_gate, lambda_q1, lambda_k1, lambda_q2, lambda_k2, attn_subln_w,
           rel_bias, hgrn_lb_logits, hgrn_norm_w, w_branch_attn, w_branch_hgrn, w_out,
           ln1_g, ln1_b, w_ffn_gate, w_ffn_up, ffn_conv_w, ffn_conv_b, w_ffn_down,
           ln2_g, ln2_b):
    B, S, D = x.shape
    depth = w_in.shape[0]
    att_qk = ATT_HEADS * 2 * ATT_DH
    att_w = ATT_HEADS * HEAD_W
    hg_w = HG_HEADS * HEAD_W
    offs = {"aq": 0, "ak": att_qk, "av": 2 * att_qk}
    offs["hq"] = offs["av"] + att_w
    offs["hf"] = offs["hq"] + hg_w
    offs["hi"] = offs["hf"] + hg_w
    offs["hg"] = offs["hi"] + hg_w
    offs["gate"] = offs["hg"] + hg_w
    alpha = (2 * depth) ** 0.25

    lb_all = jnp.cumsum(jax.nn.softmax(hgrn_lb_logits.astype(F32), axis=0), axis=0)

    h = x
    for l in range(depth):
        lam_init = 0.8 - 0.6 * math.exp(-0.3 * l)
        lam = (jnp.exp(jnp.sum(lambda_q1[l].astype(F32) * lambda_k1[l].astype(F32)))
               - jnp.exp(jnp.sum(lambda_q2[l].astype(F32) * lambda_k2[l].astype(F32)))
               + lam_init).reshape(1)

        w_mix_bf = w_in[l, :, :offs["gate"]].astype(BF16)
        w_gate_bf = w_in[l, :, offs["gate"]:].astype(BF16)
        q, k, vt, yh = _mix(h, w_mix_bf, lb_all[l][None, :],
                            hgrn_norm_w[l][None, :].astype(F32), offs)

        ya = _attention(q, k, vt, rel_bias.astype(F32),
                        attn_subln_w[l][:, None].astype(F32), lam, 1.0 - lam_init)

        h1 = _merge(ya.reshape(B * S, att_w), yh.reshape(B * S, hg_w), h.reshape(B * S, D),
                    w_branch_attn[l].astype(BF16), w_branch_hgrn[l].astype(BF16),
                    w_gate_bf, b_gate[l][None, :], w_out[l].astype(BF16),
                    ln1_g[l][None, :], ln1_b[l][None, :], alpha)

        h = _ffn(h1.reshape(B, S, D), w_ffn_gate[l].astype(BF16), w_ffn_up[l].astype(BF16),
                 ffn_conv_w[l], ffn_conv_b[l][None, :], w_ffn_down[l].astype(BF16),
                 ln2_g[l][None, :], ln2_b[l][None, :], alpha)
    return h
```

```python
import functools
import math

import numpy as np
import jax
import jax.numpy as jnp
from jax import lax
from jax.experimental import pallas as pl
from jax.experimental.pallas import tpu as pltpu

F32 = jnp.float32
BF16 = jnp.bfloat16

CHUNK = 64
ATT_HEADS = 4
ATT_DH = 64
HG_HEADS = 4
HEAD_W = 128
NUM_BUCKETS = 32
MAX_DISTANCE = 128
EPS = 1e-5
NEG = -1e30

ATT_BQ = 128
ATT_BK = 256
ATT_G = 4
ATT_GK = ATT_G * ATT_BQ // ATT_BK
ATT_NEAR = (0, -1, -2)
ATT_VR = HEAD_W + 16
LOG2E = math.log2(math.e)
HG_C = 128
HG_LEVELS = (64, 32, 16, 8, 4)
HG_DIAG = 4
FFN_FC = 256
ROW_CHUNK = 256

VMEM_LIMIT = 56 * 1024 * 1024


def _cparams(n_axes):
    return pltpu.CompilerParams(
        dimension_semantics=("arbitrary",) * n_axes,
        vmem_limit_bytes=VMEM_LIMIT)


def _const_spec(shape):
    nd = len(shape)
    return pl.BlockSpec(shape, lambda *_: (0,) * nd, pipeline_mode=pl.Buffered(1))


def _mix_kernel(x_ref, w_ref, lb_ref, nw_ref, q_ref, k_ref, vt_ref, yh_ref,
                state_ref, kbuf_ref, bbuf_ref, *, tm, offs):
    C, H, W, RC, PAD = HG_C, HG_HEADS, HEAD_W, ROW_CHUNK, 8

    @pl.when(pl.program_id(1) == 0)
    def _():
        state_ref[...] = jnp.zeros_like(state_ref)

    kbuf_ref[:, 0:PAD, :] = jnp.zeros((H, PAD, W), F32)
    bbuf_ref[:, 0:PAD, :] = jnp.zeros((H, PAD, W), F32)

    lb = lb_ref[...]
    row = lax.broadcasted_iota(jnp.int32, (C, W), 0)
    r2 = lax.broadcasted_iota(jnp.int32, (C, C), 0)
    c2 = lax.broadcasted_iota(jnp.int32, (C, C), 1)
    tri = (c2 <= r2).astype(BF16)
    is_q = {hs: ((row // hs) % 2) == 1 for hs in HG_LEVELS}
    sign = {hs: jnp.where(is_q[hs], 1.0, -1.0).astype(F32) for hs in HG_LEVELS}
    keep = {hs: ((r2 // (2 * hs)) == (c2 // (2 * hs)))
            & (((r2 // hs) % 2) == 1) & (((c2 // hs) % 2) == 0) for hs in HG_LEVELS}
    same_blk = (r2 // HG_DIAG) == (c2 // HG_DIAG)
    on_diag = {d: ((r2 - c2) == d) & same_blk for d in range(HG_DIAG)}
    heads = range(H)
    cols = [slice(h * W, (h + 1) * W) for h in heads]

    def level_ref(b, hs):
        if hs >= PAD:
            parts = [jnp.broadcast_to(b[g * 2 * hs + hs - 1:g * 2 * hs + hs, :], (2 * hs, W))
                     for g in range(C // (2 * hs))]
            return parts[0] if len(parts) == 1 else jnp.concatenate(parts, axis=0)
        b3 = b.reshape(C // (2 * hs), 2 * hs, W)
        return jnp.broadcast_to(b3[:, hs - 1:hs, :], b3.shape).reshape(C, W)

    def projection(r):
        rows = slice(r * RC, (r + 1) * RC)
        out = {}

        def mm(off):
            return jnp.dot(out["xb"], w_ref[:, off:off + 512], preferred_element_type=F32)

        def sec_q():
            out["xb"] = x_ref[rows, :].astype(BF16)
            q_ref[rows, :] = (mm(offs["aq"]) * (ATT_DH ** -0.5 * LOG2E)).astype(BF16)

        def sec_k():
            k_ref[rows, :] = mm(offs["ak"]).astype(BF16)

        def sec_v():
            v = mm(offs["av"])
            for hd in range(ATT_HEADS):
                vt_ref[r, hd * ATT_VR:hd * ATT_VR + HEAD_W, :] = (
                    v[:, hd * HEAD_W:(hd + 1) * HEAD_W].T.astype(BF16))
                vt_ref[r, hd * ATT_VR + HEAD_W:(hd + 1) * ATT_VR, :] = jnp.ones(
                    (ATT_VR - HEAD_W, ATT_BK), BF16)

        def sec_hq():
            out["hq"] = mm(offs["hq"])

        def sec_hf():
            sig = jax.nn.sigmoid(mm(offs["hf"]))
            out["lf"] = jnp.log2(lb + (1.0 - lb) * sig)
            out["kk"] = (1.0 - lb) * (1.0 - sig)

        def sec_hi():
            out["hi"] = mm(offs["hi"]).astype(BF16)

        def sec_hg():
            hog = mm(offs["hg"])
            out["sog"] = hog * jax.nn.sigmoid(hog)

        return [sec_q, sec_k, sec_v, sec_hq, sec_hf, sec_hi, sec_hg], out

    def recurrence(r, vals):
        pieces = []
        for sub in range(RC // C):
            sl = slice(sub * C, (sub + 1) * C)
            t0 = r * RC + sub * C
            st = {}
            s0 = 3 * sub

            def cumsum_prep(sl=sl, st=st):
                lf = vals["lf"][sl]
                st["p0"] = lf.astype(BF16)
                st["p1"] = (lf - st["p0"].astype(F32)).astype(BF16)

            def cumsum_fire(st=st):
                b_all = (jnp.dot(tri, st["p0"], preferred_element_type=F32)
                         + jnp.dot(tri, st["p1"], preferred_element_type=F32))
                st["b"] = [b_all[:, cols[h]] for h in heads]

            def inter_prep(sl=sl, st=st):
                st["qf"] = [vals["hq"][sl, cols[h]] for h in heads]
                st["kf"] = [vals["kk"][sl, cols[h]] for h in heads]
                st["iv"] = [vals["hi"][sl, cols[h]] for h in heads]
                st["st"] = [state_ref[h] for h in heads]
                st["qt"] = [(st["qf"][h] * jnp.exp2(st["b"][h])).astype(BF16) for h in heads]
                st["stb"] = [st["st"][h].astype(BF16) for h in heads]
                st["a"] = [None] * H

            def inter_fire(st=st):
                st["inter"] = [lax.dot_general(st["qt"][h], st["stb"][h],
                                               (((1,), (1,)), ((), ())),
                                               preferred_element_type=F32) for h in heads]

            def level_prep(hs, st=st):
                ys = []
                for h in heads:
                    b = st["b"][h]
                    e = jnp.exp2((b - level_ref(b, hs)) * sign[hs])
                    ys.append((jnp.where(is_q[hs], st["qf"][h], st["kf"][h]) * e).astype(BF16))
                st["y", hs] = ys

            def level_fire(hs, st=st):
                for h in heads:
                    y = st["y", hs][h]
                    sc = lax.dot_general(y, y, (((1,), (1,)), ((), ())),
                                         preferred_element_type=F32)
                    sc = jnp.where(keep[hs], sc, 0.0)
                    st["a"][h] = sc if st["a"][h] is None else st["a"][h] + sc

            def diagonals(st=st):
                for h in heads:
                    kbuf_ref[h, PAD:, :] = st["kf"][h]
                    bbuf_ref[h, PAD:, :] = st["b"][h]
                diag = [None] * H
                for d in range(HG_DIAG):
                    for h in heads:
                        if d == 0:
                            term = st["qf"][h] * st["kf"][h]
                        else:
                            ks = kbuf_ref[h, PAD - d:PAD - d + C, :]
                            bs = bbuf_ref[h, PAD - d:PAD - d + C, :]
                            term = st["qf"][h] * ks * jnp.exp2(st["b"][h] - bs)
                        dsum = jnp.sum(term, axis=-1, keepdims=True)
                        placed = jnp.where(on_diag[d], dsum, 0.0)
                        diag[h] = placed if diag[h] is None else diag[h] + placed
                st["diag"] = diag

            def output_prep(st=st):
                st["ab"] = [(st["a"][h] + st["diag"][h]).astype(BF16) for h in heads]

            def output_fire(st=st):
                st["o"] = [st["inter"][h] + jnp.dot(st["ab"][h], st["iv"][h],
                                                    preferred_element_type=F32)
                           for h in heads]

            def update_prep(st=st):
                st["b_last"] = [st["b"][h][C - 1:C, :] for h in heads]
                st["k_end"] = [(st["kf"][h] * jnp.exp2(st["b_last"][h] - st["b"][h])
                                ).astype(BF16) for h in heads]

            def update_fire(st=st):
                for h in heads:
                    upd = lax.dot_general(st["iv"][h], st["k_end"][h],
                                          (((0,), (0,)), ((), ())),
                                          preferred_element_type=F32)
                    state_ref[h] = st["st"][h] * jnp.exp2(st["b_last"][h]) + upd

            def emit(sl=sl, st=st, t0=t0):
                for h in heads:
                    o = st["o"][h]
                    ms = jnp.mean(o * o, axis=-1, keepdims=True)
                    y = o * lax.rsqrt(ms + EPS) * nw_ref[...] * vals["sog"][sl, cols[h]]
                    yh_ref[t0:t0 + C, cols[h]] = y.astype(BF16)

            levels = [(functools.partial(level_prep, hs), functools.partial(level_fire, hs))
                      for hs in HG_LEVELS]
            pieces.append((0, cumsum_prep, cumsum_fire))
            pieces.append((s0 + 1, inter_prep, inter_fire))
            pieces += [(s0 + 1, p, f) for p, f in levels[:2]]
            pieces += [(s0 + 2, p, f) for p, f in levels[2:]]
            pieces.append((s0 + 3, update_prep, update_fire))
            pieces.append((s0 + 3, diagonals, None))
            pieces.append((s0 + 4, output_prep, output_fire))
            pieces.append((s0 + 5, emit, None))
        return pieces

    n_chunks = tm // RC
    sections = []
    schedule = {}
    for r in range(n_chunks):
        proj, vals = projection(r)
        sections += proj
        base = (r + 1) * len(proj)
        for slot, prep, fire in recurrence(r, vals):
            schedule.setdefault(base + slot, []).append((prep, fire))
    for t in range(max(schedule) + 2):
        for _, fire in schedule.get(t - 1, []):
            if fire is not None:
                fire()
        if t < len(sections):
            sections[t]()
        for prep, _ in schedule.get(t, []):
            prep()


def _mix(x, w_bf, lb, norm_w, offs, tm=1024):
    B, S, D = x.shape
    assert ROW_CHUNK == ATT_BK and ROW_CHUNK % HG_C == 0
    row = lambda w: pl.BlockSpec((None, tm, w), lambda b, s: (b, s, 0))
    out_shape = (
        jax.ShapeDtypeStruct((B, S, 512), BF16),
        jax.ShapeDtypeStruct((B, S, 512), BF16),
        jax.ShapeDtypeStruct((B, S // ATT_BK, ATT_HEADS * ATT_VR, ATT_BK), BF16),
        jax.ShapeDtypeStruct((B, S, HG_HEADS * HEAD_W), BF16),
    )
    out_specs = (
        row(512), row(512),
        pl.BlockSpec((None, tm // ATT_BK, ATT_HEADS * ATT_VR, ATT_BK),
                     lambda b, s: (b, s, 0, 0)),
        row(HG_HEADS * HEAD_W),
    )
    return pl.pallas_call(
        functools.partial(_mix_kernel, tm=tm, offs=offs),
        grid=(B, S // tm),
        in_specs=[row(D), _const_spec(w_bf.shape), _const_spec(lb.shape),
                  _const_spec(norm_w.shape)],
        out_specs=out_specs,
        out_shape=out_shape,
        scratch_shapes=[pltpu.VMEM((HG_HEADS, HEAD_W, HEAD_W), F32),
                        pltpu.VMEM((HG_HEADS, HG_C + 8, HEAD_W), F32),
                        pltpu.VMEM((HG_HEADS, HG_C + 8, HEAD_W), F32)],
        compiler_params=_cparams(2),
        name="inproj_hgrn2",
    )(x, w_bf, lb, norm_w)


def _rel_bucket_np(rel):
    nb = NUM_BUCKETS // 2
    max_exact = nb // 2
    ret = np.where(rel > 0, nb, 0)
    n = np.abs(rel)
    nf = np.maximum(n, 1).astype(np.float32)
    large = max_exact + (np.log(nf / np.float32(max_exact))
                         / np.float32(math.log(MAX_DISTANCE / max_exact))
                         * np.float32(nb - max_exact)).astype(np.int32)
    large = np.minimum(large, nb - 1)
    return ret + np.where(n < max_exact, n, large)


def _attn_bucket_tiles():
    i = np.arange(ATT_BQ)[None, :]
    j = np.arange(ATT_BK)[:, None]
    tiles = []
    for d in ATT_NEAR:
        kpos = d * ATT_BQ + j
        live = (kpos // CHUNK) <= (i // CHUNK)
        tiles.append(np.where(live, _rel_bucket_np(kpos - i), -1))
    far = _rel_bucket_np(np.asarray((min(ATT_NEAR) - 1) * ATT_BQ + ATT_BK - 1))
    assert int(far) == NUM_BUCKETS // 2 - 1
    return np.stack(tiles).astype(np.int32)


def _attn_kernel(lam_ref, rb_ref, bkt_ref, q_ref, k_ref, vt_ref, w_ref, o_ref,
                 bias_ref, qs_ref, s0_ref, s1_ref, cm0_ref, cm1_ref, p0_ref, p1_ref, a_ref,
                 m_ref, acc_ref, *, out_scale):
    S = q_ref.shape[0]
    nkb = S // ATT_BK
    W2 = 2 * ATT_BQ
    near = tuple(range(ATT_GK)) + (-1,)
    lam = lam_ref[0]
    h = pl.program_id(0)

    @pl.when(pl.program_id(1) == 0)
    def _():
        far = rb_ref[NUM_BUCKETS // 2 - 1, h]
        for n, kbrel in enumerate(near):
            for j in range(ATT_G):
                d = (kbrel * ATT_BK) // ATT_BQ - j
                if d > max(ATT_NEAR):
                    bias_ref[n, j] = jnp.full((ATT_BK, W2), NEG, F32)
                elif d < min(ATT_NEAR):
                    bias_ref[n, j] = jnp.zeros((ATT_BK, W2), F32)
        for t, dt in enumerate(ATT_NEAR):
            bk = bkt_ref[t]
            tile = jnp.full(bk.shape, NEG, F32)
            for bucket in range(NUM_BUCKETS):
                tile = jnp.where(bk == bucket, (rb_ref[bucket, h] - far) * LOG2E, tile)
            tile2 = jnp.concatenate([tile, tile], axis=1)
            for n, kbrel in enumerate(near):
                for j in range(ATT_G):
                    if (kbrel * ATT_BK) // ATT_BQ - j == dt:
                        bias_ref[n, j] = tile2

    lane = lax.broadcasted_iota(jnp.int32, (ATT_BQ, HEAD_W), 1)
    first_map = lane < ATT_DH

    s_bufs, p_bufs, cm_bufs = (s0_ref, s1_ref), (p0_ref, p1_ref), (cm0_ref, cm1_ref)

    def scores(step, j, buf):
        kb, bias_idx = step
        kblk = k_ref[pl.ds(pl.multiple_of(kb * ATT_BK, ATT_BK), ATT_BK), :]
        s = lax.dot_general(kblk, qs_ref[j], (((1,), (1,)), ((), ())),
                            preferred_element_type=F32)
        if bias_idx is not None:
            s = s + bias_ref[bias_idx, j]
        s_bufs[buf][j] = s
        cm_bufs[buf][j] = jnp.max(s.reshape(ATT_BK // 8, 8, W2), axis=0)

    def soft(j, bufs, first=False):
        tiles = [s_bufs[b][j] for b in bufs]
        m_new = functools.reduce(
            jnp.maximum, [jnp.max(cm_bufs[b][j], axis=0, keepdims=True) for b in bufs])
        if first:
            a_ref[j] = jnp.ones_like(m_new)
        else:
            m_old = m_ref[j]
            m_new = jnp.maximum(m_old, m_new)
            a_ref[j] = jnp.exp2(m_old - m_new)
        m_ref[j] = m_new
        for b, s in zip(bufs, tiles):
            p_bufs[b][j] = jnp.exp2(s - m_new).astype(BF16)

    def accum(kbs, j, bufs, first=False):
        pv = functools.reduce(
            lambda x, y: x + y,
            [jnp.dot(vt_ref[kb], p_bufs[b][j], preferred_element_type=F32)
             for kb, b in zip(kbs, bufs)])
        if first:
            acc_ref[j] = pv
        else:
            acc_ref[j] = a_ref[j] * acc_ref[j] + pv

    AHEAD = 2

    def two_steps(steps, first=False, prev_kbs=None, dead=()):
        def live(j):
            return [i for i in (0, 1) if (i, j) not in dead]

        for k in range(ATT_G):
            ua = k + AHEAD
            for i in (0, 1):
                si = 2 * (ua // ATT_G) + i
                if (si, ua % ATT_G) not in dead:
                    scores(steps[si], ua % ATT_G, i)
            if k >= 1:
                accum([steps[i][0] for i in live(k - 1)], k - 1, live(k - 1), first=first)
            elif prev_kbs is not None:
                accum(prev_kbs, ATT_G - 1, (0, 1))
            soft(k, live(k), first=first)

    def inside(g):
        return [(g * ATT_GK + i, i) for i in range(ATT_GK)]

    def before(g):
        return (jnp.maximum(g * ATT_GK - 1, 0), ATT_GK)

    def far(n):
        return (jnp.clip(n - (ATT_GK + 1), 0, nkb - 1), None)

    def step_kb(g, n):
        return jnp.where(n == ATT_GK, before(g)[0], far(n)[0])

    dead = {(i, j) for i in range(ATT_GK) for j in range(ATT_G)
            if (i * ATT_BK) // ATT_BQ - j > max(ATT_NEAR)}

    def open_group(g):
        q0 = pl.multiple_of(g * (ATT_G * ATT_BQ), ATT_G * ATT_BQ)
        for j in range(ATT_G):
            q = q_ref[pl.ds(q0 + j * ATT_BQ, ATT_BQ), :]
            zero = jnp.zeros_like(q)
            qs_ref[j, :ATT_BQ, :] = jnp.where(first_map, q, zero)
            qs_ref[j, ATT_BQ:, :] = jnp.where(first_map, zero, q)
        for j in range(AHEAD):
            for i in range(ATT_GK):
                if (i, j) not in dead:
                    scores(inside(g)[i], j, i)

    def first_steps(g):
        acc_ref[ATT_G - 1] = jnp.zeros(acc_ref.shape[1:], F32)
        two_steps(inside(g) + [before(g), far(ATT_GK + 1)], first=True, dead=dead)

    def close_group(g, n_steps):
        kbs = [jnp.where(g > 0, step_kb(g, n_steps - 2 + i), inside(g)[i][0]) for i in (0, 1)]
        accum(kbs, ATT_G - 1, (0, 1))

    def emit_group(g):
        q0 = pl.multiple_of(g * (ATT_G * ATT_BQ), ATT_G * ATT_BQ)
        for j in range(ATT_G):
            on = acc_ref[j, :HEAD_W, :] * (1.0 / acc_ref[j, HEAD_W:HEAD_W + 1, :])
            o = on[:, :ATT_BQ] - lam * on[:, ATT_BQ:]
            ms = jnp.mean(o * o, axis=0, keepdims=True)
            y = o * lax.rsqrt(ms + EPS) * w_ref[...] * out_scale
            o_ref[pl.ds(q0 + j * ATT_BQ, ATT_BQ), :] = y.T.astype(BF16)

    n_groups = S // (ATT_G * ATT_BQ)
    open_group(0)
    first_steps(0)

    def group(g, carry):
        n_steps = jnp.where(g > 0, ATT_GK + 1 + (g * ATT_GK - 1), ATT_GK)

        def far_steps(n):
            two_steps([far(n + i) for i in range(4)],
                      prev_kbs=[step_kb(g, n - 2), step_kb(g, n - 1)])

        n_far_pairs = jnp.maximum(n_steps // 2 - 2, 0)
        first_far = ATT_GK + 2

        def far_block(n, pairs):
            for i in range(pairs):
                far_steps(n + 2 * i)

        def six_far_steps(i, c):
            far_block(first_far + 6 * i, 3)
            return c

        n_six = n_far_pairs // 3
        lax.fori_loop(0, n_six, six_far_steps, 0)
        for left in (1, 2):
            @pl.when(n_far_pairs - 3 * n_six == left)
            def _(left=left):
                far_block(first_far + 6 * n_six, left)

        @pl.when(g + 1 < n_groups)
        def _():
            close_group(g, n_steps)
            open_group(g + 1)
            emit_group(g)
            first_steps(g + 1)
            two_steps([before(g + 1)] + [far(ATT_GK + 1 + i) for i in range(3)],
                      prev_kbs=[kb for kb, _ in inside(g + 1)])

        @pl.when(g + 1 == n_groups)
        def _():
            close_group(g, n_steps)
            emit_group(g)

        return carry

    lax.fori_loop(0, n_groups, group, 0)


def _attention(q, k, vt, rel_bias, subln_w, lam, out_scale):
    B, S, _ = q.shape
    nkb = S // ATT_BK
    W2 = 2 * ATT_BQ
    assert ATT_GK == 2 and ATT_BK == 2 * ATT_BQ
    bkt = jnp.asarray(_attn_bucket_tiles())
    head = pl.BlockSpec((None, S, HEAD_W), lambda h, b: (b, 0, h))
    smem = pl.BlockSpec(memory_space=pltpu.SMEM)
    return pl.pallas_call(
        functools.partial(_attn_kernel, out_scale=out_scale),
        grid=(ATT_HEADS, B),
        in_specs=[
            smem, smem, _const_spec(bkt.shape),
            head, head,
            pl.BlockSpec((None, nkb, ATT_VR, ATT_BK), lambda h, b: (b, 0, h, 0)),
            pl.BlockSpec((HEAD_W, 1), lambda h, b: (0, 0)),
        ],
        out_specs=head,
        out_shape=jax.ShapeDtypeStruct((B, S, ATT_HEADS * HEAD_W), BF16),
        scratch_shapes=[
            pltpu.VMEM((ATT_GK + 1, ATT_G, ATT_BK, W2), F32),
            pltpu.VMEM((ATT_G, W2, HEAD_W), BF16),
            pltpu.VMEM((ATT_G, ATT_BK, W2), F32),
            pltpu.VMEM((ATT_G, ATT_BK, W2), F32),
            pltpu.VMEM((ATT_G, 8, W2), F32),
            pltpu.VMEM((ATT_G, 8, W2), F32),
            pltpu.VMEM((ATT_G, ATT_BK, W2), BF16),
            pltpu.VMEM((ATT_G, ATT_BK, W2), BF16),
            pltpu.VMEM((ATT_G, 1, W2), F32),
            pltpu.VMEM((ATT_G, 1, W2), F32),
            pltpu.VMEM((ATT_G, ATT_VR, W2), F32),
        ],
        compiler_params=_cparams(2),
        name="diffattn",
    )(lam, rel_bias, bkt, q, k, vt, subln_w)


def _layer_norm(z, g, b):
    mu = jnp.mean(z, axis=-1, keepdims=True)
    zc = z - mu
    var = jnp.mean(zc * zc, axis=-1, keepdims=True)
    return zc * lax.rsqrt(var + EPS) * g + b


def _merge_kernel(ya_ref, yh_ref, x_ref, wa_ref, wh_ref, wg_ref, bg_ref, wo_ref,
                  g_ref, b_ref, o_ref, *, alpha):
    D = x_ref.shape[-1]
    tm = x_ref.shape[0]

    def branches(r):
        rows = slice(r * ROW_CHUNK, (r + 1) * ROW_CHUNK)
        xb = x_ref[rows, :].astype(BF16)
        pa = jnp.dot(ya_ref[rows, :], wa_ref[...], preferred_element_type=F32)
        ph = jnp.dot(yh_ref[rows, :], wh_ref[...], preferred_element_type=F32)
        ga = jnp.dot(xb, wg_ref[:, :D], preferred_element_type=F32)
        gh = jnp.dot(xb, wg_ref[:, D:], preferred_element_type=F32)
        return pa, ph, ga, gh

    n_chunks = tm // ROW_CHUNK
    nxt = branches(0)
    for r in range(n_chunks):
        rows = slice(r * ROW_CHUNK, (r + 1) * ROW_CHUNK)
        pa, ph, ga, gh = nxt
        if r + 1 < n_chunks:
            nxt = branches(r + 1)
        merged = (jax.nn.sigmoid(ga + bg_ref[:, :D]) * pa
                  + jax.nn.sigmoid(gh + bg_ref[:, D:]) * ph)
        mix = jnp.dot(merged.astype(BF16), wo_ref[...], preferred_element_type=F32)
        o_ref[rows, :] = _layer_norm(alpha * x_ref[rows, :] + mix, g_ref[...], b_ref[...])


def _merge(ya, yh, x, wa, wh, wg, bg, wo, g, b, alpha, tm=1024):
    N, D = x.shape
    row = lambda w: pl.BlockSpec((tm, w), lambda i: (i, 0))
    return pl.pallas_call(
        functools.partial(_merge_kernel, alpha=alpha),
        grid=(N // tm,),
        in_specs=[row(ya.shape[1]), row(yh.shape[1]), row(D),
                  _const_spec(wa.shape), _const_spec(wh.shape), _const_spec(wg.shape),
                  _const_spec(bg.shape), _const_spec(wo.shape),
                  _const_spec(g.shape), _const_spec(b.shape)],
        out_specs=row(D),
        out_shape=jax.ShapeDtypeStruct((N, D), F32),
        compiler_params=_cparams(1),
        name="merge_ln",
    )(ya, yh, x, wa, wh, wg, bg, wo, g, b)


def _ffn_kernel(h_ref, wg_ref, wu_ref, cw_ref, cb_ref, wd_ref, g_ref, b_ref, o_ref,
                tail_ref, gbuf_ref, *, alpha, tm):
    d_ff = wg_ref.shape[1]
    PAD = 8

    @pl.when(pl.program_id(1) == 0)
    def _():
        tail_ref[...] = jnp.zeros_like(tail_ref)

    RC = ROW_CHUNK
    nfc = d_ff // FFN_FC
    steps = [(r, fc) for r in range(tm // RC) for fc in range(nfc)]
    hb = {}

    def gate_up(r, fc):
        if r not in hb:
            hb[r] = h_ref[r * RC:(r + 1) * RC, :].astype(BF16)
        cols = slice(fc * FFN_FC, (fc + 1) * FFN_FC)
        return (jnp.dot(hb[r], wg_ref[:, cols], preferred_element_type=F32),
                jnp.dot(hb[r], wu_ref[:, cols], preferred_element_type=F32))

    nxt = gate_up(*steps[0])
    acc = None
    for t, (r, fc) in enumerate(steps):
        rows = slice(r * RC, (r + 1) * RC)
        cols = slice(fc * FFN_FC, (fc + 1) * FFN_FC)
        gte, up = nxt
        if t + 1 < len(steps):
            nxt = gate_up(*steps[t + 1])
        gbuf = gbuf_ref.at[t % 2]
        gbuf[0:PAD, :] = tail_ref[:, cols]
        gbuf[PAD:, :] = gte
        tail_ref[:, cols] = gte[RC - PAD:, :]
        conv = (gte * cw_ref[2:3, cols]
                + gbuf[PAD - 1:PAD - 1 + RC, :] * cw_ref[1:2, cols]
                + gbuf[PAD - 2:PAD - 2 + RC, :] * cw_ref[0:1, cols]
                + cb_ref[:, cols])
        act = jax.nn.gelu(conv) * up
        down = jnp.dot(act.astype(BF16), wd_ref[cols, :], preferred_element_type=F32)
        acc = down if fc == 0 else acc + down
        if fc == nfc - 1:
            o_ref[rows, :] = _layer_norm(alpha * h_ref[rows, :] + acc, g_ref[...], b_ref[...])


def _ffn(h, wg, wu, cw, cb, wd, g, b, alpha, tm=1024):
    B, S, D = h.shape
    d_ff = wg.shape[1]
    row = pl.BlockSpec((None, tm, D), lambda bi, s: (bi, s, 0))
    return pl.pallas_call(
        functools.partial(_ffn_kernel, alpha=alpha, tm=tm),
        grid=(B, S // tm),
        in_specs=[row, _const_spec(wg.shape), _const_spec(wu.shape), _const_spec(cw.shape),
                  _const_spec(cb.shape), _const_spec(wd.shape),
                  _const_spec(g.shape), _const_spec(b.shape)],
        out_specs=row,
        out_shape=jax.ShapeDtypeStruct((B, S, D), F32),
        scratch_shapes=[pltpu.VMEM((8, d_ff), F32),
                        pltpu.VMEM((2, ROW_CHUNK + 8, FFN_FC), F32)],
        compiler_params=_cparams(2),
        name="ffn_ln",
    )(h, wg, wu, cw, cb, wd, g, b)


def kernel(x, w_in, b_gate, lambda_q1, lambda_k1, lambda_q2, lambda_k2, attn_subln_w,
           rel_bias, hgrn_lb_logits, hgrn_norm_w, w_branch_attn, w_branch_hgrn, w_out,
           ln1_g, ln1_b, w_ffn_gate, w_ffn_up, ffn_conv_w, ffn_conv_b, w_ffn_down,
           ln2_g, ln2_b):
    B, S, D = x.shape
    depth = w_in.shape[0]
    att_qk = ATT_HEADS * 2 * ATT_DH
    att_w = ATT_HEADS * HEAD_W
    hg_w = HG_HEADS * HEAD_W
    offs = {"aq": 0, "ak": att_qk, "av": 2 * att_qk}
    offs["hq"] = offs["av"] + att_w
    offs["hf"] = offs["hq"] + hg_w
    offs["hi"] = offs["hf"] + hg_w
    offs["hg"] = offs["hi"] + hg_w
    offs["gate"] = offs["hg"] + hg_w
    alpha = (2 * depth) ** 0.25

    lb_all = jnp.cumsum(jax.nn.softmax(hgrn_lb_logits.astype(F32), axis=0), axis=0)

    h = x
    for l in range(depth):
        lam_init = 0.8 - 0.6 * math.exp(-0.3 * l)
        lam = (jnp.exp(jnp.sum(lambda_q1[l].astype(F32) * lambda_k1[l].astype(F32)))
               - jnp.exp(jnp.sum(lambda_q2[l].astype(F32) * lambda_k2[l].astype(F32)))
               + lam_init).reshape(1)

        w_mix_bf = w_in[l, :, :offs["gate"]].astype(BF16)
        w_gate_bf = w_in[l, :, offs["gate"]:].astype(BF16)
        q, k, vt, yh = _mix(h, w_mix_bf, lb_all[l][None, :],
                            hgrn_norm_w[l][None, :].astype(F32), offs)

        ya = _attention(q, k, vt, rel_bias.astype(F32),
                        attn_subln_w[l][:, None].astype(F32), lam, 1.0 - lam_init)

        h1 = _merge(ya.reshape(B * S, att_w), yh.reshape(B * S, hg_w), h.reshape(B * S, D),
                    w_branch_attn[l].astype(BF16), w_branch_hgrn[l].astype(BF16),
                    w_gate_bf, b_gate[l][None, :], w_out[l].astype(BF16),
                    ln1_g[l][None, :], ln1_b[l][None, :], alpha)

        h = _ffn(h1.reshape(B, S, D), w_ffn_gate[l].astype(BF16), w_ffn_up[l].astype(BF16),
                 ffn_conv_w[l], ffn_conv_b[l][None, :], w_ffn_down[l].astype(BF16),
                 ln2_g[l][None, :], ln2_b[l][None, :], alpha)
    return h
```

```python
import functools
import math

import numpy as np
import jax
import jax.numpy as jnp
from jax import lax
from jax.experimental import pallas as pl
from jax.experimental.pallas import tpu as pltpu

F32 = jnp.float32
BF16 = jnp.bfloat16

CHUNK = 64
ATT_HEADS = 4
ATT_DH = 64
HG_HEADS = 4
HEAD_W = 128
NUM_BUCKETS = 32
MAX_DISTANCE = 128
EPS = 1e-5
NEG = -1e30

ATT_BQ = 128
ATT_BK = 256
ATT_G = 4
ATT_GK = ATT_G * ATT_BQ // ATT_BK
ATT_NEAR = (0, -1, -2)
ATT_VR = HEAD_W + 16
LOG2E = math.log2(math.e)
HG_C = 128
HG_LEVELS = (64, 32, 16, 8, 4)
HG_DIAG = 4
FFN_FC = 256
ROW_CHUNK = 256

VMEM_LIMIT = 56 * 1024 * 1024


def _cparams(n_axes):
    return pltpu.CompilerParams(
        dimension_semantics=("arbitrary",) * n_axes,
        vmem_limit_bytes=VMEM_LIMIT)


def _const_spec(shape):
    nd = len(shape)
    return pl.BlockSpec(shape, lambda *_: (0,) * nd, pipeline_mode=pl.Buffered(1))


def _mix_kernel(x_ref, w_ref, lb_ref, nw_ref, q_ref, k_ref, vt_ref, yh_ref,
                state_ref, kbuf_ref, bbuf_ref, *, tm, offs):
    C, H, W, RC, PAD = HG_C, HG_HEADS, HEAD_W, ROW_CHUNK, 8

    @pl.when(pl.program_id(1) == 0)
    def _():
        state_ref[...] = jnp.zeros_like(state_ref)

    kbuf_ref[:, 0:PAD, :] = jnp.zeros((H, PAD, W), F32)
    bbuf_ref[:, 0:PAD, :] = jnp.zeros((H, PAD, W), F32)

    lb = lb_ref[...]
    row = lax.broadcasted_iota(jnp.int32, (C, W), 0)
    r2 = lax.broadcasted_iota(jnp.int32, (C, C), 0)
    c2 = lax.broadcasted_iota(jnp.int32, (C, C), 1)
    tri = (c2 <= r2).astype(BF16)
    is_q = {hs: ((row // hs) % 2) == 1 for hs in HG_LEVELS}
    sign = {hs: jnp.where(is_q[hs], 1.0, -1.0).astype(F32) for hs in HG_LEVELS}
    keep = {hs: ((r2 // (2 * hs)) == (c2 // (2 * hs)))
            & (((r2 // hs) % 2) == 1) & (((c2 // hs) % 2) == 0) for hs in HG_LEVELS}
    same_blk = (r2 // HG_DIAG) == (c2 // HG_DIAG)
    on_diag = {d: ((r2 - c2) == d) & same_blk for d in range(HG_DIAG)}
    heads = range(H)
    cols = [slice(h * W, (h + 1) * W) for h in heads]

    def level_ref(b, hs):
        if hs >= PAD:
            parts = [jnp.broadcast_to(b[g * 2 * hs + hs - 1:g * 2 * hs + hs, :], (2 * hs, W))
                     for g in range(C // (2 * hs))]
            return parts[0] if len(parts) == 1 else jnp.concatenate(parts, axis=0)
        b3 = b.reshape(C // (2 * hs), 2 * hs, W)
        return jnp.broadcast_to(b3[:, hs - 1:hs, :], b3.shape).reshape(C, W)

    def projection(r):
        rows = slice(r * RC, (r + 1) * RC)
        out = {}

        def mm(off):
            return jnp.dot(out["xb"], w_ref[:, off:off + 512], preferred_element_type=F32)

        def sec_q():
            out["xb"] = x_ref[rows, :].astype(BF16)
            q_ref[rows, :] = (mm(offs["aq"]) * (ATT_DH ** -0.5 * LOG2E)).astype(BF16)

        def sec_k():
            k_ref[rows, :] = mm(offs["ak"]).astype(BF16)

        def sec_v():
            v = mm(offs["av"])
            for hd in range(ATT_HEADS):
                vt_ref[r, hd * ATT_VR:hd * ATT_VR + HEAD_W, :] = (
                    v[:, hd * HEAD_W:(hd + 1) * HEAD_W].T.astype(BF16))
                vt_ref[r, hd * ATT_VR + HEAD_W:(hd + 1) * ATT_VR, :] = jnp.ones(
                    (ATT_VR - HEAD_W, ATT_BK), BF16)

        def sec_hq():
            out["hq"] = mm(offs["hq"])

        def sec_hf():
            sig = jax.nn.sigmoid(mm(offs["hf"]))
            out["lf"] = jnp.log2(lb + (1.0 - lb) * sig)
            out["kk"] = (1.0 - lb) * (1.0 - sig)

        def sec_hi():
            out["hi"] = mm(offs["hi"]).astype(BF16)

        def sec_hg():
            hog = mm(offs["hg"])
            out["sog"] = hog * jax.nn.sigmoid(hog)

        return [sec_q, sec_k, sec_v, sec_hq, sec_hf, sec_hi, sec_hg], out

    def recurrence(r, vals):
        pieces = []
        for sub in range(RC // C):
            sl = slice(sub * C, (sub + 1) * C)
            t0 = r * RC + sub * C
            st = {}
            s0 = 3 * sub

            def cumsum_prep(sl=sl, st=st):
                lf = vals["lf"][sl]
                st["p0"] = lf.astype(BF16)
                st["p1"] = (lf - st["p0"].astype(F32)).astype(BF16)

            def cumsum_fire(st=st):
                b_all = (jnp.dot(tri, st["p0"], preferred_element_type=F32)
                         + jnp.dot(tri, st["p1"], preferred_element_type=F32))
                st["b"] = [b_all[:, cols[h]] for h in heads]

            def inter_prep(sl=sl, st=st):
                st["qf"] = [vals["hq"][sl, cols[h]] for h in heads]
                st["kf"] = [vals["kk"][sl, cols[h]] for h in heads]
                st["iv"] = [vals["hi"][sl, cols[h]] for h in heads]
                st["st"] = [state_ref[h] for h in heads]
                st["qt"] = [(st["qf"][h] * jnp.exp2(st["b"][h])).astype(BF16) for h in heads]
                st["stb"] = [st["st"][h].astype(BF16) for h in heads]
                st["a"] = [None] * H

            def inter_fire(st=st):
                st["inter"] = [lax.dot_general(st["qt"][h], st["stb"][h],
                                               (((1,), (1,)), ((), ())),
                                               preferred_element_type=F32) for h in heads]

            def level_prep(hs, st=st):
                ys = []
                for h in heads:
                    b = st["b"][h]
                    e = jnp.exp2((b - level_ref(b, hs)) * sign[hs])
                    ys.append((jnp.where(is_q[hs], st["qf"][h], st["kf"][h]) * e).astype(BF16))
                st["y", hs] = ys

            def level_fire(hs, st=st):
                for h in heads:
                    y = st["y", hs][h]
                    sc = lax.dot_general(y, y, (((1,), (1,)), ((), ())),
                                         preferred_element_type=F32)
                    sc = jnp.where(keep[hs], sc, 0.0)
                    st["a"][h] = sc if st["a"][h] is None else st["a"][h] + sc

            def diagonals(st=st):
                for h in heads:
                    kbuf_ref[h, PAD:, :] = st["kf"][h]
                    bbuf_ref[h, PAD:, :] = st["b"][h]
                diag = [None] * H
                for d in range(HG_DIAG):
                    for h in heads:
                        if d == 0:
                            term = st["qf"][h] * st["kf"][h]
                        else:
                            ks = kbuf_ref[h, PAD - d:PAD - d + C, :]
                            bs = bbuf_ref[h, PAD - d:PAD - d + C, :]
                            term = st["qf"][h] * ks * jnp.exp2(st["b"][h] - bs)
                        dsum = jnp.sum(term, axis=-1, keepdims=True)
                        placed = jnp.where(on_diag[d], dsum, 0.0)
                        diag[h] = placed if diag[h] is None else diag[h] + placed
                st["diag"] = diag

            def output_prep(st=st):
                st["ab"] = [(st["a"][h] + st["diag"][h]).astype(BF16) for h in heads]

            def output_fire(st=st):
                st["o"] = [st["inter"][h] + jnp.dot(st["ab"][h], st["iv"][h],
                                                    preferred_element_type=F32)
                           for h in heads]

            def update_prep(st=st):
                st["b_last"] = [st["b"][h][C - 1:C, :] for h in heads]
                st["k_end"] = [(st["kf"][h] * jnp.exp2(st["b_last"][h] - st["b"][h])
                                ).astype(BF16) for h in heads]

            def update_fire(st=st):
                for h in heads:
                    upd = lax.dot_general(st["iv"][h], st["k_end"][h],
                                          (((0,), (0,)), ((), ())),
                                          preferred_element_type=F32)
                    state_ref[h] = st["st"][h] * jnp.exp2(st["b_last"][h]) + upd

            def emit(sl=sl, st=st, t0=t0):
                for h in heads:
                    o = st["o"][h]
                    ms = jnp.mean(o * o, axis=-1, keepdims=True)
                    y = o * lax.rsqrt(ms + EPS) * nw_ref[...] * vals["sog"][sl, cols[h]]
                    yh_ref[t0:t0 + C, cols[h]] = y.astype(BF16)

            levels = [(functools.partial(level_prep, hs), functools.partial(level_fire, hs))
                      for hs in HG_LEVELS]
            pieces.append((0, cumsum_prep, cumsum_fire))
            pieces.append((s0 + 1, inter_prep, inter_fire))
            pieces += [(s0 + 1, p, f) for p, f in levels[:2]]
            pieces += [(s0 + 2, p, f) for p, f in levels[2:]]
            pieces.append((s0 + 3, update_prep, update_fire))
            pieces.append((s0 + 3, diagonals, None))
            pieces.append((s0 + 4, output_prep, output_fire))
            pieces.append((s0 + 5, emit, None))
        return pieces

    n_chunks = tm // RC
    sections = []
    schedule = {}
    for r in range(n_chunks):
        proj, vals = projection(r)
        sections += proj
        base = (r + 1) * len(proj)
        for slot, prep, fire in recurrence(r, vals):
            schedule.setdefault(base + slot, []).append((prep, fire))
    for t in range(max(schedule) + 2):
        for _, fire in schedule.get(t - 1, []):
            if fire is not None:
                fire()
        if t < len(sections):
            sections[t]()
        for prep, _ in schedule.get(t, []):
            prep()


def _mix(x, w_bf, lb, norm_w, offs, tm=1024):
    B, S, D = x.shape
    assert ROW_CHUNK == ATT_BK and ROW_CHUNK % HG_C == 0
    row = lambda w: pl.BlockSpec((None, tm, w), lambda b, s: (b, s, 0))
    out_shape = (
        jax.ShapeDtypeStruct((B, S, 512), BF16),
        jax.ShapeDtypeStruct((B, S, 512), BF16),
        jax.ShapeDtypeStruct((B, S // ATT_BK, ATT_HEADS * ATT_VR, ATT_BK), BF16),
        jax.ShapeDtypeStruct((B, S, HG_HEADS * HEAD_W), BF16),
    )
    out_specs = (
        row(512), row(512),
        pl.BlockSpec((None, tm // ATT_BK, ATT_HEADS * ATT_VR, ATT_BK),
                     lambda b, s: (b, s, 0, 0)),
        row(HG_HEADS * HEAD_W),
    )
    return pl.pallas_call(
        functools.partial(_mix_kernel, tm=tm, offs=offs),
        grid=(B, S // tm),
        in_specs=[row(D), _const_spec(w_bf.shape), _const_spec(lb.shape),
                  _const_spec(norm_w.shape)],
        out_specs=out_specs,
        out_shape=out_shape,
        scratch_shapes=[pltpu.VMEM((HG_HEADS, HEAD_W, HEAD_W), F32),
                        pltpu.VMEM((HG_HEADS, HG_C + 8, HEAD_W), F32),
                        pltpu.VMEM((HG_HEADS, HG_C + 8, HEAD_W), F32)],
        compiler_params=_cparams(2),
        name="inproj_hgrn2",
    )(x, w_bf, lb, norm_w)


def _rel_bucket_np(rel):
    nb = NUM_BUCKETS // 2
    max_exact = nb // 2
    ret = np.where(rel > 0, nb, 0)
    n = np.abs(rel)
    nf = np.maximum(n, 1).astype(np.float32)
    large = max_exact + (np.log(nf / np.float32(max_exact))
                         / np.float32(math.log(MAX_DISTANCE / max_exact))
                         * np.float32(nb - max_exact)).astype(np.int32)
    large = np.minimum(large, nb - 1)
    return ret + np.where(n < max_exact, n, large)


def _attn_bucket_tiles():
    i = np.arange(ATT_BQ)[None, :]
    j = np.arange(ATT_BK)[:, None]
    tiles = []
    for d in ATT_NEAR:
        kpos = d * ATT_BQ + j
        live = (kpos // CHUNK) <= (i // CHUNK)
        tiles.append(np.where(live, _rel_bucket_np(kpos - i), -1))
    far = _rel_bucket_np(np.asarray((min(ATT_NEAR) - 1) * ATT_BQ + ATT_BK - 1))
    assert int(far) == NUM_BUCKETS // 2 - 1
    return np.stack(tiles).astype(np.int32)


def _attn_kernel(lam_ref, rb_ref, bkt_ref, q_ref, k_ref, vt_ref, w_ref, o_ref,
                 bias_ref, qs_ref, s0_ref, s1_ref, p0_ref, p1_ref, a_ref,
                 m_ref, acc_ref, *, out_scale):
    S = q_ref.shape[0]
    nkb = S // ATT_BK
    W2 = 2 * ATT_BQ
    near = tuple(range(ATT_GK)) + (-1,)
    lam = lam_ref[0]
    h = pl.program_id(0)

    @pl.when(pl.program_id(1) == 0)
    def _():
        far = rb_ref[NUM_BUCKETS // 2 - 1, h]
        for n, kbrel in enumerate(near):
            for j in range(ATT_G):
                d = (kbrel * ATT_BK) // ATT_BQ - j
                if d > max(ATT_NEAR):
                    bias_ref[n, j] = jnp.full((ATT_BK, W2), NEG, F32)
                elif d < min(ATT_NEAR):
                    bias_ref[n, j] = jnp.zeros((ATT_BK, W2), F32)
        for t, dt in enumerate(ATT_NEAR):
            bk = bkt_ref[t]
            tile = jnp.full(bk.shape, NEG, F32)
            for bucket in range(NUM_BUCKETS):
                tile = jnp.where(bk == bucket, (rb_ref[bucket, h] - far) * LOG2E, tile)
            tile2 = jnp.concatenate([tile, tile], axis=1)
            for n, kbrel in enumerate(near):
                for j in range(ATT_G):
                    if (kbrel * ATT_BK) // ATT_BQ - j == dt:
                        bias_ref[n, j] = tile2

    lane = lax.broadcasted_iota(jnp.int32, (ATT_BQ, HEAD_W), 1)
    first_map = lane < ATT_DH

    s_bufs, p_bufs = (s0_ref, s1_ref), (p0_ref, p1_ref)

    def scores(step, j, buf):
        kb, bias_idx = step
        kblk = k_ref[pl.ds(pl.multiple_of(kb * ATT_BK, ATT_BK), ATT_BK), :]
        s = lax.dot_general(kblk, qs_ref[j], (((1,), (1,)), ((), ())),
                            preferred_element_type=F32)
        if bias_idx is not None:
            s = s + bias_ref[bias_idx, j]
        s_bufs[buf][j] = s

    def soft(j, bufs, first=False):
        tiles = [s_bufs[b][j] for b in bufs]
        m_new = functools.reduce(
            jnp.maximum, [jnp.max(s, axis=0, keepdims=True) for s in tiles])
        if first:
            a_ref[j] = jnp.ones_like(m_new)
        else:
            m_old = m_ref[j]
            m_new = jnp.maximum(m_old, m_new)
            a_ref[j] = jnp.exp2(m_old - m_new)
        m_ref[j] = m_new
        for b, s in zip(bufs, tiles):
            p_bufs[b][j] = jnp.exp2(s - m_new).astype(BF16)

    def accum(kbs, j, bufs, first=False):
        pv = functools.reduce(
            lambda x, y: x + y,
            [jnp.dot(vt_ref[kb], p_bufs[b][j], preferred_element_type=F32)
             for kb, b in zip(kbs, bufs)])
        if first:
            acc_ref[j] = pv
        else:
            acc_ref[j] = a_ref[j] * acc_ref[j] + pv

    AHEAD = 2

    def two_steps(steps, first=False, prev_kbs=None, dead=()):
        def live(j):
            return [i for i in (0, 1) if (i, j) not in dead]

        for k in range(ATT_G):
            ua = k + AHEAD
            for i in (0, 1):
                si = 2 * (ua // ATT_G) + i
                if (si, ua % ATT_G) not in dead:
                    scores(steps[si], ua % ATT_G, i)
            if k >= 1:
                accum([steps[i][0] for i in live(k - 1)], k - 1, live(k - 1), first=first)
            elif prev_kbs is not None:
                accum(prev_kbs, ATT_G - 1, (0, 1))
            soft(k, live(k), first=first)

    def inside(g):
        return [(g * ATT_GK + i, i) for i in range(ATT_GK)]

    def before(g):
        return (jnp.maximum(g * ATT_GK - 1, 0), ATT_GK)

    def far(n):
        return (jnp.clip(n - (ATT_GK + 1), 0, nkb - 1), None)

    def step_kb(g, n):
        return jnp.where(n == ATT_GK, before(g)[0], far(n)[0])

    dead = {(i, j) for i in range(ATT_GK) for j in range(ATT_G)
            if (i * ATT_BK) // ATT_BQ - j > max(ATT_NEAR)}

    def open_group(g):
        q0 = pl.multiple_of(g * (ATT_G * ATT_BQ), ATT_G * ATT_BQ)
        for j in range(ATT_G):
            q = q_ref[pl.ds(q0 + j * ATT_BQ, ATT_BQ), :]
            zero = jnp.zeros_like(q)
            qs_ref[j, :ATT_BQ, :] = jnp.where(first_map, q, zero)
            qs_ref[j, ATT_BQ:, :] = jnp.where(first_map, zero, q)
        for j in range(AHEAD):
            for i in range(ATT_GK):
                if (i, j) not in dead:
                    scores(inside(g)[i], j, i)

    def first_steps(g):
        acc_ref[ATT_G - 1] = jnp.zeros(acc_ref.shape[1:], F32)
        two_steps(inside(g) + [before(g), far(ATT_GK + 1)], first=True, dead=dead)

    def close_group(g, n_steps):
        kbs = [jnp.where(g > 0, step_kb(g, n_steps - 2 + i), inside(g)[i][0]) for i in (0, 1)]
        accum(kbs, ATT_G - 1, (0, 1))

    def emit_group(g):
        q0 = pl.multiple_of(g * (ATT_G * ATT_BQ), ATT_G * ATT_BQ)
        for j in range(ATT_G):
            on = acc_ref[j, :HEAD_W, :] * (1.0 / acc_ref[j, HEAD_W:HEAD_W + 1, :])
            o = on[:, :ATT_BQ] - lam * on[:, ATT_BQ:]
            ms = jnp.mean(o * o, axis=0, keepdims=True)
            y = o * lax.rsqrt(ms + EPS) * w_ref[...] * out_scale
            o_ref[pl.ds(q0 + j * ATT_BQ, ATT_BQ), :] = y.T.astype(BF16)

    n_groups = S // (ATT_G * ATT_BQ)
    open_group(0)
    first_steps(0)

    def group(g, carry):
        n_steps = jnp.where(g > 0, ATT_GK + 1 + (g * ATT_GK - 1), ATT_GK)

        def far_steps(n):
            two_steps([far(n + i) for i in range(4)],
                      prev_kbs=[step_kb(g, n - 2), step_kb(g, n - 1)])

        n_far_pairs = jnp.maximum(n_steps // 2 - 2, 0)
        first_far = ATT_GK + 2

        def far_block(n, pairs):
            for i in range(pairs):
                far_steps(n + 2 * i)

        def six_far_steps(i, c):
            far_block(first_far + 6 * i, 3)
            return c

        n_six = n_far_pairs // 3
        lax.fori_loop(0, n_six, six_far_steps, 0)
        for left in (1, 2):
            @pl.when(n_far_pairs - 3 * n_six == left)
            def _(left=left):
                far_block(first_far + 6 * n_six, left)

        @pl.when(g + 1 < n_groups)
        def _():
            close_group(g, n_steps)
            open_group(g + 1)
            emit_group(g)
            first_steps(g + 1)
            two_steps([before(g + 1)] + [far(ATT_GK + 1 + i) for i in range(3)],
                      prev_kbs=[kb for kb, _ in inside(g + 1)])

        @pl.when(g + 1 == n_groups)
        def _():
            close_group(g, n_steps)
            emit_group(g)

        return carry

    lax.fori_loop(0, n_groups, group, 0)


def _attention(q, k, vt, rel_bias, subln_w, lam, out_scale):
    B, S, _ = q.shape
    nkb = S // ATT_BK
    W2 = 2 * ATT_BQ
    assert ATT_GK == 2 and ATT_BK == 2 * ATT_BQ
    bkt = jnp.asarray(_attn_bucket_tiles())
    head = pl.BlockSpec((None, S, HEAD_W), lambda h, b: (b, 0, h))
    smem = pl.BlockSpec(memory_space=pltpu.SMEM)
    return pl.pallas_call(
        functools.partial(_attn_kernel, out_scale=out_scale),
        grid=(ATT_HEADS, B),
        in_specs=[
            smem, smem, _const_spec(bkt.shape),
            head, head,
            pl.BlockSpec((None, nkb, ATT_VR, ATT_BK), lambda h, b: (b, 0, h, 0)),
            pl.BlockSpec((HEAD_W, 1), lambda h, b: (0, 0)),
        ],
        out_specs=head,
        out_shape=jax.ShapeDtypeStruct((B, S, ATT_HEADS * HEAD_W), BF16),
        scratch_shapes=[
            pltpu.VMEM((ATT_GK + 1, ATT_G, ATT_BK, W2), F32),
            pltpu.VMEM((ATT_G, W2, HEAD_W), BF16),
            pltpu.VMEM((ATT_G, ATT_BK, W2), F32),
            pltpu.VMEM((ATT_G, ATT_BK, W2), F32),
            pltpu.VMEM((ATT_G, ATT_BK, W2), BF16),
            pltpu.VMEM((ATT_G, ATT_BK, W2), BF16),
            pltpu.VMEM((ATT_G, 1, W2), F32),
            pltpu.VMEM((ATT_G, 1, W2), F32),
            pltpu.VMEM((ATT_G, ATT_VR, W2), F32),
        ],
        compiler_params=_cparams(2),
        name="diffattn",
    )(lam, rel_bias, bkt, q, k, vt, subln_w)


def _layer_norm(z, g, b):
    mu = jnp.mean(z, axis=-1, keepdims=True)
    zc = z - mu
    var = jnp.mean(zc * zc, axis=-1, keepdims=True)
    return zc * lax.rsqrt(var + EPS) * g + b


def _merge_kernel(ya_ref, yh_ref, x_ref, wa_ref, wh_ref, wg_ref, bg_ref, wo_ref,
                  g_ref, b_ref, o_ref, *, alpha):
    D = x_ref.shape[-1]
    tm = x_ref.shape[0]

    def branches(r):
        rows = slice(r * ROW_CHUNK, (r + 1) * ROW_CHUNK)
        xb = x_ref[rows, :].astype(BF16)
        pa = jnp.dot(ya_ref[rows, :], wa_ref[...], preferred_element_type=F32)
        ph = jnp.dot(yh_ref[rows, :], wh_ref[...], preferred_element_type=F32)
        ga = jnp.dot(xb, wg_ref[:, :D], preferred_element_type=F32)
        gh = jnp.dot(xb, wg_ref[:, D:], preferred_element_type=F32)
        return pa, ph, ga, gh

    n_chunks = tm // ROW_CHUNK
    nxt = branches(0)
    for r in range(n_chunks):
        rows = slice(r * ROW_CHUNK, (r + 1) * ROW_CHUNK)
        pa, ph, ga, gh = nxt
        if r + 1 < n_chunks:
            nxt = branches(r + 1)
        merged = (jax.nn.sigmoid(ga + bg_ref[:, :D]) * pa
                  + jax.nn.sigmoid(gh + bg_ref[:, D:]) * ph)
        mix = jnp.dot(merged.astype(BF16), wo_ref[...], preferred_element_type=F32)
        o_ref[rows, :] = _layer_norm(alpha * x_ref[rows, :] + mix, g_ref[...], b_ref[...])


def _merge(ya, yh, x, wa, wh, wg, bg, wo, g, b, alpha, tm=1024):
    N, D = x.shape
    row = lambda w: pl.BlockSpec((tm, w), lambda i: (i, 0))
    return pl.pallas_call(
        functools.partial(_merge_kernel, alpha=alpha),
        grid=(N // tm,),
        in_specs=[row(ya.shape[1]), row(yh.shape[1]), row(D),
                  _const_spec(wa.shape), _const_spec(wh.shape), _const_spec(wg.shape),
                  _const_spec(bg.shape), _const_spec(wo.shape),
                  _const_spec(g.shape), _const_spec(b.shape)],
        out_specs=row(D),
        out_shape=jax.ShapeDtypeStruct((N, D), F32),
        compiler_params=_cparams(1),
        name="merge_ln",
    )(ya, yh, x, wa, wh, wg, bg, wo, g, b)


def _ffn_kernel(h_ref, wg_ref, wu_ref, cw_ref, cb_ref, wd_ref, g_ref, b_ref, o_ref,
                tail_ref, gbuf_ref, *, alpha, tm):
    d_ff = wg_ref.shape[1]
    PAD = 8

    @pl.when(pl.program_id(1) == 0)
    def _():
        tail_ref[...] = jnp.zeros_like(tail_ref)

    RC = ROW_CHUNK
    nfc = d_ff // FFN_FC
    slab_groups = [tuple(range(fc, min(fc + 2, nfc))) for fc in range(0, nfc, 2)]
    steps = [(r, grp) for r in range(tm // RC) for grp in slab_groups]
    hb = {}

    def gate_up(r, grp):
        if r not in hb:
            hb[r] = h_ref[r * RC:(r + 1) * RC, :].astype(BF16)
        outs = []
        for fc in grp:
            cols = slice(fc * FFN_FC, (fc + 1) * FFN_FC)
            outs.append((jnp.dot(hb[r], wg_ref[:, cols], preferred_element_type=F32),
                         jnp.dot(hb[r], wu_ref[:, cols], preferred_element_type=F32)))
        return outs

    nxt = gate_up(*steps[0])
    acc = None
    for t, (r, grp) in enumerate(steps):
        rows = slice(r * RC, (r + 1) * RC)
        cur = nxt
        if t + 1 < len(steps):
            nxt = gate_up(*steps[t + 1])
        down = None
        for i, (fc, (gte, up)) in enumerate(zip(grp, cur)):
            cols = slice(fc * FFN_FC, (fc + 1) * FFN_FC)
            gbuf = gbuf_ref.at[i]
            gbuf[0:PAD, :] = tail_ref[:, cols]
            gbuf[PAD:, :] = gte
            tail_ref[:, cols] = gte[RC - PAD:, :]
            conv = (gte * cw_ref[2:3, cols]
                    + gbuf[PAD - 1:PAD - 1 + RC, :] * cw_ref[1:2, cols]
                    + gbuf[PAD - 2:PAD - 2 + RC, :] * cw_ref[0:1, cols]
                    + cb_ref[:, cols])
            act = jax.nn.gelu(conv) * up
            part = jnp.dot(act.astype(BF16), wd_ref[cols, :], preferred_element_type=F32)
            down = part if down is None else down + part
        acc = down if grp[0] == 0 else acc + down
        if grp[-1] == nfc - 1:
            o_ref[rows, :] = _layer_norm(alpha * h_ref[rows, :] + acc, g_ref[...], b_ref[...])


def _ffn(h, wg, wu, cw, cb, wd, g, b, alpha, tm=1024):
    B, S, D = h.shape
    d_ff = wg.shape[1]
    row = pl.BlockSpec((None, tm, D), lambda bi, s: (bi, s, 0))
    return pl.pallas_call(
        functools.partial(_ffn_kernel, alpha=alpha, tm=tm),
        grid=(B, S // tm),
        in_specs=[row, _const_spec(wg.shape), _const_spec(wu.shape), _const_spec(cw.shape),
                  _const_spec(cb.shape), _const_spec(wd.shape),
                  _const_spec(g.shape), _const_spec(b.shape)],
        out_specs=row,
        out_shape=jax.ShapeDtypeStruct((B, S, D), F32),
        scratch_shapes=[pltpu.VMEM((8, d_ff), F32),
                        pltpu.VMEM((2, ROW_CHUNK + 8, FFN_FC), F32)],
        compiler_params=_cparams(2),
        name="ffn_ln",
    )(h, wg, wu, cw, cb, wd, g, b)


def kernel(x, w_in, b_gate, lambda_q1, lambda_k1, lambda_q2, lambda_k2, attn_subln_w,
           rel_bias, hgrn_lb_logits, hgrn_norm_w, w_branch_attn, w_branch_hgrn, w_out,
           ln1_g, ln1_b, w_ffn_gate, w_ffn_up, ffn_conv_w, ffn_conv_b, w_ffn_down,
           ln2_g, ln2_b):
    B, S, D = x.shape
    depth = w_in.shape[0]
    att_qk = ATT_HEADS * 2 * ATT_DH
    att_w = ATT_HEADS * HEAD_W
    hg_w = HG_HEADS * HEAD_W
    offs = {"aq": 0, "ak": att_qk, "av": 2 * att_qk}
    offs["hq"] = offs["av"] + att_w
    offs["hf"] = offs["hq"] + hg_w
    offs["hi"] = offs["hf"] + hg_w
    offs["hg"] = offs["hi"] + hg_w
    offs["gate"] = offs["hg"] + hg_w
    alpha = (2 * depth) ** 0.25

    lb_all = jnp.cumsum(jax.nn.softmax(hgrn_lb_logits.astype(F32), axis=0), axis=0)

    h = x
    for l in range(depth):
        lam_init = 0.8 - 0.6 * math.exp(-0.3 * l)
        lam = (jnp.exp(jnp.sum(lambda_q1[l].astype(F32) * lambda_k1[l].astype(F32)))
               - jnp.exp(jnp.sum(lambda_q2[l].astype(F32) * lambda_k2[l].astype(F32)))
               + lam_init).reshape(1)

        w_mix_bf = w_in[l, :, :offs["gate"]].astype(BF16)
        w_gate_bf = w_in[l, :, offs["gate"]:].astype(BF16)
        q, k, vt, yh = _mix(h, w_mix_bf, lb_all[l][None, :],
                            hgrn_norm_w[l][None, :].astype(F32), offs)

        ya = _attention(q, k, vt, rel_bias.astype(F32),
                        attn_subln_w[l][:, None].astype(F32), lam, 1.0 - lam_init)

        h1 = _merge(ya.reshape(B * S, att_w), yh.reshape(B * S, hg_w), h.reshape(B * S, D),
                    w_branch_attn[l].astype(BF16), w_branch_hgrn[l].astype(BF16),
                    w_gate_bf, b_gate[l][None, :], w_out[l].astype(BF16),
                    ln1_g[l][None, :], ln1_b[l][None, :], alpha)

        h = _ffn(h1.reshape(B, S, D), w_ffn_gate[l].astype(BF16), w_ffn_up[l].astype(BF16),
                 ffn_conv_w[l], ffn_conv_b[l][None, :], w_ffn_down[l].astype(BF16),
                 ln2_g[l][None, :], ln2_b[l][None, :], alpha)
    return h
```

```python
import functools
import math

import numpy as np
import jax
import jax.numpy as jnp
from jax import lax
from jax.experimental import pallas as pl
from jax.experimental.pallas import tpu as pltpu

F32 = jnp.float32
BF16 = jnp.bfloat16

CHUNK = 64
ATT_HEADS = 4
ATT_DH = 64
HG_HEADS = 4
HEAD_W = 128
NUM_BUCKETS = 32
MAX_DISTANCE = 128
EPS = 1e-5
NEG = -1e30

ATT_BQ = 128
ATT_BK = 256
ATT_G = 4
ATT_GK = ATT_G * ATT_BQ // ATT_BK
ATT_NEAR = (0, -1, -2)
ATT_VR = HEAD_W + 16
LOG2E = math.log2(math.e)
HG_C = 128
HG_LEVELS = (64, 32, 16, 8, 4)
HG_DIAG = 4
FFN_FC = 256
ROW_CHUNK = 256

VMEM_LIMIT = 56 * 1024 * 1024


def _cparams(n_axes):
    return pltpu.CompilerParams(
        dimension_semantics=("arbitrary",) * n_axes,
        vmem_limit_bytes=VMEM_LIMIT)


def _const_spec(shape):
    nd = len(shape)
    return pl.BlockSpec(shape, lambda *_: (0,) * nd, pipeline_mode=pl.Buffered(1))


def _mix_kernel(x_ref, w_ref, lb_ref, nw_ref, q_ref, k_ref, vt_ref, yh_ref,
                state_ref, kbuf_ref, bbuf_ref, *, tm, offs):
    C, H, W, RC, PAD = HG_C, HG_HEADS, HEAD_W, ROW_CHUNK, 8

    @pl.when(pl.program_id(1) == 0)
    def _():
        state_ref[...] = jnp.zeros_like(state_ref)

    kbuf_ref[:, 0:PAD, :] = jnp.zeros((H, PAD, W), F32)
    bbuf_ref[:, 0:PAD, :] = jnp.zeros((H, PAD, W), F32)

    lb = lb_ref[...]
    row = lax.broadcasted_iota(jnp.int32, (C, W), 0)
    r2 = lax.broadcasted_iota(jnp.int32, (C, C), 0)
    c2 = lax.broadcasted_iota(jnp.int32, (C, C), 1)
    tri = (c2 <= r2).astype(BF16)
    is_q = {hs: ((row // hs) % 2) == 1 for hs in HG_LEVELS}
    sign = {hs: jnp.where(is_q[hs], 1.0, -1.0).astype(F32) for hs in HG_LEVELS}
    keep = {hs: ((r2 // (2 * hs)) == (c2 // (2 * hs)))
            & (((r2 // hs) % 2) == 1) & (((c2 // hs) % 2) == 0) for hs in HG_LEVELS}
    same_blk = (r2 // HG_DIAG) == (c2 // HG_DIAG)
    on_diag = {d: ((r2 - c2) == d) & same_blk for d in range(HG_DIAG)}
    heads = range(H)
    cols = [slice(h * W, (h + 1) * W) for h in heads]

    def level_ref(b, hs):
        if hs >= PAD:
            parts = [jnp.broadcast_to(b[g * 2 * hs + hs - 1:g * 2 * hs + hs, :], (2 * hs, W))
                     for g in range(C // (2 * hs))]
            return parts[0] if len(parts) == 1 else jnp.concatenate(parts, axis=0)
        b3 = b.reshape(C // (2 * hs), 2 * hs, W)
        return jnp.broadcast_to(b3[:, hs - 1:hs, :], b3.shape).reshape(C, W)

    def projection(r):
        rows = slice(r * RC, (r + 1) * RC)
        out = {}

        def mm(off):
            return jnp.dot(out["xb"], w_ref[:, off:off + 512], preferred_element_type=F32)

        def sec_q():
            out["xb"] = x_ref[rows, :].astype(BF16)
            q_ref[rows, :] = (mm(offs["aq"]) * (ATT_DH ** -0.5 * LOG2E)).astype(BF16)

        def sec_k():
            k_ref[rows, :] = mm(offs["ak"]).astype(BF16)

        def sec_v():
            v = mm(offs["av"])
            for hd in range(ATT_HEADS):
                vt_ref[r, hd * ATT_VR:hd * ATT_VR + HEAD_W, :] = (
                    v[:, hd * HEAD_W:(hd + 1) * HEAD_W].T.astype(BF16))
                vt_ref[r, hd * ATT_VR + HEAD_W:(hd + 1) * ATT_VR, :] = jnp.ones(
                    (ATT_VR - HEAD_W, ATT_BK), BF16)

        def sec_hq():
            out["hq"] = mm(offs["hq"])

        def sec_hf():
            sig = jax.nn.sigmoid(mm(offs["hf"]))
            out["lf"] = jnp.log2(lb + (1.0 - lb) * sig)
            out["kk"] = (1.0 - lb) * (1.0 - sig)

        def sec_hi():
            out["hi"] = mm(offs["hi"]).astype(BF16)

        def sec_hg():
            hog = mm(offs["hg"])
            out["sog"] = hog * jax.nn.sigmoid(hog)

        return [sec_q, sec_k, sec_v, sec_hq, sec_hf, sec_hi, sec_hg], out

    def recurrence(r, vals):
        pieces = []
        for sub in range(RC // C):
            sl = slice(sub * C, (sub + 1) * C)
            t0 = r * RC + sub * C
            st = {}
            s0 = 3 * sub

            def cumsum_prep(sl=sl, st=st):
                lf = vals["lf"][sl]
                st["p0"] = lf.astype(BF16)
                st["p1"] = (lf - st["p0"].astype(F32)).astype(BF16)

            def cumsum_fire(st=st):
                b_all = (jnp.dot(tri, st["p0"], preferred_element_type=F32)
                         + jnp.dot(tri, st["p1"], preferred_element_type=F32))
                st["b"] = [b_all[:, cols[h]] for h in heads]

            def inter_prep(sl=sl, st=st):
                st["qf"] = [vals["hq"][sl, cols[h]] for h in heads]
                st["kf"] = [vals["kk"][sl, cols[h]] for h in heads]
                st["iv"] = [vals["hi"][sl, cols[h]] for h in heads]
                st["st"] = [state_ref[h] for h in heads]
                st["qt"] = [(st["qf"][h] * jnp.exp2(st["b"][h])).astype(BF16) for h in heads]
                st["stb"] = [st["st"][h].astype(BF16) for h in heads]
                st["a"] = [None] * H

            def inter_fire(st=st):
                st["inter"] = [lax.dot_general(st["qt"][h], st["stb"][h],
                                               (((1,), (1,)), ((), ())),
                                               preferred_element_type=F32) for h in heads]

            def level_prep(hs, st=st):
                ys = []
                for h in heads:
                    b = st["b"][h]
                    e = jnp.exp2((b - level_ref(b, hs)) * sign[hs])
                    ys.append((jnp.where(is_q[hs], st["qf"][h], st["kf"][h]) * e).astype(BF16))
                st["y", hs] = ys

            def level_fire(hs, st=st):
                for h in heads:
                    y = st["y", hs][h]
                    sc = lax.dot_general(y, y, (((1,), (1,)), ((), ())),
                                         preferred_element_type=F32)
                    sc = jnp.where(keep[hs], sc, 0.0)
                    st["a"][h] = sc if st["a"][h] is None else st["a"][h] + sc

            def diagonals(st=st):
                for h in heads:
                    kbuf_ref[h, PAD:, :] = st["kf"][h]
                    bbuf_ref[h, PAD:, :] = st["b"][h]
                diag = [None] * H
                for d in range(HG_DIAG):
                    for h in heads:
                        if d == 0:
                            term = st["qf"][h] * st["kf"][h]
                        else:
                            ks = kbuf_ref[h, PAD - d:PAD - d + C, :]
                            bs = bbuf_ref[h, PAD - d:PAD - d + C, :]
                            term = st["qf"][h] * ks * jnp.exp2(st["b"][h] - bs)
                        dsum = jnp.sum(term, axis=-1, keepdims=True)
                        placed = jnp.where(on_diag[d], dsum, 0.0)
                        diag[h] = placed if diag[h] is None else diag[h] + placed
                st["diag"] = diag

            def output_prep(st=st):
                st["ab"] = [(st["a"][h] + st["diag"][h]).astype(BF16) for h in heads]

            def output_fire(st=st):
                st["o"] = [st["inter"][h] + jnp.dot(st["ab"][h], st["iv"][h],
                                                    preferred_element_type=F32)
                           for h in heads]

            def update_prep(st=st):
                st["b_last"] = [st["b"][h][C - 1:C, :] for h in heads]
                st["k_end"] = [(st["kf"][h] * jnp.exp2(st["b_last"][h] - st["b"][h])
                                ).astype(BF16) for h in heads]

            def update_fire(st=st):
                for h in heads:
                    upd = lax.dot_general(st["iv"][h], st["k_end"][h],
                                          (((0,), (0,)), ((), ())),
                                          preferred_element_type=F32)
                    state_ref[h] = st["st"][h] * jnp.exp2(st["b_last"][h]) + upd

            def emit(sl=sl, st=st, t0=t0):
                for h in heads:
                    o = st["o"][h]
                    ms = jnp.mean(o * o, axis=-1, keepdims=True)
                    y = o * lax.rsqrt(ms + EPS) * nw_ref[...] * vals["sog"][sl, cols[h]]
                    yh_ref[t0:t0 + C, cols[h]] = y.astype(BF16)

            levels = [(functools.partial(level_prep, hs), functools.partial(level_fire, hs))
                      for hs in HG_LEVELS]
            pieces.append((0, cumsum_prep, cumsum_fire))
            pieces.append((s0 + 1, inter_prep, inter_fire))
            pieces += [(s0 + 1, p, f) for p, f in levels[:2]]
            pieces += [(s0 + 2, p, f) for p, f in levels[2:]]
            pieces.append((s0 + 3, update_prep, update_fire))
            pieces.append((s0 + 3, diagonals, None))
            pieces.append((s0 + 4, output_prep, output_fire))
            pieces.append((s0 + 5, emit, None))
        return pieces

    n_chunks = tm // RC
    sections = []
    schedule = {}
    for r in range(n_chunks):
        proj, vals = projection(r)
        sections += proj
        base = (r + 1) * len(proj)
        for slot, prep, fire in recurrence(r, vals):
            schedule.setdefault(base + slot, []).append((prep, fire))
    for t in range(max(schedule) + 2):
        for _, fire in schedule.get(t - 1, []):
            if fire is not None:
                fire()
        if t < len(sections):
            sections[t]()
        for prep, _ in schedule.get(t, []):
            prep()


def _mix(x, w_bf, lb, norm_w, offs, tm=1024):
    B, S, D = x.shape
    assert ROW_CHUNK == ATT_BK and ROW_CHUNK % HG_C == 0
    row = lambda w: pl.BlockSpec((None, tm, w), lambda b, s: (b, s, 0))
    out_shape = (
        jax.ShapeDtypeStruct((B, S, 512), BF16),
        jax.ShapeDtypeStruct((B, S, 512), BF16),
        jax.ShapeDtypeStruct((B, S // ATT_BK, ATT_HEADS * ATT_VR, ATT_BK), BF16),
        jax.ShapeDtypeStruct((B, S, HG_HEADS * HEAD_W), BF16),
    )
    out_specs = (
        row(512), row(512),
        pl.BlockSpec((None, tm // ATT_BK, ATT_HEADS * ATT_VR, ATT_BK),
                     lambda b, s: (b, s, 0, 0)),
        row(HG_HEADS * HEAD_W),
    )
    return pl.pallas_call(
        functools.partial(_mix_kernel, tm=tm, offs=offs),
        grid=(B, S // tm),
        in_specs=[row(D), _const_spec(w_bf.shape), _const_spec(lb.shape),
                  _const_spec(norm_w.shape)],
        out_specs=out_specs,
        out_shape=out_shape,
        scratch_shapes=[pltpu.VMEM((HG_HEADS, HEAD_W, HEAD_W), F32),
                        pltpu.VMEM((HG_HEADS, HG_C + 8, HEAD_W), F32),
                        pltpu.VMEM((HG_HEADS, HG_C + 8, HEAD_W), F32)],
        compiler_params=_cparams(2),
        name="inproj_hgrn2",
    )(x, w_bf, lb, norm_w)


def _rel_bucket_np(rel):
    nb = NUM_BUCKETS // 2
    max_exact = nb // 2
    ret = np.where(rel > 0, nb, 0)
    n = np.abs(rel)
    nf = np.maximum(n, 1).astype(np.float32)
    large = max_exact + (np.log(nf / np.float32(max_exact))
                         / np.float32(math.log(MAX_DISTANCE / max_exact))
                         * np.float32(nb - max_exact)).astype(np.int32)
    large = np.minimum(large, nb - 1)
    return ret + np.where(n < max_exact, n, large)


def _attn_bucket_tiles():
    i = np.arange(ATT_BQ)[None, :]
    j = np.arange(ATT_BK)[:, None]
    tiles = []
    for d in ATT_NEAR:
        kpos = d * ATT_BQ + j
        live = (kpos // CHUNK) <= (i // CHUNK)
        tiles.append(np.where(live, _rel_bucket_np(kpos - i), -1))
    far = _rel_bucket_np(np.asarray((min(ATT_NEAR) - 1) * ATT_BQ + ATT_BK - 1))
    assert int(far) == NUM_BUCKETS // 2 - 1
    return np.stack(tiles).astype(np.int32)


def _attn_kernel(lam_ref, rb_ref, bkt_ref, q_ref, k_ref, vt_ref, w_ref, o_ref,
                 bias_ref, qs_ref, s0_ref, s1_ref, p0_ref, p1_ref, acc_ref,
                 a_ref, m_ref, *, out_scale):
    S = q_ref.shape[0]
    nkb = S // ATT_BK
    W2 = 2 * ATT_BQ
    near = tuple(range(ATT_GK)) + (-1,)
    lam = lam_ref[0]
    h = pl.program_id(0)

    @pl.when(pl.program_id(1) == 0)
    def _():
        far = rb_ref[NUM_BUCKETS // 2 - 1, h]
        for n, kbrel in enumerate(near):
            for j in range(ATT_G):
                d = (kbrel * ATT_BK) // ATT_BQ - j
                if d > max(ATT_NEAR):
                    bias_ref[n, j] = jnp.full((ATT_BK, W2), NEG, F32)
                elif d < min(ATT_NEAR):
                    bias_ref[n, j] = jnp.zeros((ATT_BK, W2), F32)
        for t, dt in enumerate(ATT_NEAR):
            bk = bkt_ref[t]
            tile = jnp.full(bk.shape, NEG, F32)
            for bucket in range(NUM_BUCKETS):
                tile = jnp.where(bk == bucket, (rb_ref[bucket, h] - far) * LOG2E, tile)
            tile2 = jnp.concatenate([tile, tile], axis=1)
            for n, kbrel in enumerate(near):
                for j in range(ATT_G):
                    if (kbrel * ATT_BK) // ATT_BQ - j == dt:
                        bias_ref[n, j] = tile2

    lane = lax.broadcasted_iota(jnp.int32, (ATT_BQ, HEAD_W), 1)
    first_map = lane < ATT_DH

    s_bufs, p_bufs = (s0_ref, s1_ref), (p0_ref, p1_ref)

    def scores(step, j, buf):
        kb, bias_idx = step
        kblk = k_ref[pl.ds(pl.multiple_of(kb * ATT_BK, ATT_BK), ATT_BK), :]
        s = lax.dot_general(kblk, qs_ref[j], (((1,), (1,)), ((), ())),
                            preferred_element_type=F32)
        if bias_idx is not None:
            s = s + bias_ref[bias_idx, j]
        s_bufs[buf][j] = s

    def soft(j, bufs, first=False):
        tiles = [s_bufs[b][j] for b in bufs]
        m_new = functools.reduce(
            jnp.maximum, [jnp.max(s, axis=0, keepdims=True) for s in tiles])
        if first:
            a_ref[j] = jnp.ones_like(m_new)
        else:
            m_old = m_ref[j]
            m_new = jnp.maximum(m_old, m_new)
            a_ref[j] = jnp.exp2(m_old - m_new)
        m_ref[j] = m_new
        for b, s in zip(bufs, tiles):
            p_bufs[b][j] = jnp.exp2(s - m_new).astype(BF16)

    def accum(kbs, j, bufs, first=False):
        pv = functools.reduce(
            lambda x, y: x + y,
            [jnp.dot(vt_ref[kb], p_bufs[b][j], preferred_element_type=F32)
             for kb, b in zip(kbs, bufs)])
        if first:
            acc_ref[j] = pv
        else:
            acc_ref[j] = a_ref[j] * acc_ref[j] + pv

    AHEAD = 2

    def two_steps(steps, first=False, prev_kbs=None, dead=()):
        def live(j):
            return [i for i in (0, 1) if (i, j) not in dead]

        for k in range(ATT_G):
            ua = k + AHEAD
            for i in (0, 1):
                si = 2 * (ua // ATT_G) + i
                if (si, ua % ATT_G) not in dead:
                    scores(steps[si], ua % ATT_G, i)
            if k >= 1:
                accum([steps[i][0] for i in live(k - 1)], k - 1, live(k - 1), first=first)
            elif prev_kbs is not None:
                accum(prev_kbs, ATT_G - 1, (0, 1))
            soft(k, live(k), first=first)

    def inside(g):
        return [(g * ATT_GK + i, i) for i in range(ATT_GK)]

    def before(g):
        return (jnp.maximum(g * ATT_GK - 1, 0), ATT_GK)

    def far(n):
        return (jnp.clip(n - (ATT_GK + 1), 0, nkb - 1), None)

    def step_kb(g, n):
        return jnp.where(n == ATT_GK, before(g)[0], far(n)[0])

    dead = {(i, j) for i in range(ATT_GK) for j in range(ATT_G)
            if (i * ATT_BK) // ATT_BQ - j > max(ATT_NEAR)}

    def open_group(g):
        q0 = pl.multiple_of(g * (ATT_G * ATT_BQ), ATT_G * ATT_BQ)
        for j in range(ATT_G):
            q = q_ref[pl.ds(q0 + j * ATT_BQ, ATT_BQ), :]
            zero = jnp.zeros_like(q)
            qs_ref[j, :ATT_BQ, :] = jnp.where(first_map, q, zero)
            qs_ref[j, ATT_BQ:, :] = jnp.where(first_map, zero, q)
        for j in range(AHEAD):
            for i in range(ATT_GK):
                if (i, j) not in dead:
                    scores(inside(g)[i], j, i)

    def first_steps(g):
        acc_ref[ATT_G - 1] = jnp.zeros(acc_ref.shape[1:], F32)
        two_steps(inside(g) + [before(g), far(ATT_GK + 1)], first=True, dead=dead)

    def close_group(g, n_steps):
        kbs = [jnp.where(g > 0, step_kb(g, n_steps - 2 + i), inside(g)[i][0]) for i in (0, 1)]
        accum(kbs, ATT_G - 1, (0, 1))

    def emit_group(g):
        q0 = pl.multiple_of(g * (ATT_G * ATT_BQ), ATT_G * ATT_BQ)
        for j in range(ATT_G):
            on = acc_ref[j, :HEAD_W, :] * (1.0 / acc_ref[j, HEAD_W:HEAD_W + 1, :])
            o = on[:, :ATT_BQ] - lam * on[:, ATT_BQ:]
            ms = jnp.mean(o * o, axis=0, keepdims=True)
            y = o * lax.rsqrt(ms + EPS) * w_ref[...] * out_scale
            o_ref[pl.ds(q0 + j * ATT_BQ, ATT_BQ), :] = y.T.astype(BF16)

    n_groups = S // (ATT_G * ATT_BQ)
    open_group(0)
    first_steps(0)

    def group(g, carry):
        n_steps = jnp.where(g > 0, ATT_GK + 1 + (g * ATT_GK - 1), ATT_GK)

        def far_steps(n):
            two_steps([far(n + i) for i in range(4)],
                      prev_kbs=[step_kb(g, n - 2), step_kb(g, n - 1)])

        n_far_pairs = jnp.maximum(n_steps // 2 - 2, 0)
        first_far = ATT_GK + 2

        def far_block(n, pairs):
            for i in range(pairs):
                far_steps(n + 2 * i)

        def six_far_steps(i, c):
            far_block(first_far + 6 * i, 3)
            return c

        n_six = n_far_pairs // 3
        lax.fori_loop(0, n_six, six_far_steps, 0)
        for left in (1, 2):
            @pl.when(n_far_pairs - 3 * n_six == left)
            def _(left=left):
                far_block(first_far + 6 * n_six, left)

        @pl.when(g + 1 < n_groups)
        def _():
            close_group(g, n_steps)
            open_group(g + 1)
            emit_group(g)
            first_steps(g + 1)
            two_steps([before(g + 1)] + [far(ATT_GK + 1 + i) for i in range(3)],
                      prev_kbs=[kb for kb, _ in inside(g + 1)])

        @pl.when(g + 1 == n_groups)
        def _():
            close_group(g, n_steps)
            emit_group(g)

        return carry

    lax.fori_loop(0, n_groups, group, 0)


def _attention(q, k, vt, rel_bias, subln_w, lam, out_scale):
    B, S, _ = q.shape
    nkb = S // ATT_BK
    W2 = 2 * ATT_BQ
    assert ATT_GK == 2 and ATT_BK == 2 * ATT_BQ
    bkt = jnp.asarray(_attn_bucket_tiles())
    head = pl.BlockSpec((None, S, HEAD_W), lambda h, b: (b, 0, h))
    smem = pl.BlockSpec(memory_space=pltpu.SMEM)
    return pl.pallas_call(
        functools.partial(_attn_kernel, out_scale=out_scale),
        grid=(ATT_HEADS, B),
        in_specs=[
            smem, smem, _const_spec(bkt.shape),
            head, head,
            pl.BlockSpec((None, nkb, ATT_VR, ATT_BK), lambda h, b: (b, 0, h, 0)),
            pl.BlockSpec((HEAD_W, 1), lambda h, b: (0, 0)),
        ],
        out_specs=head,
        out_shape=jax.ShapeDtypeStruct((B, S, ATT_HEADS * HEAD_W), BF16),
        scratch_shapes=[
            pltpu.VMEM((ATT_GK + 1, ATT_G, ATT_BK, W2), F32),
            pltpu.VMEM((ATT_G, W2, HEAD_W), BF16),
            pltpu.VMEM((ATT_G, ATT_BK, W2), F32),
            pltpu.VMEM((ATT_G, ATT_BK, W2), F32),
            pltpu.VMEM((ATT_G, ATT_BK, W2), BF16),
            pltpu.VMEM((ATT_G, ATT_BK, W2), BF16),
            pltpu.VMEM((ATT_G, ATT_VR, W2), F32),
            pltpu.VMEM((ATT_G, 1, W2), F32),
            pltpu.VMEM((ATT_G, 1, W2), F32),
        ],
        compiler_params=_cparams(2),
        name="diffattn",
    )(lam, rel_bias, bkt, q, k, vt, subln_w)


def _layer_norm(z, g, b):
    mu = jnp.mean(z, axis=-1, keepdims=True)
    zc = z - mu
    var = jnp.mean(zc * zc, axis=-1, keepdims=True)
    return zc * lax.rsqrt(var + EPS) * g + b


def _merge_kernel(ya_ref, yh_ref, x_ref, wa_ref, wh_ref, wg_ref, bg_ref, wo_ref,
                  g_ref, b_ref, o_ref, *, alpha):
    D = x_ref.shape[-1]
    tm = x_ref.shape[0]

    def branches(r):
        rows = slice(r * ROW_CHUNK, (r + 1) * ROW_CHUNK)
        xb = x_ref[rows, :].astype(BF16)
        pa = jnp.dot(ya_ref[rows, :], wa_ref[...], preferred_element_type=F32)
        ph = jnp.dot(yh_ref[rows, :], wh_ref[...], preferred_element_type=F32)
        ga = jnp.dot(xb, wg_ref[:, :D], preferred_element_type=F32)
        gh = jnp.dot(xb, wg_ref[:, D:], preferred_element_type=F32)
        return pa, ph, ga, gh

    n_chunks = tm // ROW_CHUNK
    nxt = branches(0)
    for r in range(n_chunks):
        rows = slice(r * ROW_CHUNK, (r + 1) * ROW_CHUNK)
        pa, ph, ga, gh = nxt
        if r + 1 < n_chunks:
            nxt = branches(r + 1)
        merged = (jax.nn.sigmoid(ga + bg_ref[:, :D]) * pa
                  + jax.nn.sigmoid(gh + bg_ref[:, D:]) * ph)
        mix = jnp.dot(merged.astype(BF16), wo_ref[...], preferred_element_type=F32)
        o_ref[rows, :] = _layer_norm(alpha * x_ref[rows, :] + mix, g_ref[...], b_ref[...])


def _merge(ya, yh, x, wa, wh, wg, bg, wo, g, b, alpha, tm=1024):
    N, D = x.shape
    row = lambda w: pl.BlockSpec((tm, w), lambda i: (i, 0))
    return pl.pallas_call(
        functools.partial(_merge_kernel, alpha=alpha),
        grid=(N // tm,),
        in_specs=[row(ya.shape[1]), row(yh.shape[1]), row(D),
                  _const_spec(wa.shape), _const_spec(wh.shape), _const_spec(wg.shape),
                  _const_spec(bg.shape), _const_spec(wo.shape),
                  _const_spec(g.shape), _const_spec(b.shape)],
        out_specs=row(D),
        out_shape=jax.ShapeDtypeStruct((N, D), F32),
        compiler_params=_cparams(1),
        name="merge_ln",
    )(ya, yh, x, wa, wh, wg, bg, wo, g, b)


def _ffn_kernel(h_ref, wg_ref, wu_ref, cw_ref, cb_ref, wd_ref, g_ref, b_ref, o_ref,
                tail_ref, gbuf_ref, *, alpha, tm):
    d_ff = wg_ref.shape[1]
    PAD = 8

    @pl.when(pl.program_id(1) == 0)
    def _():
        tail_ref[...] = jnp.zeros_like(tail_ref)

    RC = ROW_CHUNK
    nfc = d_ff // FFN_FC
    steps = [(r, fc) for r in range(tm // RC) for fc in range(nfc)]
    hb = {}

    def gate_up(r, fc):
        if r not in hb:
            hb[r] = h_ref[r * RC:(r + 1) * RC, :].astype(BF16)
        cols = slice(fc * FFN_FC, (fc + 1) * FFN_FC)
        return (jnp.dot(hb[r], wg_ref[:, cols], preferred_element_type=F32),
                jnp.dot(hb[r], wu_ref[:, cols], preferred_element_type=F32))

    nxt = gate_up(*steps[0])
    acc = None
    for t, (r, fc) in enumerate(steps):
        rows = slice(r * RC, (r + 1) * RC)
        cols = slice(fc * FFN_FC, (fc + 1) * FFN_FC)
        gte, up = nxt
        if t + 1 < len(steps):
            nxt = gate_up(*steps[t + 1])
        gbuf = gbuf_ref.at[t % 2]
        gbuf[0:PAD, :] = tail_ref[:, cols]
        gbuf[PAD:, :] = gte
        tail_ref[:, cols] = gte[RC - PAD:, :]
        conv = (gte * cw_ref[2:3, cols]
                + gbuf[PAD - 1:PAD - 1 + RC, :] * cw_ref[1:2, cols]
                + gbuf[PAD - 2:PAD - 2 + RC, :] * cw_ref[0:1, cols]
                + cb_ref[:, cols])
        act = jax.nn.gelu(conv) * up
        down = jnp.dot(act.astype(BF16), wd_ref[cols, :], preferred_element_type=F32)
        acc = down if fc == 0 else acc + down
        if fc == nfc - 1:
            o_ref[rows, :] = _layer_norm(alpha * h_ref[rows, :] + acc, g_ref[...], b_ref[...])


def _ffn(h, wg, wu, cw, cb, wd, g, b, alpha, tm=1024):
    B, S, D = h.shape
    d_ff = wg.shape[1]
    row = pl.BlockSpec((None, tm, D), lambda bi, s: (bi, s, 0))
    return pl.pallas_call(
        functools.partial(_ffn_kernel, alpha=alpha, tm=tm),
        grid=(B, S // tm),
        in_specs=[row, _const_spec(wg.shape), _const_spec(wu.shape), _const_spec(cw.shape),
                  _const_spec(cb.shape), _const_spec(wd.shape),
                  _const_spec(g.shape), _const_spec(b.shape)],
        out_specs=row,
        out_shape=jax.ShapeDtypeStruct((B, S, D), F32),
        scratch_shapes=[pltpu.VMEM((8, d_ff), F32),
                        pltpu.VMEM((2, ROW_CHUNK + 8, FFN_FC), F32)],
        compiler_params=_cparams(2),
        name="ffn_ln",
    )(h, wg, wu, cw, cb, wd, g, b)


def kernel(x, w_in, b_gate, lambda_q1, lambda_k1, lambda_q2, lambda_k2, attn_subln_w,
           rel_bias, hgrn_lb_logits, hgrn_norm_w, w_branch_attn, w_branch_hgrn, w_out,
           ln1_g, ln1_b, w_ffn_gate, w_ffn_up, ffn_conv_w, ffn_conv_b, w_ffn_down,
           ln2_g, ln2_b):
    B, S, D = x.shape
    depth = w_in.shape[0]
    att_qk = ATT_HEADS * 2 * ATT_DH
    att_w = ATT_HEADS * HEAD_W
    hg_w = HG_HEADS * HEAD_W
    offs = {"aq": 0, "ak": att_qk, "av": 2 * att_qk}
    offs["hq"] = offs["av"] + att_w
    offs["hf"] = offs["hq"] + hg_w
    offs["hi"] = offs["hf"] + hg_w
    offs["hg"] = offs["hi"] + hg_w
    offs["gate"] = offs["hg"] + hg_w
    alpha = (2 * depth) ** 0.25

    lb_all = jnp.cumsum(jax.nn.softmax(hgrn_lb_logits.astype(F32), axis=0), axis=0)

    h = x
    for l in range(depth):
        lam_init = 0.8 - 0.6 * math.exp(-0.3 * l)
        lam = (jnp.exp(jnp.sum(lambda_q1[l].astype(F32) * lambda_k1[l].astype(F32)))
               - jnp.exp(jnp.sum(lambda_q2[l].astype(F32) * lambda_k2[l].astype(F32)))
               + lam_init).reshape(1)

        w_mix_bf = w_in[l, :, :offs["gate"]].astype(BF16)
        w_gate_bf = w_in[l, :, offs["gate"]:].astype(BF16)
        q, k, vt, yh = _mix(h, w_mix_bf, lb_all[l][None, :],
                            hgrn_norm_w[l][None, :].astype(F32), offs)

        ya = _attention(q, k, vt, rel_bias.astype(F32),
                        attn_subln_w[l][:, None].astype(F32), lam, 1.0 - lam_init)

        h1 = _merge(ya.reshape(B * S, att_w), yh.reshape(B * S, hg_w), h.reshape(B * S, D),
                    w_branch_attn[l].astype(BF16), w_branch_hgrn[l].astype(BF16),
                    w_gate_bf, b_gate[l][None, :], w_out[l].astype(BF16),
                    ln1_g[l][None, :], ln1_b[l][None, :], alpha)

        h = _ffn(h1.reshape(B, S, D), w_ffn_gate[l].astype(BF16), w_ffn_up[l].astype(BF16),
                 ffn_conv_w[l], ffn_conv_b[l][None, :], w_ffn_down[l].astype(BF16),
                 ln2_g[l][None, :], ln2_b[l][None, :], alpha)
    return h
```

```python
import functools
import math

import numpy as np
import jax
import jax.numpy as jnp
from jax import lax
from jax.experimental import pallas as pl
from jax.experimental.pallas import tpu as pltpu

F32 = jnp.float32
BF16 = jnp.bfloat16

CHUNK = 64
ATT_HEADS = 4
ATT_DH = 64
HG_HEADS = 4
HEAD_W = 128
NUM_BUCKETS = 32
MAX_DISTANCE = 128
EPS = 1e-5
NEG = -1e30

ATT_BQ = 128
ATT_BK = 256
ATT_G = 4
ATT_GK = ATT_G * ATT_BQ // ATT_BK
ATT_NEAR = (0, -1, -2)
ATT_VR = HEAD_W + 16
LOG2E = math.log2(math.e)
HG_C = 128
HG_LEVELS = (64, 32, 16, 8, 4)
HG_DIAG = 4
FFN_FC = 256
ROW_CHUNK = 256

VMEM_LIMIT = 56 * 1024 * 1024


def _cparams(n_axes, fusible_inputs=None):
    return pltpu.CompilerParams(
        dimension_semantics=("arbitrary",) * n_axes,
        vmem_limit_bytes=VMEM_LIMIT,
        allow_input_fusion=fusible_inputs)


def _const_spec(shape):
    nd = len(shape)
    return pl.BlockSpec(shape, lambda *_: (0,) * nd, pipeline_mode=pl.Buffered(1))


def _mix_kernel(x_ref, w_ref, lb_ref, nw_ref, q_ref, k_ref, vt_ref, yh_ref,
                state_ref, kbuf_ref, bbuf_ref, *, tm, offs):
    C, H, W, RC, PAD = HG_C, HG_HEADS, HEAD_W, ROW_CHUNK, 8

    @pl.when(pl.program_id(1) == 0)
    def _():
        state_ref[...] = jnp.zeros_like(state_ref)

    kbuf_ref[:, 0:PAD, :] = jnp.zeros((H, PAD, W), F32)
    bbuf_ref[:, 0:PAD, :] = jnp.zeros((H, PAD, W), F32)

    lb = lb_ref[...]
    row = lax.broadcasted_iota(jnp.int32, (C, W), 0)
    r2 = lax.broadcasted_iota(jnp.int32, (C, C), 0)
    c2 = lax.broadcasted_iota(jnp.int32, (C, C), 1)
    tri = (c2 <= r2).astype(BF16)
    is_q = {hs: ((row // hs) % 2) == 1 for hs in HG_LEVELS}
    sign = {hs: jnp.where(is_q[hs], 1.0, -1.0).astype(F32) for hs in HG_LEVELS}
    keep = {hs: ((r2 // (2 * hs)) == (c2 // (2 * hs)))
            & (((r2 // hs) % 2) == 1) & (((c2 // hs) % 2) == 0) for hs in HG_LEVELS}
    same_blk = (r2 // HG_DIAG) == (c2 // HG_DIAG)
    on_diag = {d: ((r2 - c2) == d) & same_blk for d in range(HG_DIAG)}
    heads = range(H)
    cols = [slice(h * W, (h + 1) * W) for h in heads]

    def level_ref(b, hs):
        if hs >= PAD:
            parts = [jnp.broadcast_to(b[g * 2 * hs + hs - 1:g * 2 * hs + hs, :], (2 * hs, W))
                     for g in range(C // (2 * hs))]
            return parts[0] if len(parts) == 1 else jnp.concatenate(parts, axis=0)
        b3 = b.reshape(C // (2 * hs), 2 * hs, W)
        return jnp.broadcast_to(b3[:, hs - 1:hs, :], b3.shape).reshape(C, W)

    def projection(r):
        rows = slice(r * RC, (r + 1) * RC)
        out = {}

        def mm(off):
            return jnp.dot(out["xb"], w_ref[:, off:off + 512], preferred_element_type=F32)

        def sec_q():
            out["xb"] = x_ref[rows, :].astype(BF16)
            q_ref[rows, :] = (mm(offs["aq"]) * (ATT_DH ** -0.5 * LOG2E)).astype(BF16)

        def sec_k():
            k_ref[rows, :] = mm(offs["ak"]).astype(BF16)

        def sec_v():
            v = mm(offs["av"])
            for hd in range(ATT_HEADS):
                vt_ref[r, hd * ATT_VR:hd * ATT_VR + HEAD_W, :] = (
                    v[:, hd * HEAD_W:(hd + 1) * HEAD_W].T.astype(BF16))
                vt_ref[r, hd * ATT_VR + HEAD_W:(hd + 1) * ATT_VR, :] = jnp.ones(
                    (ATT_VR - HEAD_W, ATT_BK), BF16)

        def sec_hq():
            out["hq"] = mm(offs["hq"])

        def sec_hf():
            sig = jax.nn.sigmoid(mm(offs["hf"]))
            out["lf"] = jnp.log2(lb + (1.0 - lb) * sig)
            out["kk"] = (1.0 - lb) * (1.0 - sig)

        def sec_hi():
            out["hi"] = mm(offs["hi"]).astype(BF16)

        def sec_hg():
            hog = mm(offs["hg"])
            out["sog"] = hog * jax.nn.sigmoid(hog)

        return [sec_q, sec_k, sec_v, sec_hq, sec_hf, sec_hi, sec_hg], out

    def recurrence(r, vals):
        pieces = []
        for sub in range(RC // C):
            sl = slice(sub * C, (sub + 1) * C)
            t0 = r * RC + sub * C
            st = {}
            s0 = 3 * sub

            def cumsum_prep(sl=sl, st=st):
                lf = vals["lf"][sl]
                st["p0"] = lf.astype(BF16)
                st["p1"] = (lf - st["p0"].astype(F32)).astype(BF16)

            def cumsum_fire(st=st):
                b_all = (jnp.dot(tri, st["p0"], preferred_element_type=F32)
                         + jnp.dot(tri, st["p1"], preferred_element_type=F32))
                st["b"] = [b_all[:, cols[h]] for h in heads]

            def inter_prep(sl=sl, st=st):
                st["qf"] = [vals["hq"][sl, cols[h]] for h in heads]
                st["kf"] = [vals["kk"][sl, cols[h]] for h in heads]
                st["iv"] = [vals["hi"][sl, cols[h]] for h in heads]
                st["st"] = [state_ref[h] for h in heads]
                st["qt"] = [(st["qf"][h] * jnp.exp2(st["b"][h])).astype(BF16) for h in heads]
                st["stb"] = [st["st"][h].astype(BF16) for h in heads]
                st["a"] = [None] * H

            def inter_fire(st=st):
                st["inter"] = [lax.dot_general(st["qt"][h], st["stb"][h],
                                               (((1,), (1,)), ((), ())),
                                               preferred_element_type=F32) for h in heads]

            def level_prep(hs, st=st):
                ys = []
                for h in heads:
                    b = st["b"][h]
                    e = jnp.exp2((b - level_ref(b, hs)) * sign[hs])
                    ys.append((jnp.where(is_q[hs], st["qf"][h], st["kf"][h]) * e).astype(BF16))
                st["y", hs] = ys

            def level_fire(hs, st=st):
                for h in heads:
                    y = st["y", hs][h]
                    sc = lax.dot_general(y, y, (((1,), (1,)), ((), ())),
                                         preferred_element_type=F32)
                    sc = jnp.where(keep[hs], sc, 0.0)
                    st["a"][h] = sc if st["a"][h] is None else st["a"][h] + sc

            def diagonals(st=st):
                for h in heads:
                    kbuf_ref[h, PAD:, :] = st["kf"][h]
                    bbuf_ref[h, PAD:, :] = st["b"][h]
                diag = [None] * H
                for d in range(HG_DIAG):
                    for h in heads:
                        if d == 0:
                            term = st["qf"][h] * st["kf"][h]
                        else:
                            ks = kbuf_ref[h, PAD - d:PAD - d + C, :]
                            bs = bbuf_ref[h, PAD - d:PAD - d + C, :]
                            term = st["qf"][h] * ks * jnp.exp2(st["b"][h] - bs)
                        dsum = jnp.sum(term, axis=-1, keepdims=True)
                        placed = jnp.where(on_diag[d], dsum, 0.0)
                        diag[h] = placed if diag[h] is None else diag[h] + placed
                st["diag"] = diag

            def output_prep(st=st):
                st["ab"] = [(st["a"][h] + st["diag"][h]).astype(BF16) for h in heads]

            def output_fire(st=st):
                st["o"] = [st["inter"][h] + jnp.dot(st["ab"][h], st["iv"][h],
                                                    preferred_element_type=F32)
                           for h in heads]

            def update_prep(st=st):
                st["b_last"] = [st["b"][h][C - 1:C, :] for h in heads]
                st["k_end"] = [(st["kf"][h] * jnp.exp2(st["b_last"][h] - st["b"][h])
                                ).astype(BF16) for h in heads]

            def update_fire(st=st):
                for h in heads:
                    upd = lax.dot_general(st["iv"][h], st["k_end"][h],
                                          (((0,), (0,)), ((), ())),
                                          preferred_element_type=F32)
                    state_ref[h] = st["st"][h] * jnp.exp2(st["b_last"][h]) + upd

            def emit(sl=sl, st=st, t0=t0):
                for h in heads:
                    o = st["o"][h]
                    ms = jnp.mean(o * o, axis=-1, keepdims=True)
                    y = o * lax.rsqrt(ms + EPS) * nw_ref[...] * vals["sog"][sl, cols[h]]
                    yh_ref[t0:t0 + C, cols[h]] = y.astype(BF16)

            levels = [(functools.partial(level_prep, hs), functools.partial(level_fire, hs))
                      for hs in HG_LEVELS]
            pieces.append((0, cumsum_prep, cumsum_fire))
            pieces.append((s0 + 1, inter_prep, inter_fire))
            pieces += [(s0 + 1, p, f) for p, f in levels[:2]]
            pieces += [(s0 + 2, p, f) for p, f in levels[2:]]
            pieces.append((s0 + 3, update_prep, update_fire))
            pieces.append((s0 + 3, diagonals, None))
            pieces.append((s0 + 4, output_prep, output_fire))
            pieces.append((s0 + 5, emit, None))
        return pieces

    n_chunks = tm // RC
    sections = []
    schedule = {}
    for r in range(n_chunks):
        proj, vals = projection(r)
        sections += proj
        base = (r + 1) * len(proj)
        for slot, prep, fire in recurrence(r, vals):
            schedule.setdefault(base + slot, []).append((prep, fire))
    for t in range(max(schedule) + 2):
        for _, fire in schedule.get(t - 1, []):
            if fire is not None:
                fire()
        if t < len(sections):
            sections[t]()
        for prep, _ in schedule.get(t, []):
            prep()


def _mix(x, w_bf, lb, norm_w, offs, tm=1024):
    B, S, D = x.shape
    assert ROW_CHUNK == ATT_BK and ROW_CHUNK % HG_C == 0
    row = lambda w: pl.BlockSpec((None, tm, w), lambda b, s: (b, s, 0))
    out_shape = (
        jax.ShapeDtypeStruct((B, S, 512), BF16),
        jax.ShapeDtypeStruct((B, S, 512), BF16),
        jax.ShapeDtypeStruct((B, S // ATT_BK, ATT_HEADS * ATT_VR, ATT_BK), BF16),
        jax.ShapeDtypeStruct((B, S, HG_HEADS * HEAD_W), BF16),
    )
    out_specs = (
        row(512), row(512),
        pl.BlockSpec((None, tm // ATT_BK, ATT_HEADS * ATT_VR, ATT_BK),
                     lambda b, s: (b, s, 0, 0)),
        row(HG_HEADS * HEAD_W),
    )
    return pl.pallas_call(
        functools.partial(_mix_kernel, tm=tm, offs=offs),
        grid=(B, S // tm),
        in_specs=[row(D), _const_spec(w_bf.shape), _const_spec(lb.shape),
                  _const_spec(norm_w.shape)],
        out_specs=out_specs,
        out_shape=out_shape,
        scratch_shapes=[pltpu.VMEM((HG_HEADS, HEAD_W, HEAD_W), F32),
                        pltpu.VMEM((HG_HEADS, HG_C + 8, HEAD_W), F32),
                        pltpu.VMEM((HG_HEADS, HG_C + 8, HEAD_W), F32)],
        compiler_params=_cparams(2, [False, True, False, False]),
        name="inproj_hgrn2",
    )(x, w_bf, lb, norm_w)


def _rel_bucket_np(rel):
    nb = NUM_BUCKETS // 2
    max_exact = nb // 2
    ret = np.where(rel > 0, nb, 0)
    n = np.abs(rel)
    nf = np.maximum(n, 1).astype(np.float32)
    large = max_exact + (np.log(nf / np.float32(max_exact))
                         / np.float32(math.log(MAX_DISTANCE / max_exact))
                         * np.float32(nb - max_exact)).astype(np.int32)
    large = np.minimum(large, nb - 1)
    return ret + np.where(n < max_exact, n, large)


def _attn_bucket_tiles():
    i = np.arange(ATT_BQ)[None, :]
    j = np.arange(ATT_BK)[:, None]
    tiles = []
    for d in ATT_NEAR:
        kpos = d * ATT_BQ + j
        live = (kpos // CHUNK) <= (i // CHUNK)
        tiles.append(np.where(live, _rel_bucket_np(kpos - i), -1))
    far = _rel_bucket_np(np.asarray((min(ATT_NEAR) - 1) * ATT_BQ + ATT_BK - 1))
    assert int(far) == NUM_BUCKETS // 2 - 1
    return np.stack(tiles).astype(np.int32)


def _attn_kernel(lam_ref, rb_ref, bkt_ref, q_ref, k_ref, vt_ref, w_ref, o_ref,
                 bias_ref, qs_ref, s0_ref, s1_ref, p0_ref, p1_ref, a_ref,
                 m_ref, acc_ref, *, out_scale):
    S = q_ref.shape[0]
    nkb = S // ATT_BK
    W2 = 2 * ATT_BQ
    near = tuple(range(ATT_GK)) + (-1,)
    lam = lam_ref[0]
    h = pl.program_id(0)

    @pl.when(pl.program_id(1) == 0)
    def _():
        far = rb_ref[NUM_BUCKETS // 2 - 1, h]
        for n, kbrel in enumerate(near):
            for j in range(ATT_G):
                d = (kbrel * ATT_BK) // ATT_BQ - j
                if d > max(ATT_NEAR):
                    bias_ref[n, j] = jnp.full((ATT_BK, W2), NEG, F32)
                elif d < min(ATT_NEAR):
                    bias_ref[n, j] = jnp.zeros((ATT_BK, W2), F32)
        for t, dt in enumerate(ATT_NEAR):
            bk = bkt_ref[t]
            tile = jnp.full(bk.shape, NEG, F32)
            for bucket in range(NUM_BUCKETS):
                tile = jnp.where(bk == bucket, (rb_ref[bucket, h] - far) * LOG2E, tile)
            tile2 = jnp.concatenate([tile, tile], axis=1)
            for n, kbrel in enumerate(near):
                for j in range(ATT_G):
                    if (kbrel * ATT_BK) // ATT_BQ - j == dt:
                        bias_ref[n, j] = tile2

    lane = lax.broadcasted_iota(jnp.int32, (ATT_BQ, HEAD_W), 1)
    first_map = lane < ATT_DH

    s_bufs, p_bufs = (s0_ref, s1_ref), (p0_ref, p1_ref)

    def scores(step, j, buf):
        kb, bias_idx = step
        kblk = k_ref[pl.ds(pl.multiple_of(kb * ATT_BK, ATT_BK), ATT_BK), :]
        s = lax.dot_general(kblk, qs_ref[j], (((1,), (1,)), ((), ())),
                            preferred_element_type=F32)
        if bias_idx is not None:
            s = s + bias_ref[bias_idx, j]
        s_bufs[buf][j] = s

    def soft(j, bufs, first=False):
        tiles = [s_bufs[b][j] for b in bufs]
        m_new = functools.reduce(
            jnp.maximum, [jnp.max(s, axis=0, keepdims=True) for s in tiles])
        if first:
            a_ref[j] = jnp.ones_like(m_new)
        else:
            m_old = m_ref[j]
            m_new = jnp.maximum(m_old, m_new)
            a_ref[j] = jnp.exp2(m_old - m_new)
        m_ref[j] = m_new
        for b, s in zip(bufs, tiles):
            p_bufs[b][j] = jnp.exp2(s - m_new).astype(BF16)

    def accum(kbs, j, bufs, first=False):
        pv = functools.reduce(
            lambda x, y: x + y,
            [jnp.dot(vt_ref[kb], p_bufs[b][j], preferred_element_type=F32)
             for kb, b in zip(kbs, bufs)])
        if first:
            acc_ref[j] = pv
        else:
            acc_ref[j] = a_ref[j] * acc_ref[j] + pv

    AHEAD = 2

    def two_steps(steps, first=False, prev_kbs=None, dead=()):
        def live(j):
            return [i for i in (0, 1) if (i, j) not in dead]

        for k in range(ATT_G):
            ua = k + AHEAD
            for i in (0, 1):
                si = 2 * (ua // ATT_G) + i
                if (si, ua % ATT_G) not in dead:
                    scores(steps[si], ua % ATT_G, i)
            if k >= 1:
                accum([steps[i][0] for i in live(k - 1)], k - 1, live(k - 1), first=first)
            elif prev_kbs is not None:
                accum(prev_kbs, ATT_G - 1, (0, 1))
            soft(k, live(k), first=first)

    def inside(g):
        return [(g * ATT_GK + i, i) for i in range(ATT_GK)]

    def before(g):
        return (jnp.maximum(g * ATT_GK - 1, 0), ATT_GK)

    def far(n):
        return (jnp.clip(n - (ATT_GK + 1), 0, nkb - 1), None)

    def step_kb(g, n):
        return jnp.where(n == ATT_GK, before(g)[0], far(n)[0])

    dead = {(i, j) for i in range(ATT_GK) for j in range(ATT_G)
            if (i * ATT_BK) // ATT_BQ - j > max(ATT_NEAR)}

    def open_group(g):
        q0 = pl.multiple_of(g * (ATT_G * ATT_BQ), ATT_G * ATT_BQ)
        for j in range(ATT_G):
            q = q_ref[pl.ds(q0 + j * ATT_BQ, ATT_BQ), :]
            zero = jnp.zeros_like(q)
            qs_ref[j, :ATT_BQ, :] = jnp.where(first_map, q, zero)
            qs_ref[j, ATT_BQ:, :] = jnp.where(first_map, zero, q)
        for j in range(AHEAD):
            for i in range(ATT_GK):
                if (i, j) not in dead:
                    scores(inside(g)[i], j, i)

    def first_steps(g):
        acc_ref[ATT_G - 1] = jnp.zeros(acc_ref.shape[1:], F32)
        two_steps(inside(g) + [before(g), far(ATT_GK + 1)], first=True, dead=dead)

    def close_group(g, n_steps):
        kbs = [jnp.where(g > 0, step_kb(g, n_steps - 2 + i), inside(g)[i][0]) for i in (0, 1)]
        accum(kbs, ATT_G - 1, (0, 1))

    def emit_group(g):
        q0 = pl.multiple_of(g * (ATT_G * ATT_BQ), ATT_G * ATT_BQ)
        for j in range(ATT_G):
            on = acc_ref[j, :HEAD_W, :] * (1.0 / acc_ref[j, HEAD_W:HEAD_W + 1, :])
            o = on[:, :ATT_BQ] - lam * on[:, ATT_BQ:]
            ms = jnp.mean(o * o, axis=0, keepdims=True)
            y = o * lax.rsqrt(ms + EPS) * w_ref[...] * out_scale
            o_ref[pl.ds(q0 + j * ATT_BQ, ATT_BQ), :] = y.T.astype(BF16)

    n_groups = S // (ATT_G * ATT_BQ)
    open_group(0)
    first_steps(0)

    def group(g, carry):
        n_steps = jnp.where(g > 0, ATT_GK + 1 + (g * ATT_GK - 1), ATT_GK)

        def far_steps(n):
            two_steps([far(n + i) for i in range(4)],
                      prev_kbs=[step_kb(g, n - 2), step_kb(g, n - 1)])

        n_far_pairs = jnp.maximum(n_steps // 2 - 2, 0)
        first_far = ATT_GK + 2

        def far_block(n, pairs):
            for i in range(pairs):
                far_steps(n + 2 * i)

        def six_far_steps(i, c):
            far_block(first_far + 6 * i, 3)
            return c

        n_six = n_far_pairs // 3
        lax.fori_loop(0, n_six, six_far_steps, 0)
        for left in (1, 2):
            @pl.when(n_far_pairs - 3 * n_six == left)
            def _(left=left):
                far_block(first_far + 6 * n_six, left)

        @pl.when(g + 1 < n_groups)
        def _():
            close_group(g, n_steps)
            open_group(g + 1)
            emit_group(g)
            first_steps(g + 1)
            two_steps([before(g + 1)] + [far(ATT_GK + 1 + i) for i in range(3)],
                      prev_kbs=[kb for kb, _ in inside(g + 1)])

        @pl.when(g + 1 == n_groups)
        def _():
            close_group(g, n_steps)
            emit_group(g)

        return carry

    lax.fori_loop(0, n_groups, group, 0)


def _attention(q, k, vt, rel_bias, subln_w, lam, out_scale):
    B, S, _ = q.shape
    nkb = S // ATT_BK
    W2 = 2 * ATT_BQ
    assert ATT_GK == 2 and ATT_BK == 2 * ATT_BQ
    bkt = jnp.asarray(_attn_bucket_tiles())
    head = pl.BlockSpec((None, S, HEAD_W), lambda h, b: (b, 0, h))
    smem = pl.BlockSpec(memory_space=pltpu.SMEM)
    return pl.pallas_call(
        functools.partial(_attn_kernel, out_scale=out_scale),
        grid=(ATT_HEADS, B),
        in_specs=[
            smem, smem, _const_spec(bkt.shape),
            head, head,
            pl.BlockSpec((None, nkb, ATT_VR, ATT_BK), lambda h, b: (b, 0, h, 0)),
            pl.BlockSpec((HEAD_W, 1), lambda h, b: (0, 0)),
        ],
        out_specs=head,
        out_shape=jax.ShapeDtypeStruct((B, S, ATT_HEADS * HEAD_W), BF16),
        scratch_shapes=[
            pltpu.VMEM((ATT_GK + 1, ATT_G, ATT_BK, W2), F32),
            pltpu.VMEM((ATT_G, W2, HEAD_W), BF16),
            pltpu.VMEM((ATT_G, ATT_BK, W2), F32),
            pltpu.VMEM((ATT_G, ATT_BK, W2), F32),
            pltpu.VMEM((ATT_G, ATT_BK, W2), BF16),
            pltpu.VMEM((ATT_G, ATT_BK, W2), BF16),
            pltpu.VMEM((ATT_G, 1, W2), F32),
            pltpu.VMEM((ATT_G, 1, W2), F32),
            pltpu.VMEM((ATT_G, ATT_VR, W2), F32),
        ],
        compiler_params=_cparams(2),
        name="diffattn",
    )(lam, rel_bias, bkt, q, k, vt, subln_w)


def _layer_norm(z, g, b):
    mu = jnp.mean(z, axis=-1, keepdims=True)
    zc = z - mu
    var = jnp.mean(zc * zc, axis=-1, keepdims=True)
    return zc * lax.rsqrt(var + EPS) * g + b


def _merge_kernel(ya_ref, yh_ref, x_ref, wa_ref, wh_ref, wg_ref, bg_ref, wo_ref,
                  g_ref, b_ref, o_ref, *, alpha):
    D = x_ref.shape[-1]
    tm = x_ref.shape[0]

    def branches(r):
        rows = slice(r * ROW_CHUNK, (r + 1) * ROW_CHUNK)
        xb = x_ref[rows, :].astype(BF16)
        pa = jnp.dot(ya_ref[rows, :], wa_ref[...], preferred_element_type=F32)
        ph = jnp.dot(yh_ref[rows, :], wh_ref[...], preferred_element_type=F32)
        ga = jnp.dot(xb, wg_ref[:, :D], preferred_element_type=F32)
        gh = jnp.dot(xb, wg_ref[:, D:], preferred_element_type=F32)
        return pa, ph, ga, gh

    n_chunks = tm // ROW_CHUNK
    nxt = branches(0)
    for r in range(n_chunks):
        rows = slice(r * ROW_CHUNK, (r + 1) * ROW_CHUNK)
        pa, ph, ga, gh = nxt
        if r + 1 < n_chunks:
            nxt = branches(r + 1)
        merged = (jax.nn.sigmoid(ga + bg_ref[:, :D]) * pa
                  + jax.nn.sigmoid(gh + bg_ref[:, D:]) * ph)
        mix = jnp.dot(merged.astype(BF16), wo_ref[...], preferred_element_type=F32)
        o_ref[rows, :] = _layer_norm(alpha * x_ref[rows, :] + mix, g_ref[...], b_ref[...])


def _merge(ya, yh, x, wa, wh, wg, bg, wo, g, b, alpha, tm=1024):
    N, D = x.shape
    row = lambda w: pl.BlockSpec((tm, w), lambda i: (i, 0))
    return pl.pallas_call(
        functools.partial(_merge_kernel, alpha=alpha),
        grid=(N // tm,),
        in_specs=[row(ya.shape[1]), row(yh.shape[1]), row(D),
                  _const_spec(wa.shape), _const_spec(wh.shape), _const_spec(wg.shape),
                  _const_spec(bg.shape), _const_spec(wo.shape),
                  _const_spec(g.shape), _const_spec(b.shape)],
        out_specs=row(D),
        out_shape=jax.ShapeDtypeStruct((N, D), F32),
        compiler_params=_cparams(1, [False, False, False, True, True, True, False, True,
                                     False, False]),
        name="merge_ln",
    )(ya, yh, x, wa, wh, wg, bg, wo, g, b)


def _ffn_kernel(h_ref, wg_ref, wu_ref, cw_ref, cb_ref, wd_ref, g_ref, b_ref, o_ref,
                tail_ref, gbuf_ref, *, alpha, tm):
    d_ff = wg_ref.shape[1]
    PAD = 8

    @pl.when(pl.program_id(1) == 0)
    def _():
        tail_ref[...] = jnp.zeros_like(tail_ref)

    RC = ROW_CHUNK
    nfc = d_ff // FFN_FC
    steps = [(r, fc) for r in range(tm // RC) for fc in range(nfc)]
    hb = {}

    def gate_up(r, fc):
        if r not in hb:
            hb[r] = h_ref[r * RC:(r + 1) * RC, :].astype(BF16)
        cols = slice(fc * FFN_FC, (fc + 1) * FFN_FC)
        return (jnp.dot(hb[r], wg_ref[:, cols], preferred_element_type=F32),
                jnp.dot(hb[r], wu_ref[:, cols], preferred_element_type=F32))

    nxt = gate_up(*steps[0])
    acc = None
    for t, (r, fc) in enumerate(steps):
        rows = slice(r * RC, (r + 1) * RC)
        cols = slice(fc * FFN_FC, (fc + 1) * FFN_FC)
        gte, up = nxt
        if t + 1 < len(steps):
            nxt = gate_up(*steps[t + 1])
        gbuf = gbuf_ref.at[t % 2]
        gbuf[0:PAD, :] = tail_ref[:, cols]
        gbuf[PAD:, :] = gte
        tail_ref[:, cols] = gte[RC - PAD:, :]
        conv = (gte * cw_ref[2:3, cols]
                + gbuf[PAD - 1:PAD - 1 + RC, :] * cw_ref[1:2, cols]
                + gbuf[PAD - 2:PAD - 2 + RC, :] * cw_ref[0:1, cols]
                + cb_ref[:, cols])
        act = jax.nn.gelu(conv) * up
        down = jnp.dot(act.astype(BF16), wd_ref[cols, :], preferred_element_type=F32)
        acc = down if fc == 0 else acc + down
        if fc == nfc - 1:
            o_ref[rows, :] = _layer_norm(alpha * h_ref[rows, :] + acc, g_ref[...], b_ref[...])


def _ffn(h, wg, wu, cw, cb, wd, g, b, alpha, tm=1024):
    B, S, D = h.shape
    d_ff = wg.shape[1]
    row = pl.BlockSpec((None, tm, D), lambda bi, s: (bi, s, 0))
    return pl.pallas_call(
        functools.partial(_ffn_kernel, alpha=alpha, tm=tm),
        grid=(B, S // tm),
        in_specs=[row, _const_spec(wg.shape), _const_spec(wu.shape), _const_spec(cw.shape),
                  _const_spec(cb.shape), _const_spec(wd.shape),
                  _const_spec(g.shape), _const_spec(b.shape)],
        out_specs=row,
        out_shape=jax.ShapeDtypeStruct((B, S, D), F32),
        scratch_shapes=[pltpu.VMEM((8, d_ff), F32),
                        pltpu.VMEM((2, ROW_CHUNK + 8, FFN_FC), F32)],
        compiler_params=_cparams(2, [False, True, True, False, False, True, False, False]),
        name="ffn_ln",
    )(h, wg, wu, cw, cb, wd, g, b)


def kernel(x, w_in, b_gate, lambda_q1, lambda_k1, lambda_q2, lambda_k2, attn_subln_w,
           rel_bias, hgrn_lb_logits, hgrn_norm_w, w_branch_attn, w_branch_hgrn, w_out,
           ln1_g, ln1_b, w_ffn_gate, w_ffn_up, ffn_conv_w, ffn_conv_b, w_ffn_down,
           ln2_g, ln2_b):
    B, S, D = x.shape
    depth = w_in.shape[0]
    att_qk = ATT_HEADS * 2 * ATT_DH
    att_w = ATT_HEADS * HEAD_W
    hg_w = HG_HEADS * HEAD_W
    offs = {"aq": 0, "ak": att_qk, "av": 2 * att_qk}
    offs["hq"] = offs["av"] + att_w
    offs["hf"] = offs["hq"] + hg_w
    offs["hi"] = offs["hf"] + hg_w
    offs["hg"] = offs["hi"] + hg_w
    offs["gate"] = offs["hg"] + hg_w
    alpha = (2 * depth) ** 0.25

    lb_all = jnp.cumsum(jax.nn.softmax(hgrn_lb_logits.astype(F32), axis=0), axis=0)

    h = x
    for l in range(depth):
        lam_init = 0.8 - 0.6 * math.exp(-0.3 * l)
        lam = (jnp.exp(jnp.sum(lambda_q1[l].astype(F32) * lambda_k1[l].astype(F32)))
               - jnp.exp(jnp.sum(lambda_q2[l].astype(F32) * lambda_k2[l].astype(F32)))
               + lam_init).reshape(1)

        w_mix_bf = w_in[l, :, :offs["gate"]].astype(BF16)
        w_gate_bf = w_in[l, :, offs["gate"]:].astype(BF16)
        q, k, vt, yh = _mix(h, w_mix_bf, lb_all[l][None, :],
                            hgrn_norm_w[l][None, :].astype(F32), offs)

        ya = _attention(q, k, vt, rel_bias.astype(F32),
                        attn_subln_w[l][:, None].astype(F32), lam, 1.0 - lam_init)

        h1 = _merge(ya.reshape(B * S, att_w), yh.reshape(B * S, hg_w), h.reshape(B * S, D),
                    w_branch_attn[l].astype(BF16), w_branch_hgrn[l].astype(BF16),
                    w_gate_bf, b_gate[l][None, :], w_out[l].astype(BF16),
                    ln1_g[l][None, :], ln1_b[l][None, :], alpha)

        h = _ffn(h1.reshape(B, S, D), w_ffn_gate[l].astype(BF16), w_ffn_up[l].astype(BF16),
                 ffn_conv_w[l], ffn_conv_b[l][None, :], w_ffn_down[l].astype(BF16),
                 ln2_g[l][None, :], ln2_b[l][None, :], alpha)
    return h
```

```python
import functools
import math

import numpy as np
import jax
import jax.numpy as jnp
from jax import lax
from jax.experimental import pallas as pl
from jax.experimental.pallas import tpu as pltpu

F32 = jnp.float32
BF16 = jnp.bfloat16

CHUNK = 64
ATT_HEADS = 4
ATT_DH = 64
HG_HEADS = 4
HEAD_W = 128
NUM_BUCKETS = 32
MAX_DISTANCE = 128
EPS = 1e-5
NEG = -1e30

ATT_BQ = 128
ATT_BK = 256
ATT_G = 4
ATT_GK = ATT_G * ATT_BQ // ATT_BK
ATT_NEAR = (0, -1, -2)
ATT_VR = HEAD_W + 16
LOG2E = math.log2(math.e)
HG_C = 128
HG_LEVELS = (64, 32, 16, 8, 4)
HG_DIAG = 4
FFN_FC = 256
ROW_CHUNK = 256

VMEM_LIMIT = 56 * 1024 * 1024


def _cparams(n_axes, independent_axes=0):
    semantics = ("parallel",) * independent_axes + ("arbitrary",) * (n_axes - independent_axes)
    return pltpu.CompilerParams(dimension_semantics=semantics, vmem_limit_bytes=VMEM_LIMIT)


def _const_spec(shape):
    nd = len(shape)
    return pl.BlockSpec(shape, lambda *_: (0,) * nd, pipeline_mode=pl.Buffered(1))


def _mix_kernel(x_ref, w_ref, lb_ref, nw_ref, q_ref, k_ref, vt_ref, yh_ref,
                state_ref, kbuf_ref, bbuf_ref, *, tm, offs):
    C, H, W, RC, PAD = HG_C, HG_HEADS, HEAD_W, ROW_CHUNK, 8

    @pl.when(pl.program_id(1) == 0)
    def _():
        state_ref[...] = jnp.zeros_like(state_ref)

    kbuf_ref[:, 0:PAD, :] = jnp.zeros((H, PAD, W), F32)
    bbuf_ref[:, 0:PAD, :] = jnp.zeros((H, PAD, W), F32)

    lb = lb_ref[...]
    row = lax.broadcasted_iota(jnp.int32, (C, W), 0)
    r2 = lax.broadcasted_iota(jnp.int32, (C, C), 0)
    c2 = lax.broadcasted_iota(jnp.int32, (C, C), 1)
    tri = (c2 <= r2).astype(BF16)
    is_q = {hs: ((row // hs) % 2) == 1 for hs in HG_LEVELS}
    sign = {hs: jnp.where(is_q[hs], 1.0, -1.0).astype(F32) for hs in HG_LEVELS}
    keep = {hs: ((r2 // (2 * hs)) == (c2 // (2 * hs)))
            & (((r2 // hs) % 2) == 1) & (((c2 // hs) % 2) == 0) for hs in HG_LEVELS}
    same_blk = (r2 // HG_DIAG) == (c2 // HG_DIAG)
    on_diag = {d: ((r2 - c2) == d) & same_blk for d in range(HG_DIAG)}
    heads = range(H)
    cols = [slice(h * W, (h + 1) * W) for h in heads]

    def level_ref(b, hs):
        if hs >= PAD:
            parts = [jnp.broadcast_to(b[g * 2 * hs + hs - 1:g * 2 * hs + hs, :], (2 * hs, W))
                     for g in range(C // (2 * hs))]
            return parts[0] if len(parts) == 1 else jnp.concatenate(parts, axis=0)
        b3 = b.reshape(C // (2 * hs), 2 * hs, W)
        return jnp.broadcast_to(b3[:, hs - 1:hs, :], b3.shape).reshape(C, W)

    def projection(r):
        rows = slice(r * RC, (r + 1) * RC)
        out = {}

        def mm(off):
            return jnp.dot(out["xb"], w_ref[:, off:off + 512], preferred_element_type=F32)

        def sec_q():
            out["xb"] = x_ref[rows, :].astype(BF16)
            q_ref[rows, :] = (mm(offs["aq"]) * (ATT_DH ** -0.5 * LOG2E)).astype(BF16)

        def sec_k():
            k_ref[rows, :] = mm(offs["ak"]).astype(BF16)

        def sec_v():
            v = mm(offs["av"])
            for hd in range(ATT_HEADS):
                vt_ref[r, hd * ATT_VR:hd * ATT_VR + HEAD_W, :] = (
                    v[:, hd * HEAD_W:(hd + 1) * HEAD_W].T.astype(BF16))
                vt_ref[r, hd * ATT_VR + HEAD_W:(hd + 1) * ATT_VR, :] = jnp.ones(
                    (ATT_VR - HEAD_W, ATT_BK), BF16)

        def sec_hq():
            out["hq"] = mm(offs["hq"])

        def sec_hf():
            sig = jax.nn.sigmoid(mm(offs["hf"]))
            out["lf"] = jnp.log2(lb + (1.0 - lb) * sig)
            out["kk"] = (1.0 - lb) * (1.0 - sig)

        def sec_hi():
            out["hi"] = mm(offs["hi"]).astype(BF16)

        def sec_hg():
            hog = mm(offs["hg"])
            out["sog"] = hog * jax.nn.sigmoid(hog)

        return [sec_q, sec_k, sec_v, sec_hq, sec_hf, sec_hi, sec_hg], out

    def recurrence(r, vals):
        pieces = []
        for sub in range(RC // C):
            sl = slice(sub * C, (sub + 1) * C)
            t0 = r * RC + sub * C
            st = {}
            s0 = 3 * sub

            def cumsum_prep(sl=sl, st=st):
                lf = vals["lf"][sl]
                st["p0"] = lf.astype(BF16)
                st["p1"] = (lf - st["p0"].astype(F32)).astype(BF16)

            def cumsum_fire(st=st):
                b_all = (jnp.dot(tri, st["p0"], preferred_element_type=F32)
                         + jnp.dot(tri, st["p1"], preferred_element_type=F32))
                st["b"] = [b_all[:, cols[h]] for h in heads]

            def inter_prep(sl=sl, st=st):
                st["qf"] = [vals["hq"][sl, cols[h]] for h in heads]
                st["kf"] = [vals["kk"][sl, cols[h]] for h in heads]
                st["iv"] = [vals["hi"][sl, cols[h]] for h in heads]
                st["st"] = [state_ref[h] for h in heads]
                st["qt"] = [(st["qf"][h] * jnp.exp2(st["b"][h])).astype(BF16) for h in heads]
                st["stb"] = [st["st"][h].astype(BF16) for h in heads]
                st["a"] = [None] * H

            def inter_fire(st=st):
                st["inter"] = [lax.dot_general(st["qt"][h], st["stb"][h],
                                               (((1,), (1,)), ((), ())),
                                               preferred_element_type=F32) for h in heads]

            def level_prep(hs, st=st):
                ys = []
                for h in heads:
                    b = st["b"][h]
                    e = jnp.exp2((b - level_ref(b, hs)) * sign[hs])
                    ys.append((jnp.where(is_q[hs], st["qf"][h], st["kf"][h]) * e).astype(BF16))
                st["y", hs] = ys

            def level_fire(hs, st=st):
                for h in heads:
                    y = st["y", hs][h]
                    sc = lax.dot_general(y, y, (((1,), (1,)), ((), ())),
                                         preferred_element_type=F32)
                    sc = jnp.where(keep[hs], sc, 0.0)
                    st["a"][h] = sc if st["a"][h] is None else st["a"][h] + sc

            def diagonals(st=st):
                for h in heads:
                    kbuf_ref[h, PAD:, :] = st["kf"][h]
                    bbuf_ref[h, PAD:, :] = st["b"][h]
                diag = [None] * H
                for d in range(HG_DIAG):
                    for h in heads:
                        if d == 0:
                            term = st["qf"][h] * st["kf"][h]
                        else:
                            ks = kbuf_ref[h, PAD - d:PAD - d + C, :]
                            bs = bbuf_ref[h, PAD - d:PAD - d + C, :]
                            term = st["qf"][h] * ks * jnp.exp2(st["b"][h] - bs)
                        dsum = jnp.sum(term, axis=-1, keepdims=True)
                        placed = jnp.where(on_diag[d], dsum, 0.0)
                        diag[h] = placed if diag[h] is None else diag[h] + placed
                st["diag"] = diag

            def output_prep(st=st):
                st["ab"] = [(st["a"][h] + st["diag"][h]).astype(BF16) for h in heads]

            def output_fire(st=st):
                st["o"] = [st["inter"][h] + jnp.dot(st["ab"][h], st["iv"][h],
                                                    preferred_element_type=F32)
                           for h in heads]

            def update_prep(st=st):
                st["b_last"] = [st["b"][h][C - 1:C, :] for h in heads]
                st["k_end"] = [(st["kf"][h] * jnp.exp2(st["b_last"][h] - st["b"][h])
                                ).astype(BF16) for h in heads]

            def update_fire(st=st):
                for h in heads:
                    upd = lax.dot_general(st["iv"][h], st["k_end"][h],
                                          (((0,), (0,)), ((), ())),
                                          preferred_element_type=F32)
                    state_ref[h] = st["st"][h] * jnp.exp2(st["b_last"][h]) + upd

            def emit(sl=sl, st=st, t0=t0):
                for h in heads:
                    o = st["o"][h]
                    ms = jnp.mean(o * o, axis=-1, keepdims=True)
                    y = o * lax.rsqrt(ms + EPS) * nw_ref[...] * vals["sog"][sl, cols[h]]
                    yh_ref[t0:t0 + C, cols[h]] = y.astype(BF16)

            levels = [(functools.partial(level_prep, hs), functools.partial(level_fire, hs))
                      for hs in HG_LEVELS]
            pieces.append((0, cumsum_prep, cumsum_fire))
            pieces.append((s0 + 1, inter_prep, inter_fire))
            pieces += [(s0 + 1, p, f) for p, f in levels[:2]]
            pieces += [(s0 + 2, p, f) for p, f in levels[2:]]
            pieces.append((s0 + 3, update_prep, update_fire))
            pieces.append((s0 + 3, diagonals, None))
            pieces.append((s0 + 4, output_prep, output_fire))
            pieces.append((s0 + 5, emit, None))
        return pieces

    n_chunks = tm // RC
    sections = []
    schedule = {}
    for r in range(n_chunks):
        proj, vals = projection(r)
        sections += proj
        base = (r + 1) * len(proj)
        for slot, prep, fire in recurrence(r, vals):
            schedule.setdefault(base + slot, []).append((prep, fire))
    for t in range(max(schedule) + 2):
        for _, fire in schedule.get(t - 1, []):
            if fire is not None:
                fire()
        if t < len(sections):
            sections[t]()
        for prep, _ in schedule.get(t, []):
            prep()


def _mix(x, w_bf, lb, norm_w, offs, tm=1024):
    B, S, D = x.shape
    assert ROW_CHUNK == ATT_BK and ROW_CHUNK % HG_C == 0
    row = lambda w: pl.BlockSpec((None, tm, w), lambda b, s: (b, s, 0))
    out_shape = (
        jax.ShapeDtypeStruct((B, S, 512), BF16),
        jax.ShapeDtypeStruct((B, S, 512), BF16),
        jax.ShapeDtypeStruct((B, S // ATT_BK, ATT_HEADS * ATT_VR, ATT_BK), BF16),
        jax.ShapeDtypeStruct((B, S, HG_HEADS * HEAD_W), BF16),
    )
    out_specs = (
        row(512), row(512),
        pl.BlockSpec((None, tm // ATT_BK, ATT_HEADS * ATT_VR, ATT_BK),
                     lambda b, s: (b, s, 0, 0)),
        row(HG_HEADS * HEAD_W),
    )
    return pl.pallas_call(
        functools.partial(_mix_kernel, tm=tm, offs=offs),
        grid=(B, S // tm),
        in_specs=[row(D), _const_spec(w_bf.shape), _const_spec(lb.shape),
                  _const_spec(norm_w.shape)],
        out_specs=out_specs,
        out_shape=out_shape,
        scratch_shapes=[pltpu.VMEM((HG_HEADS, HEAD_W, HEAD_W), F32),
                        pltpu.VMEM((HG_HEADS, HG_C + 8, HEAD_W), F32),
                        pltpu.VMEM((HG_HEADS, HG_C + 8, HEAD_W), F32)],
        compiler_params=_cparams(2, independent_axes=1),
        name="inproj_hgrn2",
    )(x, w_bf, lb, norm_w)


def _rel_bucket_np(rel):
    nb = NUM_BUCKETS // 2
    max_exact = nb // 2
    ret = np.where(rel > 0, nb, 0)
    n = np.abs(rel)
    nf = np.maximum(n, 1).astype(np.float32)
    large = max_exact + (np.log(nf / np.float32(max_exact))
                         / np.float32(math.log(MAX_DISTANCE / max_exact))
                         * np.float32(nb - max_exact)).astype(np.int32)
    large = np.minimum(large, nb - 1)
    return ret + np.where(n < max_exact, n, large)


def _attn_bucket_tiles():
    i = np.arange(ATT_BQ)[None, :]
    j = np.arange(ATT_BK)[:, None]
    tiles = []
    for d in ATT_NEAR:
        kpos = d * ATT_BQ + j
        live = (kpos // CHUNK) <= (i // CHUNK)
        tiles.append(np.where(live, _rel_bucket_np(kpos - i), -1))
    far = _rel_bucket_np(np.asarray((min(ATT_NEAR) - 1) * ATT_BQ + ATT_BK - 1))
    assert int(far) == NUM_BUCKETS // 2 - 1
    return np.stack(tiles).astype(np.int32)


def _attn_kernel(lam_ref, rb_ref, bkt_ref, q_ref, k_ref, vt_ref, w_ref, o_ref,
                 bias_ref, qs_ref, s0_ref, s1_ref, p0_ref, p1_ref, a_ref,
                 m_ref, acc_ref, *, out_scale):
    S = q_ref.shape[0]
    nkb = S // ATT_BK
    W2 = 2 * ATT_BQ
    near = tuple(range(ATT_GK)) + (-1,)
    lam = lam_ref[0]
    h = pl.program_id(0)

    @pl.when(pl.program_id(1) == 0)
    def _():
        far = rb_ref[NUM_BUCKETS // 2 - 1, h]
        for n, kbrel in enumerate(near):
            for j in range(ATT_G):
                d = (kbrel * ATT_BK) // ATT_BQ - j
                if d > max(ATT_NEAR):
                    bias_ref[n, j] = jnp.full((ATT_BK, W2), NEG, F32)
                elif d < min(ATT_NEAR):
                    bias_ref[n, j] = jnp.zeros((ATT_BK, W2), F32)
        for t, dt in enumerate(ATT_NEAR):
            bk = bkt_ref[t]
            tile = jnp.full(bk.shape, NEG, F32)
            for bucket in range(NUM_BUCKETS):
                tile = jnp.where(bk == bucket, (rb_ref[bucket, h] - far) * LOG2E, tile)
            tile2 = jnp.concatenate([tile, tile], axis=1)
            for n, kbrel in enumerate(near):
                for j in range(ATT_G):
                    if (kbrel * ATT_BK) // ATT_BQ - j == dt:
                        bias_ref[n, j] = tile2

    lane = lax.broadcasted_iota(jnp.int32, (ATT_BQ, HEAD_W), 1)
    first_map = lane < ATT_DH

    s_bufs, p_bufs = (s0_ref, s1_ref), (p0_ref, p1_ref)

    def scores(step, j, buf):
        kb, bias_idx = step
        kblk = k_ref[pl.ds(pl.multiple_of(kb * ATT_BK, ATT_BK), ATT_BK), :]
        s = lax.dot_general(kblk, qs_ref[j], (((1,), (1,)), ((), ())),
                            preferred_element_type=F32)
        if bias_idx is not None:
            s = s + bias_ref[bias_idx, j]
        s_bufs[buf][j] = s

    def soft(j, bufs, first=False):
        tiles = [s_bufs[b][j] for b in bufs]
        m_new = functools.reduce(
            jnp.maximum, [jnp.max(s, axis=0, keepdims=True) for s in tiles])
        if first:
            a_ref[j] = jnp.ones_like(m_new)
        else:
            m_old = m_ref[j]
            m_new = jnp.maximum(m_old, m_new)
            a_ref[j] = jnp.exp2(m_old - m_new)
        m_ref[j] = m_new
        for b, s in zip(bufs, tiles):
            p_bufs[b][j] = jnp.exp2(s - m_new).astype(BF16)

    def accum(kbs, j, bufs, first=False):
        pv = functools.reduce(
            lambda x, y: x + y,
            [jnp.dot(vt_ref[kb], p_bufs[b][j], preferred_element_type=F32)
             for kb, b in zip(kbs, bufs)])
        if first:
            acc_ref[j] = pv
        else:
            acc_ref[j] = a_ref[j] * acc_ref[j] + pv

    AHEAD = 2

    def two_steps(steps, first=False, prev_kbs=None, dead=()):
        def live(j):
            return [i for i in (0, 1) if (i, j) not in dead]

        for k in range(ATT_G):
            ua = k + AHEAD
            for i in (0, 1):
                si = 2 * (ua // ATT_G) + i
                if (si, ua % ATT_G) not in dead:
                    scores(steps[si], ua % ATT_G, i)
            if k >= 1:
                accum([steps[i][0] for i in live(k - 1)], k - 1, live(k - 1), first=first)
            elif prev_kbs is not None:
                accum(prev_kbs, ATT_G - 1, (0, 1))
            soft(k, live(k), first=first)

    def inside(g):
        return [(g * ATT_GK + i, i) for i in range(ATT_GK)]

    def before(g):
        return (jnp.maximum(g * ATT_GK - 1, 0), ATT_GK)

    def far(n):
        return (jnp.clip(n - (ATT_GK + 1), 0, nkb - 1), None)

    def step_kb(g, n):
        return jnp.where(n == ATT_GK, before(g)[0], far(n)[0])

    dead = {(i, j) for i in range(ATT_GK) for j in range(ATT_G)
            if (i * ATT_BK) // ATT_BQ - j > max(ATT_NEAR)}

    def open_group(g):
        q0 = pl.multiple_of(g * (ATT_G * ATT_BQ), ATT_G * ATT_BQ)
        for j in range(ATT_G):
            q = q_ref[pl.ds(q0 + j * ATT_BQ, ATT_BQ), :]
            zero = jnp.zeros_like(q)
            qs_ref[j, :ATT_BQ, :] = jnp.where(first_map, q, zero)
            qs_ref[j, ATT_BQ:, :] = jnp.where(first_map, zero, q)
        for j in range(AHEAD):
            for i in range(ATT_GK):
                if (i, j) not in dead:
                    scores(inside(g)[i], j, i)

    def first_steps(g):
        acc_ref[ATT_G - 1] = jnp.zeros(acc_ref.shape[1:], F32)
        two_steps(inside(g) + [before(g), far(ATT_GK + 1)], first=True, dead=dead)

    def close_group(g, n_steps):
        kbs = [jnp.where(g > 0, step_kb(g, n_steps - 2 + i), inside(g)[i][0]) for i in (0, 1)]
        accum(kbs, ATT_G - 1, (0, 1))

    def emit_group(g):
        q0 = pl.multiple_of(g * (ATT_G * ATT_BQ), ATT_G * ATT_BQ)
        for j in range(ATT_G):
            on = acc_ref[j, :HEAD_W, :] * (1.0 / acc_ref[j, HEAD_W:HEAD_W + 1, :])
            o = on[:, :ATT_BQ] - lam * on[:, ATT_BQ:]
            ms = jnp.mean(o * o, axis=0, keepdims=True)
            y = o * lax.rsqrt(ms + EPS) * w_ref[...] * out_scale
            o_ref[pl.ds(q0 + j * ATT_BQ, ATT_BQ), :] = y.T.astype(BF16)

    n_groups = S // (ATT_G * ATT_BQ)
    open_group(0)
    first_steps(0)

    def group(g, carry):
        n_steps = jnp.where(g > 0, ATT_GK + 1 + (g * ATT_GK - 1), ATT_GK)

        def far_steps(n):
            two_steps([far(n + i) for i in range(4)],
                      prev_kbs=[step_kb(g, n - 2), step_kb(g, n - 1)])

        n_far_pairs = jnp.maximum(n_steps // 2 - 2, 0)
        first_far = ATT_GK + 2

        def far_block(n, pairs):
            for i in range(pairs):
                far_steps(n + 2 * i)

        def six_far_steps(i, c):
            far_block(first_far + 6 * i, 3)
            return c

        n_six = n_far_pairs // 3
        lax.fori_loop(0, n_six, six_far_steps, 0)
        for left in (1, 2):
            @pl.when(n_far_pairs - 3 * n_six == left)
            def _(left=left):
                far_block(first_far + 6 * n_six, left)

        @pl.when(g + 1 < n_groups)
        def _():
            close_group(g, n_steps)
            open_group(g + 1)
            emit_group(g)
            first_steps(g + 1)
            two_steps([before(g + 1)] + [far(ATT_GK + 1 + i) for i in range(3)],
                      prev_kbs=[kb for kb, _ in inside(g + 1)])

        @pl.when(g + 1 == n_groups)
        def _():
            close_group(g, n_steps)
            emit_group(g)

        return carry

    lax.fori_loop(0, n_groups, group, 0)


def _attention(q, k, vt, rel_bias, subln_w, lam, out_scale):
    B, S, _ = q.shape
    nkb = S // ATT_BK
    W2 = 2 * ATT_BQ
    assert ATT_GK == 2 and ATT_BK == 2 * ATT_BQ
    bkt = jnp.asarray(_attn_bucket_tiles())
    head = pl.BlockSpec((None, S, HEAD_W), lambda h, b: (b, 0, h))
    smem = pl.BlockSpec(memory_space=pltpu.SMEM)
    return pl.pallas_call(
        functools.partial(_attn_kernel, out_scale=out_scale),
        grid=(ATT_HEADS, B),
        in_specs=[
            smem, smem, _const_spec(bkt.shape),
            head, head,
            pl.BlockSpec((None, nkb, ATT_VR, ATT_BK), lambda h, b: (b, 0, h, 0)),
            pl.BlockSpec((HEAD_W, 1), lambda h, b: (0, 0)),
        ],
        out_specs=head,
        out_shape=jax.ShapeDtypeStruct((B, S, ATT_HEADS * HEAD_W), BF16),
        scratch_shapes=[
            pltpu.VMEM((ATT_GK + 1, ATT_G, ATT_BK, W2), F32),
            pltpu.VMEM((ATT_G, W2, HEAD_W), BF16),
            pltpu.VMEM((ATT_G, ATT_BK, W2), F32),
            pltpu.VMEM((ATT_G, ATT_BK, W2), F32),
            pltpu.VMEM((ATT_G, ATT_BK, W2), BF16),
            pltpu.VMEM((ATT_G, ATT_BK, W2), BF16),
            pltpu.VMEM((ATT_G, 1, W2), F32),
            pltpu.VMEM((ATT_G, 1, W2), F32),
            pltpu.VMEM((ATT_G, ATT_VR, W2), F32),
        ],
        compiler_params=_cparams(2),
        name="diffattn",
    )(lam, rel_bias, bkt, q, k, vt, subln_w)


def _layer_norm(z, g, b):
    mu = jnp.mean(z, axis=-1, keepdims=True)
    zc = z - mu
    var = jnp.mean(zc * zc, axis=-1, keepdims=True)
    return zc * lax.rsqrt(var + EPS) * g + b


def _merge_kernel(ya_ref, yh_ref, x_ref, wa_ref, wh_ref, wg_ref, bg_ref, wo_ref,
                  g_ref, b_ref, o_ref, *, alpha):
    D = x_ref.shape[-1]
    tm = x_ref.shape[0]

    def branches(r):
        rows = slice(r * ROW_CHUNK, (r + 1) * ROW_CHUNK)
        xb = x_ref[rows, :].astype(BF16)
        pa = jnp.dot(ya_ref[rows, :], wa_ref[...], preferred_element_type=F32)
        ph = jnp.dot(yh_ref[rows, :], wh_ref[...], preferred_element_type=F32)
        ga = jnp.dot(xb, wg_ref[:, :D], preferred_element_type=F32)
        gh = jnp.dot(xb, wg_ref[:, D:], preferred_element_type=F32)
        return pa, ph, ga, gh

    n_chunks = tm // ROW_CHUNK
    nxt = branches(0)
    for r in range(n_chunks):
        rows = slice(r * ROW_CHUNK, (r + 1) * ROW_CHUNK)
        pa, ph, ga, gh = nxt
        if r + 1 < n_chunks:
            nxt = branches(r + 1)
        merged = (jax.nn.sigmoid(ga + bg_ref[:, :D]) * pa
                  + jax.nn.sigmoid(gh + bg_ref[:, D:]) * ph)
        mix = jnp.dot(merged.astype(BF16), wo_ref[...], preferred_element_type=F32)
        o_ref[rows, :] = _layer_norm(alpha * x_ref[rows, :] + mix, g_ref[...], b_ref[...])


def _merge(ya, yh, x, wa, wh, wg, bg, wo, g, b, alpha, tm=1024):
    N, D = x.shape
    row = lambda w: pl.BlockSpec((tm, w), lambda i: (i, 0))
    return pl.pallas_call(
        functools.partial(_merge_kernel, alpha=alpha),
        grid=(N // tm,),
        in_specs=[row(ya.shape[1]), row(yh.shape[1]), row(D),
                  _const_spec(wa.shape), _const_spec(wh.shape), _const_spec(wg.shape),
                  _const_spec(bg.shape), _const_spec(wo.shape),
                  _const_spec(g.shape), _const_spec(b.shape)],
        out_specs=row(D),
        out_shape=jax.ShapeDtypeStruct((N, D), F32),
        compiler_params=_cparams(1, independent_axes=1),
        name="merge_ln",
    )(ya, yh, x, wa, wh, wg, bg, wo, g, b)


def _ffn_kernel(h_ref, wg_ref, wu_ref, cw_ref, cb_ref, wd_ref, g_ref, b_ref, o_ref,
                tail_ref, gbuf_ref, *, alpha, tm):
    d_ff = wg_ref.shape[1]
    PAD = 8

    @pl.when(pl.program_id(1) == 0)
    def _():
        tail_ref[...] = jnp.zeros_like(tail_ref)

    RC = ROW_CHUNK
    nfc = d_ff // FFN_FC
    steps = [(r, fc) for r in range(tm // RC) for fc in range(nfc)]
    hb = {}

    def gate_up(r, fc):
        if r not in hb:
            hb[r] = h_ref[r * RC:(r + 1) * RC, :].astype(BF16)
        cols = slice(fc * FFN_FC, (fc + 1) * FFN_FC)
        return (jnp.dot(hb[r], wg_ref[:, cols], preferred_element_type=F32),
                jnp.dot(hb[r], wu_ref[:, cols], preferred_element_type=F32))

    nxt = gate_up(*steps[0])
    acc = None
    for t, (r, fc) in enumerate(steps):
        rows = slice(r * RC, (r + 1) * RC)
        cols = slice(fc * FFN_FC, (fc + 1) * FFN_FC)
        gte, up = nxt
        if t + 1 < len(steps):
            nxt = gate_up(*steps[t + 1])
        gbuf = gbuf_ref.at[t % 2]
        gbuf[0:PAD, :] = tail_ref[:, cols]
        gbuf[PAD:, :] = gte
        tail_ref[:, cols] = gte[RC - PAD:, :]
        conv = (gte * cw_ref[2:3, cols]
                + gbuf[PAD - 1:PAD - 1 + RC, :] * cw_ref[1:2, cols]
                + gbuf[PAD - 2:PAD - 2 + RC, :] * cw_ref[0:1, cols]
                + cb_ref[:, cols])
        act = jax.nn.gelu(conv) * up
        down = jnp.dot(act.astype(BF16), wd_ref[cols, :], preferred_element_type=F32)
        acc = down if fc == 0 else acc + down
        if fc == nfc - 1:
            o_ref[rows, :] = _layer_norm(alpha * h_ref[rows, :] + acc, g_ref[...], b_ref[...])


def _ffn(h, wg, wu, cw, cb, wd, g, b, alpha, tm=1024):
    B, S, D = h.shape
    d_ff = wg.shape[1]
    row = pl.BlockSpec((None, tm, D), lambda bi, s: (bi, s, 0))
    return pl.pallas_call(
        functools.partial(_ffn_kernel, alpha=alpha, tm=tm),
        grid=(B, S // tm),
        in_specs=[row, _const_spec(wg.shape), _const_spec(wu.shape), _const_spec(cw.shape),
                  _const_spec(cb.shape), _const_spec(wd.shape),
                  _const_spec(g.shape), _const_spec(b.shape)],
        out_specs=row,
        out_shape=jax.ShapeDtypeStruct((B, S, D), F32),
        scratch_shapes=[pltpu.VMEM((8, d_ff), F32),
                        pltpu.VMEM((2, ROW_CHUNK + 8, FFN_FC), F32)],
        compiler_params=_cparams(2, independent_axes=1),
        name="ffn_ln",
    )(h, wg, wu, cw, cb, wd, g, b)


def kernel(x, w_in, b_gate, lambda_q1, lambda_k1, lambda_q2, lambda_k2, attn_subln_w,
           rel_bias, hgrn_lb_logits, hgrn_norm_w, w_branch_attn, w_branch_hgrn, w_out,
           ln1_g, ln1_b, w_ffn_gate, w_ffn_up, ffn_conv_w, ffn_conv_b, w_ffn_down,
           ln2_g, ln2_b):
    B, S, D = x.shape
    depth = w_in.shape[0]
    att_qk = ATT_HEADS * 2 * ATT_DH
    att_w = ATT_HEADS * HEAD_W
    hg_w = HG_HEADS * HEAD_W
    offs = {"aq": 0, "ak": att_qk, "av": 2 * att_qk}
    offs["hq"] = offs["av"] + att_w
    offs["hf"] = offs["hq"] + hg_w
    offs["hi"] = offs["hf"] + hg_w
    offs["hg"] = offs["hi"] + hg_w
    offs["gate"] = offs["hg"] + hg_w
    alpha = (2 * depth) ** 0.25

    lb_all = jnp.cumsum(jax.nn.softmax(hgrn_lb_logits.astype(F32), axis=0), axis=0)

    h = x
    for l in range(depth):
        lam_init = 0.8 - 0.6 * math.exp(-0.3 * l)
        lam = (jnp.exp(jnp.sum(lambda_q1[l].astype(F32) * lambda_k1[l].astype(F32)))
               - jnp.exp(jnp.sum(lambda_q2[l].astype(F32) * lambda_k2[l].astype(F32)))
               + lam_init).reshape(1)

        w_mix_bf = w_in[l, :, :offs["gate"]].astype(BF16)
        w_gate_bf = w_in[l, :, offs["gate"]:].astype(BF16)
        q, k, vt, yh = _mix(h, w_mix_bf, lb_all[l][None, :],
                            hgrn_norm_w[l][None, :].astype(F32), offs)

        ya = _attention(q, k, vt, rel_bias.astype(F32),
                        attn_subln_w[l][:, None].astype(F32), lam, 1.0 - lam_init)

        h1 = _merge(ya.reshape(B * S, att_w), yh.reshape(B * S, hg_w), h.reshape(B * S, D),
                    w_branch_attn[l].astype(BF16), w_branch_hgrn[l].astype(BF16),
                    w_gate_bf, b_gate[l][None, :], w_out[l].astype(BF16),
                    ln1_g[l][None, :], ln1_b[l][None, :], alpha)

        h = _ffn(h1.reshape(B, S, D), w_ffn_gate[l].astype(BF16), w_ffn_up[l].astype(BF16),
                 ffn_conv_w[l], ffn_conv_b[l][None, :], w_ffn_down[l].astype(BF16),
                 ln2_g[l][None, :], ln2_b[l][None, :], alpha)
    return h
```
